```python
import math
import jax, jax.numpy as jnp
from jax import lax
import numpy as np

D_MODEL = 1024
BATCH = 16
SEQ = 2048
DEPTH = 2

CTX_LEN = 256
GRID_W = 64
D_MIX = D_MODEL
DN_HEAD_DIM = 128
DN_WIDTH = D_MIX // 2
DN_HEADS = DN_WIDTH // DN_HEAD_DIM
DN_CHUNK = 64
RG_WIDTH = D_MIX - DN_WIDTH
RG_BLOCKS = 8
RG_BLOCK_DIM = RG_WIDTH // RG_BLOCKS
RG_C = 8.0
CONV_K = 4
CONV_PAD_L = 2
N_GROUPS = 4
EXPERTS_PER_GROUP = 8
N_EXPERTS = N_GROUPS * EXPERTS_PER_GROUP
TOP_K = 2
D_EXPERT = D_MODEL // 4
DEEP_ALPHA = (2 * DEPTH) ** 0.25
DEEP_BETA = (8 * DEPTH) ** -0.25
LN_EPS = 1e-5
NORM_EPS = 1e-6
SPLITS = (3 * DN_WIDTH, 4 * DN_WIDTH, 4 * DN_WIDTH + 2 * DN_HEADS, 4 * DN_WIDTH + 4 * DN_HEADS,
          4 * DN_WIDTH + 4 * DN_HEADS + RG_WIDTH)
D_IN = 4 * DN_WIDTH + 4 * DN_HEADS + 2 * RG_WIDTH

kernel_name = "hybrid_deltanet_rglru_hmoe_dit"


def layer_norm(x):
    xf = x.astype(jnp.float32)
    mu = jnp.mean(xf, axis=-1, keepdims=True)
    var = jnp.mean(jnp.square(xf - mu), axis=-1, keepdims=True)
    return (xf - mu) * lax.rsqrt(var + LN_EPS)


def modulate(x, shift, scale):
    return (layer_norm(x) * (1.0 + scale) + shift).astype(x.dtype)


def post_norm(r, g, b):
    return (layer_norm(r) * g + b).astype(r.dtype)


def l2norm(t):
    return t * lax.rsqrt(jnp.sum(t * t, axis=-1, keepdims=True) + NORM_EPS)


def _rev(t, d):
    return jnp.flip(t, axis=1) if d == 1 else t


def dwconv_centred(x, w):
    L = x.shape[1]
    xp = jnp.pad(x, ((0, 0), (CONV_PAD_L, CONV_K - 1 - CONV_PAD_L), (0, 0)))
    y = xp[:, 0:L] * w[0]
    for j in range(1, CONV_K):
        y = y + xp[:, j:j + L] * w[j]
    return y


def conv_latent_rows(x, w):
    B, L, C = x.shape
    rows = L // GRID_W
    return dwconv_centred(x.reshape(B * rows, GRID_W, C), w).reshape(B, L, C)


def gated_delta_chunked(q, k, v, g, beta, s0):
    B, T, H, DK = q.shape
    DV = v.shape[-1]
    C = DN_CHUNK
    N = T // C
    to_chunks = lambda t: t.reshape(B, N, C, H, -1).transpose(0, 3, 1, 2, 4)
    q, k, v = to_chunks(q), to_chunks(k), to_chunks(v)
    beta = to_chunks(beta[..., None])
    gc = jnp.cumsum(to_chunks(g[..., None])[..., 0], axis=-1)
    incl = jnp.tril(jnp.ones((C, C), dtype=bool))
    strict = jnp.tril(jnp.ones((C, C), dtype=bool), -1)
    decay = jnp.exp(jnp.where(incl, gc[..., :, None] - gc[..., None, :], -jnp.inf))
    kb = k * beta
    vb = v * beta
    a_strict = jnp.where(strict, jnp.einsum('bhnid,bhnjd->bhnij', kb, k) * decay, 0.0)
    eye = jnp.eye(C, dtype=q.dtype)
    rhs = jnp.concatenate([vb, kb * jnp.exp(gc)[..., None]], axis=-1)
    sol = lax.linalg.triangular_solve(eye + a_strict, rhs, left_side=True, lower=True, unit_diagonal=True)
    u, w = sol[..., :DV], sol[..., DV:]
    intra = jnp.einsum('bhnid,bhnjd->bhnij', q, k) * decay
    q_dec = q * jnp.exp(gc)[..., None]
    k_dec = k * jnp.exp(gc[..., -1:] - gc)[..., None]
    g_last = jnp.exp(gc[..., -1])

    def step(S, xs):
        qd, kd, u_i, w_i, at, gl = xs
        v_new = u_i - jnp.einsum('bhck,bhkv->bhcv', w_i, S)
        o = jnp.einsum('bhck,bhkv->bhcv', qd, S) + jnp.einsum('bhij,bhjv->bhiv', at, v_new)
        S = S * gl[..., None, None] + jnp.einsum('bhck,bhcv->bhkv', kd, v_new)
        return S, o

    xs = tuple(jnp.moveaxis(t, 2, 0) for t in (q_dec, k_dec, u, w, intra, g_last))
    s_final, o = lax.scan(step, s0, xs)
    o = o.transpose(1, 0, 3, 2, 4).reshape(B, T, H, DV)
    return o, s_final


def dn_prep(qkv, beta_raw, decay_raw, a_log, dt_bias):
    B, L, _ = qkv.shape
    qkv = jax.nn.silu(qkv).astype(jnp.float32)
    q, k, v = jnp.split(qkv, 3, axis=-1)
    q = l2norm(q.reshape(B, L, DN_HEADS, DN_HEAD_DIM)) * (DN_HEAD_DIM ** -0.5)
    k = l2norm(k.reshape(B, L, DN_HEADS, DN_HEAD_DIM))
    v = v.reshape(B, L, DN_HEADS, DN_HEAD_DIM)
    beta = jax.nn.sigmoid(beta_raw.astype(jnp.float32)).reshape(B, L, 2, DN_HEADS)
    g = -jnp.exp(a_log.astype(jnp.float32)) * jax.nn.softplus(
        decay_raw.astype(jnp.float32).reshape(B, L, 2, DN_HEADS) + dt_bias.astype(jnp.float32))
    return q, k, v, beta, g


def bidir_deltanet(dn_c, dn_l):
    qc, kc, vc, bc, gcx = dn_c
    ql, kl, vl, bl, glx = dn_l
    B = qc.shape[0]
    outs_c, outs_l = [], []
    for d in range(2):
        s0 = jnp.zeros((B, DN_HEADS, DN_HEAD_DIM, DN_HEAD_DIM), jnp.float32)
        oc, sc = gated_delta_chunked(_rev(qc, d), _rev(kc, d), _rev(vc, d), _rev(gcx[:, :, d], d),
                                     _rev(bc[:, :, d], d), s0)
        ol, _ = gated_delta_chunked(_rev(ql, d), _rev(kl, d), _rev(vl, d), _rev(glx[:, :, d], d),
                                    _rev(bl[:, :, d], d), sc)
        outs_c.append(_rev(oc, d))
        outs_l.append(_rev(ol, d))
    return outs_c[0] + outs_c[1], outs_l[0] + outs_l[1]


def gated_rmsnorm(o, z, w):
    B, L = o.shape[:2]
    o = o * lax.rsqrt(jnp.mean(o * o, axis=-1, keepdims=True) + NORM_EPS) * w.astype(jnp.float32)
    return o.reshape(B, L, DN_WIDTH) * jax.nn.silu(z.astype(jnp.float32))


def rglru_coeffs(xc, w_a, b_a, w_i, b_i, lam):
    B, L, _ = xc.shape
    xb = xc.reshape(B, L, RG_BLOCKS, RG_BLOCK_DIM)
    r = jax.nn.sigmoid(jnp.einsum('blnc,ncd->blnd', xb, w_a.astype(jnp.float32)).reshape(B, L, RG_WIDTH)
                       + b_a.astype(jnp.float32))
    i = jax.nn.sigmoid(jnp.einsum('blnc,ncd->blnd', xb, w_i.astype(jnp.float32)).reshape(B, L, RG_WIDTH)
                       + b_i.astype(jnp.float32))
    log_a = -RG_C * r * jax.nn.softplus(-lam.astype(jnp.float32))
    a = jnp.exp(log_a)
    b = jnp.sqrt(-jnp.expm1(2.0 * log_a)) * (i * xc)
    return a, b


def linear_scan(a, b, h0):
    b = b.at[:, 0].add(a[:, 0] * h0)

    def combine(left, right):
        a_l, b_l = left
        a_r, b_r = right
        return a_l * a_r, a_r * b_l + b_r

    _, h = lax.associative_scan(combine, (a, b), axis=1)
    return h


def bidir_rglru(xc_c, xc_l, wa, ba, wi, bi, lam):
    B = xc_c.shape[0]
    outs_c, outs_l = [], []
    for d in range(2):
        a_c, b_c = rglru_coeffs(_rev(xc_c, d), wa[d], ba[d], wi[d], bi[d], lam[d])
        h_c = linear_scan(a_c, b_c, jnp.zeros((B, RG_WIDTH), jnp.float32))
        a_l, b_l = rglru_coeffs(_rev(xc_l, d), wa[d], ba[d], wi[d], bi[d], lam[d])
        h_l = linear_scan(a_l, b_l, h_c[:, -1])
        outs_c.append(_rev(h_c, d))
        outs_l.append(_rev(h_l, d))
    return outs_c[0] + outs_c[1], outs_l[0] + outs_l[1]


def hybrid_mixer(h_c, h_l, w_in, conv_qkv_w, dn_a_log, dn_dt_bias, dn_onorm_w, rg_conv_w, rg_conv_b,
                 rg_wa, rg_ba, rg_wi, rg_bi, rg_lambda, w_out):
    n_ctx = h_c.shape[1]
    proj = jnp.concatenate([h_c, h_l], axis=1) @ w_in
    qkv_c, z_c, br_c, dr_c, rx_c, ry_c = jnp.split(proj[:, :n_ctx], SPLITS, axis=-1)
    qkv_l, z_l, br_l, dr_l, rx_l, ry_l = jnp.split(proj[:, n_ctx:], SPLITS, axis=-1)
    dn_c = dn_prep(dwconv_centred(qkv_c, conv_qkv_w), br_c, dr_c, dn_a_log, dn_dt_bias)
    dn_l = dn_prep(conv_latent_rows(qkv_l, conv_qkv_w), br_l, dr_l, dn_a_log, dn_dt_bias)
    o_c, o_l = bidir_deltanet(dn_c, dn_l)
    y_dn_c = gated_rmsnorm(o_c, z_c, dn_onorm_w)
    y_dn_l = gated_rmsnorm(o_l, z_l, dn_onorm_w)
    xc_c = (dwconv_centred(rx_c, rg_conv_w) + rg_conv_b).astype(jnp.float32)
    xc_l = (conv_latent_rows(rx_l, rg_conv_w) + rg_conv_b).astype(jnp.float32)
    hr_c, hr_l = bidir_rglru(xc_c, xc_l, rg_wa, rg_ba, rg_wi, rg_bi, rg_lambda)
    y_rg_c = hr_c * jax.nn.gelu(ry_c.astype(jnp.float32))
    y_rg_l = hr_l * jax.nn.gelu(ry_l.astype(jnp.float32))
    y = jnp.concatenate([jnp.concatenate([y_dn_c, y_rg_c], axis=-1),
                         jnp.concatenate([y_dn_l, y_rg_l], axis=-1)], axis=1).astype(h_l.dtype) @ w_out
    return y[:, :n_ctx], y[:, n_ctx:]


def hierarchical_moe(h, wg, bg, we, be, w_gate, w_up, w_down):
    T = h.shape[0]
    hf = h.astype(jnp.float32)
    rows = jnp.arange(T)
    group_logits = hf @ wg.astype(jnp.float32) + bg.astype(jnp.float32)
    g_sel = jnp.argmax(group_logits, axis=-1)
    p_group = jax.nn.softmax(group_logits, axis=-1)[rows, g_sel]
    expert_logits = (hf @ we.astype(jnp.float32) + be.astype(jnp.float32)).reshape(
        T, N_GROUPS, EXPERTS_PER_GROUP)[rows, g_sel]
    top_val, top_idx = lax.top_k(expert_logits, TOP_K)
    top_w = jax.nn.softmax(top_val, axis=-1) * p_group[:, None]
    expert_id = g_sel[:, None] * EXPERTS_PER_GROUP + top_idx
    combine = jnp.einsum('tk,tke->te', top_w,
                         jax.nn.one_hot(expert_id, N_EXPERTS, dtype=jnp.float32)).astype(h.dtype)
    y = None
    for gi in range(N_GROUPS):
        sl = slice(gi * EXPERTS_PER_GROUP, (gi + 1) * EXPERTS_PER_GROUP)
        a = jnp.einsum('td,edf->tef', h, w_gate[sl])
        b = jnp.einsum('td,edf->tef', h, w_up[sl])
        part = jnp.einsum('tef,efd->td', jax.nn.silu(a) * b * combine[:, sl, None], w_down[sl])
        y = part if y is None else y + part
    return y


def setup_inputs(seed: int = 0) -> dict:
    key = jax.random.key(seed)
    ks = jax.random.split(key, 28)
    f32 = jnp.float32
    nrm = lambda k, shape, s: jax.random.normal(k, shape, f32) * s
    x = nrm(ks[0], (BATCH, SEQ, D_MODEL), 1.0)
    c = nrm(ks[1], (BATCH, D_MODEL), 1.0)
    ctx = nrm(ks[2], (BATCH, CTX_LEN, D_MODEL), 1.0)
    c_ctx = nrm(ks[3], (D_MODEL,), 1.0)
    w_ada = nrm(ks[4], (DEPTH, D_MODEL, 6 * D_MODEL), 0.5 * D_MODEL ** -0.5)
    b_ada = nrm(ks[5], (DEPTH, 6 * D_MODEL), 0.02)
    w_in = nrm(ks[6], (DEPTH, D_MODEL, D_IN), D_MODEL ** -0.5)
    conv_qkv_w = nrm(ks[7], (DEPTH, CONV_K, 3 * DN_WIDTH), CONV_K ** -0.5)
    dn_a_log = jnp.log(jax.random.uniform(ks[8], (DEPTH, 2, DN_HEADS), f32, 1.0, 16.0))
    dt = jnp.exp(jax.random.uniform(ks[9], (DEPTH, 2, DN_HEADS), f32, math.log(1e-3), math.log(1e-1)))
    dn_dt_bias = dt + jnp.log(-jnp.expm1(-dt))
    dn_onorm_w = 1.0 + nrm(ks[10], (DEPTH, DN_HEAD_DIM), 0.02)
    rg_conv_w = nrm(ks[11], (DEPTH, CONV_K, RG_WIDTH), CONV_K ** -0.5)
    rg_conv_b = nrm(ks[12], (DEPTH, RG_WIDTH), 0.02)
    rg_wa = nrm(ks[13], (DEPTH, 2, RG_BLOCKS, RG_BLOCK_DIM, RG_BLOCK_DIM), RG_BLOCK_DIM ** -0.5)
    rg_ba = nrm(ks[14], (DEPTH, 2, RG_WIDTH), 0.02)
    rg_wi = nrm(ks[15], (DEPTH, 2, RG_BLOCKS, RG_BLOCK_DIM, RG_BLOCK_DIM), RG_BLOCK_DIM ** -0.5)
    rg_bi = nrm(ks[16], (DEPTH, 2, RG_WIDTH), 0.02)
    a_pow = jax.random.uniform(ks[17], (DEPTH, 2, RG_WIDTH), f32, 0.9, 0.999)
    a0 = a_pow ** (1.0 / RG_C)
    rg_lambda = jnp.log(a0) - jnp.log1p(-a0)
    w_out = nrm(ks[18], (DEPTH, D_MIX, D_MODEL), DEEP_BETA * D_MIX ** -0.5)
    ln_g = 1.0 + nrm(ks[19], (DEPTH, 2, D_MODEL), 0.02)
    ln_b = nrm(ks[20], (DEPTH, 2, D_MODEL), 0.02)
    router_wg = nrm(ks[21], (DEPTH, D_MODEL, N_GROUPS), D_MODEL ** -0.5)
    router_bg = nrm(ks[22], (DEPTH, N_GROUPS), 0.01)
    router_we = nrm(ks[23], (DEPTH, D_MODEL, N_EXPERTS), D_MODEL ** -0.5)
    router_be = nrm(ks[24], (DEPTH, N_EXPERTS), 0.01)
    w_e_gate = nrm(ks[25], (DEPTH, N_EXPERTS, D_MODEL, D_EXPERT), D_MODEL ** -0.5)
    w_e_up = nrm(ks[26], (DEPTH, N_EXPERTS, D_MODEL, D_EXPERT), D_MODEL ** -0.5)
    w_e_down = nrm(ks[27], (DEPTH, N_EXPERTS, D_EXPERT, D_MODEL), DEEP_BETA * D_EXPERT ** -0.5)
    return {"x": x, "c": c, "ctx": ctx, "c_ctx": c_ctx, "w_ada": w_ada, "b_ada": b_ada, "w_in": w_in,
            "conv_qkv_w": conv_qkv_w, "dn_a_log": dn_a_log, "dn_dt_bias": dn_dt_bias,
            "dn_onorm_w": dn_onorm_w, "rg_conv_w": rg_conv_w, "rg_conv_b": rg_conv_b, "rg_wa": rg_wa,
            "rg_ba": rg_ba, "rg_wi": rg_wi, "rg_bi": rg_bi, "rg_lambda": rg_lambda, "w_out": w_out,
            "ln_g": ln_g, "ln_b": ln_b, "router_wg": router_wg, "router_bg": router_bg,
            "router_we": router_we, "router_be": router_be, "w_e_gate": w_e_gate, "w_e_up": w_e_up,
            "w_e_down": w_e_down}


def reference(x, c, ctx, c_ctx, w_ada, b_ada, w_in, conv_qkv_w, dn_a_log, dn_dt_bias, dn_onorm_w,
              rg_conv_w, rg_conv_b, rg_wa, rg_ba, rg_wi, rg_bi, rg_lambda, w_out, ln_g, ln_b,
              router_wg, router_bg, router_we, router_be, w_e_gate, w_e_up, w_e_down):
    B, L, D = x.shape
    n_ctx = ctx.shape[1]
    for l in range(DEPTH):
        last = l == DEPTH - 1
        mod_l = (jax.nn.silu(c) @ w_ada[l] + b_ada[l])[:, None, :]
        mod_c = (jax.nn.silu(c_ctx) @ w_ada[l] + b_ada[l])[None, None, :]
        sh1_l, sc1_l, gt1_l, sh2_l, sc2_l, gt2_l = jnp.split(mod_l, 6, axis=-1)
        sh1_c, sc1_c, gt1_c, sh2_c, sc2_c, gt2_c = jnp.split(mod_c, 6, axis=-1)
        u_c, u_l = hybrid_mixer(modulate(ctx, sh1_c, sc1_c), modulate(x, sh1_l, sc1_l), w_in[l],
                                conv_qkv_w[l], dn_a_log[l], dn_dt_bias[l], dn_onorm_w[l], rg_conv_w[l],
                                rg_conv_b[l], rg_wa[l], rg_ba[l], rg_wi[l], rg_bi[l], rg_lambda[l], w_out[l])
        x = post_norm(DEEP_ALPHA * x + gt1_l * u_l, ln_g[l, 0], ln_b[l, 0])
        moe_args = (router_wg[l], router_bg[l], router_we[l], router_be[l], w_e_gate[l], w_e_up[l], w_e_down[l])
        if last:
            f_l = hierarchical_moe(modulate(x, sh2_l, sc2_l).reshape(B * L, D), *moe_args).reshape(B, L, D)
        else:
            ctx = post_norm(DEEP_ALPHA * ctx + gt1_c * u_c, ln_g[l, 0], ln_b[l, 0])
            h = jnp.concatenate([modulate(ctx, sh2_c, sc2_c), modulate(x, sh2_l, sc2_l)], axis=1)
            f = hierarchical_moe(h.reshape(B * (n_ctx + L), D), *moe_args).reshape(B, n_ctx + L, D)
            f_c, f_l = f[:, :n_ctx], f[:, n_ctx:]
            ctx = post_norm(DEEP_ALPHA * ctx + gt2_c * f_c, ln_g[l, 1], ln_b[l, 1])
        x = post_norm(DEEP_ALPHA * x + gt2_l * f_l, ln_g[l, 1], ln_b[l, 1])
    return x
```

```python
import functools
import math

import jax
import jax.numpy as jnp
from jax import lax
from jax.experimental import pallas as pl
from jax.experimental.pallas import tpu as pltpu

F32 = jnp.float32
BF16 = jnp.bfloat16
HIGHEST = lax.Precision.HIGHEST

D_MODEL = 1024
DEPTH = 2
GRID_W = 64
DN_HEAD_DIM = 128
DN_WIDTH = 512
DN_HEADS = 4
RG_WIDTH = 512
RG_BLOCKS = 8
RG_BLOCK_DIM = 64
RG_C = 8.0
CONV_K = 4
N_GROUPS = 4
EXPERTS_PER_GROUP = 8
N_EXPERTS = 32
D_EXPERT = 256
DEEP_ALPHA = (2 * DEPTH) ** 0.25
LN_EPS = 1e-5
NORM_EPS = 1e-6

LANES = 128
CHUNK = 64
D_PROJ = 3 * DN_WIDTH + DN_WIDTH + 2 * RG_WIDTH + LANES
COL_Z = 3
COL_RX = 4
COL_RY = 5
COL_GATE = (3 * DN_WIDTH + DN_WIDTH + 2 * RG_WIDTH) // LANES
ROW_TILE = 512
MOE_TILE = 256
VMEM_LIMIT = 56 * 1024 * 1024


def _cparams(sem):
    return pltpu.CompilerParams(dimension_semantics=sem, vmem_limit_bytes=VMEM_LIMIT)


def _layer_norm(x):
    mu = jnp.mean(x, axis=-1, keepdims=True)
    xc = x - mu
    var = jnp.mean(xc * xc, axis=-1, keepdims=True)
    return xc * lax.rsqrt(var + LN_EPS)


def _per_batch(x, v, nb):
    rows, c = x.shape
    return jnp.broadcast_to(v[None], (rows // nb, nb, c)).reshape(rows, c)


def _sigmoid(x):
    return 1.0 / (1.0 + jnp.exp(-x))


def _silu(x):
    return x * _sigmoid(x)


def _mod_kernel(cc_ref, w_ref, b_ref, o_ref):
    s = _silu(cc_ref[...])
    o_ref[...] = jnp.dot(s, w_ref[...], precision=HIGHEST, preferred_element_type=F32) + b_ref[...]


def _modulation(cc, w_ada, b_ada):
    depth, d, n6 = w_ada.shape
    rows = cc.shape[0]
    tn = 1536
    return pl.pallas_call(
        _mod_kernel,
        grid=(depth, n6 // tn),
        in_specs=[
            pl.BlockSpec((rows, d), lambda l, j: (0, 0)),
            pl.BlockSpec((None, d, tn), lambda l, j: (l, 0, j)),
            pl.BlockSpec((None, 1, tn), lambda l, j: (l, 0, j)),
        ],
        out_specs=pl.BlockSpec((None, rows, tn), lambda l, j: (l, 0, j)),
        out_shape=jax.ShapeDtypeStruct((depth, rows, n6), F32),
        compiler_params=_cparams(("parallel", "parallel")),
        name="adaln_mod",
    )(cc, w_ada, b_ada.reshape(depth, 1, n6))


def _inproj_kernel(nb, h_ref, sh_ref, sc_ref, w_ref, o_ref):
    xn = _layer_norm(h_ref[...])
    hm = xn * (1.0 + _per_batch(xn, sc_ref[...], nb)) + _per_batch(xn, sh_ref[...], nb)
    o_ref[...] = jnp.dot(hm.astype(BF16), w_ref[...], preferred_element_type=F32)


def _in_projection(h, mod_rows, w_in2, nb, ctx_rows):
    rows, d = h.shape
    tm = ROW_TILE
    lat = lambda i: jnp.where(i * tm >= ctx_rows, 1, 0)
    return pl.pallas_call(
        functools.partial(_inproj_kernel, nb),
        grid=(rows // tm,),
        in_specs=[
            pl.BlockSpec((tm, d), lambda i: (i, 0)),
            pl.BlockSpec((None, nb, d), lambda i: (lat(i), 0, 0)),
            pl.BlockSpec((None, nb, d), lambda i: (lat(i), 0, 1)),
            pl.BlockSpec((d, D_PROJ), lambda i: (0, 0)),
        ],
        out_specs=pl.BlockSpec((tm, D_PROJ), lambda i: (i, 0)),
        out_shape=jax.ShapeDtypeStruct((rows, D_PROJ), F32),
        compiler_params=_cparams(("parallel",)),
        name="ln_mod_inproj",
    )(h, mod_rows, mod_rows, w_in2)


def _conv_tile(x, prev, nxt, w, nb, use_prev, use_next):
    prev = jnp.where(use_prev, prev, 0.0)
    nxt = jnp.where(use_next, nxt, 0.0)
    xm2 = jnp.concatenate([prev, x[: -2 * nb]], axis=0)
    xm1 = jnp.concatenate([prev[nb:], x[:-nb]], axis=0)
    xp1 = jnp.concatenate([x[nb:], nxt], axis=0)
    return xm2 * w[0:1] + xm1 * w[1:2] + x * w[2:3] + xp1 * w[3:4]


def _halo_flags(i, n_ctx_tiles):
    use_prev = jnp.logical_and(i > 0, i < n_ctx_tiles)
    use_next = i < n_ctx_tiles - 1
    return use_prev, use_next


def _qkv_prep_kernel(nb, n_ctx_tiles, x_ref, prev_ref, next_ref, w_ref, o_ref):
    i = pl.program_id(0)
    part = pl.program_id(1)
    use_prev, use_next = _halo_flags(i, n_ctx_tiles)
    y = _silu(_conv_tile(x_ref[...], prev_ref[...], next_ref[...], w_ref[...], nb, use_prev, use_next))
    qscale = jnp.where(part == 0, DN_HEAD_DIM ** -0.5, 1.0).astype(F32)
    for hh in range(DN_HEADS):
        sl = slice(hh * DN_HEAD_DIM, (hh + 1) * DN_HEAD_DIM)
        yh = y[:, sl]
        inv = lax.rsqrt(jnp.sum(yh * yh, axis=-1, keepdims=True) + NORM_EPS) * qscale
        o_ref[hh] = yh * jnp.where(part < 2, inv, 1.0)


def _halo_specs(tile_rows, nb, cols, col_of, n_row_blocks16):
    prev_b = tile_rows // (2 * nb)
    next_b = tile_rows // nb
    return [
        pl.BlockSpec((tile_rows, cols), lambda i, *a: (i, col_of(*a))),
        pl.BlockSpec((2 * nb, cols), lambda i, *a: (jnp.maximum(i * prev_b - 1, 0), col_of(*a))),
        pl.BlockSpec((nb, cols), lambda i, *a: (jnp.minimum((i + 1) * next_b, n_row_blocks16 - 1), col_of(*a))),
    ]


def _qkv_prep(proj, conv_w, nb, n_ctx_tiles):
    rows = proj.shape[0]
    tr = CHUNK * nb
    return pl.pallas_call(
        functools.partial(_qkv_prep_kernel, nb, n_ctx_tiles),
        grid=(rows // tr, 3),
        in_specs=_halo_specs(tr, nb, DN_WIDTH, lambda p: p, rows // nb)
        + [pl.BlockSpec((CONV_K, DN_WIDTH), lambda i, p: (0, p))],
        out_specs=pl.BlockSpec((DN_HEADS, tr, DN_HEAD_DIM), lambda i, p: (p, i, 0)),
        out_shape=jax.ShapeDtypeStruct((3 * DN_HEADS, rows, DN_HEAD_DIM), F32),
        compiler_params=_cparams(("parallel", "parallel")),
        name="qkv_conv_norm",
    )(proj, proj, proj, conv_w)


def _neg_expm1(x):
    u = jnp.exp(x)
    lu = jnp.log(u)
    em1 = jnp.where(u == 1.0, x, (u - 1.0) * x / jnp.where(lu == 0.0, 1.0, lu))
    em1 = jnp.where(u == 0.0, -1.0, em1)
    return -em1


def _rg_prep_kernel(nb, n_ctx_tiles, x_ref, prev_ref, next_ref, ry_ref, cw_ref, cb_ref, wg_ref, bg_ref, lam_ref,
                    a0_ref, b0_ref, a1_ref, b1_ref, gy_ref):
    i = pl.program_id(0)
    use_prev, use_next = _halo_flags(i, n_ctx_tiles)
    xc = _conv_tile(x_ref[...], prev_ref[...], next_ref[...], cw_ref[...], nb, use_prev, use_next) + cb_ref[...]
    gates = jnp.dot(xc.astype(BF16), wg_ref[...], preferred_element_type=F32) + bg_ref[...]
    sp = jax.nn.softplus(-lam_ref[...])
    for d, (a_ref, b_ref) in enumerate(((a0_ref, b0_ref), (a1_ref, b1_ref))):
        r = _sigmoid(gates[:, (2 * d) * RG_WIDTH:(2 * d + 1) * RG_WIDTH])
        ig = _sigmoid(gates[:, (2 * d + 1) * RG_WIDTH:(2 * d + 2) * RG_WIDTH])
        log_a = -RG_C * r * sp[d:d + 1]
        a_ref[...] = jnp.exp(log_a)
        b_ref[...] = jnp.sqrt(_neg_expm1(2.0 * log_a)) * (ig * xc)
    gy_ref[...] = jax.nn.gelu(ry_ref[...])


def _rg_prep(proj, conv_w, conv_b, wg, bg, lam, nb, n_ctx_tiles):
    rows = proj.shape[0]
    tr = CHUNK * nb
    w = RG_WIDTH
    out = jax.ShapeDtypeStruct((rows, w), F32)
    full = lambda shape: pl.BlockSpec(shape, lambda i: (0,) * len(shape))
    return pl.pallas_call(
        functools.partial(_rg_prep_kernel, nb, n_ctx_tiles),
        grid=(rows // tr,),
        in_specs=_halo_specs(tr, nb, w, lambda: COL_RX, rows // nb)
        + [pl.BlockSpec((tr, w), lambda i: (i, COL_RY)),
           full((CONV_K, w)), full((1, w)), full((w, 4 * w)), full((1, 4 * w)), full((2, w))],
        out_specs=[pl.BlockSpec((tr, w), lambda i: (i, 0))] * 5,
        out_shape=[out] * 5,
        compiler_params=_cparams(("parallel",)),
        name="rglru_coeffs",
    )(proj, proj, proj, proj, conv_w, conv_b, wg, bg, lam)


def _bwd_chunk(i, n_ctx, n_tot):
    return jnp.where(i < n_ctx, n_ctx - 1 - i, n_tot + n_ctx - 1 - i)


def _deltanet_step(d, b, nb, q_ref, k_ref, v_ref, g_ref, o_ref, alog_ref, dtb_ref, s_ref):
    c = CHUNK
    dh = DN_HEAD_DIM
    row = lax.broadcasted_iota(jnp.int32, (c, c), 0)
    col = lax.broadcasted_iota(jnp.int32, (c, c), 1)
    incl = (row >= col) if d == 0 else (row <= col)
    strict = (row > col) if d == 0 else (row < col)
    rows_b = pl.ds(b, c, stride=nb)
    raw = g_ref[rows_b, :]
    beta_all = _sigmoid(raw)
    g_all = -jnp.exp(alog_ref[...]) * jax.nn.softplus(raw + dtb_ref[...])
    gc_all = jnp.dot(incl.astype(F32), g_all, precision=HIGHEST, preferred_element_type=F32)
    g_tot = jnp.sum(g_all, axis=0, keepdims=True)
    egc_all = jnp.exp(gc_all)
    kdec_all = jnp.exp(g_tot - gc_all)
    glast_all = jnp.exp(g_tot)
    gc_t = gc_all.T
    for hh in range(DN_HEADS):
        cb = d * DN_HEADS + hh
        cg = 2 * DN_HEADS + cb
        si = b * (2 * DN_HEADS) + cb
        q = q_ref[hh, rows_b, :]
        k = k_ref[hh, rows_b, :]
        v = v_ref[hh, rows_b, :]
        beta = beta_all[:, cb:cb + 1]
        egc = egc_all[:, cg:cg + 1]
        diff = gc_all[:, cg:cg + 1] - gc_t[cg:cg + 1, :]
        decay = jnp.exp(jnp.where(incl, diff, -1e30))
        kbeta = k * beta
        stack = jnp.concatenate([kbeta, q], axis=0).astype(BF16)
        prod = lax.dot_general(stack, k.astype(BF16), (((1,), (1,)), ((), ())), preferred_element_type=F32)
        n_mat = jnp.where(strict, -prod[:c] * decay, 0.0)
        intra = prod[c:] * decay
        x = jnp.concatenate([v * beta, kbeta * egc], axis=1)
        npow = n_mat
        for j in range(6):
            x = x + jnp.dot(npow, x, precision=HIGHEST, preferred_element_type=F32)
            if j < 5:
                npow = jnp.dot(npow, npow, precision=HIGHEST, preferred_element_type=F32)
        u = x[:, :dh]
        w = x[:, dh:]
        s = s_ref[si]
        wq = jnp.concatenate([w, q * egc], axis=0).astype(BF16)
        ws_qs = jnp.dot(wq, s.astype(BF16), preferred_element_type=F32)
        v_new = u - ws_qs[:c]
        o = ws_qs[c:] + jnp.dot(intra.astype(BF16), v_new.astype(BF16), preferred_element_type=F32)
        kd = (k * kdec_all[:, cg:cg + 1]).astype(BF16)
        s_ref[si] = s * glast_all[:, cg:cg + 1] + lax.dot_general(
            kd, v_new.astype(BF16), (((0,), (0,)), ((), ())), preferred_element_type=F32)
        o_ref[hh, rows_b, :] = o


def _deltanet_kernel(nb, qf_ref, kf_ref, vf_ref, gf_ref, qb_ref, kb_ref, vb_ref, gb_ref, alog_ref, dtb_ref,
                     of_ref, ob_ref, s_ref):
    @pl.when(pl.program_id(0) == 0)
    def _():
        s_ref[...] = jnp.zeros_like(s_ref)

    def body(b, carry):
        _deltanet_step(0, b, nb, qf_ref, kf_ref, vf_ref, gf_ref, of_ref, alog_ref, dtb_ref, s_ref)
        _deltanet_step(1, b, nb, qb_ref, kb_ref, vb_ref, gb_ref, ob_ref, alog_ref, dtb_ref, s_ref)
        return carry

    lax.fori_loop(0, nb, body, 0)


def _deltanet(qkv, proj, alog_vec, dtb_vec, nb, n_ctx_chunks):
    rows = qkv.shape[1]
    tr = CHUNK * nb
    n_tot = rows // tr
    fwd = lambda i: i
    bwd = lambda i: _bwd_chunk(i, n_ctx_chunks, n_tot)

    def specs(order):
        return [pl.BlockSpec((DN_HEADS, tr, DN_HEAD_DIM), lambda i, p=p: (p, order(i), 0)) for p in range(3)] + [
            pl.BlockSpec((tr, LANES), lambda i: (order(i), COL_GATE))]

    vec = pl.BlockSpec((1, LANES), lambda i: (0, 0))
    out = jax.ShapeDtypeStruct((DN_HEADS, rows, DN_HEAD_DIM), F32)
    return pl.pallas_call(
        functools.partial(_deltanet_kernel, nb),
        grid=(n_tot,),
        in_specs=specs(fwd) + specs(bwd) + [vec, vec],
        out_specs=[pl.BlockSpec((DN_HEADS, tr, DN_HEAD_DIM), lambda i: (0, fwd(i), 0)),
                   pl.BlockSpec((DN_HEADS, tr, DN_HEAD_DIM), lambda i: (0, bwd(i), 0))],
        out_shape=[out, out],
        scratch_shapes=[pltpu.VMEM((nb * 2 * DN_HEADS, DN_HEAD_DIM, DN_HEAD_DIM), F32)],
        compiler_params=_cparams(("arbitrary",)),
        name="deltanet",
    )(qkv, qkv, qkv, proj, qkv, qkv, qkv, proj, alog_vec, dtb_vec)


def _lru_scan_kernel(nb, a0_ref, b0_ref, a1_ref, b1_ref, hf_ref, hb_ref, st_ref):
    @pl.when(pl.program_id(0) == 0)
    def _():
        st_ref[...] = jnp.zeros_like(st_ref)

    def body(t, carry):
        h0, h1 = carry
        r0 = pl.ds(pl.multiple_of(t * nb, nb), nb)
        r1 = pl.ds(pl.multiple_of((CHUNK - 1 - t) * nb, nb), nb)
        h0 = a0_ref[r0, :] * h0 + b0_ref[r0, :]
        h1 = a1_ref[r1, :] * h1 + b1_ref[r1, :]
        hf_ref[r0, :] = h0
        hb_ref[r1, :] = h1
        return h0, h1

    h0, h1 = lax.fori_loop(0, CHUNK, body, (st_ref[0], st_ref[1]), unroll=8)
    st_ref[0] = h0
    st_ref[1] = h1


def _lru_scan(a0, b0, a1, b1, nb, n_ctx_chunks):
    rows, w = a0.shape
    tr = CHUNK * nb
    n_tot = rows // tr
    fwd = pl.BlockSpec((tr, w), lambda i: (i, 0))
    bwd = pl.BlockSpec((tr, w), lambda i: (_bwd_chunk(i, n_ctx_chunks, n_tot), 0))
    out = jax.ShapeDtypeStruct((rows, w), F32)
    return pl.pallas_call(
        functools.partial(_lru_scan_kernel, nb),
        grid=(n_tot,),
        in_specs=[fwd, fwd, bwd, bwd],
        out_specs=[fwd, bwd],
        out_shape=[out, out],
        scratch_shapes=[pltpu.VMEM((2, nb, w), F32)],
        compiler_params=_cparams(("arbitrary",)),
        name="rglru_scan",
    )(a0, b0, a1, b1)


def _mixer_out_kernel(nb, of_ref, ob_ref, z_ref, hf_ref, hb_ref, gy_ref, onw_ref, w_ref, h_ref, gt_ref, g_ref, b_ref,
                      o_ref):
    z = z_ref[...]
    parts = []
    for hh in range(DN_HEADS):
        sl = slice(hh * DN_HEAD_DIM, (hh + 1) * DN_HEAD_DIM)
        oh = of_ref[hh] + ob_ref[hh]
        inv = lax.rsqrt(jnp.mean(oh * oh, axis=-1, keepdims=True) + NORM_EPS)
        parts.append(oh * inv * onw_ref[...] * _silu(z[:, sl]))
    parts.append((hf_ref[...] + hb_ref[...]) * gy_ref[...])
    y = jnp.concatenate(parts, axis=1).astype(BF16)
    u = jnp.dot(y, w_ref[...], preferred_element_type=F32)
    r = DEEP_ALPHA * h_ref[...] + _per_batch(u, gt_ref[...], nb) * u
    o_ref[...] = _layer_norm(r) * g_ref[...] + b_ref[...]


def _mixer_out(o_f, o_b, proj, hf, hb, gy, onorm_w, w_out, h, mod_rows, ln_g, ln_b, nb, ctx_rows):
    rows, d = h.shape
    tm = ROW_TILE
    lat = lambda i: jnp.where(i * tm >= ctx_rows, 1, 0)
    half = pl.BlockSpec((tm, DN_WIDTH), lambda i: (i, 0))
    planes = pl.BlockSpec((DN_HEADS, tm, DN_HEAD_DIM), lambda i: (0, i, 0))
    vec = lambda n: pl.BlockSpec((1, n), lambda i: (0, 0))
    return pl.pallas_call(
        functools.partial(_mixer_out_kernel, nb),
        grid=(rows // tm,),
        in_specs=[planes, planes, pl.BlockSpec((tm, DN_WIDTH), lambda i: (i, COL_Z)), half, half, half,
                  vec(DN_HEAD_DIM), pl.BlockSpec((d, d), lambda i: (0, 0)),
                  pl.BlockSpec((tm, d), lambda i: (i, 0)),
                  pl.BlockSpec((None, nb, d), lambda i: (lat(i), 0, 2)),
                  vec(d), vec(d)],
        out_specs=pl.BlockSpec((tm, d), lambda i: (i, 0)),
        out_shape=jax.ShapeDtypeStruct((rows, d), F32),
        compiler_params=_cparams(("parallel",)),
        name="mixer_out",
    )(o_f, o_b, proj, hf, hb, gy, onorm_w, w_out, h, mod_rows, ln_g, ln_b)


INFO_E0, INFO_E1, INFO_W0, INFO_W1, INFO_R0, INFO_R1 = range(6)


def _first_index(mask, lane):
    return jnp.min(jnp.where(mask, lane, LANES), axis=-1, keepdims=True)


def _router_kernel(nb, x_ref, sh_ref, sc_ref, wr_ref, br_ref, hm_ref, info_ref, cnt_ref, run_ref):
    @pl.when(pl.program_id(0) == 0)
    def _():
        run_ref[...] = jnp.zeros_like(run_ref)

    xn = _layer_norm(x_ref[...])
    hm = xn * (1.0 + _per_batch(xn, sc_ref[...], nb)) + _per_batch(xn, sh_ref[...], nb)
    hm_ref[...] = hm
    tm = hm.shape[0]
    logits = jnp.dot(hm, wr_ref[...], precision=HIGHEST, preferred_element_type=F32) + br_ref[...]
    lane = lax.broadcasted_iota(jnp.int32, (tm, LANES), 1)
    neg = -jnp.inf
    is_g = lane < N_GROUPS
    gmax = jnp.max(jnp.where(is_g, logits, neg), axis=-1, keepdims=True)
    g_sel = _first_index(jnp.logical_and(is_g, logits == gmax), lane)
    p_group = 1.0 / jnp.sum(jnp.where(is_g, jnp.exp(logits - gmax), 0.0), axis=-1, keepdims=True)
    lo = N_GROUPS + EXPERTS_PER_GROUP * g_sel
    in_grp = jnp.logical_and(lane >= lo, lane < lo + EXPERTS_PER_GROUP)
    m1 = jnp.max(jnp.where(in_grp, logits, neg), axis=-1, keepdims=True)
    i1 = _first_index(jnp.logical_and(in_grp, logits == m1), lane)
    rest = jnp.logical_and(in_grp, lane != i1)
    m2 = jnp.max(jnp.where(rest, logits, neg), axis=-1, keepdims=True)
    i2 = _first_index(jnp.logical_and(rest, logits == m2), lane)
    e2 = jnp.exp(m2 - m1)
    w1 = p_group / (1.0 + e2)
    w2 = p_group * e2 / (1.0 + e2)
    e_a = i1 - N_GROUPS
    e_b = i2 - N_GROUPS
    oh_a = lane == e_a
    oh_b = lane == e_b
    oh = jnp.logical_or(oh_a, oh_b).astype(F32)
    r_i = lax.broadcasted_iota(jnp.int32, (tm, tm), 0)
    c_i = lax.broadcasted_iota(jnp.int32, (tm, tm), 1)
    before = jnp.dot((r_i > c_i).astype(BF16), oh.astype(BF16), preferred_element_type=F32) + run_ref[...]
    rank_a = jnp.sum(jnp.where(oh_a, before, 0.0), axis=-1, keepdims=True)
    rank_b = jnp.sum(jnp.where(oh_b, before, 0.0), axis=-1, keepdims=True)
    run_ref[...] = run_ref[...] + jnp.sum(oh, axis=0, keepdims=True)
    cnt_ref[...] = run_ref[...]
    info = jnp.zeros((tm, LANES), F32)
    for idx, val in ((INFO_E0, e_a.astype(F32)), (INFO_E1, e_b.astype(F32)), (INFO_W0, w1), (INFO_W1, w2),
                     (INFO_R0, rank_a), (INFO_R1, rank_b)):
        info = jnp.where(lane == idx, val, info)
    info_ref[...] = info


def _router(x1, mod_rows, wr, br, nb, ctx_rows):
    rows, d = x1.shape
    tm = ROW_TILE
    lat = lambda i: jnp.where(i * tm >= ctx_rows, 1, 0)
    return pl.pallas_call(
        functools.partial(_router_kernel, nb),
        grid=(rows // tm,),
        in_specs=[pl.BlockSpec((tm, d), lambda i: (i, 0)),
                  pl.BlockSpec((None, nb, d), lambda i: (lat(i), 0, 3)),
                  pl.BlockSpec((None, nb, d), lambda i: (lat(i), 0, 4)),
                  pl.BlockSpec((d, LANES), lambda i: (0, 0)),
                  pl.BlockSpec((1, LANES), lambda i: (0, 0))],
        out_specs=[pl.BlockSpec((tm, d), lambda i: (i, 0)),
                   pl.BlockSpec((tm, LANES), lambda i: (i, 0)),
                   pl.BlockSpec((1, LANES), lambda i: (0, 0))],
        out_shape=[jax.ShapeDtypeStruct((rows, d), F32),
                   jax.ShapeDtypeStruct((rows, LANES), F32),
                   jax.ShapeDtypeStruct((1, LANES), F32)],
        scratch_shapes=[pltpu.VMEM((1, LANES), F32)],
        compiler_params=_cparams(("arbitrary",)),
        name="moe_router",
    )(x1, mod_rows, mod_rows, wr, br)


def _expert_ffn_kernel(te_ref, nv_ref, pair_ref, hm_hbm, wg_ref, wu_ref, wd_ref, out_hbm, xbuf, ybuf, sem_in, sem_out):
    j = pl.program_id(0)
    nv = nv_ref[j]
    tmx = xbuf.shape[0]
    n_tok = hm_hbm.shape[0]

    def gather(r):
        tok = jnp.minimum(pair_ref[0, r] // 2, n_tok - 1)
        return pltpu.make_async_copy(hm_hbm.at[pl.ds(tok, 1)], xbuf.at[pl.ds(r, 1)], sem_in)

    def scatter(r):
        return pltpu.make_async_copy(ybuf.at[pl.ds(r, 1)], out_hbm.at[pl.ds(pair_ref[0, r], 1)], sem_out)

    @pl.when(nv > 0)
    def _():
        lax.fori_loop(0, tmx, lambda r, c: (gather(r).start(), c)[1], 0)
        lax.fori_loop(0, tmx, lambda r, c: (gather(r).wait(), c)[1], 0)
        x = xbuf[...].astype(BF16)
        a = jnp.dot(x, wg_ref[...], preferred_element_type=F32)
        b = jnp.dot(x, wu_ref[...], preferred_element_type=F32)
        hid = (_silu(a) * b).astype(BF16)
        ybuf[...] = jnp.dot(hid, wd_ref[...], preferred_element_type=F32)
        lax.fori_loop(0, nv, lambda r, c: (scatter(r).start(), c)[1], 0)
        lax.fori_loop(0, nv, lambda r, c: (scatter(r).wait(), c)[1], 0)


def _expert_ffn(hm, tile_expert, tile_valid, pair_of_row, w_gate, w_up, w_down):
    rows, d = hm.shape
    n_tiles, tmx = pair_of_row.shape
    pair_of_row = pair_of_row.reshape(n_tiles, 1, tmx)
    de = w_gate.shape[-1]
    grid_spec = pltpu.PrefetchScalarGridSpec(
        num_scalar_prefetch=2,
        grid=(n_tiles,),
        in_specs=[
            pl.BlockSpec((None, 1, tmx), lambda j, te, nv: (j, 0, 0), memory_space=pltpu.SMEM),
            pl.BlockSpec(memory_space=pl.ANY),
            pl.BlockSpec((None, d, de), lambda j, te, nv: (te[j], 0, 0)),
            pl.BlockSpec((None, d, de), lambda j, te, nv: (te[j], 0, 0)),
            pl.BlockSpec((None, de, d), lambda j, te, nv: (te[j], 0, 0)),
        ],
        out_specs=pl.BlockSpec(memory_space=pl.ANY),
        scratch_shapes=[pltpu.VMEM((tmx, d), F32), pltpu.VMEM((tmx, d), F32),
                        pltpu.SemaphoreType.DMA(()), pltpu.SemaphoreType.DMA(())],
    )
    return pl.pallas_call(
        _expert_ffn_kernel,
        grid_spec=grid_spec,
        out_shape=jax.ShapeDtypeStruct((2 * rows, d), F32),
        compiler_params=_cparams(("arbitrary",)),
        name="expert_ffn",
    )(tile_expert, tile_valid, pair_of_row, hm, w_gate, w_up, w_down)


def _moe_out_kernel(nb, y_ref, info_ref, x_ref, gt_ref, g_ref, b_ref, o_ref):
    d = x_ref.shape[1]
    info = info_ref[...]
    f = y_ref[:, :d] * info[:, INFO_W0:INFO_W0 + 1] + y_ref[:, d:] * info[:, INFO_W1:INFO_W1 + 1]
    r = DEEP_ALPHA * x_ref[...] + _per_batch(f, gt_ref[...], nb) * f
    o_ref[...] = _layer_norm(r) * g_ref[...] + b_ref[...]


def _moe_out(y2, info, x1, mod_rows, ln_g, ln_b, nb, ctx_rows):
    rows, d = x1.shape
    tm = ROW_TILE
    lat = lambda i: jnp.where(i * tm >= ctx_rows, 1, 0)
    vec = pl.BlockSpec((1, d), lambda i: (0, 0))
    return pl.pallas_call(
        functools.partial(_moe_out_kernel, nb),
        grid=(rows // tm,),
        in_specs=[pl.BlockSpec((tm, 2 * d), lambda i: (i, 0)),
                  pl.BlockSpec((tm, LANES), lambda i: (i, 0)),
                  pl.BlockSpec((tm, d), lambda i: (i, 0)),
                  pl.BlockSpec((None, nb, d), lambda i: (lat(i), 0, 5)),
                  vec, vec],
        out_specs=pl.BlockSpec((tm, d), lambda i: (i, 0)),
        out_shape=jax.ShapeDtypeStruct((rows, d), F32),
        compiler_params=_cparams(("parallel",)),
        name="moe_out",
    )(y2.reshape(rows, 2 * d), info, x1, mod_rows, ln_g, ln_b)


def _routing_tables(info, counts, n_tiles, tmx):
    rows = info.shape[0]
    cnt = counts[0, :N_EXPERTS].astype(jnp.int32)
    padded = ((cnt + tmx - 1) // tmx) * tmx
    ends = jnp.cumsum(padded)
    starts = ends - padded
    eid = info[:, INFO_E0:INFO_E1 + 1].astype(jnp.int32)
    rank = info[:, INFO_R0:INFO_R1 + 1].astype(jnp.int32)
    dest = (starts[eid] + rank).reshape(-1)
    dummy = 2 * rows
    pair_of_row = jnp.full((n_tiles * tmx,), dummy, jnp.int32).at[dest].set(jnp.arange(2 * rows, dtype=jnp.int32))
    tile_start = jnp.arange(n_tiles, dtype=jnp.int32) * tmx
    n_before = jnp.sum((ends[None, :] <= tile_start[:, None]).astype(jnp.int32), axis=1)
    tile_expert = jnp.minimum(n_before, N_EXPERTS - 1)
    tile_valid = jnp.clip(starts[tile_expert] + cnt[tile_expert] - tile_start, 0, tmx).astype(jnp.int32)
    return tile_expert, tile_valid, pair_of_row.reshape(n_tiles, tmx)


def _relayout_w_in(w_in):
    qkvz = w_in[..., :4 * DN_WIDTH]
    gates = w_in[..., 4 * DN_WIDTH:4 * DN_WIDTH + 4 * DN_HEADS]
    rxy = w_in[..., 4 * DN_WIDTH + 4 * DN_HEADS:]
    pad = jnp.zeros(w_in.shape[:-1] + (LANES - 4 * DN_HEADS,), w_in.dtype)
    return jnp.concatenate([qkvz, rxy, gates, pad], axis=-1).astype(BF16)


def _block_diag(w):
    n, c, _ = w.shape
    eye = jnp.eye(n, dtype=w.dtype)
    return (eye[:, None, :, None] * w[:, :, None, :]).reshape(n * c, n * c)


def _lane_vec(vals, offset):
    return jnp.zeros((1, LANES), F32).at[0, offset:offset + vals.size].set(vals.reshape(-1))


def kernel(x, c, ctx, c_ctx, w_ada, b_ada, w_in, conv_qkv_w, dn_a_log, dn_dt_bias, dn_onorm_w, rg_conv_w, rg_conv_b, rg_wa, rg_ba, rg_wi, rg_bi, rg_lambda, w_out, ln_g, ln_b, router_wg, router_bg, router_we, router_be, w_e_gate, w_e_up, w_e_down):
    nb, t_lat, d = x.shape
    t_ctx = ctx.shape[1]
    assert d == D_MODEL and nb % 8 == 0 and t_lat % GRID_W == 0 and t_ctx % CHUNK == 0
    assert (t_ctx * nb) % ROW_TILE == 0 and (t_lat * nb) % ROW_TILE == 0
    tt = t_ctx + t_lat
    rows = tt * nb
    ctx_rows = t_ctx * nb
    n_ctx_chunks = t_ctx // CHUNK

    h = jnp.concatenate([jnp.swapaxes(ctx, 0, 1), jnp.swapaxes(x, 0, 1)], axis=0).reshape(rows, d)

    n_cc = ((nb + 1 + 7) // 8) * 8
    cc = jnp.zeros((n_cc, d), F32).at[:nb].set(c).at[nb].set(c_ctx)
    mod = _modulation(cc, w_ada, b_ada)
    mod_ctx = jnp.broadcast_to(mod[:, nb:nb + 1], (DEPTH, nb, 6 * d))
    mod_rows = jnp.stack([mod_ctx, mod[:, :nb]], axis=1)

    w_in2 = _relayout_w_in(w_in)
    n_tiles = (2 * rows) // MOE_TILE + N_EXPERTS
    for l in range(DEPTH):
        mr = mod_rows[l]
        proj = _in_projection(h, mr, w_in2[l], nb, ctx_rows)
        qkv = _qkv_prep(proj, conv_qkv_w[l], nb, n_ctx_chunks)
        wg = jnp.concatenate([_block_diag(rg_wa[l, 0]), _block_diag(rg_wi[l, 0]),
                              _block_diag(rg_wa[l, 1]), _block_diag(rg_wi[l, 1])], axis=1).astype(BF16)
        bg = jnp.concatenate([rg_ba[l, 0], rg_bi[l, 0], rg_ba[l, 1], rg_bi[l, 1]])[None]
        a0, b0, a1, b1, gy = _rg_prep(proj, rg_conv_w[l], rg_conv_b[l][None], wg, bg, rg_lambda[l], nb, n_ctx_chunks)
        alog_vec = _lane_vec(dn_a_log[l], 2 * DN_HEADS)
        dtb_vec = _lane_vec(dn_dt_bias[l], 2 * DN_HEADS)
        o_f, o_b = _deltanet(qkv, proj, alog_vec, dtb_vec, nb, n_ctx_chunks)
        hf, hb = _lru_scan(a0, b0, a1, b1, nb, n_ctx_chunks)
        x1 = _mixer_out(o_f, o_b, proj, hf, hb, gy, dn_onorm_w[l][None], w_out[l].astype(BF16), h, mr,
                        ln_g[l, 0][None], ln_b[l, 0][None], nb, ctx_rows)
        wr = jnp.zeros((d, LANES), F32).at[:, :N_GROUPS].set(router_wg[l]).at[
            :, N_GROUPS:N_GROUPS + N_EXPERTS].set(router_we[l])
        br = _lane_vec(jnp.concatenate([router_bg[l], router_be[l]]), 0)
        hm, info, counts = _router(x1, mr, wr, br, nb, ctx_rows)
        tile_expert, tile_valid, pair_of_row = _routing_tables(info, counts, n_tiles, MOE_TILE)
        y2 = _expert_ffn(hm, tile_expert, tile_valid, pair_of_row,
                         w_e_gate[l].astype(BF16), w_e_up[l].astype(BF16), w_e_down[l].astype(BF16))
        h = _moe_out(y2, info, x1, mr, ln_g[l, 1][None], ln_b[l, 1][None], nb, ctx_rows)

    out = h.reshape(tt, nb, d)[t_ctx:]
    return jnp.swapaxes(out, 0, 1)
```

```python
import functools
import math

import jax
import jax.numpy as jnp
from jax import lax
from jax.experimental import pallas as pl
from jax.experimental.pallas import tpu as pltpu

F32 = jnp.float32
BF16 = jnp.bfloat16
HIGHEST = lax.Precision.HIGHEST

D_MODEL = 1024
DEPTH = 2
GRID_W = 64
DN_HEAD_DIM = 128
DN_WIDTH = 512
DN_HEADS = 4
RG_WIDTH = 512
RG_BLOCKS = 8
RG_BLOCK_DIM = 64
RG_C = 8.0
CONV_K = 4
N_GROUPS = 4
EXPERTS_PER_GROUP = 8
N_EXPERTS = 32
D_EXPERT = 256
DEEP_ALPHA = (2 * DEPTH) ** 0.25
LN_EPS = 1e-5
NORM_EPS = 1e-6

LANES = 128
CHUNK = 64
SOLVE_BLOCK = 16
D_PROJ = 3 * DN_WIDTH + DN_WIDTH + 2 * RG_WIDTH + LANES
COL_Z = 3
COL_RX = 4
COL_RY = 5
COL_GATE = (3 * DN_WIDTH + DN_WIDTH + 2 * RG_WIDTH) // LANES
ROW_TILE = 512
MOE_TILE = 256
VMEM_LIMIT = 56 * 1024 * 1024


def _cparams(sem):
    return pltpu.CompilerParams(dimension_semantics=sem, vmem_limit_bytes=VMEM_LIMIT)


def _layer_norm(x):
    mu = jnp.mean(x, axis=-1, keepdims=True)
    xc = x - mu
    var = jnp.mean(xc * xc, axis=-1, keepdims=True)
    return xc * lax.rsqrt(var + LN_EPS)


def _per_batch(x, v, nb):
    rows, c = x.shape
    return jnp.broadcast_to(v[None], (rows // nb, nb, c)).reshape(rows, c)


def _sigmoid(x):
    return 1.0 / (1.0 + jnp.exp(-x))


def _silu(x):
    return x * _sigmoid(x)


def _mod_kernel(cc_ref, w_ref, b_ref, o_ref):
    s = _silu(cc_ref[...])
    o_ref[...] = jnp.dot(s, w_ref[...], precision=HIGHEST, preferred_element_type=F32) + b_ref[...]


def _modulation(cc, w_ada, b_ada):
    depth, d, n6 = w_ada.shape
    rows = cc.shape[0]
    tn = 1536
    return pl.pallas_call(
        _mod_kernel,
        grid=(depth, n6 // tn),
        in_specs=[
            pl.BlockSpec((rows, d), lambda l, j: (0, 0)),
            pl.BlockSpec((None, d, tn), lambda l, j: (l, 0, j)),
            pl.BlockSpec((None, 1, tn), lambda l, j: (l, 0, j)),
        ],
        out_specs=pl.BlockSpec((None, rows, tn), lambda l, j: (l, 0, j)),
        out_shape=jax.ShapeDtypeStruct((depth, rows, n6), F32),
        compiler_params=_cparams(("parallel", "parallel")),
        name="adaln_mod",
    )(cc, w_ada, b_ada.reshape(depth, 1, n6))


def _inproj_kernel(nb, h_ref, sh_ref, sc_ref, w_ref, o_ref):
    xn = _layer_norm(h_ref[...])
    hm = xn * (1.0 + _per_batch(xn, sc_ref[...], nb)) + _per_batch(xn, sh_ref[...], nb)
    o_ref[...] = jnp.dot(hm.astype(BF16), w_ref[...], preferred_element_type=F32)


def _in_projection(h, mod_rows, w_in2, nb, ctx_rows):
    rows, d = h.shape
    tm = ROW_TILE
    lat = lambda i: jnp.where(i * tm >= ctx_rows, 1, 0)
    return pl.pallas_call(
        functools.partial(_inproj_kernel, nb),
        grid=(rows // tm,),
        in_specs=[
            pl.BlockSpec((tm, d), lambda i: (i, 0)),
            pl.BlockSpec((None, nb, d), lambda i: (lat(i), 0, 0)),
            pl.BlockSpec((None, nb, d), lambda i: (lat(i), 0, 1)),
            pl.BlockSpec((d, D_PROJ), lambda i: (0, 0)),
        ],
        out_specs=pl.BlockSpec((tm, D_PROJ), lambda i: (i, 0)),
        out_shape=jax.ShapeDtypeStruct((rows, D_PROJ), F32),
        compiler_params=_cparams(("parallel",)),
        name="ln_mod_inproj",
    )(h, mod_rows, mod_rows, w_in2)


def _conv_tile(x, prev, nxt, w, nb, use_prev, use_next):
    prev = jnp.where(use_prev, prev, 0.0)
    nxt = jnp.where(use_next, nxt, 0.0)
    xm2 = jnp.concatenate([prev, x[: -2 * nb]], axis=0)
    xm1 = jnp.concatenate([prev[nb:], x[:-nb]], axis=0)
    xp1 = jnp.concatenate([x[nb:], nxt], axis=0)
    return xm2 * w[0:1] + xm1 * w[1:2] + x * w[2:3] + xp1 * w[3:4]


def _halo_flags(i, n_ctx_tiles):
    use_prev = jnp.logical_and(i > 0, i < n_ctx_tiles)
    use_next = i < n_ctx_tiles - 1
    return use_prev, use_next


def _qkv_prep_kernel(nb, n_ctx_tiles, x_ref, prev_ref, next_ref, w_ref, o_ref):
    i = pl.program_id(0)
    part = pl.program_id(1)
    use_prev, use_next = _halo_flags(i, n_ctx_tiles)
    y = _silu(_conv_tile(x_ref[...], prev_ref[...], next_ref[...], w_ref[...], nb, use_prev, use_next))
    qscale = jnp.where(part == 0, DN_HEAD_DIM ** -0.5, 1.0).astype(F32)
    for hh in range(DN_HEADS):
        sl = slice(hh * DN_HEAD_DIM, (hh + 1) * DN_HEAD_DIM)
        yh = y[:, sl]
        inv = lax.rsqrt(jnp.sum(yh * yh, axis=-1, keepdims=True) + NORM_EPS) * qscale
        o_ref[hh] = yh * jnp.where(part < 2, inv, 1.0)


def _halo_specs(tile_rows, nb, cols, col_of, n_row_blocks16):
    prev_b = tile_rows // (2 * nb)
    next_b = tile_rows // nb
    return [
        pl.BlockSpec((tile_rows, cols), lambda i, *a: (i, col_of(*a))),
        pl.BlockSpec((2 * nb, cols), lambda i, *a: (jnp.maximum(i * prev_b - 1, 0), col_of(*a))),
        pl.BlockSpec((nb, cols), lambda i, *a: (jnp.minimum((i + 1) * next_b, n_row_blocks16 - 1), col_of(*a))),
    ]


def _qkv_prep(proj, conv_w, nb, n_ctx_tiles):
    rows = proj.shape[0]
    tr = CHUNK * nb
    return pl.pallas_call(
        functools.partial(_qkv_prep_kernel, nb, n_ctx_tiles),
        grid=(rows // tr, 3),
        in_specs=_halo_specs(tr, nb, DN_WIDTH, lambda p: p, rows // nb)
        + [pl.BlockSpec((CONV_K, DN_WIDTH), lambda i, p: (0, p))],
        out_specs=pl.BlockSpec((DN_HEADS, tr, DN_HEAD_DIM), lambda i, p: (p, i, 0)),
        out_shape=jax.ShapeDtypeStruct((3 * DN_HEADS, rows, DN_HEAD_DIM), F32),
        compiler_params=_cparams(("parallel", "parallel")),
        name="qkv_conv_norm",
    )(proj, proj, proj, conv_w)


def _neg_expm1(x):
    u = jnp.exp(x)
    lu = jnp.log(u)
    em1 = jnp.where(u == 1.0, x, (u - 1.0) * x / jnp.where(lu == 0.0, 1.0, lu))
    em1 = jnp.where(u == 0.0, -1.0, em1)
    return -em1


def _rg_prep_kernel(nb, n_ctx_tiles, x_ref, prev_ref, next_ref, ry_ref, cw_ref, cb_ref, wg_ref, bg_ref, lam_ref,
                    a0_ref, b0_ref, a1_ref, b1_ref, gy_ref):
    i = pl.program_id(0)
    use_prev, use_next = _halo_flags(i, n_ctx_tiles)
    xc = _conv_tile(x_ref[...], prev_ref[...], next_ref[...], cw_ref[...], nb, use_prev, use_next) + cb_ref[...]
    gates = jnp.dot(xc.astype(BF16), wg_ref[...], preferred_element_type=F32) + bg_ref[...]
    sp = jax.nn.softplus(-lam_ref[...])
    for d, (a_ref, b_ref) in enumerate(((a0_ref, b0_ref), (a1_ref, b1_ref))):
        r = _sigmoid(gates[:, (2 * d) * RG_WIDTH:(2 * d + 1) * RG_WIDTH])
        ig = _sigmoid(gates[:, (2 * d + 1) * RG_WIDTH:(2 * d + 2) * RG_WIDTH])
        log_a = -RG_C * r * sp[d:d + 1]
        a_ref[...] = jnp.exp(log_a)
        b_ref[...] = jnp.sqrt(_neg_expm1(2.0 * log_a)) * (ig * xc)
    gy_ref[...] = jax.nn.gelu(ry_ref[...])


def _rg_prep(proj, conv_w, conv_b, wg, bg, lam, nb, n_ctx_tiles):
    rows = proj.shape[0]
    tr = CHUNK * nb
    w = RG_WIDTH
    out = jax.ShapeDtypeStruct((rows, w), F32)
    full = lambda shape: pl.BlockSpec(shape, lambda i: (0,) * len(shape))
    return pl.pallas_call(
        functools.partial(_rg_prep_kernel, nb, n_ctx_tiles),
        grid=(rows // tr,),
        in_specs=_halo_specs(tr, nb, w, lambda: COL_RX, rows // nb)
        + [pl.BlockSpec((tr, w), lambda i: (i, COL_RY)),
           full((CONV_K, w)), full((1, w)), full((w, 4 * w)), full((1, 4 * w)), full((2, w))],
        out_specs=[pl.BlockSpec((tr, w), lambda i: (i, 0))] * 5,
        out_shape=[out] * 5,
        compiler_params=_cparams(("parallel",)),
        name="rglru_coeffs",
    )(proj, proj, proj, proj, conv_w, conv_b, wg, bg, lam)


def _bwd_chunk(i, n_ctx, n_tot):
    return jnp.where(i < n_ctx, n_ctx - 1 - i, n_tot + n_ctx - 1 - i)


def _bdot(a, b):
    return jnp.dot(a.astype(BF16), b.astype(BF16), preferred_element_type=F32)


def _deltanet_gates(d, raw, alog, dtb):
    c = CHUNK
    row = lax.broadcasted_iota(jnp.int32, (c, c), 0)
    col = lax.broadcasted_iota(jnp.int32, (c, c), 1)
    incl = (row >= col) if d == 0 else (row <= col)
    g_all = -jnp.exp(alog) * jax.nn.softplus(raw + dtb)
    gc_all = jnp.dot(incl.astype(F32), g_all, precision=HIGHEST, preferred_element_type=F32)
    g_tot = jnp.sum(g_all, axis=0, keepdims=True)
    return dict(incl=incl, strict=(row > col) if d == 0 else (row < col),
                diag_blk=(row // SOLVE_BLOCK) == (col // SOLVE_BLOCK),
                beta=_sigmoid(raw), gc=gc_all, gc_t=gc_all.T, egc=jnp.exp(gc_all),
                kdec=jnp.exp(g_tot - gc_all), glast=jnp.exp(g_tot))


def _deltanet_heads(gts, cbs, qs, ks, vs, ss):
    c = CHUNK
    dh = DN_HEAD_DIM
    n = len(cbs)
    rng = range(n)
    cgs = [2 * DN_HEADS + cb for cb in cbs]
    beta = [gts[i]["beta"][:, cbs[i]:cbs[i] + 1] for i in rng]
    egc = [gts[i]["egc"][:, cgs[i]:cgs[i] + 1] for i in rng]
    decay = [jnp.exp(jnp.where(gts[i]["incl"], gts[i]["gc"][:, cgs[i]:cgs[i] + 1] - gts[i]["gc_t"][cgs[i]:cgs[i] + 1, :],
                               -1e30)) for i in rng]
    kbeta = [ks[i] * beta[i] for i in rng]
    prod = [lax.dot_general(jnp.concatenate([kbeta[i], qs[i]], axis=0).astype(BF16), ks[i].astype(BF16),
                            (((1,), (1,)), ((), ())), preferred_element_type=F32) for i in rng]
    a_mat = [jnp.where(gts[i]["strict"], prod[i][:c] * decay[i], 0.0) for i in rng]
    intra = [prod[i][c:] * decay[i] for i in rng]
    p = [jnp.where(gts[i]["diag_blk"], -a_mat[i], 0.0) for i in rng]
    y = [jnp.concatenate([vs[i] * beta[i], kbeta[i] * egc[i], jnp.where(gts[i]["diag_blk"], 0.0, a_mat[i])], axis=1)
         for i in rng]
    for j in range(4):
        y = [y[i] + _bdot(p[i], y[i]) for i in rng]
        if j < 3:
            p = [_bdot(p[i], p[i]) for i in rng]
    e = [y[i][:, 2 * dh:] for i in rng]
    y = [y[i][:, :2 * dh] for i in rng]
    e2 = [_bdot(e[i], e[i]) for i in rng]
    y = [y[i] + _bdot(e2[i], y[i]) for i in rng]
    y = [y[i] - _bdot(e[i], y[i]) for i in rng]
    ws_qs = [_bdot(jnp.concatenate([y[i][:, dh:], qs[i] * egc[i]], axis=0), ss[i]) for i in rng]
    v_new = [y[i][:, :dh] - ws_qs[i][:c] for i in rng]
    o = [ws_qs[i][c:] + _bdot(intra[i], v_new[i]) for i in rng]
    s_new = [ss[i] * gts[i]["glast"][:, cgs[i]:cgs[i] + 1] + lax.dot_general(
        (ks[i] * gts[i]["kdec"][:, cgs[i]:cgs[i] + 1]).astype(BF16), v_new[i].astype(BF16),
        (((0,), (0,)), ((), ())), preferred_element_type=F32) for i in rng]
    return o, s_new


def _deltanet_kernel(nb, qf_ref, kf_ref, vf_ref, gf_ref, qb_ref, kb_ref, vb_ref, gb_ref, alog_ref, dtb_ref,
                     of_ref, ob_ref, s_ref):
    @pl.when(pl.program_id(0) == 0)
    def _():
        s_ref[...] = jnp.zeros_like(s_ref)

    dirs = ((qf_ref, kf_ref, vf_ref, gf_ref, of_ref), (qb_ref, kb_ref, vb_ref, gb_ref, ob_ref))

    def body(b, carry):
        rows_b = pl.ds(b, CHUNK, stride=nb)
        idx = [(d, hh) for d in range(2) for hh in range(DN_HEADS)]
        cbs = [d * DN_HEADS + hh for d, hh in idx]
        gates = [_deltanet_gates(d, dirs[d][3][rows_b, :], alog_ref[...], dtb_ref[...]) for d in range(2)]
        qs = [dirs[d][0][hh, rows_b, :] for d, hh in idx]
        ks = [dirs[d][1][hh, rows_b, :] for d, hh in idx]
        vs = [dirs[d][2][hh, rows_b, :] for d, hh in idx]
        ss = [s_ref[b * (2 * DN_HEADS) + cb] for cb in cbs]
        o, s_new = _deltanet_heads([gates[d] for d, _ in idx], cbs, qs, ks, vs, ss)
        for i, (d, hh) in enumerate(idx):
            dirs[d][4][hh, rows_b, :] = o[i]
            s_ref[b * (2 * DN_HEADS) + cbs[i]] = s_new[i]
        return carry

    lax.fori_loop(0, nb, body, 0)


def _deltanet(qkv, proj, alog_vec, dtb_vec, nb, n_ctx_chunks):
    rows = qkv.shape[1]
    tr = CHUNK * nb
    n_tot = rows // tr
    fwd = lambda i: i
    bwd = lambda i: _bwd_chunk(i, n_ctx_chunks, n_tot)

    def specs(order):
        return [pl.BlockSpec((DN_HEADS, tr, DN_HEAD_DIM), lambda i, p=p: (p, order(i), 0)) for p in range(3)] + [
            pl.BlockSpec((tr, LANES), lambda i: (order(i), COL_GATE))]

    vec = pl.BlockSpec((1, LANES), lambda i: (0, 0))
    out = jax.ShapeDtypeStruct((DN_HEADS, rows, DN_HEAD_DIM), F32)
    return pl.pallas_call(
        functools.partial(_deltanet_kernel, nb),
        grid=(n_tot,),
        in_specs=specs(fwd) + specs(bwd) + [vec, vec],
        out_specs=[pl.BlockSpec((DN_HEADS, tr, DN_HEAD_DIM), lambda i: (0, fwd(i), 0)),
                   pl.BlockSpec((DN_HEADS, tr, DN_HEAD_DIM), lambda i: (0, bwd(i), 0))],
        out_shape=[out, out],
        scratch_shapes=[pltpu.VMEM((nb * 2 * DN_HEADS, DN_HEAD_DIM, DN_HEAD_DIM), F32)],
        compiler_params=_cparams(("arbitrary",)),
        name="deltanet",
    )(qkv, qkv, qkv, proj, qkv, qkv, qkv, proj, alog_vec, dtb_vec)


def _lru_scan_kernel(nb, a0_ref, b0_ref, a1_ref, b1_ref, hf_ref, hb_ref, st_ref):
    @pl.when(pl.program_id(0) == 0)
    def _():
        st_ref[...] = jnp.zeros_like(st_ref)

    def body(t, carry):
        h0, h1 = carry
        r0 = pl.ds(pl.multiple_of(t * nb, nb), nb)
        r1 = pl.ds(pl.multiple_of((CHUNK - 1 - t) * nb, nb), nb)
        h0 = a0_ref[r0, :] * h0 + b0_ref[r0, :]
        h1 = a1_ref[r1, :] * h1 + b1_ref[r1, :]
        hf_ref[r0, :] = h0
        hb_ref[r1, :] = h1
        return h0, h1

    h0, h1 = lax.fori_loop(0, CHUNK, body, (st_ref[0], st_ref[1]), unroll=8)
    st_ref[0] = h0
    st_ref[1] = h1


def _lru_scan(a0, b0, a1, b1, nb, n_ctx_chunks):
    rows, w = a0.shape
    tr = CHUNK * nb
    n_tot = rows // tr
    fwd = pl.BlockSpec((tr, w), lambda i: (i, 0))
    bwd = pl.BlockSpec((tr, w), lambda i: (_bwd_chunk(i, n_ctx_chunks, n_tot), 0))
    out = jax.ShapeDtypeStruct((rows, w), F32)
    return pl.pallas_call(
        functools.partial(_lru_scan_kernel, nb),
        grid=(n_tot,),
        in_specs=[fwd, fwd, bwd, bwd],
        out_specs=[fwd, bwd],
        out_shape=[out, out],
        scratch_shapes=[pltpu.VMEM((2, nb, w), F32)],
        compiler_params=_cparams(("arbitrary",)),
        name="rglru_scan",
    )(a0, b0, a1, b1)


def _mixer_out_kernel(nb, of_ref, ob_ref, z_ref, hf_ref, hb_ref, gy_ref, onw_ref, w_ref, h_ref, gt_ref, g_ref, b_ref,
                      o_ref):
    z = z_ref[...]
    parts = []
    for hh in range(DN_HEADS):
        sl = slice(hh * DN_HEAD_DIM, (hh + 1) * DN_HEAD_DIM)
        oh = of_ref[hh] + ob_ref[hh]
        inv = lax.rsqrt(jnp.mean(oh * oh, axis=-1, keepdims=True) + NORM_EPS)
        parts.append(oh * inv * onw_ref[...] * _silu(z[:, sl]))
    parts.append((hf_ref[...] + hb_ref[...]) * gy_ref[...])
    y = jnp.concatenate(parts, axis=1).astype(BF16)
    u = jnp.dot(y, w_ref[...], preferred_element_type=F32)
    r = DEEP_ALPHA * h_ref[...] + _per_batch(u, gt_ref[...], nb) * u
    o_ref[...] = _layer_norm(r) * g_ref[...] + b_ref[...]


def _mixer_out(o_f, o_b, proj, hf, hb, gy, onorm_w, w_out, h, mod_rows, ln_g, ln_b, nb, ctx_rows):
    rows, d = h.shape
    tm = ROW_TILE
    lat = lambda i: jnp.where(i * tm >= ctx_rows, 1, 0)
    half = pl.BlockSpec((tm, DN_WIDTH), lambda i: (i, 0))
    planes = pl.BlockSpec((DN_HEADS, tm, DN_HEAD_DIM), lambda i: (0, i, 0))
    vec = lambda n: pl.BlockSpec((1, n), lambda i: (0, 0))
    return pl.pallas_call(
        functools.partial(_mixer_out_kernel, nb),
        grid=(rows // tm,),
        in_specs=[planes, planes, pl.BlockSpec((tm, DN_WIDTH), lambda i: (i, COL_Z)), half, half, half,
                  vec(DN_HEAD_DIM), pl.BlockSpec((d, d), lambda i: (0, 0)),
                  pl.BlockSpec((tm, d), lambda i: (i, 0)),
                  pl.BlockSpec((None, nb, d), lambda i: (lat(i), 0, 2)),
                  vec(d), vec(d)],
        out_specs=pl.BlockSpec((tm, d), lambda i: (i, 0)),
        out_shape=jax.ShapeDtypeStruct((rows, d), F32),
        compiler_params=_cparams(("parallel",)),
        name="mixer_out",
    )(o_f, o_b, proj, hf, hb, gy, onorm_w, w_out, h, mod_rows, ln_g, ln_b)


INFO_E0, INFO_E1, INFO_W0, INFO_W1, INFO_R0, INFO_R1 = range(6)


def _first_index(mask, lane):
    return jnp.min(jnp.where(mask, lane, LANES), axis=-1, keepdims=True)


def _router_kernel(nb, x_ref, sh_ref, sc_ref, wr_ref, br_ref, hm_ref, info_ref, cnt_ref, run_ref):
    @pl.when(pl.program_id(0) == 0)
    def _():
        run_ref[...] = jnp.zeros_like(run_ref)

    xn = _layer_norm(x_ref[...])
    hm = xn * (1.0 + _per_batch(xn, sc_ref[...], nb)) + _per_batch(xn, sh_ref[...], nb)
    hm_ref[...] = hm
    tm = hm.shape[0]
    logits = jnp.dot(hm, wr_ref[...], precision=HIGHEST, preferred_element_type=F32) + br_ref[...]
    lane = lax.broadcasted_iota(jnp.int32, (tm, LANES), 1)
    neg = -jnp.inf
    is_g = lane < N_GROUPS
    gmax = jnp.max(jnp.where(is_g, logits, neg), axis=-1, keepdims=True)
    g_sel = _first_index(jnp.logical_and(is_g, logits == gmax), lane)
    p_group = 1.0 / jnp.sum(jnp.where(is_g, jnp.exp(logits - gmax), 0.0), axis=-1, keepdims=True)
    lo = N_GROUPS + EXPERTS_PER_GROUP * g_sel
    in_grp = jnp.logical_and(lane >= lo, lane < lo + EXPERTS_PER_GROUP)
    m1 = jnp.max(jnp.where(in_grp, logits, neg), axis=-1, keepdims=True)
    i1 = _first_index(jnp.logical_and(in_grp, logits == m1), lane)
    rest = jnp.logical_and(in_grp, lane != i1)
    m2 = jnp.max(jnp.where(rest, logits, neg), axis=-1, keepdims=True)
    i2 = _first_index(jnp.logical_and(rest, logits == m2), lane)
    e2 = jnp.exp(m2 - m1)
    w1 = p_group / (1.0 + e2)
    w2 = p_group * e2 / (1.0 + e2)
    e_a = i1 - N_GROUPS
    e_b = i2 - N_GROUPS
    oh_a = lane == e_a
    oh_b = lane == e_b
    oh = jnp.logical_or(oh_a, oh_b).astype(F32)
    r_i = lax.broadcasted_iota(jnp.int32, (tm, tm), 0)
    c_i = lax.broadcasted_iota(jnp.int32, (tm, tm), 1)
    before = jnp.dot((r_i > c_i).astype(BF16), oh.astype(BF16), preferred_element_type=F32) + run_ref[...]
    rank_a = jnp.sum(jnp.where(oh_a, before, 0.0), axis=-1, keepdims=True)
    rank_b = jnp.sum(jnp.where(oh_b, before, 0.0), axis=-1, keepdims=True)
    run_ref[...] = run_ref[...] + jnp.sum(oh, axis=0, keepdims=True)
    cnt_ref[...] = run_ref[...]
    info = jnp.zeros((tm, LANES), F32)
    for idx, val in ((INFO_E0, e_a.astype(F32)), (INFO_E1, e_b.astype(F32)), (INFO_W0, w1), (INFO_W1, w2),
                     (INFO_R0, rank_a), (INFO_R1, rank_b)):
        info = jnp.where(lane == idx, val, info)
    info_ref[...] = info


def _router(x1, mod_rows, wr, br, nb, ctx_rows):
    rows, d = x1.shape
    tm = ROW_TILE
    lat = lambda i: jnp.where(i * tm >= ctx_rows, 1, 0)
    return pl.pallas_call(
        functools.partial(_router_kernel, nb),
        grid=(rows // tm,),
        in_specs=[pl.BlockSpec((tm, d), lambda i: (i, 0)),
                  pl.BlockSpec((None, nb, d), lambda i: (lat(i), 0, 3)),
                  pl.BlockSpec((None, nb, d), lambda i: (lat(i), 0, 4)),
                  pl.BlockSpec((d, LANES), lambda i: (0, 0)),
                  pl.BlockSpec((1, LANES), lambda i: (0, 0))],
        out_specs=[pl.BlockSpec((tm, d), lambda i: (i, 0)),
                   pl.BlockSpec((tm, LANES), lambda i: (i, 0)),
                   pl.BlockSpec((1, LANES), lambda i: (0, 0))],
        out_shape=[jax.ShapeDtypeStruct((rows, d), F32),
                   jax.ShapeDtypeStruct((rows, LANES), F32),
                   jax.ShapeDtypeStruct((1, LANES), F32)],
        scratch_shapes=[pltpu.VMEM((1, LANES), F32)],
        compiler_params=_cparams(("arbitrary",)),
        name="moe_router",
    )(x1, mod_rows, mod_rows, wr, br)


def _expert_ffn_kernel(te_ref, src0_ref, srcn_ref, dst_ref, hm_hbm, wg_ref, wu_ref, wd_ref, out_hbm,
                       xbuf, ybuf, sem_in, sem_out):
    j = pl.program_id(0)
    last = pl.num_programs(0) - 1
    tmx = xbuf.shape[1]
    slot = j % 2
    nslot = 1 - slot

    def gather_all(src_ref, s):
        for r in range(tmx):
            pltpu.make_async_copy(hm_hbm.at[pl.ds(src_ref[0, r], 1)], xbuf.at[s, pl.ds(r, 1)],
                                  sem_in.at[s]).start(priority=r % 2)

    def wait_rows(buf, sem, s):
        pltpu.make_async_copy(buf.at[s], buf.at[s], sem.at[s]).wait()

    @pl.when(j == 0)
    def _():
        gather_all(src0_ref, 0)

    wait_rows(xbuf, sem_in, slot)

    @pl.when(j >= 2)
    def _():
        wait_rows(ybuf, sem_out, slot)

    gather_all(srcn_ref, nslot)
    x = xbuf[slot].astype(BF16)
    a = jnp.dot(x, wg_ref[...], preferred_element_type=F32)
    b = jnp.dot(x, wu_ref[...], preferred_element_type=F32)
    hid = (_silu(a) * b).astype(BF16)
    ybuf[slot] = jnp.dot(hid, wd_ref[...], preferred_element_type=F32)
    for r in range(tmx):
        pltpu.make_async_copy(ybuf.at[slot, pl.ds(r, 1)], out_hbm.at[pl.ds(dst_ref[0, r], 1)],
                              sem_out.at[slot]).start(priority=r % 2)

    @pl.when(j == last)
    def _():
        wait_rows(xbuf, sem_in, nslot)
        wait_rows(ybuf, sem_out, slot)

        @pl.when(j >= 1)
        def _():
            wait_rows(ybuf, sem_out, nslot)


def _expert_ffn(hm, tile_expert, src_tok, dst_row, out_rows, w_gate, w_up, w_down):
    rows, d = hm.shape
    n_tiles, tmx = src_tok.shape
    src_tok = src_tok.reshape(n_tiles, 1, tmx)
    dst_row = dst_row.reshape(n_tiles, 1, tmx)
    de = w_gate.shape[-1]
    smem = lambda f: pl.BlockSpec((None, 1, tmx), f, memory_space=pltpu.SMEM)
    grid_spec = pltpu.PrefetchScalarGridSpec(
        num_scalar_prefetch=1,
        grid=(n_tiles,),
        in_specs=[
            smem(lambda j, te: (0, 0, 0)),
            smem(lambda j, te: (jnp.minimum(j + 1, n_tiles - 1), 0, 0)),
            smem(lambda j, te: (j, 0, 0)),
            pl.BlockSpec(memory_space=pl.ANY),
            pl.BlockSpec((None, d, de), lambda j, te: (te[j], 0, 0)),
            pl.BlockSpec((None, d, de), lambda j, te: (te[j], 0, 0)),
            pl.BlockSpec((None, de, d), lambda j, te: (te[j], 0, 0)),
        ],
        out_specs=pl.BlockSpec(memory_space=pl.ANY),
        scratch_shapes=[pltpu.VMEM((2, tmx, d), F32), pltpu.VMEM((2, tmx, d), F32),
                        pltpu.SemaphoreType.DMA((2,)), pltpu.SemaphoreType.DMA((2,))],
    )
    return pl.pallas_call(
        _expert_ffn_kernel,
        grid_spec=grid_spec,
        out_shape=jax.ShapeDtypeStruct((out_rows, d), F32),
        compiler_params=_cparams(("arbitrary",)),
        name="expert_ffn",
    )(tile_expert, src_tok, src_tok, dst_row, hm, w_gate, w_up, w_down)


def _moe_out_kernel(nb, y_ref, info_ref, x_ref, gt_ref, g_ref, b_ref, o_ref):
    info = info_ref[...]
    f = y_ref[0] * info[:, INFO_W0:INFO_W0 + 1] + y_ref[1] * info[:, INFO_W1:INFO_W1 + 1]
    r = DEEP_ALPHA * x_ref[...] + _per_batch(f, gt_ref[...], nb) * f
    o_ref[...] = _layer_norm(r) * g_ref[...] + b_ref[...]


def _moe_out(y2, info, x1, mod_rows, ln_g, ln_b, nb, ctx_rows):
    rows, d = x1.shape
    tm = ROW_TILE
    lat = lambda i: jnp.where(i * tm >= ctx_rows, 1, 0)
    vec = pl.BlockSpec((1, d), lambda i: (0, 0))
    return pl.pallas_call(
        functools.partial(_moe_out_kernel, nb),
        grid=(rows // tm,),
        in_specs=[pl.BlockSpec((2, tm, d), lambda i: (0, i, 0)),
                  pl.BlockSpec((tm, LANES), lambda i: (i, 0)),
                  pl.BlockSpec((tm, d), lambda i: (i, 0)),
                  pl.BlockSpec((None, nb, d), lambda i: (lat(i), 0, 5)),
                  vec, vec],
        out_specs=pl.BlockSpec((tm, d), lambda i: (i, 0)),
        out_shape=jax.ShapeDtypeStruct((rows, d), F32),
        compiler_params=_cparams(("parallel",)),
        name="moe_out",
    )(y2, info, x1, mod_rows, ln_g, ln_b)


def _routing_tables(info, counts, n_tiles, tmx):
    rows = info.shape[0]
    plane_rows = rows + tmx
    cnt = counts[0, :N_EXPERTS].astype(jnp.int32)
    padded = ((cnt + tmx - 1) // tmx) * tmx
    ends = jnp.cumsum(padded)
    starts = ends - padded
    eid = info[:, INFO_E0:INFO_E1 + 1].astype(jnp.int32)
    rank = info[:, INFO_R0:INFO_R1 + 1].astype(jnp.int32)
    dest = (starts[eid] + rank).reshape(-1)
    pair_of_row = jnp.full((n_tiles * tmx,), -1, jnp.int32).at[dest].set(jnp.arange(2 * rows, dtype=jnp.int32))
    pair_of_row = pair_of_row.reshape(n_tiles, tmx)
    real = pair_of_row >= 0
    tok = pair_of_row >> 1
    tile = jnp.arange(n_tiles, dtype=jnp.int32)[:, None]
    r_in_tile = jnp.arange(tmx, dtype=jnp.int32)[None, :]
    src_tok = jnp.where(real, tok, 0)
    dst_row = jnp.where(real, (pair_of_row & 1) * plane_rows + tok, (tile % 2) * plane_rows + rows + r_in_tile)
    tile_start = jnp.arange(n_tiles, dtype=jnp.int32) * tmx
    n_before = jnp.sum((ends[None, :] <= tile_start[:, None]).astype(jnp.int32), axis=1)
    tile_expert = jnp.minimum(n_before, N_EXPERTS - 1)
    return tile_expert, src_tok, dst_row, plane_rows


def _relayout_w_in(w_in):
    qkvz = w_in[..., :4 * DN_WIDTH]
    gates = w_in[..., 4 * DN_WIDTH:4 * DN_WIDTH + 4 * DN_HEADS]
    rxy = w_in[..., 4 * DN_WIDTH + 4 * DN_HEADS:]
    pad = jnp.zeros(w_in.shape[:-1] + (LANES - 4 * DN_HEADS,), w_in.dtype)
    return jnp.concatenate([qkvz, rxy, gates, pad], axis=-1).astype(BF16)


def _block_diag(w):
    n, c, _ = w.shape
    eye = jnp.eye(n, dtype=w.dtype)
    return (eye[:, None, :, None] * w[:, :, None, :]).reshape(n * c, n * c)


def _lane_vec(vals, offset):
    return jnp.zeros((1, LANES), F32).at[0, offset:offset + vals.size].set(vals.reshape(-1))


def kernel(x, c, ctx, c_ctx, w_ada, b_ada, w_in, conv_qkv_w, dn_a_log, dn_dt_bias, dn_onorm_w, rg_conv_w, rg_conv_b, rg_wa, rg_ba, rg_wi, rg_bi, rg_lambda, w_out, ln_g, ln_b, router_wg, router_bg, router_we, router_be, w_e_gate, w_e_up, w_e_down):
    nb, t_lat, d = x.shape
    t_ctx = ctx.shape[1]
    assert d == D_MODEL and nb % 8 == 0 and t_lat % GRID_W == 0 and t_ctx % CHUNK == 0
    assert (t_ctx * nb) % ROW_TILE == 0 and (t_lat * nb) % ROW_TILE == 0
    tt = t_ctx + t_lat
    rows = tt * nb
    ctx_rows = t_ctx * nb
    n_ctx_chunks = t_ctx // CHUNK

    h = jnp.concatenate([jnp.swapaxes(ctx, 0, 1), jnp.swapaxes(x, 0, 1)], axis=0).reshape(rows, d)

    n_cc = ((nb + 1 + 7) // 8) * 8
    cc = jnp.zeros((n_cc, d), F32).at[:nb].set(c).at[nb].set(c_ctx)
    mod = _modulation(cc, w_ada, b_ada)
    mod_ctx = jnp.broadcast_to(mod[:, nb:nb + 1], (DEPTH, nb, 6 * d))
    mod_rows = jnp.stack([mod_ctx, mod[:, :nb]], axis=1)

    w_in2 = _relayout_w_in(w_in)
    n_tiles = (2 * rows) // MOE_TILE + N_EXPERTS
    for l in range(DEPTH):
        mr = mod_rows[l]
        proj = _in_projection(h, mr, w_in2[l], nb, ctx_rows)
        qkv = _qkv_prep(proj, conv_qkv_w[l], nb, n_ctx_chunks)
        wg = jnp.concatenate([_block_diag(rg_wa[l, 0]), _block_diag(rg_wi[l, 0]),
                              _block_diag(rg_wa[l, 1]), _block_diag(rg_wi[l, 1])], axis=1).astype(BF16)
        bg = jnp.concatenate([rg_ba[l, 0], rg_bi[l, 0], rg_ba[l, 1], rg_bi[l, 1]])[None]
        a0, b0, a1, b1, gy = _rg_prep(proj, rg_conv_w[l], rg_conv_b[l][None], wg, bg, rg_lambda[l], nb, n_ctx_chunks)
        alog_vec = _lane_vec(dn_a_log[l], 2 * DN_HEADS)
        dtb_vec = _lane_vec(dn_dt_bias[l], 2 * DN_HEADS)
        o_f, o_b = _deltanet(qkv, proj, alog_vec, dtb_vec, nb, n_ctx_chunks)
        hf, hb = _lru_scan(a0, b0, a1, b1, nb, n_ctx_chunks)
        x1 = _mixer_out(o_f, o_b, proj, hf, hb, gy, dn_onorm_w[l][None], w_out[l].astype(BF16), h, mr,
                        ln_g[l, 0][None], ln_b[l, 0][None], nb, ctx_rows)
        wr = jnp.zeros((d, LANES), F32).at[:, :N_GROUPS].set(router_wg[l]).at[
            :, N_GROUPS:N_GROUPS + N_EXPERTS].set(router_we[l])
        br = _lane_vec(jnp.concatenate([router_bg[l], router_be[l]]), 0)
        hm, info, counts = _router(x1, mr, wr, br, nb, ctx_rows)
        tile_expert, src_tok, dst_row, plane_rows = _routing_tables(info, counts, n_tiles, MOE_TILE)
        y2 = _expert_ffn(hm, tile_expert, src_tok, dst_row, 2 * plane_rows,
                         w_e_gate[l].astype(BF16), w_e_up[l].astype(BF16), w_e_down[l].astype(BF16))
        h = _moe_out(y2.reshape(2, plane_rows, d), info, x1, mr, ln_g[l, 1][None], ln_b[l, 1][None], nb, ctx_rows)

    out = h.reshape(tt, nb, d)[t_ctx:]
    return jnp.swapaxes(out, 0, 1)
```

```python
import functools
import math

import jax
import jax.numpy as jnp
from jax import lax
from jax.experimental import pallas as pl
from jax.experimental.pallas import tpu as pltpu

F32 = jnp.float32
BF16 = jnp.bfloat16
HIGHEST = lax.Precision.HIGHEST

D_MODEL = 1024
DEPTH = 2
GRID_W = 64
DN_HEAD_DIM = 128
DN_WIDTH = 512
DN_HEADS = 4
RG_WIDTH = 512
RG_BLOCKS = 8
RG_BLOCK_DIM = 64
RG_C = 8.0
CONV_K = 4
N_GROUPS = 4
EXPERTS_PER_GROUP = 8
N_EXPERTS = 32
D_EXPERT = 256
DEEP_ALPHA = (2 * DEPTH) ** 0.25
LN_EPS = 1e-5
NORM_EPS = 1e-6

LANES = 128
CHUNK = 64
SOLVE_BLOCK = 16
D_PROJ = 3 * DN_WIDTH + DN_WIDTH + 2 * RG_WIDTH + LANES
COL_Z = 3
COL_RX = 4
COL_RY = 5
COL_GATE = (3 * DN_WIDTH + DN_WIDTH + 2 * RG_WIDTH) // LANES
ROW_TILE = 512
MOE_TILE = 256
VMEM_LIMIT = 56 * 1024 * 1024


def _cparams(sem):
    return pltpu.CompilerParams(dimension_semantics=sem, vmem_limit_bytes=VMEM_LIMIT)


def _layer_norm(x):
    mu = jnp.mean(x, axis=-1, keepdims=True)
    xc = x - mu
    var = jnp.mean(xc * xc, axis=-1, keepdims=True)
    return xc * lax.rsqrt(var + LN_EPS)


def _per_batch(x, v, nb):
    rows, c = x.shape
    return jnp.broadcast_to(v[None], (rows // nb, nb, c)).reshape(rows, c)


def _sigmoid(x):
    return 1.0 / (1.0 + jnp.exp(-x))


def _silu(x):
    return x * _sigmoid(x)


def _mod_kernel(cc_ref, w_ref, b_ref, o_ref):
    s = _silu(cc_ref[...])
    o_ref[...] = jnp.dot(s, w_ref[...], precision=HIGHEST, preferred_element_type=F32) + b_ref[...]


def _modulation(cc, w_ada, b_ada):
    depth, d, n6 = w_ada.shape
    rows = cc.shape[0]
    tn = 1536
    return pl.pallas_call(
        _mod_kernel,
        grid=(depth, n6 // tn),
        in_specs=[
            pl.BlockSpec((rows, d), lambda l, j: (0, 0)),
            pl.BlockSpec((None, d, tn), lambda l, j: (l, 0, j)),
            pl.BlockSpec((None, 1, tn), lambda l, j: (l, 0, j)),
        ],
        out_specs=pl.BlockSpec((None, rows, tn), lambda l, j: (l, 0, j)),
        out_shape=jax.ShapeDtypeStruct((depth, rows, n6), F32),
        compiler_params=_cparams(("parallel", "parallel")),
        name="adaln_mod",
    )(cc, w_ada, b_ada.reshape(depth, 1, n6))


def _inproj_kernel(nb, h_ref, sh_ref, sc_ref, w_ref, o_ref):
    xn = _layer_norm(h_ref[...])
    hm = xn * (1.0 + _per_batch(xn, sc_ref[...], nb)) + _per_batch(xn, sh_ref[...], nb)
    o_ref[...] = jnp.dot(hm.astype(BF16), w_ref[...], preferred_element_type=F32)


def _in_projection(h, mod_rows, w_in2, nb, ctx_rows):
    rows, d = h.shape
    tm = ROW_TILE
    lat = lambda i: jnp.where(i * tm >= ctx_rows, 1, 0)
    return pl.pallas_call(
        functools.partial(_inproj_kernel, nb),
        grid=(rows // tm,),
        in_specs=[
            pl.BlockSpec((tm, d), lambda i: (i, 0)),
            pl.BlockSpec((None, nb, d), lambda i: (lat(i), 0, 0)),
            pl.BlockSpec((None, nb, d), lambda i: (lat(i), 0, 1)),
            pl.BlockSpec((d, D_PROJ), lambda i: (0, 0)),
        ],
        out_specs=pl.BlockSpec((tm, D_PROJ), lambda i: (i, 0)),
        out_shape=jax.ShapeDtypeStruct((rows, D_PROJ), F32),
        compiler_params=_cparams(("parallel",)),
        name="ln_mod_inproj",
    )(h, mod_rows, mod_rows, w_in2)


def _conv_tile(x, prev, nxt, w, nb, use_prev, use_next):
    prev = jnp.where(use_prev, prev, 0.0)
    nxt = jnp.where(use_next, nxt, 0.0)
    xm2 = jnp.concatenate([prev, x[: -2 * nb]], axis=0)
    xm1 = jnp.concatenate([prev[nb:], x[:-nb]], axis=0)
    xp1 = jnp.concatenate([x[nb:], nxt], axis=0)
    return xm2 * w[0:1] + xm1 * w[1:2] + x * w[2:3] + xp1 * w[3:4]


def _halo_flags(i, n_ctx_tiles):
    use_prev = jnp.logical_and(i > 0, i < n_ctx_tiles)
    use_next = i < n_ctx_tiles - 1
    return use_prev, use_next


def _qkv_prep_kernel(nb, n_ctx_tiles, x_ref, prev_ref, next_ref, w_ref, o_ref):
    i = pl.program_id(0)
    part = pl.program_id(1)
    use_prev, use_next = _halo_flags(i, n_ctx_tiles)
    y = _silu(_conv_tile(x_ref[...], prev_ref[...], next_ref[...], w_ref[...], nb, use_prev, use_next))
    qscale = jnp.where(part == 0, DN_HEAD_DIM ** -0.5, 1.0).astype(F32)
    for hh in range(DN_HEADS):
        sl = slice(hh * DN_HEAD_DIM, (hh + 1) * DN_HEAD_DIM)
        yh = y[:, sl]
        inv = lax.rsqrt(jnp.sum(yh * yh, axis=-1, keepdims=True) + NORM_EPS) * qscale
        o_ref[hh] = yh * jnp.where(part < 2, inv, 1.0)


def _halo_specs(tile_rows, nb, cols, col_of, n_row_blocks16, tile_of=lambda i: i):
    prev_b = tile_rows // (2 * nb)
    next_b = tile_rows // nb
    return [
        pl.BlockSpec((tile_rows, cols), lambda i, *a: (tile_of(i), col_of(*a))),
        pl.BlockSpec((2 * nb, cols), lambda i, *a: (jnp.maximum(tile_of(i) * prev_b - 1, 0), col_of(*a))),
        pl.BlockSpec((nb, cols),
                     lambda i, *a: (jnp.minimum((tile_of(i) + 1) * next_b, n_row_blocks16 - 1), col_of(*a))),
    ]


def _qkv_prep(proj, conv_w, nb, n_ctx_tiles):
    rows = proj.shape[0]
    tr = CHUNK * nb
    return pl.pallas_call(
        functools.partial(_qkv_prep_kernel, nb, n_ctx_tiles),
        grid=(rows // tr, 3),
        in_specs=_halo_specs(tr, nb, DN_WIDTH, lambda p: p, rows // nb)
        + [pl.BlockSpec((CONV_K, DN_WIDTH), lambda i, p: (0, p))],
        out_specs=pl.BlockSpec((DN_HEADS, tr, DN_HEAD_DIM), lambda i, p: (p, i, 0)),
        out_shape=jax.ShapeDtypeStruct((3 * DN_HEADS, rows, DN_HEAD_DIM), F32),
        compiler_params=_cparams(("parallel", "parallel")),
        name="qkv_conv_norm",
    )(proj, proj, proj, conv_w)


def _bwd_chunk(i, n_ctx, n_tot):
    return jnp.where(i < n_ctx, n_ctx - 1 - i, n_tot + n_ctx - 1 - i)


def _rglru_kernel(nb, n_ctx_tiles, n_tot, xf_ref, pf_ref, nf_ref, xb_ref, pb_ref, nb_ref, ry_ref, cw_ref, cb_ref,
                  wg0_ref, wg1_ref, bg0_ref, bg1_ref, lam_ref, hf_ref, hb_ref, gy_ref, a_scr, b_scr, st_ref):
    i = pl.program_id(0)

    @pl.when(i == 0)
    def _():
        st_ref[...] = jnp.zeros_like(st_ref)

    w = RG_WIDTH
    sp = jax.nn.softplus(-lam_ref[...])
    tiles = (i, _bwd_chunk(i, n_ctx_tiles, n_tot))
    for d, (x_ref, p_ref, n_ref, wg_ref, bg_ref) in enumerate(
            ((xf_ref, pf_ref, nf_ref, wg0_ref, bg0_ref), (xb_ref, pb_ref, nb_ref, wg1_ref, bg1_ref))):
        use_prev, use_next = _halo_flags(tiles[d], n_ctx_tiles)
        xc = _conv_tile(x_ref[...], p_ref[...], n_ref[...], cw_ref[...], nb, use_prev, use_next) + cb_ref[...]
        gates = jnp.dot(xc.astype(BF16), wg_ref[...], preferred_element_type=F32) + bg_ref[...]
        r = _sigmoid(gates[:, :w])
        ig = _sigmoid(gates[:, w:])
        a = jnp.exp(-RG_C * r * sp[d:d + 1])
        y = jnp.maximum(1.0 - a * a, 0.0)
        a_scr[d] = a
        b_scr[d] = jnp.where(y > 0.0, y * lax.rsqrt(y), 0.0) * (ig * xc)
    gy_ref[...] = jax.nn.gelu(ry_ref[...])

    def body(t, carry):
        h0, h1 = carry
        r0 = pl.ds(pl.multiple_of(t * nb, nb), nb)
        r1 = pl.ds(pl.multiple_of((CHUNK - 1 - t) * nb, nb), nb)
        h0 = a_scr[0, r0, :] * h0 + b_scr[0, r0, :]
        h1 = a_scr[1, r1, :] * h1 + b_scr[1, r1, :]
        hf_ref[r0, :] = h0
        hb_ref[r1, :] = h1
        return h0, h1

    h0, h1 = lax.fori_loop(0, CHUNK, body, (st_ref[0], st_ref[1]), unroll=8)
    st_ref[0] = h0
    st_ref[1] = h1


def _rglru(proj, conv_w, conv_b, wg, bg, lam, nb, n_ctx_tiles):
    rows = proj.shape[0]
    tr = CHUNK * nb
    n_tot = rows // tr
    w = RG_WIDTH
    out = jax.ShapeDtypeStruct((rows, w), F32)
    fwd = lambda i: i
    bwd = lambda i: _bwd_chunk(i, n_ctx_tiles, n_tot)
    full = lambda shape: pl.BlockSpec(shape, lambda i: (0,) * len(shape))
    return pl.pallas_call(
        functools.partial(_rglru_kernel, nb, n_ctx_tiles, n_tot),
        grid=(n_tot,),
        in_specs=_halo_specs(tr, nb, w, lambda: COL_RX, rows // nb, fwd)
        + _halo_specs(tr, nb, w, lambda: COL_RX, rows // nb, bwd)
        + [pl.BlockSpec((tr, w), lambda i: (i, COL_RY)), full((CONV_K, w)), full((1, w)),
           pl.BlockSpec((w, 2 * w), lambda i: (0, 0)), pl.BlockSpec((w, 2 * w), lambda i: (0, 1)),
           pl.BlockSpec((1, 2 * w), lambda i: (0, 0)), pl.BlockSpec((1, 2 * w), lambda i: (0, 1)),
           full((2, w))],
        out_specs=[pl.BlockSpec((tr, w), lambda i: (fwd(i), 0)), pl.BlockSpec((tr, w), lambda i: (bwd(i), 0)),
                   pl.BlockSpec((tr, w), lambda i: (i, 0))],
        out_shape=[out] * 3,
        scratch_shapes=[pltpu.VMEM((2, tr, w), F32), pltpu.VMEM((2, tr, w), F32), pltpu.VMEM((2, nb, w), F32)],
        compiler_params=_cparams(("arbitrary",)),
        name="rglru",
    )(proj, proj, proj, proj, proj, proj, proj, conv_w, conv_b, wg, wg, bg, bg, lam)


def _bdot(a, b):
    return jnp.dot(a.astype(BF16), b.astype(BF16), preferred_element_type=F32)


def _deltanet_gates(d, raw, alog, dtb):
    c = CHUNK
    row = lax.broadcasted_iota(jnp.int32, (c, c), 0)
    col = lax.broadcasted_iota(jnp.int32, (c, c), 1)
    incl = (row >= col) if d == 0 else (row <= col)
    g_all = -jnp.exp(alog) * jax.nn.softplus(raw + dtb)
    gc_all = jnp.dot(incl.astype(F32), g_all, precision=HIGHEST, preferred_element_type=F32)
    g_tot = jnp.sum(g_all, axis=0, keepdims=True)
    return dict(incl=incl, strict=(row > col) if d == 0 else (row < col),
                diag_blk=(row // SOLVE_BLOCK) == (col // SOLVE_BLOCK),
                beta=_sigmoid(raw), gc=gc_all, gc_t=gc_all.T, egc=jnp.exp(gc_all),
                kdec=jnp.exp(g_tot - gc_all), glast=jnp.exp(g_tot))


def _deltanet_heads(gts, cbs, qs, ks, vs, ss):
    c = CHUNK
    dh = DN_HEAD_DIM
    n = len(cbs)
    rng = range(n)
    cgs = [2 * DN_HEADS + cb for cb in cbs]
    beta = [gts[i]["beta"][:, cbs[i]:cbs[i] + 1] for i in rng]
    egc = [gts[i]["egc"][:, cgs[i]:cgs[i] + 1] for i in rng]
    decay = [jnp.exp(jnp.where(gts[i]["incl"], gts[i]["gc"][:, cgs[i]:cgs[i] + 1] - gts[i]["gc_t"][cgs[i]:cgs[i] + 1, :],
                               -1e30)) for i in rng]
    kbeta = [ks[i] * beta[i] for i in rng]
    prod = [lax.dot_general(jnp.concatenate([kbeta[i], qs[i]], axis=0).astype(BF16), ks[i].astype(BF16),
                            (((1,), (1,)), ((), ())), preferred_element_type=F32) for i in rng]
    a_mat = [jnp.where(gts[i]["strict"], prod[i][:c] * decay[i], 0.0) for i in rng]
    intra = [prod[i][c:] * decay[i] for i in rng]
    p = [jnp.where(gts[i]["diag_blk"], -a_mat[i], 0.0) for i in rng]
    y = [jnp.concatenate([vs[i] * beta[i], kbeta[i] * egc[i], jnp.where(gts[i]["diag_blk"], 0.0, a_mat[i])], axis=1)
         for i in rng]
    for j in range(4):
        y = [y[i] + _bdot(p[i], y[i]) for i in rng]
        if j < 3:
            p = [_bdot(p[i], p[i]) for i in rng]
    e = [y[i][:, 2 * dh:] for i in rng]
    y = [y[i][:, :2 * dh] for i in rng]
    e2 = [_bdot(e[i], e[i]) for i in rng]
    y = [y[i] + _bdot(e2[i], y[i]) for i in rng]
    y = [y[i] - _bdot(e[i], y[i]) for i in rng]
    ws_qs = [_bdot(jnp.concatenate([y[i][:, dh:], qs[i] * egc[i]], axis=0), ss[i]) for i in rng]
    v_new = [y[i][:, :dh] - ws_qs[i][:c] for i in rng]
    o = [ws_qs[i][c:] + _bdot(intra[i], v_new[i]) for i in rng]
    s_new = [ss[i] * gts[i]["glast"][:, cgs[i]:cgs[i] + 1] + lax.dot_general(
        (ks[i] * gts[i]["kdec"][:, cgs[i]:cgs[i] + 1]).astype(BF16), v_new[i].astype(BF16),
        (((0,), (0,)), ((), ())), preferred_element_type=F32) for i in rng]
    return o, s_new


def _deltanet_kernel(nb, qf_ref, kf_ref, vf_ref, gf_ref, qb_ref, kb_ref, vb_ref, gb_ref, alog_ref, dtb_ref,
                     of_ref, ob_ref, s_ref):
    @pl.when(pl.program_id(0) == 0)
    def _():
        s_ref[...] = jnp.zeros_like(s_ref)

    dirs = ((qf_ref, kf_ref, vf_ref, gf_ref, of_ref), (qb_ref, kb_ref, vb_ref, gb_ref, ob_ref))

    def body(b, carry):
        rows_b = pl.ds(b, CHUNK, stride=nb)
        idx = [(d, hh) for d in range(2) for hh in range(DN_HEADS)]
        cbs = [d * DN_HEADS + hh for d, hh in idx]
        gates = [_deltanet_gates(d, dirs[d][3][rows_b, :], alog_ref[...], dtb_ref[...]) for d in range(2)]
        qs = [dirs[d][0][hh, rows_b, :] for d, hh in idx]
        ks = [dirs[d][1][hh, rows_b, :] for d, hh in idx]
        vs = [dirs[d][2][hh, rows_b, :] for d, hh in idx]
        ss = [s_ref[b * (2 * DN_HEADS) + cb] for cb in cbs]
        o, s_new = _deltanet_heads([gates[d] for d, _ in idx], cbs, qs, ks, vs, ss)
        for i, (d, hh) in enumerate(idx):
            dirs[d][4][hh, rows_b, :] = o[i]
            s_ref[b * (2 * DN_HEADS) + cbs[i]] = s_new[i]
        return carry

    lax.fori_loop(0, nb, body, 0)


def _deltanet(qkv, proj, alog_vec, dtb_vec, nb, n_ctx_chunks):
    rows = qkv.shape[1]
    tr = CHUNK * nb
    n_tot = rows // tr
    fwd = lambda i: i
    bwd = lambda i: _bwd_chunk(i, n_ctx_chunks, n_tot)

    def specs(order):
        return [pl.BlockSpec((DN_HEADS, tr, DN_HEAD_DIM), lambda i, p=p: (p, order(i), 0)) for p in range(3)] + [
            pl.BlockSpec((tr, LANES), lambda i: (order(i), COL_GATE))]

    vec = pl.BlockSpec((1, LANES), lambda i: (0, 0))
    out = jax.ShapeDtypeStruct((DN_HEADS, rows, DN_HEAD_DIM), F32)
    return pl.pallas_call(
        functools.partial(_deltanet_kernel, nb),
        grid=(n_tot,),
        in_specs=specs(fwd) + specs(bwd) + [vec, vec],
        out_specs=[pl.BlockSpec((DN_HEADS, tr, DN_HEAD_DIM), lambda i: (0, fwd(i), 0)),
                   pl.BlockSpec((DN_HEADS, tr, DN_HEAD_DIM), lambda i: (0, bwd(i), 0))],
        out_shape=[out, out],
        scratch_shapes=[pltpu.VMEM((nb * 2 * DN_HEADS, DN_HEAD_DIM, DN_HEAD_DIM), F32)],
        compiler_params=_cparams(("arbitrary",)),
        name="deltanet",
    )(qkv, qkv, qkv, proj, qkv, qkv, qkv, proj, alog_vec, dtb_vec)


def _mixer_out_kernel(nb, of_ref, ob_ref, z_ref, hf_ref, hb_ref, gy_ref, onw_ref, w_ref, h_ref, gt_ref, g_ref, b_ref,
                      o_ref):
    z = z_ref[...]
    parts = []
    for hh in range(DN_HEADS):
        sl = slice(hh * DN_HEAD_DIM, (hh + 1) * DN_HEAD_DIM)
        oh = of_ref[hh] + ob_ref[hh]
        inv = lax.rsqrt(jnp.mean(oh * oh, axis=-1, keepdims=True) + NORM_EPS)
        parts.append(oh * inv * onw_ref[...] * _silu(z[:, sl]))
    parts.append((hf_ref[...] + hb_ref[...]) * gy_ref[...])
    y = jnp.concatenate(parts, axis=1).astype(BF16)
    u = jnp.dot(y, w_ref[...], preferred_element_type=F32)
    r = DEEP_ALPHA * h_ref[...] + _per_batch(u, gt_ref[...], nb) * u
    o_ref[...] = _layer_norm(r) * g_ref[...] + b_ref[...]


def _mixer_out(o_f, o_b, proj, hf, hb, gy, onorm_w, w_out, h, mod_rows, ln_g, ln_b, nb, ctx_rows, row0):
    rows, d = h.shape
    tm = ROW_TILE
    t0 = row0 // tm
    lat = lambda i: jnp.where((i + t0) * tm >= ctx_rows, 1, 0)
    half = pl.BlockSpec((tm, DN_WIDTH), lambda i: (i + t0, 0))
    planes = pl.BlockSpec((DN_HEADS, tm, DN_HEAD_DIM), lambda i: (0, i + t0, 0))
    vec = lambda n: pl.BlockSpec((1, n), lambda i: (0, 0))
    return pl.pallas_call(
        functools.partial(_mixer_out_kernel, nb),
        grid=((rows - row0) // tm,),
        in_specs=[planes, planes, pl.BlockSpec((tm, DN_WIDTH), lambda i: (i + t0, COL_Z)), half, half, half,
                  vec(DN_HEAD_DIM), pl.BlockSpec((d, d), lambda i: (0, 0)),
                  pl.BlockSpec((tm, d), lambda i: (i + t0, 0)),
                  pl.BlockSpec((None, nb, d), lambda i: (lat(i), 0, 2)),
                  vec(d), vec(d)],
        out_specs=pl.BlockSpec((tm, d), lambda i: (i, 0)),
        out_shape=jax.ShapeDtypeStruct((rows - row0, d), F32),
        compiler_params=_cparams(("parallel",)),
        name="mixer_out",
    )(o_f, o_b, proj, hf, hb, gy, onorm_w, w_out, h, mod_rows, ln_g, ln_b)


INFO_E0, INFO_E1, INFO_W0, INFO_W1, INFO_R0, INFO_R1 = range(6)


ROW_SUB = D_MODEL // LANES


def _store_row_tiles(ref, x):
    n = x.shape[0]
    for c in range(ROW_SUB):
        ref[pl.ds(c, n, stride=ROW_SUB), :] = x[:, c * LANES:(c + 1) * LANES]


def _load_row_tiles(ref, n):
    return jnp.concatenate([ref[pl.ds(c, n, stride=ROW_SUB), :] for c in range(ROW_SUB)], axis=1)


def _first_index(mask, lane):
    return jnp.min(jnp.where(mask, lane, LANES), axis=-1, keepdims=True)


def _router_kernel(nb, x_ref, sh_ref, sc_ref, wr_ref, br_ref, hm_ref, info_ref, cnt_ref, run_ref):
    @pl.when(pl.program_id(0) == 0)
    def _():
        run_ref[...] = jnp.zeros_like(run_ref)

    xn = _layer_norm(x_ref[...])
    hm = xn * (1.0 + _per_batch(xn, sc_ref[...], nb)) + _per_batch(xn, sh_ref[...], nb)
    _store_row_tiles(hm_ref, hm)
    tm = hm.shape[0]
    h_hi = hm.astype(BF16)
    h_lo = (hm - h_hi.astype(F32)).astype(BF16)
    hi_hl = jnp.dot(h_hi, wr_ref[...], preferred_element_type=F32)
    logits = (hi_hl[:, :LANES] + hi_hl[:, LANES:] + jnp.dot(h_lo, wr_ref[:, :LANES], preferred_element_type=F32)
              + br_ref[...])
    lane = lax.broadcasted_iota(jnp.int32, (tm, LANES), 1)
    neg = -jnp.inf
    is_g = lane < N_GROUPS
    gmax = jnp.max(jnp.where(is_g, logits, neg), axis=-1, keepdims=True)
    g_sel = _first_index(jnp.logical_and(is_g, logits == gmax), lane)
    p_group = 1.0 / jnp.sum(jnp.where(is_g, jnp.exp(logits - gmax), 0.0), axis=-1, keepdims=True)
    lo = N_GROUPS + EXPERTS_PER_GROUP * g_sel
    in_grp = jnp.logical_and(lane >= lo, lane < lo + EXPERTS_PER_GROUP)
    m1 = jnp.max(jnp.where(in_grp, logits, neg), axis=-1, keepdims=True)
    i1 = _first_index(jnp.logical_and(in_grp, logits == m1), lane)
    rest = jnp.logical_and(in_grp, lane != i1)
    m2 = jnp.max(jnp.where(rest, logits, neg), axis=-1, keepdims=True)
    i2 = _first_index(jnp.logical_and(rest, logits == m2), lane)
    e2 = jnp.exp(m2 - m1)
    w1 = p_group / (1.0 + e2)
    w2 = p_group * e2 / (1.0 + e2)
    e_a = i1 - N_GROUPS
    e_b = i2 - N_GROUPS
    oh_a = lane == e_a
    oh_b = lane == e_b
    oh = jnp.logical_or(oh_a, oh_b).astype(F32)
    r_i = lax.broadcasted_iota(jnp.int32, (tm, tm), 0)
    c_i = lax.broadcasted_iota(jnp.int32, (tm, tm), 1)
    before = jnp.dot((r_i > c_i).astype(BF16), oh.astype(BF16), preferred_element_type=F32) + run_ref[...]
    rank_a = jnp.sum(jnp.where(oh_a, before, 0.0), axis=-1, keepdims=True)
    rank_b = jnp.sum(jnp.where(oh_b, before, 0.0), axis=-1, keepdims=True)
    run_ref[...] = run_ref[...] + jnp.sum(oh, axis=0, keepdims=True)
    cnt_ref[...] = run_ref[...]
    info = jnp.zeros((tm, LANES), F32)
    for idx, val in ((INFO_E0, e_a.astype(F32)), (INFO_E1, e_b.astype(F32)), (INFO_W0, w1), (INFO_W1, w2),
                     (INFO_R0, rank_a), (INFO_R1, rank_b)):
        info = jnp.where(lane == idx, val, info)
    info_ref[...] = info


def _router(x1, mod_rows, wr, br, nb, ctx_rows):
    rows, d = x1.shape
    tm = ROW_TILE
    lat = lambda i: jnp.where(i * tm >= ctx_rows, 1, 0)
    return pl.pallas_call(
        functools.partial(_router_kernel, nb),
        grid=(rows // tm,),
        in_specs=[pl.BlockSpec((tm, d), lambda i: (i, 0)),
                  pl.BlockSpec((None, nb, d), lambda i: (lat(i), 0, 3)),
                  pl.BlockSpec((None, nb, d), lambda i: (lat(i), 0, 4)),
                  pl.BlockSpec((d, 2 * LANES), lambda i: (0, 0)),
                  pl.BlockSpec((1, LANES), lambda i: (0, 0))],
        out_specs=[pl.BlockSpec((tm * ROW_SUB, LANES), lambda i: (i, 0)),
                   pl.BlockSpec((tm, LANES), lambda i: (i, 0)),
                   pl.BlockSpec((1, LANES), lambda i: (0, 0))],
        out_shape=[jax.ShapeDtypeStruct((rows * ROW_SUB, LANES), F32),
                   jax.ShapeDtypeStruct((rows, LANES), F32),
                   jax.ShapeDtypeStruct((1, LANES), F32)],
        scratch_shapes=[pltpu.VMEM((1, LANES), F32)],
        compiler_params=_cparams(("arbitrary",)),
        name="moe_router",
    )(x1, mod_rows, mod_rows, wr, br)


def _expert_ffn_kernel(te_ref, src0_ref, srcn_ref, dst_ref, hm_hbm, wg_ref, wu_ref, wd_ref, out_hbm,
                       xbuf, ybuf, sem_in, sem_out):
    j = pl.program_id(0)
    last = pl.num_programs(0) - 1
    rs = ROW_SUB
    tmx = xbuf.shape[1] // rs
    slot = j % 2
    nslot = 1 - slot

    def row_tile(ref, r):
        return ref.at[pl.ds(pl.multiple_of(r, rs), rs)]

    def gather_all(src_ref, s):
        for r in range(tmx):
            pltpu.make_async_copy(row_tile(hm_hbm, src_ref[0, r]), xbuf.at[s, pl.ds(r * rs, rs)],
                                  sem_in.at[s]).start(priority=r % 2)

    def wait_rows(buf, sem, s):
        pltpu.make_async_copy(buf.at[s], buf.at[s], sem.at[s]).wait()

    @pl.when(j == 0)
    def _():
        gather_all(src0_ref, 0)

    wait_rows(xbuf, sem_in, slot)

    @pl.when(j >= 2)
    def _():
        wait_rows(ybuf, sem_out, slot)

    gather_all(srcn_ref, nslot)
    x = _load_row_tiles(xbuf.at[slot], tmx).astype(BF16)
    a = jnp.dot(x, wg_ref[...], preferred_element_type=F32)
    b = jnp.dot(x, wu_ref[...], preferred_element_type=F32)
    hid = (_silu(a) * b).astype(BF16)
    _store_row_tiles(ybuf.at[slot], jnp.dot(hid, wd_ref[...], preferred_element_type=F32))
    for r in range(tmx):
        pltpu.make_async_copy(ybuf.at[slot, pl.ds(r * rs, rs)], row_tile(out_hbm, dst_ref[0, r]),
                              sem_out.at[slot]).start(priority=r % 2)

    @pl.when(j == last)
    def _():
        wait_rows(xbuf, sem_in, nslot)
        wait_rows(ybuf, sem_out, slot)

        @pl.when(j >= 1)
        def _():
            wait_rows(ybuf, sem_out, nslot)


def _expert_ffn(hm, tile_expert, src_tok, dst_row, out_rows, w_gate, w_up, w_down):
    n_tiles, tmx = src_tok.shape
    src_tok = (src_tok * ROW_SUB).reshape(n_tiles, 1, tmx)
    dst_row = (dst_row * ROW_SUB).reshape(n_tiles, 1, tmx)
    d, de = w_gate.shape[-2:]
    smem = lambda f: pl.BlockSpec((None, 1, tmx), f, memory_space=pltpu.SMEM)
    grid_spec = pltpu.PrefetchScalarGridSpec(
        num_scalar_prefetch=1,
        grid=(n_tiles,),
        in_specs=[
            smem(lambda j, te: (0, 0, 0)),
            smem(lambda j, te: (jnp.minimum(j + 1, n_tiles - 1), 0, 0)),
            smem(lambda j, te: (j, 0, 0)),
            pl.BlockSpec(memory_space=pl.ANY),
            pl.BlockSpec((None, d, de), lambda j, te: (te[j], 0, 0)),
            pl.BlockSpec((None, d, de), lambda j, te: (te[j], 0, 0)),
            pl.BlockSpec((None, de, d), lambda j, te: (te[j], 0, 0)),
        ],
        out_specs=pl.BlockSpec(memory_space=pl.ANY),
        scratch_shapes=[pltpu.VMEM((2, tmx * ROW_SUB, LANES), F32), pltpu.VMEM((2, tmx * ROW_SUB, LANES), F32),
                        pltpu.SemaphoreType.DMA((2,)), pltpu.SemaphoreType.DMA((2,))],
    )
    return pl.pallas_call(
        _expert_ffn_kernel,
        grid_spec=grid_spec,
        out_shape=jax.ShapeDtypeStruct((out_rows * ROW_SUB, LANES), F32),
        compiler_params=_cparams(("arbitrary",)),
        name="expert_ffn",
    )(tile_expert, src_tok, src_tok, dst_row, hm, w_gate, w_up, w_down)


def _moe_out_kernel(nb, y_ref, info_ref, x_ref, gt_ref, g_ref, b_ref, o_ref):
    info = info_ref[...]
    tm = info.shape[0]
    f = (_load_row_tiles(y_ref.at[0], tm) * info[:, INFO_W0:INFO_W0 + 1]
         + _load_row_tiles(y_ref.at[1], tm) * info[:, INFO_W1:INFO_W1 + 1])
    r = DEEP_ALPHA * x_ref[...] + _per_batch(f, gt_ref[...], nb) * f
    o_ref[...] = _layer_norm(r) * g_ref[...] + b_ref[...]


def _moe_out(y2, info, x1, mod_rows, ln_g, ln_b, nb, ctx_rows):
    rows, d = x1.shape
    tm = ROW_TILE
    lat = lambda i: jnp.where(i * tm >= ctx_rows, 1, 0)
    vec = pl.BlockSpec((1, d), lambda i: (0, 0))
    return pl.pallas_call(
        functools.partial(_moe_out_kernel, nb),
        grid=(rows // tm,),
        in_specs=[pl.BlockSpec((2, tm * ROW_SUB, LANES), lambda i: (0, i, 0)),
                  pl.BlockSpec((tm, LANES), lambda i: (i, 0)),
                  pl.BlockSpec((tm, d), lambda i: (i, 0)),
                  pl.BlockSpec((None, nb, d), lambda i: (lat(i), 0, 5)),
                  vec, vec],
        out_specs=pl.BlockSpec((tm, d), lambda i: (i, 0)),
        out_shape=jax.ShapeDtypeStruct((rows, d), F32),
        compiler_params=_cparams(("parallel",)),
        name="moe_out",
    )(y2, info, x1, mod_rows, ln_g, ln_b)


def _routing_tables(info, counts, n_tiles, tmx):
    rows = info.shape[0]
    plane_rows = rows + tmx
    cnt = counts[0, :N_EXPERTS].astype(jnp.int32)
    padded = ((cnt + tmx - 1) // tmx) * tmx
    ends = jnp.cumsum(padded)
    starts = ends - padded
    eid = info[:, INFO_E0:INFO_E1 + 1].astype(jnp.int32)
    rank = info[:, INFO_R0:INFO_R1 + 1].astype(jnp.int32)
    dest = (starts[eid] + rank).reshape(-1)
    pair_of_row = jnp.full((n_tiles * tmx,), -1, jnp.int32).at[dest].set(jnp.arange(2 * rows, dtype=jnp.int32))
    pair_of_row = pair_of_row.reshape(n_tiles, tmx)
    real = pair_of_row >= 0
    tok = pair_of_row >> 1
    tile = jnp.arange(n_tiles, dtype=jnp.int32)[:, None]
    r_in_tile = jnp.arange(tmx, dtype=jnp.int32)[None, :]
    src_tok = jnp.where(real, tok, 0)
    dst_row = jnp.where(real, (pair_of_row & 1) * plane_rows + tok, (tile % 2) * plane_rows + rows + r_in_tile)
    tile_start = jnp.arange(n_tiles, dtype=jnp.int32) * tmx
    n_before = jnp.sum((ends[None, :] <= tile_start[:, None]).astype(jnp.int32), axis=1)
    tile_expert = jnp.minimum(n_before, N_EXPERTS - 1)
    return tile_expert, src_tok, dst_row, plane_rows


def _relayout_w_in(w_in):
    qkvz = w_in[..., :4 * DN_WIDTH]
    gates = w_in[..., 4 * DN_WIDTH:4 * DN_WIDTH + 4 * DN_HEADS]
    rxy = w_in[..., 4 * DN_WIDTH + 4 * DN_HEADS:]
    pad = jnp.zeros(w_in.shape[:-1] + (LANES - 4 * DN_HEADS,), w_in.dtype)
    return jnp.concatenate([qkvz, rxy, gates, pad], axis=-1).astype(BF16)


def _block_diag(w):
    n, c, _ = w.shape
    eye = jnp.eye(n, dtype=w.dtype)
    return (eye[:, None, :, None] * w[:, :, None, :]).reshape(n * c, n * c)


def _lane_vec(vals, offset):
    return jnp.zeros((1, LANES), F32).at[0, offset:offset + vals.size].set(vals.reshape(-1))


def kernel(x, c, ctx, c_ctx, w_ada, b_ada, w_in, conv_qkv_w, dn_a_log, dn_dt_bias, dn_onorm_w, rg_conv_w, rg_conv_b, rg_wa, rg_ba, rg_wi, rg_bi, rg_lambda, w_out, ln_g, ln_b, router_wg, router_bg, router_we, router_be, w_e_gate, w_e_up, w_e_down):
    nb, t_lat, d = x.shape
    t_ctx = ctx.shape[1]
    assert d == D_MODEL and nb % 8 == 0 and t_lat % GRID_W == 0 and t_ctx % CHUNK == 0
    assert (t_ctx * nb) % ROW_TILE == 0 and (t_lat * nb) % ROW_TILE == 0
    tt = t_ctx + t_lat
    rows = tt * nb
    ctx_rows = t_ctx * nb
    n_ctx_chunks = t_ctx // CHUNK

    h = jnp.concatenate([jnp.swapaxes(ctx, 0, 1), jnp.swapaxes(x, 0, 1)], axis=0).reshape(rows, d)

    n_cc = ((nb + 1 + 7) // 8) * 8
    cc = jnp.zeros((n_cc, d), F32).at[:nb].set(c).at[nb].set(c_ctx)
    mod = _modulation(cc, w_ada, b_ada)
    mod_ctx = jnp.broadcast_to(mod[:, nb:nb + 1], (DEPTH, nb, 6 * d))
    mod_rows = jnp.stack([mod_ctx, mod[:, :nb]], axis=1)

    w_in2 = _relayout_w_in(w_in)
    for l in range(DEPTH):
        row0 = ctx_rows if l == DEPTH - 1 else 0
        moe_ctx_rows = ctx_rows - row0
        n_tiles = (2 * (rows - row0)) // MOE_TILE + N_EXPERTS
        mr = mod_rows[l]
        proj = _in_projection(h, mr, w_in2[l], nb, ctx_rows)
        qkv = _qkv_prep(proj, conv_qkv_w[l], nb, n_ctx_chunks)
        wg = jnp.concatenate([_block_diag(rg_wa[l, 0]), _block_diag(rg_wi[l, 0]),
                              _block_diag(rg_wa[l, 1]), _block_diag(rg_wi[l, 1])], axis=1).astype(BF16)
        bg = jnp.concatenate([rg_ba[l, 0], rg_bi[l, 0], rg_ba[l, 1], rg_bi[l, 1]])[None]
        hf, hb, gy = _rglru(proj, rg_conv_w[l], rg_conv_b[l][None], wg, bg, rg_lambda[l], nb, n_ctx_chunks)
        alog_vec = _lane_vec(dn_a_log[l], 2 * DN_HEADS)
        dtb_vec = _lane_vec(dn_dt_bias[l], 2 * DN_HEADS)
        o_f, o_b = _deltanet(qkv, proj, alog_vec, dtb_vec, nb, n_ctx_chunks)
        x1 = _mixer_out(o_f, o_b, proj, hf, hb, gy, dn_onorm_w[l][None], w_out[l].astype(BF16), h, mr,
                        ln_g[l, 0][None], ln_b[l, 0][None], nb, ctx_rows, row0)
        wr = jnp.zeros((d, LANES), F32).at[:, :N_GROUPS].set(router_wg[l]).at[
            :, N_GROUPS:N_GROUPS + N_EXPERTS].set(router_we[l])
        br = _lane_vec(jnp.concatenate([router_bg[l], router_be[l]]), 0)
        wr_hi = wr.astype(BF16)
        wr_hl = jnp.concatenate([wr_hi, (wr - wr_hi.astype(F32)).astype(BF16)], axis=1)
        hm, info, counts = _router(x1, mr, wr_hl, br, nb, moe_ctx_rows)
        tile_expert, src_tok, dst_row, plane_rows = _routing_tables(info, counts, n_tiles, MOE_TILE)
        y2 = _expert_ffn(hm, tile_expert, src_tok, dst_row, 2 * plane_rows,
                         w_e_gate[l].astype(BF16), w_e_up[l].astype(BF16), w_e_down[l].astype(BF16))
        h = _moe_out(y2.reshape(2, plane_rows * ROW_SUB, LANES), info, x1, mr, ln_g[l, 1][None], ln_b[l, 1][None],
                     nb, moe_ctx_rows)

    return jnp.swapaxes(h.reshape(t_lat, nb, d), 0, 1)
```

```python
import functools
import math

import jax
import jax.numpy as jnp
from jax import lax
from jax.experimental import pallas as pl
from jax.experimental.pallas import tpu as pltpu

F32 = jnp.float32
BF16 = jnp.bfloat16
HIGHEST = lax.Precision.HIGHEST

D_MODEL = 1024
DEPTH = 2
GRID_W = 64
DN_HEAD_DIM = 128
DN_WIDTH = 512
DN_HEADS = 4
RG_WIDTH = 512
RG_BLOCKS = 8
RG_BLOCK_DIM = 64
RG_C = 8.0
CONV_K = 4
N_GROUPS = 4
EXPERTS_PER_GROUP = 8
N_EXPERTS = 32
D_EXPERT = 256
DEEP_ALPHA = (2 * DEPTH) ** 0.25
LN_EPS = 1e-5
NORM_EPS = 1e-6

LANES = 128
CHUNK = 64
SOLVE_BLOCK = 16
DN_BATCHES = 2
D_PROJ = 3 * DN_WIDTH + DN_WIDTH + 2 * RG_WIDTH + LANES
COL_Z = 3
COL_RX = 4
COL_RY = 5
COL_GATE = (3 * DN_WIDTH + DN_WIDTH + 2 * RG_WIDTH) // LANES
ROW_TILE = 512
MOE_TILE = 256
VMEM_LIMIT = 56 * 1024 * 1024


def _cparams(sem):
    return pltpu.CompilerParams(dimension_semantics=sem, vmem_limit_bytes=VMEM_LIMIT)


def _layer_norm(x):
    mu = jnp.mean(x, axis=-1, keepdims=True)
    xc = x - mu
    var = jnp.mean(xc * xc, axis=-1, keepdims=True)
    return xc * lax.rsqrt(var + LN_EPS)


def _per_batch(x, v, nb):
    rows, c = x.shape
    return jnp.broadcast_to(v[None], (rows // nb, nb, c)).reshape(rows, c)


def _sigmoid(x):
    return 1.0 / (1.0 + jnp.exp(-x))


def _silu(x):
    return x * _sigmoid(x)


def _mod_kernel(cc_ref, w_ref, b_ref, o_ref):
    s = _silu(cc_ref[...])
    o_ref[...] = jnp.dot(s, w_ref[...], precision=HIGHEST, preferred_element_type=F32) + b_ref[...]


def _modulation(cc, w_ada, b_ada):
    depth, d, n6 = w_ada.shape
    rows = cc.shape[0]
    tn = 1536
    return pl.pallas_call(
        _mod_kernel,
        grid=(depth, n6 // tn),
        in_specs=[
            pl.BlockSpec((rows, d), lambda l, j: (0, 0)),
            pl.BlockSpec((None, d, tn), lambda l, j: (l, 0, j)),
            pl.BlockSpec((None, 1, tn), lambda l, j: (l, 0, j)),
        ],
        out_specs=pl.BlockSpec((None, rows, tn), lambda l, j: (l, 0, j)),
        out_shape=jax.ShapeDtypeStruct((depth, rows, n6), F32),
        compiler_params=_cparams(("parallel", "parallel")),
        name="adaln_mod",
    )(cc, w_ada, b_ada.reshape(depth, 1, n6))


def _inproj_kernel(nb, h_ref, sh_ref, sc_ref, w_ref, o_ref):
    xn = _layer_norm(h_ref[...])
    hm = xn * (1.0 + _per_batch(xn, sc_ref[...], nb)) + _per_batch(xn, sh_ref[...], nb)
    o_ref[...] = jnp.dot(hm.astype(BF16), w_ref[...], preferred_element_type=F32)


def _in_projection(h, mod_rows, w_in2, nb, ctx_rows):
    rows, d = h.shape
    tm = ROW_TILE
    lat = lambda i: jnp.where(i * tm >= ctx_rows, 1, 0)
    return pl.pallas_call(
        functools.partial(_inproj_kernel, nb),
        grid=(rows // tm,),
        in_specs=[
            pl.BlockSpec((tm, d), lambda i: (i, 0)),
            pl.BlockSpec((None, nb, d), lambda i: (lat(i), 0, 0)),
            pl.BlockSpec((None, nb, d), lambda i: (lat(i), 0, 1)),
            pl.BlockSpec((d, D_PROJ), lambda i: (0, 0)),
        ],
        out_specs=pl.BlockSpec((tm, D_PROJ), lambda i: (i, 0)),
        out_shape=jax.ShapeDtypeStruct((rows, D_PROJ), F32),
        compiler_params=_cparams(("parallel",)),
        name="ln_mod_inproj",
    )(h, mod_rows, mod_rows, w_in2)


def _conv_tile(x, prev, nxt, w, nb, use_prev, use_next):
    prev = jnp.where(use_prev, prev, 0.0)
    nxt = jnp.where(use_next, nxt, 0.0)
    xm2 = jnp.concatenate([prev, x[: -2 * nb]], axis=0)
    xm1 = jnp.concatenate([prev[nb:], x[:-nb]], axis=0)
    xp1 = jnp.concatenate([x[nb:], nxt], axis=0)
    return xm2 * w[0:1] + xm1 * w[1:2] + x * w[2:3] + xp1 * w[3:4]


def _halo_flags(i, n_ctx_tiles):
    use_prev = jnp.logical_and(i > 0, i < n_ctx_tiles)
    use_next = i < n_ctx_tiles - 1
    return use_prev, use_next


def _qkv_prep_kernel(nb, n_ctx_tiles, x_ref, prev_ref, next_ref, w_ref, o_ref):
    i = pl.program_id(0)
    part = pl.program_id(1)
    use_prev, use_next = _halo_flags(i, n_ctx_tiles)
    y = _silu(_conv_tile(x_ref[...], prev_ref[...], next_ref[...], w_ref[...], nb, use_prev, use_next))
    qscale = jnp.where(part == 0, DN_HEAD_DIM ** -0.5, 1.0).astype(F32)
    for hh in range(DN_HEADS):
        sl = slice(hh * DN_HEAD_DIM, (hh + 1) * DN_HEAD_DIM)
        yh = y[:, sl]
        inv = lax.rsqrt(jnp.sum(yh * yh, axis=-1, keepdims=True) + NORM_EPS) * qscale
        o_ref[hh] = yh * jnp.where(part < 2, inv, 1.0)


def _halo_specs(tile_rows, nb, cols, col_of, n_row_blocks16, tile_of=lambda i: i):
    prev_b = tile_rows // (2 * nb)
    next_b = tile_rows // nb
    return [
        pl.BlockSpec((tile_rows, cols), lambda i, *a: (tile_of(i), col_of(*a))),
        pl.BlockSpec((2 * nb, cols), lambda i, *a: (jnp.maximum(tile_of(i) * prev_b - 1, 0), col_of(*a))),
        pl.BlockSpec((nb, cols),
                     lambda i, *a: (jnp.minimum((tile_of(i) + 1) * next_b, n_row_blocks16 - 1), col_of(*a))),
    ]


def _qkv_prep(proj, conv_w, nb, n_ctx_tiles):
    rows = proj.shape[0]
    tr = CHUNK * nb
    return pl.pallas_call(
        functools.partial(_qkv_prep_kernel, nb, n_ctx_tiles),
        grid=(rows // tr, 3),
        in_specs=_halo_specs(tr, nb, DN_WIDTH, lambda p: p, rows // nb)
        + [pl.BlockSpec((CONV_K, DN_WIDTH), lambda i, p: (0, p))],
        out_specs=pl.BlockSpec((DN_HEADS, tr, DN_HEAD_DIM), lambda i, p: (p, i, 0)),
        out_shape=jax.ShapeDtypeStruct((3 * DN_HEADS, rows, DN_HEAD_DIM), F32),
        compiler_params=_cparams(("parallel", "parallel")),
        name="qkv_conv_norm",
    )(proj, proj, proj, conv_w)


def _bwd_chunk(i, n_ctx, n_tot):
    return jnp.where(i < n_ctx, n_ctx - 1 - i, n_tot + n_ctx - 1 - i)


def _rglru_kernel(nb, n_ctx_tiles, n_tot, xf_ref, pf_ref, nf_ref, xb_ref, pb_ref, nb_ref, ry_ref, cw_ref, cb_ref,
                  wg0_ref, wg1_ref, bg0_ref, bg1_ref, lam_ref, hf_ref, hb_ref, gy_ref, a_scr, b_scr, st_ref):
    i = pl.program_id(0)

    @pl.when(i == 0)
    def _():
        st_ref[...] = jnp.zeros_like(st_ref)

    w = RG_WIDTH
    sp = jax.nn.softplus(-lam_ref[...])
    tiles = (i, _bwd_chunk(i, n_ctx_tiles, n_tot))
    for d, (x_ref, p_ref, n_ref, wg_ref, bg_ref) in enumerate(
            ((xf_ref, pf_ref, nf_ref, wg0_ref, bg0_ref), (xb_ref, pb_ref, nb_ref, wg1_ref, bg1_ref))):
        use_prev, use_next = _halo_flags(tiles[d], n_ctx_tiles)
        xc = _conv_tile(x_ref[...], p_ref[...], n_ref[...], cw_ref[...], nb, use_prev, use_next) + cb_ref[...]
        gates = jnp.dot(xc.astype(BF16), wg_ref[...], preferred_element_type=F32) + bg_ref[...]
        r = _sigmoid(gates[:, :w])
        ig = _sigmoid(gates[:, w:])
        a = jnp.exp(-RG_C * r * sp[d:d + 1])
        y = jnp.maximum(1.0 - a * a, 0.0)
        a_scr[d] = a
        b_scr[d] = jnp.where(y > 0.0, y * lax.rsqrt(y), 0.0) * (ig * xc)
    gy_ref[...] = jax.nn.gelu(ry_ref[...])

    def body(t, carry):
        h0, h1 = carry
        r0 = pl.ds(pl.multiple_of(t * nb, nb), nb)
        r1 = pl.ds(pl.multiple_of((CHUNK - 1 - t) * nb, nb), nb)
        h0 = a_scr[0, r0, :] * h0 + b_scr[0, r0, :]
        h1 = a_scr[1, r1, :] * h1 + b_scr[1, r1, :]
        hf_ref[r0, :] = h0
        hb_ref[r1, :] = h1
        return h0, h1

    h0, h1 = lax.fori_loop(0, CHUNK, body, (st_ref[0], st_ref[1]), unroll=8)
    st_ref[0] = h0
    st_ref[1] = h1


def _rglru(proj, conv_w, conv_b, wg, bg, lam, nb, n_ctx_tiles):
    rows = proj.shape[0]
    tr = CHUNK * nb
    n_tot = rows // tr
    w = RG_WIDTH
    out = jax.ShapeDtypeStruct((rows, w), F32)
    fwd = lambda i: i
    bwd = lambda i: _bwd_chunk(i, n_ctx_tiles, n_tot)
    full = lambda shape: pl.BlockSpec(shape, lambda i: (0,) * len(shape))
    return pl.pallas_call(
        functools.partial(_rglru_kernel, nb, n_ctx_tiles, n_tot),
        grid=(n_tot,),
        in_specs=_halo_specs(tr, nb, w, lambda: COL_RX, rows // nb, fwd)
        + _halo_specs(tr, nb, w, lambda: COL_RX, rows // nb, bwd)
        + [pl.BlockSpec((tr, w), lambda i: (i, COL_RY)), full((CONV_K, w)), full((1, w)),
           pl.BlockSpec((w, 2 * w), lambda i: (0, 0)), pl.BlockSpec((w, 2 * w), lambda i: (0, 1)),
           pl.BlockSpec((1, 2 * w), lambda i: (0, 0)), pl.BlockSpec((1, 2 * w), lambda i: (0, 1)),
           full((2, w))],
        out_specs=[pl.BlockSpec((tr, w), lambda i: (fwd(i), 0)), pl.BlockSpec((tr, w), lambda i: (bwd(i), 0)),
                   pl.BlockSpec((tr, w), lambda i: (i, 0))],
        out_shape=[out] * 3,
        scratch_shapes=[pltpu.VMEM((2, tr, w), F32), pltpu.VMEM((2, tr, w), F32), pltpu.VMEM((2, nb, w), F32)],
        compiler_params=_cparams(("arbitrary",)),
        name="rglru",
    )(proj, proj, proj, proj, proj, proj, proj, conv_w, conv_b, wg, wg, bg, bg, lam)


def _bdot(a, b):
    return jnp.dot(a.astype(BF16), b.astype(BF16), preferred_element_type=F32)


def _deltanet_gates(d, raw, alog, dtb):
    c = CHUNK
    row = lax.broadcasted_iota(jnp.int32, (c, c), 0)
    col = lax.broadcasted_iota(jnp.int32, (c, c), 1)
    incl = (row >= col) if d == 0 else (row <= col)
    g_all = -jnp.exp(alog) * jax.nn.softplus(raw + dtb)
    gc_all = jnp.dot(incl.astype(F32), g_all, precision=HIGHEST, preferred_element_type=F32)
    g_tot = jnp.sum(g_all, axis=0, keepdims=True)
    return dict(incl=incl, strict=(row > col) if d == 0 else (row < col),
                diag_blk=(row // SOLVE_BLOCK) == (col // SOLVE_BLOCK),
                beta=_sigmoid(raw), gc=gc_all, gc_t=gc_all.T, egc=jnp.exp(gc_all),
                kdec=jnp.exp(g_tot - gc_all), glast=jnp.exp(g_tot))


def _deltanet_heads(gts, cbs, qs, ks, vs, ss):
    c = CHUNK
    dh = DN_HEAD_DIM
    n = len(cbs)
    rng = range(n)
    cgs = [2 * DN_HEADS + cb for cb in cbs]
    beta = [gts[i]["beta"][:, cbs[i]:cbs[i] + 1] for i in rng]
    egc = [gts[i]["egc"][:, cgs[i]:cgs[i] + 1] for i in rng]
    decay = [jnp.exp(jnp.where(gts[i]["incl"], gts[i]["gc"][:, cgs[i]:cgs[i] + 1] - gts[i]["gc_t"][cgs[i]:cgs[i] + 1, :],
                               -1e30)) for i in rng]
    kbeta = [ks[i] * beta[i] for i in rng]
    prod = [lax.dot_general(jnp.concatenate([kbeta[i], qs[i]], axis=0).astype(BF16), ks[i].astype(BF16),
                            (((1,), (1,)), ((), ())), preferred_element_type=F32) for i in rng]
    a_mat = [jnp.where(gts[i]["strict"], prod[i][:c] * decay[i], 0.0) for i in rng]
    intra = [prod[i][c:] * decay[i] for i in rng]
    p = [jnp.where(gts[i]["diag_blk"], -a_mat[i], 0.0) for i in rng]
    y = [jnp.concatenate([vs[i] * beta[i], kbeta[i] * egc[i], jnp.where(gts[i]["diag_blk"], 0.0, a_mat[i])], axis=1)
         for i in rng]
    for j in range(4):
        y = [y[i] + _bdot(p[i], y[i]) for i in rng]
        if j < 3:
            p = [_bdot(p[i], p[i]) for i in rng]
    e = [y[i][:, 2 * dh:] for i in rng]
    y = [y[i][:, :2 * dh] for i in rng]
    e2 = [_bdot(e[i], e[i]) for i in rng]
    y = [y[i] + _bdot(e2[i], y[i]) for i in rng]
    y = [y[i] - _bdot(e[i], y[i]) for i in rng]
    ws_qs = [_bdot(jnp.concatenate([y[i][:, dh:], qs[i] * egc[i]], axis=0), ss[i]) for i in rng]
    v_new = [y[i][:, :dh] - ws_qs[i][:c] for i in rng]
    o = [ws_qs[i][c:] + _bdot(intra[i], v_new[i]) for i in rng]
    s_new = [ss[i] * gts[i]["glast"][:, cgs[i]:cgs[i] + 1] + lax.dot_general(
        (ks[i] * gts[i]["kdec"][:, cgs[i]:cgs[i] + 1]).astype(BF16), v_new[i].astype(BF16),
        (((0,), (0,)), ((), ())), preferred_element_type=F32) for i in rng]
    return o, s_new


def _deltanet_kernel(nb, qf_ref, kf_ref, vf_ref, gf_ref, qb_ref, kb_ref, vb_ref, gb_ref, alog_ref, dtb_ref,
                     of_ref, ob_ref, s_ref):
    @pl.when(pl.program_id(0) == 0)
    def _():
        s_ref[...] = jnp.zeros_like(s_ref)

    dirs = ((qf_ref, kf_ref, vf_ref, gf_ref, of_ref), (qb_ref, kb_ref, vb_ref, gb_ref, ob_ref))

    def body(it, carry):
        idx = [(it * DN_BATCHES + j, d, hh) for j in range(DN_BATCHES) for d in range(2) for hh in range(DN_HEADS)]
        rows_of = {j: pl.ds(it * DN_BATCHES + j, CHUNK, stride=nb) for j in range(DN_BATCHES)}
        rows_b = [rows_of[i // (2 * DN_HEADS)] for i in range(len(idx))]
        cbs = [d * DN_HEADS + hh for _, d, hh in idx]
        gates = {(j, d): _deltanet_gates(d, dirs[d][3][rows_of[j], :], alog_ref[...], dtb_ref[...])
                 for j in range(DN_BATCHES) for d in range(2)}
        gts = [gates[(i // (2 * DN_HEADS), d)] for i, (_, d, _) in enumerate(idx)]
        qs = [dirs[d][0][hh, rows_b[i], :] for i, (_, d, hh) in enumerate(idx)]
        ks = [dirs[d][1][hh, rows_b[i], :] for i, (_, d, hh) in enumerate(idx)]
        vs = [dirs[d][2][hh, rows_b[i], :] for i, (_, d, hh) in enumerate(idx)]
        ss = [s_ref[b * (2 * DN_HEADS) + cbs[i]] for i, (b, _, _) in enumerate(idx)]
        o, s_new = _deltanet_heads(gts, cbs, qs, ks, vs, ss)
        for i, (b, d, hh) in enumerate(idx):
            dirs[d][4][hh, rows_b[i], :] = o[i]
            s_ref[b * (2 * DN_HEADS) + cbs[i]] = s_new[i]
        return carry

    lax.fori_loop(0, nb // DN_BATCHES, body, 0)


def _deltanet(qkv, proj, alog_vec, dtb_vec, nb, n_ctx_chunks):
    rows = qkv.shape[1]
    tr = CHUNK * nb
    n_tot = rows // tr
    fwd = lambda i: i
    bwd = lambda i: _bwd_chunk(i, n_ctx_chunks, n_tot)

    def specs(order):
        return [pl.BlockSpec((DN_HEADS, tr, DN_HEAD_DIM), lambda i, p=p: (p, order(i), 0)) for p in range(3)] + [
            pl.BlockSpec((tr, LANES), lambda i: (order(i), COL_GATE))]

    vec = pl.BlockSpec((1, LANES), lambda i: (0, 0))
    out = jax.ShapeDtypeStruct((DN_HEADS, rows, DN_HEAD_DIM), F32)
    return pl.pallas_call(
        functools.partial(_deltanet_kernel, nb),
        grid=(n_tot,),
        in_specs=specs(fwd) + specs(bwd) + [vec, vec],
        out_specs=[pl.BlockSpec((DN_HEADS, tr, DN_HEAD_DIM), lambda i: (0, fwd(i), 0)),
                   pl.BlockSpec((DN_HEADS, tr, DN_HEAD_DIM), lambda i: (0, bwd(i), 0))],
        out_shape=[out, out],
        scratch_shapes=[pltpu.VMEM((nb * 2 * DN_HEADS, DN_HEAD_DIM, DN_HEAD_DIM), F32)],
        compiler_params=_cparams(("arbitrary",)),
        name="deltanet",
    )(qkv, qkv, qkv, proj, qkv, qkv, qkv, proj, alog_vec, dtb_vec)


def _mixer_out_kernel(nb, of_ref, ob_ref, z_ref, hf_ref, hb_ref, gy_ref, onw_ref, w_ref, h_ref, gt_ref, g_ref, b_ref,
                      o_ref):
    z = z_ref[...]
    parts = []
    for hh in range(DN_HEADS):
        sl = slice(hh * DN_HEAD_DIM, (hh + 1) * DN_HEAD_DIM)
        oh = of_ref[hh] + ob_ref[hh]
        inv = lax.rsqrt(jnp.mean(oh * oh, axis=-1, keepdims=True) + NORM_EPS)
        parts.append(oh * inv * onw_ref[...] * _silu(z[:, sl]))
    parts.append((hf_ref[...] + hb_ref[...]) * gy_ref[...])
    y = jnp.concatenate(parts, axis=1).astype(BF16)
    u = jnp.dot(y, w_ref[...], preferred_element_type=F32)
    r = DEEP_ALPHA * h_ref[...] + _per_batch(u, gt_ref[...], nb) * u
    o_ref[...] = _layer_norm(r) * g_ref[...] + b_ref[...]


def _mixer_out(o_f, o_b, proj, hf, hb, gy, onorm_w, w_out, h, mod_rows, ln_g, ln_b, nb, ctx_rows, row0):
    rows, d = h.shape
    tm = ROW_TILE
    t0 = row0 // tm
    lat = lambda i: jnp.where((i + t0) * tm >= ctx_rows, 1, 0)
    half = pl.BlockSpec((tm, DN_WIDTH), lambda i: (i + t0, 0))
    planes = pl.BlockSpec((DN_HEADS, tm, DN_HEAD_DIM), lambda i: (0, i + t0, 0))
    vec = lambda n: pl.BlockSpec((1, n), lambda i: (0, 0))
    return pl.pallas_call(
        functools.partial(_mixer_out_kernel, nb),
        grid=((rows - row0) // tm,),
        in_specs=[planes, planes, pl.BlockSpec((tm, DN_WIDTH), lambda i: (i + t0, COL_Z)), half, half, half,
                  vec(DN_HEAD_DIM), pl.BlockSpec((d, d), lambda i: (0, 0)),
                  pl.BlockSpec((tm, d), lambda i: (i + t0, 0)),
                  pl.BlockSpec((None, nb, d), lambda i: (lat(i), 0, 2)),
                  vec(d), vec(d)],
        out_specs=pl.BlockSpec((tm, d), lambda i: (i, 0)),
        out_shape=jax.ShapeDtypeStruct((rows - row0, d), F32),
        compiler_params=_cparams(("parallel",)),
        name="mixer_out",
    )(o_f, o_b, proj, hf, hb, gy, onorm_w, w_out, h, mod_rows, ln_g, ln_b)


INFO_E0, INFO_E1, INFO_W0, INFO_W1, INFO_R0, INFO_R1 = range(6)


ROW_SUB = D_MODEL // LANES


def _store_row_tiles(ref, x):
    n = x.shape[0]
    for c in range(ROW_SUB):
        ref[pl.ds(c, n, stride=ROW_SUB), :] = x[:, c * LANES:(c + 1) * LANES]


def _load_row_tiles(ref, n):
    return jnp.concatenate([ref[pl.ds(c, n, stride=ROW_SUB), :] for c in range(ROW_SUB)], axis=1)


def _first_index(mask, lane):
    return jnp.min(jnp.where(mask, lane, LANES), axis=-1, keepdims=True)


def _router_kernel(nb, x_ref, sh_ref, sc_ref, wr_ref, br_ref, hm_ref, info_ref, cnt_ref, run_ref):
    @pl.when(pl.program_id(0) == 0)
    def _():
        run_ref[...] = jnp.zeros_like(run_ref)

    xn = _layer_norm(x_ref[...])
    hm = xn * (1.0 + _per_batch(xn, sc_ref[...], nb)) + _per_batch(xn, sh_ref[...], nb)
    _store_row_tiles(hm_ref, hm)
    tm = hm.shape[0]
    h_hi = hm.astype(BF16)
    h_lo = (hm - h_hi.astype(F32)).astype(BF16)
    hi_hl = jnp.dot(h_hi, wr_ref[...], preferred_element_type=F32)
    logits = (hi_hl[:, :LANES] + hi_hl[:, LANES:] + jnp.dot(h_lo, wr_ref[:, :LANES], preferred_element_type=F32)
              + br_ref[...])
    lane = lax.broadcasted_iota(jnp.int32, (tm, LANES), 1)
    neg = -jnp.inf
    is_g = lane < N_GROUPS
    gmax = jnp.max(jnp.where(is_g, logits, neg), axis=-1, keepdims=True)
    g_sel = _first_index(jnp.logical_and(is_g, logits == gmax), lane)
    p_group = 1.0 / jnp.sum(jnp.where(is_g, jnp.exp(logits - gmax), 0.0), axis=-1, keepdims=True)
    lo = N_GROUPS + EXPERTS_PER_GROUP * g_sel
    in_grp = jnp.logical_and(lane >= lo, lane < lo + EXPERTS_PER_GROUP)
    m1 = jnp.max(jnp.where(in_grp, logits, neg), axis=-1, keepdims=True)
    i1 = _first_index(jnp.logical_and(in_grp, logits == m1), lane)
    rest = jnp.logical_and(in_grp, lane != i1)
    m2 = jnp.max(jnp.where(rest, logits, neg), axis=-1, keepdims=True)
    i2 = _first_index(jnp.logical_and(rest, logits == m2), lane)
    e2 = jnp.exp(m2 - m1)
    w1 = p_group / (1.0 + e2)
    w2 = p_group * e2 / (1.0 + e2)
    e_a = i1 - N_GROUPS
    e_b = i2 - N_GROUPS
    oh_a = lane == e_a
    oh_b = lane == e_b
    oh = jnp.logical_or(oh_a, oh_b).astype(F32)
    r_i = lax.broadcasted_iota(jnp.int32, (tm, tm), 0)
    c_i = lax.broadcasted_iota(jnp.int32, (tm, tm), 1)
    before = jnp.dot((r_i > c_i).astype(BF16), oh.astype(BF16), preferred_element_type=F32) + run_ref[...]
    rank_a = jnp.sum(jnp.where(oh_a, before, 0.0), axis=-1, keepdims=True)
    rank_b = jnp.sum(jnp.where(oh_b, before, 0.0), axis=-1, keepdims=True)
    run_ref[...] = run_ref[...] + jnp.sum(oh, axis=0, keepdims=True)
    cnt_ref[...] = run_ref[...]
    info = jnp.zeros((tm, LANES), F32)
    for idx, val in ((INFO_E0, e_a.astype(F32)), (INFO_E1, e_b.astype(F32)), (INFO_W0, w1), (INFO_W1, w2),
                     (INFO_R0, rank_a), (INFO_R1, rank_b)):
        info = jnp.where(lane == idx, val, info)
    info_ref[...] = info


def _router(x1, mod_rows, wr, br, nb, ctx_rows):
    rows, d = x1.shape
    tm = ROW_TILE
    lat = lambda i: jnp.where(i * tm >= ctx_rows, 1, 0)
    return pl.pallas_call(
        functools.partial(_router_kernel, nb),
        grid=(rows // tm,),
        in_specs=[pl.BlockSpec((tm, d), lambda i: (i, 0)),
                  pl.BlockSpec((None, nb, d), lambda i: (lat(i), 0, 3)),
                  pl.BlockSpec((None, nb, d), lambda i: (lat(i), 0, 4)),
                  pl.BlockSpec((d, 2 * LANES), lambda i: (0, 0)),
                  pl.BlockSpec((1, LANES), lambda i: (0, 0))],
        out_specs=[pl.BlockSpec((tm * ROW_SUB, LANES), lambda i: (i, 0)),
                   pl.BlockSpec((tm, LANES), lambda i: (i, 0)),
                   pl.BlockSpec((1, LANES), lambda i: (0, 0))],
        out_shape=[jax.ShapeDtypeStruct((rows * ROW_SUB, LANES), F32),
                   jax.ShapeDtypeStruct((rows, LANES), F32),
                   jax.ShapeDtypeStruct((1, LANES), F32)],
        scratch_shapes=[pltpu.VMEM((1, LANES), F32)],
        compiler_params=_cparams(("arbitrary",)),
        name="moe_router",
    )(x1, mod_rows, mod_rows, wr, br)


def _expert_ffn_kernel(te_ref, src0_ref, srcn_ref, dst_ref, hm_hbm, wg_ref, wu_ref, wd_ref, out_hbm,
                       xbuf, ybuf, sem_in, sem_out):
    j = pl.program_id(0)
    last = pl.num_programs(0) - 1
    rs = ROW_SUB
    tmx = xbuf.shape[1] // rs
    slot = j % 2
    nslot = 1 - slot

    def row_tile(ref, r):
        return ref.at[pl.ds(pl.multiple_of(r, rs), rs)]

    def gather_all(src_ref, s):
        for r in range(tmx):
            pltpu.make_async_copy(row_tile(hm_hbm, src_ref[0, r]), xbuf.at[s, pl.ds(r * rs, rs)],
                                  sem_in.at[s]).start(priority=r % 2)

    def wait_rows(buf, sem, s):
        pltpu.make_async_copy(buf.at[s], buf.at[s], sem.at[s]).wait()

    @pl.when(j == 0)
    def _():
        gather_all(src0_ref, 0)

    wait_rows(xbuf, sem_in, slot)

    @pl.when(j >= 2)
    def _():
        wait_rows(ybuf, sem_out, slot)

    gather_all(srcn_ref, nslot)
    x = _load_row_tiles(xbuf.at[slot], tmx).astype(BF16)
    a = jnp.dot(x, wg_ref[...], preferred_element_type=F32)
    b = jnp.dot(x, wu_ref[...], preferred_element_type=F32)
    hid = (_silu(a) * b).astype(BF16)
    _store_row_tiles(ybuf.at[slot], jnp.dot(hid, wd_ref[...], preferred_element_type=F32))
    for r in range(tmx):
        pltpu.make_async_copy(ybuf.at[slot, pl.ds(r * rs, rs)], row_tile(out_hbm, dst_ref[0, r]),
                              sem_out.at[slot]).start(priority=r % 2)

    @pl.when(j == last)
    def _():
        wait_rows(xbuf, sem_in, nslot)
        wait_rows(ybuf, sem_out, slot)

        @pl.when(j >= 1)
        def _():
            wait_rows(ybuf, sem_out, nslot)


def _expert_ffn(hm, tile_expert, src_tok, dst_row, out_rows, w_gate, w_up, w_down):
    n_tiles, tmx = src_tok.shape
    src_tok = (src_tok * ROW_SUB).reshape(n_tiles, 1, tmx)
    dst_row = (dst_row * ROW_SUB).reshape(n_tiles, 1, tmx)
    d, de = w_gate.shape[-2:]
    smem = lambda f: pl.BlockSpec((None, 1, tmx), f, memory_space=pltpu.SMEM)
    grid_spec = pltpu.PrefetchScalarGridSpec(
        num_scalar_prefetch=1,
        grid=(n_tiles,),
        in_specs=[
            smem(lambda j, te: (0, 0, 0)),
            smem(lambda j, te: (jnp.minimum(j + 1, n_tiles - 1), 0, 0)),
            smem(lambda j, te: (j, 0, 0)),
            pl.BlockSpec(memory_space=pl.ANY),
            pl.BlockSpec((None, d, de), lambda j, te: (te[j], 0, 0)),
            pl.BlockSpec((None, d, de), lambda j, te: (te[j], 0, 0)),
            pl.BlockSpec((None, de, d), lambda j, te: (te[j], 0, 0)),
        ],
        out_specs=pl.BlockSpec(memory_space=pl.ANY),
        scratch_shapes=[pltpu.VMEM((2, tmx * ROW_SUB, LANES), F32), pltpu.VMEM((2, tmx * ROW_SUB, LANES), F32),
                        pltpu.SemaphoreType.DMA((2,)), pltpu.SemaphoreType.DMA((2,))],
    )
    return pl.pallas_call(
        _expert_ffn_kernel,
        grid_spec=grid_spec,
        out_shape=jax.ShapeDtypeStruct((out_rows * ROW_SUB, LANES), F32),
        compiler_params=_cparams(("arbitrary",)),
        name="expert_ffn",
    )(tile_expert, src_tok, src_tok, dst_row, hm, w_gate, w_up, w_down)


def _moe_out_kernel(nb, y_ref, info_ref, x_ref, gt_ref, g_ref, b_ref, o_ref):
    info = info_ref[...]
    tm = info.shape[0]
    f = (_load_row_tiles(y_ref.at[0], tm) * info[:, INFO_W0:INFO_W0 + 1]
         + _load_row_tiles(y_ref.at[1], tm) * info[:, INFO_W1:INFO_W1 + 1])
    r = DEEP_ALPHA * x_ref[...] + _per_batch(f, gt_ref[...], nb) * f
    o_ref[...] = _layer_norm(r) * g_ref[...] + b_ref[...]


def _moe_out(y2, info, x1, mod_rows, ln_g, ln_b, nb, ctx_rows):
    rows, d = x1.shape
    tm = ROW_TILE
    lat = lambda i: jnp.where(i * tm >= ctx_rows, 1, 0)
    vec = pl.BlockSpec((1, d), lambda i: (0, 0))
    return pl.pallas_call(
        functools.partial(_moe_out_kernel, nb),
        grid=(rows // tm,),
        in_specs=[pl.BlockSpec((2, tm * ROW_SUB, LANES), lambda i: (0, i, 0)),
                  pl.BlockSpec((tm, LANES), lambda i: (i, 0)),
                  pl.BlockSpec((tm, d), lambda i: (i, 0)),
                  pl.BlockSpec((None, nb, d), lambda i: (lat(i), 0, 5)),
                  vec, vec],
        out_specs=pl.BlockSpec((tm, d), lambda i: (i, 0)),
        out_shape=jax.ShapeDtypeStruct((rows, d), F32),
        compiler_params=_cparams(("parallel",)),
        name="moe_out",
    )(y2, info, x1, mod_rows, ln_g, ln_b)


def _routing_tables(info, counts, n_tiles, tmx):
    rows = info.shape[0]
    plane_rows = rows + tmx
    cnt = counts[0, :N_EXPERTS].astype(jnp.int32)
    padded = ((cnt + tmx - 1) // tmx) * tmx
    ends = jnp.cumsum(padded)
    starts = ends - padded
    eid = info[:, INFO_E0:INFO_E1 + 1].astype(jnp.int32)
    rank = info[:, INFO_R0:INFO_R1 + 1].astype(jnp.int32)
    dest = (starts[eid] + rank).reshape(-1)
    pair_of_row = jnp.full((n_tiles * tmx,), -1, jnp.int32).at[dest].set(jnp.arange(2 * rows, dtype=jnp.int32))
    pair_of_row = pair_of_row.reshape(n_tiles, tmx)
    real = pair_of_row >= 0
    tok = pair_of_row >> 1
    tile = jnp.arange(n_tiles, dtype=jnp.int32)[:, None]
    r_in_tile = jnp.arange(tmx, dtype=jnp.int32)[None, :]
    src_tok = jnp.where(real, tok, 0)
    dst_row = jnp.where(real, (pair_of_row & 1) * plane_rows + tok, (tile % 2) * plane_rows + rows + r_in_tile)
    tile_start = jnp.arange(n_tiles, dtype=jnp.int32) * tmx
    n_before = jnp.sum((ends[None, :] <= tile_start[:, None]).astype(jnp.int32), axis=1)
    tile_expert = jnp.minimum(n_before, N_EXPERTS - 1)
    return tile_expert, src_tok, dst_row, plane_rows


def _relayout_w_in(w_in):
    qkvz = w_in[..., :4 * DN_WIDTH]
    gates = w_in[..., 4 * DN_WIDTH:4 * DN_WIDTH + 4 * DN_HEADS]
    rxy = w_in[..., 4 * DN_WIDTH + 4 * DN_HEADS:]
    pad = jnp.zeros(w_in.shape[:-1] + (LANES - 4 * DN_HEADS,), w_in.dtype)
    return jnp.concatenate([qkvz, rxy, gates, pad], axis=-1).astype(BF16)


def _block_diag(w):
    n, c, _ = w.shape
    eye = jnp.eye(n, dtype=w.dtype)
    return (eye[:, None, :, None] * w[:, :, None, :]).reshape(n * c, n * c)


def _lane_vec(vals, offset):
    return jnp.zeros((1, LANES), F32).at[0, offset:offset + vals.size].set(vals.reshape(-1))


def kernel(x, c, ctx, c_ctx, w_ada, b_ada, w_in, conv_qkv_w, dn_a_log, dn_dt_bias, dn_onorm_w, rg_conv_w, rg_conv_b, rg_wa, rg_ba, rg_wi, rg_bi, rg_lambda, w_out, ln_g, ln_b, router_wg, router_bg, router_we, router_be, w_e_gate, w_e_up, w_e_down):
    nb, t_lat, d = x.shape
    t_ctx = ctx.shape[1]
    assert d == D_MODEL and nb % 8 == 0 and t_lat % GRID_W == 0 and t_ctx % CHUNK == 0
    assert (t_ctx * nb) % ROW_TILE == 0 and (t_lat * nb) % ROW_TILE == 0
    tt = t_ctx + t_lat
    rows = tt * nb
    ctx_rows = t_ctx * nb
    n_ctx_chunks = t_ctx // CHUNK

    h = jnp.concatenate([jnp.swapaxes(ctx, 0, 1), jnp.swapaxes(x, 0, 1)], axis=0).reshape(rows, d)

    n_cc = ((nb + 1 + 7) // 8) * 8
    cc = jnp.zeros((n_cc, d), F32).at[:nb].set(c).at[nb].set(c_ctx)
    mod = _modulation(cc, w_ada, b_ada)
    mod_ctx = jnp.broadcast_to(mod[:, nb:nb + 1], (DEPTH, nb, 6 * d))
    mod_rows = jnp.stack([mod_ctx, mod[:, :nb]], axis=1)

    w_in2 = _relayout_w_in(w_in)
    for l in range(DEPTH):
        row0 = ctx_rows if l == DEPTH - 1 else 0
        moe_ctx_rows = ctx_rows - row0
        n_tiles = (2 * (rows - row0)) // MOE_TILE + N_EXPERTS
        mr = mod_rows[l]
        proj = _in_projection(h, mr, w_in2[l], nb, ctx_rows)
        qkv = _qkv_prep(proj, conv_qkv_w[l], nb, n_ctx_chunks)
        wg = jnp.concatenate([_block_diag(rg_wa[l, 0]), _block_diag(rg_wi[l, 0]),
                              _block_diag(rg_wa[l, 1]), _block_diag(rg_wi[l, 1])], axis=1).astype(BF16)
        bg = jnp.concatenate([rg_ba[l, 0], rg_bi[l, 0], rg_ba[l, 1], rg_bi[l, 1]])[None]
        hf, hb, gy = _rglru(proj, rg_conv_w[l], rg_conv_b[l][None], wg, bg, rg_lambda[l], nb, n_ctx_chunks)
        alog_vec = _lane_vec(dn_a_log[l], 2 * DN_HEADS)
        dtb_vec = _lane_vec(dn_dt_bias[l], 2 * DN_HEADS)
        o_f, o_b = _deltanet(qkv, proj, alog_vec, dtb_vec, nb, n_ctx_chunks)
        x1 = _mixer_out(o_f, o_b, proj, hf, hb, gy, dn_onorm_w[l][None], w_out[l].astype(BF16), h, mr,
                        ln_g[l, 0][None], ln_b[l, 0][None], nb, ctx_rows, row0)
        wr = jnp.zeros((d, LANES), F32).at[:, :N_GROUPS].set(router_wg[l]).at[
            :, N_GROUPS:N_GROUPS + N_EXPERTS].set(router_we[l])
        br = _lane_vec(jnp.concatenate([router_bg[l], router_be[l]]), 0)
        wr_hi = wr.astype(BF16)
        wr_hl = jnp.concatenate([wr_hi, (wr - wr_hi.astype(F32)).astype(BF16)], axis=1)
        hm, info, counts = _router(x1, mr, wr_hl, br, nb, moe_ctx_rows)
        tile_expert, src_tok, dst_row, plane_rows = _routing_tables(info, counts, n_tiles, MOE_TILE)
        y2 = _expert_ffn(hm, tile_expert, src_tok, dst_row, 2 * plane_rows,
                         w_e_gate[l].astype(BF16), w_e_up[l].astype(BF16), w_e_down[l].astype(BF16))
        h = _moe_out(y2.reshape(2, plane_rows * ROW_SUB, LANES), info, x1, mr, ln_g[l, 1][None], ln_b[l, 1][None],
                     nb, moe_ctx_rows)

    return jnp.swapaxes(h.reshape(t_lat, nb, d), 0, 1)
```

```python
import functools
import math

import jax
import jax.numpy as jnp
import numpy as np
from jax import lax
from jax.experimental import pallas as pl
from jax.experimental.pallas import tpu as pltpu

F32 = jnp.float32
BF16 = jnp.bfloat16
HIGHEST = lax.Precision.HIGHEST

D_MODEL = 1024
DEPTH = 2
GRID_W = 64
DN_HEAD_DIM = 128
DN_WIDTH = 512
DN_HEADS = 4
RG_WIDTH = 512
RG_BLOCKS = 8
RG_BLOCK_DIM = 64
RG_C = 8.0
CONV_K = 4
N_GROUPS = 4
EXPERTS_PER_GROUP = 8
N_EXPERTS = 32
D_EXPERT = 256
DEEP_ALPHA = (2 * DEPTH) ** 0.25
LN_EPS = 1e-5
NORM_EPS = 1e-6

LANES = 128
CHUNK = 64
SOLVE_BLOCK = 16
DN_BATCHES = 2
D_PROJ = 3 * DN_WIDTH + DN_WIDTH + 2 * RG_WIDTH + LANES
COL_Z = 3
COL_RX = 4
COL_RY = 5
COL_GATE = (3 * DN_WIDTH + DN_WIDTH + 2 * RG_WIDTH) // LANES
ROW_TILE = 512
MOE_TILE = 128
VMEM_LIMIT = 56 * 1024 * 1024


def _cparams(sem):
    return pltpu.CompilerParams(dimension_semantics=sem, vmem_limit_bytes=VMEM_LIMIT)


def _layer_norm(x):
    mu = jnp.mean(x, axis=-1, keepdims=True)
    xc = x - mu
    var = jnp.mean(xc * xc, axis=-1, keepdims=True)
    return xc * lax.rsqrt(var + LN_EPS)


def _per_batch(x, v, nb):
    rows, c = x.shape
    return jnp.broadcast_to(v[None], (rows // nb, nb, c)).reshape(rows, c)


def _sigmoid(x):
    return 1.0 / (1.0 + jnp.exp(-x))


def _silu(x):
    return x * _sigmoid(x)


def _mod_kernel(cc_ref, w_ref, b_ref, o_ref):
    s = _silu(cc_ref[...])
    o_ref[...] = jnp.dot(s, w_ref[...], precision=HIGHEST, preferred_element_type=F32) + b_ref[...]


def _modulation(cc, w_ada, b_ada):
    depth, d, n6 = w_ada.shape
    rows = cc.shape[0]
    tn = 1536
    return pl.pallas_call(
        _mod_kernel,
        grid=(depth, n6 // tn),
        in_specs=[
            pl.BlockSpec((rows, d), lambda l, j: (0, 0)),
            pl.BlockSpec((None, d, tn), lambda l, j: (l, 0, j)),
            pl.BlockSpec((None, 1, tn), lambda l, j: (l, 0, j)),
        ],
        out_specs=pl.BlockSpec((None, rows, tn), lambda l, j: (l, 0, j)),
        out_shape=jax.ShapeDtypeStruct((depth, rows, n6), F32),
        compiler_params=_cparams(("parallel", "parallel")),
        name="adaln_mod",
    )(cc, w_ada, b_ada.reshape(depth, 1, n6))


def _inproj_kernel(nb, h_ref, sh_ref, sc_ref, w_ref, o_ref):
    xn = _layer_norm(h_ref[...])
    hm = xn * (1.0 + _per_batch(xn, sc_ref[...], nb)) + _per_batch(xn, sh_ref[...], nb)
    o_ref[...] = jnp.dot(hm.astype(BF16), w_ref[...], preferred_element_type=F32)


def _in_projection(h, mod_rows, w_in2, nb, ctx_rows):
    rows, d = h.shape
    tm = ROW_TILE
    lat = lambda i: jnp.where(i * tm >= ctx_rows, 1, 0)
    return pl.pallas_call(
        functools.partial(_inproj_kernel, nb),
        grid=(rows // tm,),
        in_specs=[
            pl.BlockSpec((tm, d), lambda i: (i, 0)),
            pl.BlockSpec((None, nb, d), lambda i: (lat(i), 0, 0)),
            pl.BlockSpec((None, nb, d), lambda i: (lat(i), 0, 1)),
            pl.BlockSpec((d, D_PROJ), lambda i: (0, 0)),
        ],
        out_specs=pl.BlockSpec((tm, D_PROJ), lambda i: (i, 0)),
        out_shape=jax.ShapeDtypeStruct((rows, D_PROJ), F32),
        compiler_params=_cparams(("parallel",)),
        name="ln_mod_inproj",
    )(h, mod_rows, mod_rows, w_in2)


def _conv_tile(x, prev, nxt, w, nb, use_prev, use_next):
    prev = jnp.where(use_prev, prev, 0.0)
    nxt = jnp.where(use_next, nxt, 0.0)
    xm2 = jnp.concatenate([prev, x[: -2 * nb]], axis=0)
    xm1 = jnp.concatenate([prev[nb:], x[:-nb]], axis=0)
    xp1 = jnp.concatenate([x[nb:], nxt], axis=0)
    return xm2 * w[0:1] + xm1 * w[1:2] + x * w[2:3] + xp1 * w[3:4]


def _halo_flags(i, n_ctx_tiles):
    use_prev = jnp.logical_and(i > 0, i < n_ctx_tiles)
    use_next = i < n_ctx_tiles - 1
    return use_prev, use_next


def _qkv_prep_kernel(nb, n_ctx_tiles, x_ref, prev_ref, next_ref, w_ref, o_ref):
    i = pl.program_id(0)
    part = pl.program_id(1)
    use_prev, use_next = _halo_flags(i, n_ctx_tiles)
    y = _silu(_conv_tile(x_ref[...], prev_ref[...], next_ref[...], w_ref[...], nb, use_prev, use_next))
    qscale = jnp.where(part == 0, DN_HEAD_DIM ** -0.5, 1.0).astype(F32)
    for hh in range(DN_HEADS):
        sl = slice(hh * DN_HEAD_DIM, (hh + 1) * DN_HEAD_DIM)
        yh = y[:, sl]
        inv = lax.rsqrt(jnp.sum(yh * yh, axis=-1, keepdims=True) + NORM_EPS) * qscale
        o_ref[hh] = yh * jnp.where(part < 2, inv, 1.0)


def _halo_specs(tile_rows, nb, cols, col_of, n_row_blocks16, tile_of=lambda i: i):
    prev_b = tile_rows // (2 * nb)
    next_b = tile_rows // nb
    return [
        pl.BlockSpec((tile_rows, cols), lambda i, *a: (tile_of(i), col_of(*a))),
        pl.BlockSpec((2 * nb, cols), lambda i, *a: (jnp.maximum(tile_of(i) * prev_b - 1, 0), col_of(*a))),
        pl.BlockSpec((nb, cols),
                     lambda i, *a: (jnp.minimum((tile_of(i) + 1) * next_b, n_row_blocks16 - 1), col_of(*a))),
    ]


def _qkv_prep(proj, conv_w, nb, n_ctx_tiles):
    rows = proj.shape[0]
    tr = CHUNK * nb
    return pl.pallas_call(
        functools.partial(_qkv_prep_kernel, nb, n_ctx_tiles),
        grid=(rows // tr, 3),
        in_specs=_halo_specs(tr, nb, DN_WIDTH, lambda p: p, rows // nb)
        + [pl.BlockSpec((CONV_K, DN_WIDTH), lambda i, p: (0, p))],
        out_specs=pl.BlockSpec((DN_HEADS, tr, DN_HEAD_DIM), lambda i, p: (p, i, 0)),
        out_shape=jax.ShapeDtypeStruct((3 * DN_HEADS, rows, DN_HEAD_DIM), F32),
        compiler_params=_cparams(("parallel", "parallel")),
        name="qkv_conv_norm",
    )(proj, proj, proj, conv_w)


def _bwd_chunk(i, n_ctx, n_tot):
    return jnp.where(i < n_ctx, n_ctx - 1 - i, n_tot + n_ctx - 1 - i)


def _rglru_kernel(nb, n_ctx_tiles, n_tot, xf_ref, pf_ref, nf_ref, xb_ref, pb_ref, nb_ref, ry_ref, cw_ref, cb_ref,
                  wg0_ref, wg1_ref, bg0_ref, bg1_ref, lam_ref, hf_ref, hb_ref, gy_ref, a_scr, b_scr, st_ref):
    i = pl.program_id(0)

    @pl.when(i == 0)
    def _():
        st_ref[...] = jnp.zeros_like(st_ref)

    w = RG_WIDTH
    sp = jax.nn.softplus(-lam_ref[...])
    tiles = (i, _bwd_chunk(i, n_ctx_tiles, n_tot))
    for d, (x_ref, p_ref, n_ref, wg_ref, bg_ref) in enumerate(
            ((xf_ref, pf_ref, nf_ref, wg0_ref, bg0_ref), (xb_ref, pb_ref, nb_ref, wg1_ref, bg1_ref))):
        use_prev, use_next = _halo_flags(tiles[d], n_ctx_tiles)
        xc = _conv_tile(x_ref[...], p_ref[...], n_ref[...], cw_ref[...], nb, use_prev, use_next) + cb_ref[...]
        gates = jnp.dot(xc.astype(BF16), wg_ref[...], preferred_element_type=F32) + bg_ref[...]
        r = _sigmoid(gates[:, :w])
        ig = _sigmoid(gates[:, w:])
        a = jnp.exp(-RG_C * r * sp[d:d + 1])
        y = jnp.maximum(1.0 - a * a, 0.0)
        a_scr[d] = a
        b_scr[d] = jnp.where(y > 0.0, y * lax.rsqrt(y), 0.0) * (ig * xc)
    gy_ref[...] = jax.nn.gelu(ry_ref[...])

    def body(t, carry):
        h0, h1 = carry
        r0 = pl.ds(pl.multiple_of(t * nb, nb), nb)
        r1 = pl.ds(pl.multiple_of((CHUNK - 1 - t) * nb, nb), nb)
        h0 = a_scr[0, r0, :] * h0 + b_scr[0, r0, :]
        h1 = a_scr[1, r1, :] * h1 + b_scr[1, r1, :]
        hf_ref[r0, :] = h0
        hb_ref[r1, :] = h1
        return h0, h1

    h0, h1 = lax.fori_loop(0, CHUNK, body, (st_ref[0], st_ref[1]), unroll=8)
    st_ref[0] = h0
    st_ref[1] = h1


def _rglru(proj, conv_w, conv_b, wg, bg, lam, nb, n_ctx_tiles):
    rows = proj.shape[0]
    tr = CHUNK * nb
    n_tot = rows // tr
    w = RG_WIDTH
    out = jax.ShapeDtypeStruct((rows, w), F32)
    fwd = lambda i: i
    bwd = lambda i: _bwd_chunk(i, n_ctx_tiles, n_tot)
    full = lambda shape: pl.BlockSpec(shape, lambda i: (0,) * len(shape))
    return pl.pallas_call(
        functools.partial(_rglru_kernel, nb, n_ctx_tiles, n_tot),
        grid=(n_tot,),
        in_specs=_halo_specs(tr, nb, w, lambda: COL_RX, rows // nb, fwd)
        + _halo_specs(tr, nb, w, lambda: COL_RX, rows // nb, bwd)
        + [pl.BlockSpec((tr, w), lambda i: (i, COL_RY)), full((CONV_K, w)), full((1, w)),
           pl.BlockSpec((w, 2 * w), lambda i: (0, 0)), pl.BlockSpec((w, 2 * w), lambda i: (0, 1)),
           pl.BlockSpec((1, 2 * w), lambda i: (0, 0)), pl.BlockSpec((1, 2 * w), lambda i: (0, 1)),
           full((2, w))],
        out_specs=[pl.BlockSpec((tr, w), lambda i: (fwd(i), 0)), pl.BlockSpec((tr, w), lambda i: (bwd(i), 0)),
                   pl.BlockSpec((tr, w), lambda i: (i, 0))],
        out_shape=[out] * 3,
        scratch_shapes=[pltpu.VMEM((2, tr, w), F32), pltpu.VMEM((2, tr, w), F32), pltpu.VMEM((2, nb, w), F32)],
        compiler_params=_cparams(("arbitrary",)),
        name="rglru",
    )(proj, proj, proj, proj, proj, proj, proj, conv_w, conv_b, wg, wg, bg, bg, lam)


def _bdot(a, b):
    return jnp.dot(a.astype(BF16), b.astype(BF16), preferred_element_type=F32)


def _deltanet_gates(d, raw, alog, dtb):
    c = CHUNK
    row = lax.broadcasted_iota(jnp.int32, (c, c), 0)
    col = lax.broadcasted_iota(jnp.int32, (c, c), 1)
    incl = (row >= col) if d == 0 else (row <= col)
    g_all = -jnp.exp(alog) * jax.nn.softplus(raw + dtb)
    gc_all = jnp.dot(incl.astype(F32), g_all, precision=HIGHEST, preferred_element_type=F32)
    g_tot = jnp.sum(g_all, axis=0, keepdims=True)
    return dict(incl=incl, strict=(row > col) if d == 0 else (row < col),
                diag_blk=(row // SOLVE_BLOCK) == (col // SOLVE_BLOCK),
                beta=_sigmoid(raw), gc=gc_all, gc_t=gc_all.T, egc=jnp.exp(gc_all),
                kdec=jnp.exp(g_tot - gc_all), glast=jnp.exp(g_tot))


def _deltanet_heads(gts, cbs, qs, ks, vs, ss):
    c = CHUNK
    dh = DN_HEAD_DIM
    n = len(cbs)
    rng = range(n)
    cgs = [2 * DN_HEADS + cb for cb in cbs]
    beta = [gts[i]["beta"][:, cbs[i]:cbs[i] + 1] for i in rng]
    egc = [gts[i]["egc"][:, cgs[i]:cgs[i] + 1] for i in rng]
    decay = [jnp.exp(jnp.where(gts[i]["incl"], gts[i]["gc"][:, cgs[i]:cgs[i] + 1] - gts[i]["gc_t"][cgs[i]:cgs[i] + 1, :],
                               -1e30)) for i in rng]
    kbeta = [ks[i] * beta[i] for i in rng]
    prod = [lax.dot_general(jnp.concatenate([kbeta[i], qs[i]], axis=0).astype(BF16), ks[i].astype(BF16),
                            (((1,), (1,)), ((), ())), preferred_element_type=F32) for i in rng]
    a_mat = [jnp.where(gts[i]["strict"], prod[i][:c] * decay[i], 0.0) for i in rng]
    intra = [prod[i][c:] * decay[i] for i in rng]
    p = [jnp.where(gts[i]["diag_blk"], -a_mat[i], 0.0) for i in rng]
    y = [jnp.concatenate([vs[i] * beta[i], kbeta[i] * egc[i], jnp.where(gts[i]["diag_blk"], 0.0, a_mat[i])], axis=1)
         for i in rng]
    for j in range(4):
        y = [y[i] + _bdot(p[i], y[i]) for i in rng]
        if j < 3:
            p = [_bdot(p[i], p[i]) for i in rng]
    e = [y[i][:, 2 * dh:] for i in rng]
    y = [y[i][:, :2 * dh] for i in rng]
    e2 = [_bdot(e[i], e[i]) for i in rng]
    y = [y[i] + _bdot(e2[i], y[i]) for i in rng]
    y = [y[i] - _bdot(e[i], y[i]) for i in rng]
    ws_qs = [_bdot(jnp.concatenate([y[i][:, dh:], qs[i] * egc[i]], axis=0), ss[i]) for i in rng]
    v_new = [y[i][:, :dh] - ws_qs[i][:c] for i in rng]
    o = [ws_qs[i][c:] + _bdot(intra[i], v_new[i]) for i in rng]
    s_new = [ss[i] * gts[i]["glast"][:, cgs[i]:cgs[i] + 1] + lax.dot_general(
        (ks[i] * gts[i]["kdec"][:, cgs[i]:cgs[i] + 1]).astype(BF16), v_new[i].astype(BF16),
        (((0,), (0,)), ((), ())), preferred_element_type=F32) for i in rng]
    return o, s_new


def _deltanet_kernel(nb, qf_ref, kf_ref, vf_ref, gf_ref, qb_ref, kb_ref, vb_ref, gb_ref, alog_ref, dtb_ref,
                     of_ref, ob_ref, s_ref):
    @pl.when(pl.program_id(0) == 0)
    def _():
        s_ref[...] = jnp.zeros_like(s_ref)

    dirs = ((qf_ref, kf_ref, vf_ref, gf_ref, of_ref), (qb_ref, kb_ref, vb_ref, gb_ref, ob_ref))

    def body(it, carry):
        idx = [(it * DN_BATCHES + j, d, hh) for j in range(DN_BATCHES) for d in range(2) for hh in range(DN_HEADS)]
        rows_of = {j: pl.ds(it * DN_BATCHES + j, CHUNK, stride=nb) for j in range(DN_BATCHES)}
        rows_b = [rows_of[i // (2 * DN_HEADS)] for i in range(len(idx))]
        cbs = [d * DN_HEADS + hh for _, d, hh in idx]
        gates = {(j, d): _deltanet_gates(d, dirs[d][3][rows_of[j], :], alog_ref[...], dtb_ref[...])
                 for j in range(DN_BATCHES) for d in range(2)}
        gts = [gates[(i // (2 * DN_HEADS), d)] for i, (_, d, _) in enumerate(idx)]
        qs = [dirs[d][0][hh, rows_b[i], :] for i, (_, d, hh) in enumerate(idx)]
        ks = [dirs[d][1][hh, rows_b[i], :] for i, (_, d, hh) in enumerate(idx)]
        vs = [dirs[d][2][hh, rows_b[i], :] for i, (_, d, hh) in enumerate(idx)]
        ss = [s_ref[b * (2 * DN_HEADS) + cbs[i]] for i, (b, _, _) in enumerate(idx)]
        o, s_new = _deltanet_heads(gts, cbs, qs, ks, vs, ss)
        for i, (b, d, hh) in enumerate(idx):
            dirs[d][4][hh, rows_b[i], :] = o[i]
            s_ref[b * (2 * DN_HEADS) + cbs[i]] = s_new[i]
        return carry

    lax.fori_loop(0, nb // DN_BATCHES, body, 0)


def _deltanet(qkv, proj, alog_vec, dtb_vec, nb, n_ctx_chunks):
    rows = qkv.shape[1]
    tr = CHUNK * nb
    n_tot = rows // tr
    fwd = lambda i: i
    bwd = lambda i: _bwd_chunk(i, n_ctx_chunks, n_tot)

    def specs(order):
        return [pl.BlockSpec((DN_HEADS, tr, DN_HEAD_DIM), lambda i, p=p: (p, order(i), 0)) for p in range(3)] + [
            pl.BlockSpec((tr, LANES), lambda i: (order(i), COL_GATE))]

    vec = pl.BlockSpec((1, LANES), lambda i: (0, 0))
    out = jax.ShapeDtypeStruct((DN_HEADS, rows, DN_HEAD_DIM), F32)
    return pl.pallas_call(
        functools.partial(_deltanet_kernel, nb),
        grid=(n_tot,),
        in_specs=specs(fwd) + specs(bwd) + [vec, vec],
        out_specs=[pl.BlockSpec((DN_HEADS, tr, DN_HEAD_DIM), lambda i: (0, fwd(i), 0)),
                   pl.BlockSpec((DN_HEADS, tr, DN_HEAD_DIM), lambda i: (0, bwd(i), 0))],
        out_shape=[out, out],
        scratch_shapes=[pltpu.VMEM((nb * 2 * DN_HEADS, DN_HEAD_DIM, DN_HEAD_DIM), F32)],
        compiler_params=_cparams(("arbitrary",)),
        name="deltanet",
    )(qkv, qkv, qkv, proj, qkv, qkv, qkv, proj, alog_vec, dtb_vec)


def _mixer_out_kernel(nb, of_ref, ob_ref, z_ref, hf_ref, hb_ref, gy_ref, onw_ref, w_ref, h_ref, gt_ref, g_ref, b_ref,
                      o_ref):
    z = z_ref[...]
    parts = []
    for hh in range(DN_HEADS):
        sl = slice(hh * DN_HEAD_DIM, (hh + 1) * DN_HEAD_DIM)
        oh = of_ref[hh] + ob_ref[hh]
        inv = lax.rsqrt(jnp.mean(oh * oh, axis=-1, keepdims=True) + NORM_EPS)
        parts.append(oh * inv * onw_ref[...] * _silu(z[:, sl]))
    parts.append((hf_ref[...] + hb_ref[...]) * gy_ref[...])
    y = jnp.concatenate(parts, axis=1).astype(BF16)
    u = jnp.dot(y, w_ref[...], preferred_element_type=F32)
    r = DEEP_ALPHA * h_ref[...] + _per_batch(u, gt_ref[...], nb) * u
    o_ref[...] = _layer_norm(r) * g_ref[...] + b_ref[...]


def _mixer_out(o_f, o_b, proj, hf, hb, gy, onorm_w, w_out, h, mod_rows, ln_g, ln_b, nb, ctx_rows, row0):
    rows, d = h.shape
    tm = ROW_TILE
    t0 = row0 // tm
    lat = lambda i: jnp.where((i + t0) * tm >= ctx_rows, 1, 0)
    half = pl.BlockSpec((tm, DN_WIDTH), lambda i: (i + t0, 0))
    planes = pl.BlockSpec((DN_HEADS, tm, DN_HEAD_DIM), lambda i: (0, i + t0, 0))
    vec = lambda n: pl.BlockSpec((1, n), lambda i: (0, 0))
    return pl.pallas_call(
        functools.partial(_mixer_out_kernel, nb),
        grid=((rows - row0) // tm,),
        in_specs=[planes, planes, pl.BlockSpec((tm, DN_WIDTH), lambda i: (i + t0, COL_Z)), half, half, half,
                  vec(DN_HEAD_DIM), pl.BlockSpec((d, d), lambda i: (0, 0)),
                  pl.BlockSpec((tm, d), lambda i: (i + t0, 0)),
                  pl.BlockSpec((None, nb, d), lambda i: (lat(i), 0, 2)),
                  vec(d), vec(d)],
        out_specs=pl.BlockSpec((tm, d), lambda i: (i, 0)),
        out_shape=jax.ShapeDtypeStruct((rows - row0, d), F32),
        compiler_params=_cparams(("parallel",)),
        name="mixer_out",
    )(o_f, o_b, proj, hf, hb, gy, onorm_w, w_out, h, mod_rows, ln_g, ln_b)


PAIRS_PER_GROUP = EXPERTS_PER_GROUP * (EXPERTS_PER_GROUP - 1) // 2
N_CLASSES = N_GROUPS * PAIRS_PER_GROUP
assert N_CLASSES <= LANES
INFO_CLASS, INFO_RANK, INFO_W_LO, INFO_W_HI = range(4)
Y_SUB = 2 * (D_MODEL // LANES)


ROW_SUB = D_MODEL // LANES


def _store_row_tiles(ref, x):
    n = x.shape[0]
    for c in range(ROW_SUB):
        ref[pl.ds(c, n, stride=ROW_SUB), :] = x[:, c * LANES:(c + 1) * LANES]


def _load_row_tiles(ref, n):
    return jnp.concatenate([ref[pl.ds(c, n, stride=ROW_SUB), :] for c in range(ROW_SUB)], axis=1)


def _first_index(mask, lane):
    return jnp.min(jnp.where(mask, lane, LANES), axis=-1, keepdims=True)


def _router_kernel(nb, x_ref, sh_ref, sc_ref, wr_ref, br_ref, hm_ref, info_ref, cnt_ref, run_ref):
    @pl.when(pl.program_id(0) == 0)
    def _():
        run_ref[...] = jnp.zeros_like(run_ref)

    xn = _layer_norm(x_ref[...])
    hm = xn * (1.0 + _per_batch(xn, sc_ref[...], nb)) + _per_batch(xn, sh_ref[...], nb)
    _store_row_tiles(hm_ref, hm)
    tm = hm.shape[0]
    h_hi = hm.astype(BF16)
    h_lo = (hm - h_hi.astype(F32)).astype(BF16)
    hi_hl = jnp.dot(h_hi, wr_ref[...], preferred_element_type=F32)
    logits = (hi_hl[:, :LANES] + hi_hl[:, LANES:] + jnp.dot(h_lo, wr_ref[:, :LANES], preferred_element_type=F32)
              + br_ref[...])
    lane = lax.broadcasted_iota(jnp.int32, (tm, LANES), 1)
    neg = -jnp.inf
    is_g = lane < N_GROUPS
    gmax = jnp.max(jnp.where(is_g, logits, neg), axis=-1, keepdims=True)
    g_sel = _first_index(jnp.logical_and(is_g, logits == gmax), lane)
    p_group = 1.0 / jnp.sum(jnp.where(is_g, jnp.exp(logits - gmax), 0.0), axis=-1, keepdims=True)
    lo = N_GROUPS + EXPERTS_PER_GROUP * g_sel
    in_grp = jnp.logical_and(lane >= lo, lane < lo + EXPERTS_PER_GROUP)
    m1 = jnp.max(jnp.where(in_grp, logits, neg), axis=-1, keepdims=True)
    i1 = _first_index(jnp.logical_and(in_grp, logits == m1), lane)
    rest = jnp.logical_and(in_grp, lane != i1)
    m2 = jnp.max(jnp.where(rest, logits, neg), axis=-1, keepdims=True)
    i2 = _first_index(jnp.logical_and(rest, logits == m2), lane)
    e2 = jnp.exp(m2 - m1)
    w1 = p_group / (1.0 + e2)
    w2 = p_group * e2 / (1.0 + e2)
    l1 = i1 - lo
    l2 = i2 - lo
    e_lo = jnp.minimum(l1, l2)
    e_hi = jnp.maximum(l1, l2)
    pair = jnp.right_shift(e_lo * (2 * EXPERTS_PER_GROUP - 1 - e_lo), 1) + (e_hi - e_lo - 1)
    cls = g_sel * PAIRS_PER_GROUP + pair
    first_is_lo = l1 < l2
    w_lo = jnp.where(first_is_lo, w1, w2)
    w_hi = jnp.where(first_is_lo, w2, w1)
    oh = lane == cls
    r_i = lax.broadcasted_iota(jnp.int32, (tm, tm), 0)
    c_i = lax.broadcasted_iota(jnp.int32, (tm, tm), 1)
    before = jnp.dot((r_i > c_i).astype(BF16), oh.astype(BF16), preferred_element_type=F32) + run_ref[...]
    rank = jnp.sum(jnp.where(oh, before, 0.0), axis=-1, keepdims=True)
    run_ref[...] = run_ref[...] + jnp.sum(oh.astype(F32), axis=0, keepdims=True)
    cnt_ref[...] = run_ref[...]
    info = jnp.zeros((tm, LANES), F32)
    for idx, val in ((INFO_CLASS, cls.astype(F32)), (INFO_RANK, rank), (INFO_W_LO, w_lo), (INFO_W_HI, w_hi)):
        info = jnp.where(lane == idx, val, info)
    info_ref[...] = info


def _router(x1, mod_rows, wr, br, nb, ctx_rows):
    rows, d = x1.shape
    tm = ROW_TILE
    lat = lambda i: jnp.where(i * tm >= ctx_rows, 1, 0)
    return pl.pallas_call(
        functools.partial(_router_kernel, nb),
        grid=(rows // tm,),
        in_specs=[pl.BlockSpec((tm, d), lambda i: (i, 0)),
                  pl.BlockSpec((None, nb, d), lambda i: (lat(i), 0, 3)),
                  pl.BlockSpec((None, nb, d), lambda i: (lat(i), 0, 4)),
                  pl.BlockSpec((d, 2 * LANES), lambda i: (0, 0)),
                  pl.BlockSpec((1, LANES), lambda i: (0, 0))],
        out_specs=[pl.BlockSpec((tm * ROW_SUB, LANES), lambda i: (i, 0)),
                   pl.BlockSpec((tm, LANES), lambda i: (i, 0)),
                   pl.BlockSpec((1, LANES), lambda i: (0, 0))],
        out_shape=[jax.ShapeDtypeStruct((rows * ROW_SUB, LANES), F32),
                   jax.ShapeDtypeStruct((rows, LANES), F32),
                   jax.ShapeDtypeStruct((1, LANES), F32)],
        scratch_shapes=[pltpu.VMEM((1, LANES), F32)],
        compiler_params=_cparams(("arbitrary",)),
        name="moe_router",
    )(x1, mod_rows, mod_rows, wr, br)


def _dispatch_kernel(dst_ref, hm_hbm, xs_in, xs_hbm, sem):
    del xs_in
    i = pl.program_id(0)
    rs = ROW_SUB
    tb = dst_ref.shape[1]
    slot = i % 2

    def wait_block(s):
        pltpu.make_async_copy(hm_hbm.at[pl.ds(0, tb * rs)], xs_hbm.at[pl.ds(0, tb * rs)], sem.at[s]).wait()

    for r in range(tb):
        src = hm_hbm.at[pl.ds(pl.multiple_of((i * tb + r) * rs, rs), rs)]
        dst = xs_hbm.at[pl.ds(pl.multiple_of(dst_ref[0, r], rs), rs)]
        pltpu.make_async_copy(src, dst, sem.at[slot]).start(priority=r % 2)

    @pl.when(i >= 1)
    def _():
        wait_block(1 - slot)

    @pl.when(i == pl.num_programs(0) - 1)
    def _():
        wait_block(slot)


def _dispatch(hm, dest, sorted_rows):
    tb = ROW_TILE
    n_blocks = dest.shape[0] // tb
    xs0 = jnp.zeros((sorted_rows * ROW_SUB, LANES), F32)
    return pl.pallas_call(
        _dispatch_kernel,
        grid=(n_blocks,),
        in_specs=[pl.BlockSpec((None, 1, tb), lambda i: (i, 0, 0), memory_space=pltpu.SMEM),
                  pl.BlockSpec(memory_space=pl.ANY), pl.BlockSpec(memory_space=pl.ANY)],
        out_specs=pl.BlockSpec(memory_space=pl.ANY),
        out_shape=jax.ShapeDtypeStruct(xs0.shape, F32),
        scratch_shapes=[pltpu.SemaphoreType.DMA((2,))],
        input_output_aliases={2: 0},
        compiler_params=_cparams(("arbitrary",)),
        name="moe_dispatch",
    )((dest * ROW_SUB).reshape(n_blocks, 1, tb), hm, xs0)


def _class_ffn_kernel(telo_ref, tehi_ref, nact_ref, xs_ref, wgl_ref, wul_ref, wdl_ref, wgh_ref, wuh_ref, wdh_ref,
                      ys_ref):
    j = pl.program_id(0)
    tmx = xs_ref.shape[0] // ROW_SUB

    @pl.when(j < nact_ref[0])
    def _():
        x = _load_row_tiles(xs_ref, tmx).astype(BF16)
        for half, (wg_ref, wu_ref, wd_ref) in enumerate(((wgl_ref, wul_ref, wdl_ref), (wgh_ref, wuh_ref, wdh_ref))):
            a = jnp.dot(x, wg_ref[...], preferred_element_type=F32)
            b = jnp.dot(x, wu_ref[...], preferred_element_type=F32)
            y = jnp.dot((_silu(a) * b).astype(BF16), wd_ref[...], preferred_element_type=F32)
            for c in range(ROW_SUB):
                ys_ref[pl.ds(half * ROW_SUB + c, tmx, stride=Y_SUB), :] = y[:, c * LANES:(c + 1) * LANES]

    @pl.when(j >= nact_ref[0])
    def _():
        ys_ref[...] = jnp.zeros_like(ys_ref)


def _class_ffn(xs, tile_e_lo, tile_e_hi, n_active, w_gate, w_up, w_down, tmx):
    n_tiles = xs.shape[0] // (tmx * ROW_SUB)
    d, de = w_gate.shape[-2:]
    w_in = lambda te_idx: pl.BlockSpec((None, d, de), lambda j, *te: (te[te_idx][j], 0, 0))
    w_out = lambda te_idx: pl.BlockSpec((None, de, d), lambda j, *te: (te[te_idx][j], 0, 0))
    grid_spec = pltpu.PrefetchScalarGridSpec(
        num_scalar_prefetch=3,
        grid=(n_tiles,),
        in_specs=[pl.BlockSpec((tmx * ROW_SUB, LANES), lambda j, *te: (j, 0)),
                  w_in(0), w_in(0), w_out(0), w_in(1), w_in(1), w_out(1)],
        out_specs=pl.BlockSpec((tmx * Y_SUB, LANES), lambda j, *te: (j, 0)),
    )
    return pl.pallas_call(
        _class_ffn_kernel,
        grid_spec=grid_spec,
        out_shape=jax.ShapeDtypeStruct((n_tiles * tmx * Y_SUB, LANES), F32),
        compiler_params=_cparams(("arbitrary",)),
        name="expert_ffn",
    )(tile_e_lo, tile_e_hi, n_active, xs, w_gate, w_up, w_down, w_gate, w_up, w_down)


def _moe_out_kernel(nb, src0_ref, srcn_ref, ys_hbm, info_ref, x_ref, gt_ref, g_ref, b_ref, o_ref, ybuf, sem):
    i = pl.program_id(0)
    tm = info_ref.shape[0]
    slot = i % 2
    nslot = 1 - slot

    def gather_all(src_ref, s):
        for r in range(tm):
            pltpu.make_async_copy(ys_hbm.at[pl.ds(pl.multiple_of(src_ref[0, r], Y_SUB), Y_SUB)],
                                  ybuf.at[s, pl.ds(r * Y_SUB, Y_SUB)], sem.at[s]).start(priority=r % 2)

    def wait_rows(s):
        pltpu.make_async_copy(ybuf.at[s], ybuf.at[s], sem.at[s]).wait()

    @pl.when(i == 0)
    def _():
        gather_all(src0_ref, 0)

    wait_rows(slot)
    gather_all(srcn_ref, nslot)
    info = info_ref[...]
    yb = ybuf.at[slot]
    y_lo = jnp.concatenate([yb[pl.ds(c, tm, stride=Y_SUB), :] for c in range(ROW_SUB)], axis=1)
    y_hi = jnp.concatenate([yb[pl.ds(ROW_SUB + c, tm, stride=Y_SUB), :] for c in range(ROW_SUB)], axis=1)
    f = y_lo * info[:, INFO_W_LO:INFO_W_LO + 1] + y_hi * info[:, INFO_W_HI:INFO_W_HI + 1]
    r = DEEP_ALPHA * x_ref[...] + _per_batch(f, gt_ref[...], nb) * f
    o_ref[...] = _layer_norm(r) * g_ref[...] + b_ref[...]

    @pl.when(i == pl.num_programs(0) - 1)
    def _():
        wait_rows(nslot)


def _moe_out(ys, dest, info, x1, mod_rows, ln_g, ln_b, nb, ctx_rows):
    rows, d = x1.shape
    tm = ROW_TILE
    n_blocks = rows // tm
    lat = lambda i: jnp.where(i * tm >= ctx_rows, 1, 0)
    vec = pl.BlockSpec((1, d), lambda i: (0, 0))
    src = (dest * Y_SUB).reshape(n_blocks, 1, tm)
    smem = lambda f: pl.BlockSpec((None, 1, tm), f, memory_space=pltpu.SMEM)
    return pl.pallas_call(
        functools.partial(_moe_out_kernel, nb),
        grid=(n_blocks,),
        in_specs=[smem(lambda i: (0, 0, 0)),
                  smem(lambda i: (jnp.minimum(i + 1, n_blocks - 1), 0, 0)),
                  pl.BlockSpec(memory_space=pl.ANY),
                  pl.BlockSpec((tm, LANES), lambda i: (i, 0)),
                  pl.BlockSpec((tm, d), lambda i: (i, 0)),
                  pl.BlockSpec((None, nb, d), lambda i: (lat(i), 0, 5)),
                  vec, vec],
        out_specs=pl.BlockSpec((tm, d), lambda i: (i, 0)),
        out_shape=jax.ShapeDtypeStruct((rows, d), F32),
        scratch_shapes=[pltpu.VMEM((2, tm * Y_SUB, LANES), F32), pltpu.SemaphoreType.DMA((2,))],
        compiler_params=_cparams(("arbitrary",)),
        name="moe_out",
    )(src, src, ys, info, x1, mod_rows, ln_g, ln_b)


def _class_experts():
    lo, hi = [], []
    for g in range(N_GROUPS):
        for a in range(EXPERTS_PER_GROUP):
            for b in range(a + 1, EXPERTS_PER_GROUP):
                lo.append(g * EXPERTS_PER_GROUP + a)
                hi.append(g * EXPERTS_PER_GROUP + b)
    return np.asarray(lo, np.int32), np.asarray(hi, np.int32)


def _routing_tables(info, counts, n_tiles, tmx):
    cnt = counts[0, :N_CLASSES].astype(jnp.int32)
    padded = ((cnt + tmx - 1) // tmx) * tmx
    ends = jnp.cumsum(padded)
    starts = ends - padded
    cls = info[:, INFO_CLASS].astype(jnp.int32)
    dest = starts[cls] + info[:, INFO_RANK].astype(jnp.int32)
    tile_start = jnp.arange(n_tiles, dtype=jnp.int32) * tmx
    n_before = jnp.sum((ends[None, :] <= tile_start[:, None]).astype(jnp.int32), axis=1)
    tile_cls = jnp.minimum(n_before, N_CLASSES - 1)
    e_lo, e_hi = _class_experts()
    n_active = (ends[-1:] // tmx).astype(jnp.int32)
    return dest, jnp.asarray(e_lo)[tile_cls], jnp.asarray(e_hi)[tile_cls], n_active


def _relayout_w_in(w_in):
    qkvz = w_in[..., :4 * DN_WIDTH]
    gates = w_in[..., 4 * DN_WIDTH:4 * DN_WIDTH + 4 * DN_HEADS]
    rxy = w_in[..., 4 * DN_WIDTH + 4 * DN_HEADS:]
    pad = jnp.zeros(w_in.shape[:-1] + (LANES - 4 * DN_HEADS,), w_in.dtype)
    return jnp.concatenate([qkvz, rxy, gates, pad], axis=-1).astype(BF16)


def _block_diag(w):
    n, c, _ = w.shape
    eye = jnp.eye(n, dtype=w.dtype)
    return (eye[:, None, :, None] * w[:, :, None, :]).reshape(n * c, n * c)


def _lane_vec(vals, offset):
    return jnp.zeros((1, LANES), F32).at[0, offset:offset + vals.size].set(vals.reshape(-1))


def kernel(x, c, ctx, c_ctx, w_ada, b_ada, w_in, conv_qkv_w, dn_a_log, dn_dt_bias, dn_onorm_w, rg_conv_w, rg_conv_b, rg_wa, rg_ba, rg_wi, rg_bi, rg_lambda, w_out, ln_g, ln_b, router_wg, router_bg, router_we, router_be, w_e_gate, w_e_up, w_e_down):
    nb, t_lat, d = x.shape
    t_ctx = ctx.shape[1]
    assert d == D_MODEL and nb % 8 == 0 and t_lat % GRID_W == 0 and t_ctx % CHUNK == 0
    assert (t_ctx * nb) % ROW_TILE == 0 and (t_lat * nb) % ROW_TILE == 0
    tt = t_ctx + t_lat
    rows = tt * nb
    ctx_rows = t_ctx * nb
    n_ctx_chunks = t_ctx // CHUNK

    h = jnp.concatenate([jnp.swapaxes(ctx, 0, 1), jnp.swapaxes(x, 0, 1)], axis=0).reshape(rows, d)

    n_cc = ((nb + 1 + 7) // 8) * 8
    cc = jnp.zeros((n_cc, d), F32).at[:nb].set(c).at[nb].set(c_ctx)
    mod = _modulation(cc, w_ada, b_ada)
    mod_ctx = jnp.broadcast_to(mod[:, nb:nb + 1], (DEPTH, nb, 6 * d))
    mod_rows = jnp.stack([mod_ctx, mod[:, :nb]], axis=1)

    w_in2 = _relayout_w_in(w_in)
    for l in range(DEPTH):
        row0 = ctx_rows if l == DEPTH - 1 else 0
        moe_ctx_rows = ctx_rows - row0
        n_tiles = (rows - row0) // MOE_TILE + N_CLASSES
        mr = mod_rows[l]
        proj = _in_projection(h, mr, w_in2[l], nb, ctx_rows)
        qkv = _qkv_prep(proj, conv_qkv_w[l], nb, n_ctx_chunks)
        wg = jnp.concatenate([_block_diag(rg_wa[l, 0]), _block_diag(rg_wi[l, 0]),
                              _block_diag(rg_wa[l, 1]), _block_diag(rg_wi[l, 1])], axis=1).astype(BF16)
        bg = jnp.concatenate([rg_ba[l, 0], rg_bi[l, 0], rg_ba[l, 1], rg_bi[l, 1]])[None]
        hf, hb, gy = _rglru(proj, rg_conv_w[l], rg_conv_b[l][None], wg, bg, rg_lambda[l], nb, n_ctx_chunks)
        alog_vec = _lane_vec(dn_a_log[l], 2 * DN_HEADS)
        dtb_vec = _lane_vec(dn_dt_bias[l], 2 * DN_HEADS)
        o_f, o_b = _deltanet(qkv, proj, alog_vec, dtb_vec, nb, n_ctx_chunks)
        x1 = _mixer_out(o_f, o_b, proj, hf, hb, gy, dn_onorm_w[l][None], w_out[l].astype(BF16), h, mr,
                        ln_g[l, 0][None], ln_b[l, 0][None], nb, ctx_rows, row0)
        wr = jnp.zeros((d, LANES), F32).at[:, :N_GROUPS].set(router_wg[l]).at[
            :, N_GROUPS:N_GROUPS + N_EXPERTS].set(router_we[l])
        br = _lane_vec(jnp.concatenate([router_bg[l], router_be[l]]), 0)
        wr_hi = wr.astype(BF16)
        wr_hl = jnp.concatenate([wr_hi, (wr - wr_hi.astype(F32)).astype(BF16)], axis=1)
        hm, info, counts = _router(x1, mr, wr_hl, br, nb, moe_ctx_rows)
        dest, tile_e_lo, tile_e_hi, n_active = _routing_tables(info, counts, n_tiles, MOE_TILE)
        xs = _dispatch(hm, dest, n_tiles * MOE_TILE)
        ys = _class_ffn(xs, tile_e_lo, tile_e_hi, n_active,
                        w_e_gate[l].astype(BF16), w_e_up[l].astype(BF16), w_e_down[l].astype(BF16), MOE_TILE)
        h = _moe_out(ys, dest, info, x1, mr, ln_g[l, 1][None], ln_b[l, 1][None], nb, moe_ctx_rows)

    return jnp.swapaxes(h.reshape(t_lat, nb, d), 0, 1)
```

```python
import functools
import math

import jax
import jax.numpy as jnp
import numpy as np
from jax import lax
from jax.experimental import pallas as pl
from jax.experimental.pallas import tpu as pltpu

F32 = jnp.float32
BF16 = jnp.bfloat16
HIGHEST = lax.Precision.HIGHEST

D_MODEL = 1024
DEPTH = 2
GRID_W = 64
DN_HEAD_DIM = 128
DN_WIDTH = 512
DN_HEADS = 4
RG_WIDTH = 512
RG_BLOCKS = 8
RG_BLOCK_DIM = 64
RG_C = 8.0
CONV_K = 4
N_GROUPS = 4
EXPERTS_PER_GROUP = 8
N_EXPERTS = 32
D_EXPERT = 256
DEEP_ALPHA = (2 * DEPTH) ** 0.25
LN_EPS = 1e-5
NORM_EPS = 1e-6

LANES = 128
CHUNK = 64
SOLVE_BLOCK = 16
DN_BATCHES = 2
D_PROJ = 3 * DN_WIDTH + DN_WIDTH + 2 * RG_WIDTH + LANES
COL_Z = 3
COL_RX = 4
COL_RY = 5
COL_GATE = (3 * DN_WIDTH + DN_WIDTH + 2 * RG_WIDTH) // LANES
ROW_TILE = 512
MOE_TILE = 128
VMEM_LIMIT = 56 * 1024 * 1024


def _cparams(sem):
    return pltpu.CompilerParams(dimension_semantics=sem, vmem_limit_bytes=VMEM_LIMIT)


def _layer_norm(x):
    mu = jnp.mean(x, axis=-1, keepdims=True)
    xc = x - mu
    var = jnp.mean(xc * xc, axis=-1, keepdims=True)
    return xc * lax.rsqrt(var + LN_EPS)


def _per_batch(x, v, nb):
    rows, c = x.shape
    return jnp.broadcast_to(v[None], (rows // nb, nb, c)).reshape(rows, c)


def _sigmoid(x):
    return 1.0 / (1.0 + jnp.exp(-x))


def _silu(x):
    return x * _sigmoid(x)


def _mod_kernel(cc_ref, w_ref, b_ref, o_ref):
    s = _silu(cc_ref[...])
    o_ref[...] = jnp.dot(s, w_ref[...], precision=HIGHEST, preferred_element_type=F32) + b_ref[...]


def _modulation(cc, w_ada, b_ada):
    depth, d, n6 = w_ada.shape
    rows = cc.shape[0]
    tn = 1536
    return pl.pallas_call(
        _mod_kernel,
        grid=(depth, n6 // tn),
        in_specs=[
            pl.BlockSpec((rows, d), lambda l, j: (0, 0)),
            pl.BlockSpec((None, d, tn), lambda l, j: (l, 0, j)),
            pl.BlockSpec((None, 1, tn), lambda l, j: (l, 0, j)),
        ],
        out_specs=pl.BlockSpec((None, rows, tn), lambda l, j: (l, 0, j)),
        out_shape=jax.ShapeDtypeStruct((depth, rows, n6), F32),
        compiler_params=_cparams(("parallel", "parallel")),
        name="adaln_mod",
    )(cc, w_ada, b_ada.reshape(depth, 1, n6))


def _inproj_kernel(nb, h_ref, sh_ref, sc_ref, w_ref, o_ref):
    xn = _layer_norm(h_ref[...])
    hm = xn * (1.0 + _per_batch(xn, sc_ref[...], nb)) + _per_batch(xn, sh_ref[...], nb)
    o_ref[...] = jnp.dot(hm.astype(BF16), w_ref[...], preferred_element_type=F32)


def _in_projection(h, mod_rows, w_in2, nb, ctx_rows):
    rows, d = h.shape
    tm = ROW_TILE
    lat = lambda i: jnp.where(i * tm >= ctx_rows, 1, 0)
    return pl.pallas_call(
        functools.partial(_inproj_kernel, nb),
        grid=(rows // tm,),
        in_specs=[
            pl.BlockSpec((tm, d), lambda i: (i, 0)),
            pl.BlockSpec((None, nb, d), lambda i: (lat(i), 0, 0)),
            pl.BlockSpec((None, nb, d), lambda i: (lat(i), 0, 1)),
            pl.BlockSpec((d, D_PROJ), lambda i: (0, 0)),
        ],
        out_specs=pl.BlockSpec((tm, D_PROJ), lambda i: (i, 0)),
        out_shape=jax.ShapeDtypeStruct((rows, D_PROJ), F32),
        compiler_params=_cparams(("parallel",)),
        name="ln_mod_inproj",
    )(h, mod_rows, mod_rows, w_in2)


def _conv_tile(x, prev, nxt, w, nb, use_prev, use_next):
    prev = jnp.where(use_prev, prev, 0.0)
    nxt = jnp.where(use_next, nxt, 0.0)
    xm2 = jnp.concatenate([prev, x[: -2 * nb]], axis=0)
    xm1 = jnp.concatenate([prev[nb:], x[:-nb]], axis=0)
    xp1 = jnp.concatenate([x[nb:], nxt], axis=0)
    return xm2 * w[0:1] + xm1 * w[1:2] + x * w[2:3] + xp1 * w[3:4]


def _halo_flags(i, n_ctx_tiles):
    use_prev = jnp.logical_and(i > 0, i < n_ctx_tiles)
    use_next = i < n_ctx_tiles - 1
    return use_prev, use_next


def _qkv_prep_kernel(nb, n_ctx_tiles, x_ref, prev_ref, next_ref, w_ref, o_ref):
    i = pl.program_id(0)
    part = pl.program_id(1)
    use_prev, use_next = _halo_flags(i, n_ctx_tiles)
    y = _silu(_conv_tile(x_ref[...], prev_ref[...], next_ref[...], w_ref[...], nb, use_prev, use_next))
    qscale = jnp.where(part == 0, DN_HEAD_DIM ** -0.5, 1.0).astype(F32)
    for hh in range(DN_HEADS):
        sl = slice(hh * DN_HEAD_DIM, (hh + 1) * DN_HEAD_DIM)
        yh = y[:, sl]
        inv = lax.rsqrt(jnp.sum(yh * yh, axis=-1, keepdims=True) + NORM_EPS) * qscale
        o_ref[hh] = yh * jnp.where(part < 2, inv, 1.0)


def _halo_specs(tile_rows, nb, cols, col_of, n_row_blocks16, tile_of=lambda i: i):
    prev_b = tile_rows // (2 * nb)
    next_b = tile_rows // nb
    return [
        pl.BlockSpec((tile_rows, cols), lambda i, *a: (tile_of(i), col_of(*a))),
        pl.BlockSpec((2 * nb, cols), lambda i, *a: (jnp.maximum(tile_of(i) * prev_b - 1, 0), col_of(*a))),
        pl.BlockSpec((nb, cols),
                     lambda i, *a: (jnp.minimum((tile_of(i) + 1) * next_b, n_row_blocks16 - 1), col_of(*a))),
    ]


def _qkv_prep(proj, conv_w, nb, n_ctx_tiles):
    rows = proj.shape[0]
    tr = CHUNK * nb
    return pl.pallas_call(
        functools.partial(_qkv_prep_kernel, nb, n_ctx_tiles),
        grid=(rows // tr, 3),
        in_specs=_halo_specs(tr, nb, DN_WIDTH, lambda p: p, rows // nb)
        + [pl.BlockSpec((CONV_K, DN_WIDTH), lambda i, p: (0, p))],
        out_specs=pl.BlockSpec((DN_HEADS, tr, DN_HEAD_DIM), lambda i, p: (p, i, 0)),
        out_shape=jax.ShapeDtypeStruct((3 * DN_HEADS, rows, DN_HEAD_DIM), F32),
        compiler_params=_cparams(("parallel", "parallel")),
        name="qkv_conv_norm",
    )(proj, proj, proj, conv_w)


def _bwd_chunk(i, n_ctx, n_tot):
    return jnp.where(i < n_ctx, n_ctx - 1 - i, n_tot + n_ctx - 1 - i)


def _rglru_kernel(nb, n_ctx_tiles, n_tot, xf_ref, pf_ref, nf_ref, xb_ref, pb_ref, nb_ref, ry_ref, cw_ref, cb_ref,
                  wg0_ref, wg1_ref, bg0_ref, bg1_ref, lam_ref, hf_ref, hb_ref, gy_ref, a_scr, b_scr, st_ref):
    i = pl.program_id(0)

    @pl.when(i == 0)
    def _():
        st_ref[...] = jnp.zeros_like(st_ref)

    w = RG_WIDTH
    sp = jax.nn.softplus(-lam_ref[...])
    tiles = (i, _bwd_chunk(i, n_ctx_tiles, n_tot))
    for d, (x_ref, p_ref, n_ref, wg_ref, bg_ref) in enumerate(
            ((xf_ref, pf_ref, nf_ref, wg0_ref, bg0_ref), (xb_ref, pb_ref, nb_ref, wg1_ref, bg1_ref))):
        use_prev, use_next = _halo_flags(tiles[d], n_ctx_tiles)
        xc = _conv_tile(x_ref[...], p_ref[...], n_ref[...], cw_ref[...], nb, use_prev, use_next) + cb_ref[...]
        gates = jnp.dot(xc.astype(BF16), wg_ref[...], preferred_element_type=F32) + bg_ref[...]
        r = _sigmoid(gates[:, :w])
        ig = _sigmoid(gates[:, w:])
        a = jnp.exp(-RG_C * r * sp[d:d + 1])
        y = jnp.maximum(1.0 - a * a, 0.0)
        a_scr[d] = a
        b_scr[d] = jnp.where(y > 0.0, y * lax.rsqrt(y), 0.0) * (ig * xc)
    gy_ref[...] = jax.nn.gelu(ry_ref[...])

    def body(t, carry):
        h0, h1 = carry
        r0 = pl.ds(pl.multiple_of(t * nb, nb), nb)
        r1 = pl.ds(pl.multiple_of((CHUNK - 1 - t) * nb, nb), nb)
        h0 = a_scr[0, r0, :] * h0 + b_scr[0, r0, :]
        h1 = a_scr[1, r1, :] * h1 + b_scr[1, r1, :]
        hf_ref[r0, :] = h0
        hb_ref[r1, :] = h1
        return h0, h1

    h0, h1 = lax.fori_loop(0, CHUNK, body, (st_ref[0], st_ref[1]), unroll=8)
    st_ref[0] = h0
    st_ref[1] = h1


def _rglru(proj, conv_w, conv_b, wg, bg, lam, nb, n_ctx_tiles):
    rows = proj.shape[0]
    tr = CHUNK * nb
    n_tot = rows // tr
    w = RG_WIDTH
    out = jax.ShapeDtypeStruct((rows, w), F32)
    fwd = lambda i: i
    bwd = lambda i: _bwd_chunk(i, n_ctx_tiles, n_tot)
    full = lambda shape: pl.BlockSpec(shape, lambda i: (0,) * len(shape))
    return pl.pallas_call(
        functools.partial(_rglru_kernel, nb, n_ctx_tiles, n_tot),
        grid=(n_tot,),
        in_specs=_halo_specs(tr, nb, w, lambda: COL_RX, rows // nb, fwd)
        + _halo_specs(tr, nb, w, lambda: COL_RX, rows // nb, bwd)
        + [pl.BlockSpec((tr, w), lambda i: (i, COL_RY)), full((CONV_K, w)), full((1, w)),
           pl.BlockSpec((w, 2 * w), lambda i: (0, 0)), pl.BlockSpec((w, 2 * w), lambda i: (0, 1)),
           pl.BlockSpec((1, 2 * w), lambda i: (0, 0)), pl.BlockSpec((1, 2 * w), lambda i: (0, 1)),
           full((2, w))],
        out_specs=[pl.BlockSpec((tr, w), lambda i: (fwd(i), 0)), pl.BlockSpec((tr, w), lambda i: (bwd(i), 0)),
                   pl.BlockSpec((tr, w), lambda i: (i, 0))],
        out_shape=[out] * 3,
        scratch_shapes=[pltpu.VMEM((2, tr, w), F32), pltpu.VMEM((2, tr, w), F32), pltpu.VMEM((2, nb, w), F32)],
        compiler_params=_cparams(("arbitrary",)),
        name="rglru",
    )(proj, proj, proj, proj, proj, proj, proj, conv_w, conv_b, wg, wg, bg, bg, lam)


def _bdot(a, b):
    return jnp.dot(a.astype(BF16), b.astype(BF16), preferred_element_type=F32)


def _deltanet_gates(d, raw, alog, dtb):
    c = CHUNK
    row = lax.broadcasted_iota(jnp.int32, (c, c), 0)
    col = lax.broadcasted_iota(jnp.int32, (c, c), 1)
    incl = (row >= col) if d == 0 else (row <= col)
    g_all = -jnp.exp(alog) * jax.nn.softplus(raw + dtb)
    gc_all = jnp.dot(incl.astype(F32), g_all, precision=HIGHEST, preferred_element_type=F32)
    g_tot = jnp.sum(g_all, axis=0, keepdims=True)
    return dict(incl=incl, strict=(row > col) if d == 0 else (row < col),
                diag_blk=(row // SOLVE_BLOCK) == (col // SOLVE_BLOCK),
                beta=_sigmoid(raw), gc=gc_all, gc_t=gc_all.T, egc=jnp.exp(gc_all),
                kdec=jnp.exp(g_tot - gc_all), glast=jnp.exp(g_tot))


def _deltanet_heads(gts, cbs, qs, ks, vs, ss):
    c = CHUNK
    dh = DN_HEAD_DIM
    n = len(cbs)
    rng = range(n)
    cgs = [2 * DN_HEADS + cb for cb in cbs]
    beta = [gts[i]["beta"][:, cbs[i]:cbs[i] + 1] for i in rng]
    egc = [gts[i]["egc"][:, cgs[i]:cgs[i] + 1] for i in rng]
    decay = [jnp.exp(jnp.where(gts[i]["incl"], gts[i]["gc"][:, cgs[i]:cgs[i] + 1] - gts[i]["gc_t"][cgs[i]:cgs[i] + 1, :],
                               -1e30)) for i in rng]
    kbeta = [ks[i] * beta[i] for i in rng]
    prod = [lax.dot_general(jnp.concatenate([kbeta[i], qs[i]], axis=0).astype(BF16), ks[i].astype(BF16),
                            (((1,), (1,)), ((), ())), preferred_element_type=F32) for i in rng]
    a_mat = [jnp.where(gts[i]["strict"], prod[i][:c] * decay[i], 0.0) for i in rng]
    intra = [prod[i][c:] * decay[i] for i in rng]
    p = [jnp.where(gts[i]["diag_blk"], -a_mat[i], 0.0) for i in rng]
    y = [jnp.concatenate([vs[i] * beta[i], kbeta[i] * egc[i], jnp.where(gts[i]["diag_blk"], 0.0, a_mat[i])], axis=1)
         for i in rng]
    for j in range(4):
        y = [y[i] + _bdot(p[i], y[i]) for i in rng]
        if j < 3:
            p = [_bdot(p[i], p[i]) for i in rng]
    e = [y[i][:, 2 * dh:] for i in rng]
    y = [y[i][:, :2 * dh] for i in rng]
    e2 = [_bdot(e[i], e[i]) for i in rng]
    y = [y[i] + _bdot(e2[i], y[i]) for i in rng]
    y = [y[i] - _bdot(e[i], y[i]) for i in rng]
    ws_qs = [_bdot(jnp.concatenate([y[i][:, dh:], qs[i] * egc[i]], axis=0), ss[i]) for i in rng]
    v_new = [y[i][:, :dh] - ws_qs[i][:c] for i in rng]
    o = [ws_qs[i][c:] + _bdot(intra[i], v_new[i]) for i in rng]
    s_new = [ss[i] * gts[i]["glast"][:, cgs[i]:cgs[i] + 1] + lax.dot_general(
        (ks[i] * gts[i]["kdec"][:, cgs[i]:cgs[i] + 1]).astype(BF16), v_new[i].astype(BF16),
        (((0,), (0,)), ((), ())), preferred_element_type=F32) for i in rng]
    return o, s_new


def _deltanet_kernel(nb, qf_ref, kf_ref, vf_ref, gf_ref, qb_ref, kb_ref, vb_ref, gb_ref, alog_ref, dtb_ref,
                     of_ref, ob_ref, s_ref):
    @pl.when(pl.program_id(0) == 0)
    def _():
        s_ref[...] = jnp.zeros_like(s_ref)

    dirs = ((qf_ref, kf_ref, vf_ref, gf_ref, of_ref), (qb_ref, kb_ref, vb_ref, gb_ref, ob_ref))

    def body(it, carry):
        idx = [(it * DN_BATCHES + j, d, hh) for j in range(DN_BATCHES) for d in range(2) for hh in range(DN_HEADS)]
        rows_of = {j: pl.ds(it * DN_BATCHES + j, CHUNK, stride=nb) for j in range(DN_BATCHES)}
        rows_b = [rows_of[i // (2 * DN_HEADS)] for i in range(len(idx))]
        cbs = [d * DN_HEADS + hh for _, d, hh in idx]
        gates = {(j, d): _deltanet_gates(d, dirs[d][3][rows_of[j], :], alog_ref[...], dtb_ref[...])
                 for j in range(DN_BATCHES) for d in range(2)}
        gts = [gates[(i // (2 * DN_HEADS), d)] for i, (_, d, _) in enumerate(idx)]
        qs = [dirs[d][0][hh, rows_b[i], :] for i, (_, d, hh) in enumerate(idx)]
        ks = [dirs[d][1][hh, rows_b[i], :] for i, (_, d, hh) in enumerate(idx)]
        vs = [dirs[d][2][hh, rows_b[i], :] for i, (_, d, hh) in enumerate(idx)]
        ss = [s_ref[b * (2 * DN_HEADS) + cbs[i]] for i, (b, _, _) in enumerate(idx)]
        o, s_new = _deltanet_heads(gts, cbs, qs, ks, vs, ss)
        for i, (b, d, hh) in enumerate(idx):
            dirs[d][4][hh, rows_b[i], :] = o[i]
            s_ref[b * (2 * DN_HEADS) + cbs[i]] = s_new[i]
        return carry

    lax.fori_loop(0, nb // DN_BATCHES, body, 0)


def _deltanet(qkv, proj, alog_vec, dtb_vec, nb, n_ctx_chunks):
    rows = qkv.shape[1]
    tr = CHUNK * nb
    n_tot = rows // tr
    fwd = lambda i: i
    bwd = lambda i: _bwd_chunk(i, n_ctx_chunks, n_tot)

    def specs(order):
        return [pl.BlockSpec((DN_HEADS, tr, DN_HEAD_DIM), lambda i, p=p: (p, order(i), 0)) for p in range(3)] + [
            pl.BlockSpec((tr, LANES), lambda i: (order(i), COL_GATE))]

    vec = pl.BlockSpec((1, LANES), lambda i: (0, 0))
    out = jax.ShapeDtypeStruct((DN_HEADS, rows, DN_HEAD_DIM), F32)
    return pl.pallas_call(
        functools.partial(_deltanet_kernel, nb),
        grid=(n_tot,),
        in_specs=specs(fwd) + specs(bwd) + [vec, vec],
        out_specs=[pl.BlockSpec((DN_HEADS, tr, DN_HEAD_DIM), lambda i: (0, fwd(i), 0)),
                   pl.BlockSpec((DN_HEADS, tr, DN_HEAD_DIM), lambda i: (0, bwd(i), 0))],
        out_shape=[out, out],
        scratch_shapes=[pltpu.VMEM((nb * 2 * DN_HEADS, DN_HEAD_DIM, DN_HEAD_DIM), F32)],
        compiler_params=_cparams(("arbitrary",)),
        name="deltanet",
    )(qkv, qkv, qkv, proj, qkv, qkv, qkv, proj, alog_vec, dtb_vec)


def _mixer_out_kernel(nb, of_ref, ob_ref, z_ref, hf_ref, hb_ref, gy_ref, onw_ref, w_ref, h_ref, gt_ref, g_ref, b_ref,
                      o_ref):
    z = z_ref[...]
    parts = []
    for hh in range(DN_HEADS):
        sl = slice(hh * DN_HEAD_DIM, (hh + 1) * DN_HEAD_DIM)
        oh = of_ref[hh] + ob_ref[hh]
        inv = lax.rsqrt(jnp.mean(oh * oh, axis=-1, keepdims=True) + NORM_EPS)
        parts.append(oh * inv * onw_ref[...] * _silu(z[:, sl]))
    parts.append((hf_ref[...] + hb_ref[...]) * gy_ref[...])
    y = jnp.concatenate(parts, axis=1).astype(BF16)
    u = jnp.dot(y, w_ref[...], preferred_element_type=F32)
    r = DEEP_ALPHA * h_ref[...] + _per_batch(u, gt_ref[...], nb) * u
    o_ref[...] = _layer_norm(r) * g_ref[...] + b_ref[...]


def _mixer_out(o_f, o_b, proj, hf, hb, gy, onorm_w, w_out, h, mod_rows, ln_g, ln_b, nb, ctx_rows, row0):
    rows, d = h.shape
    tm = ROW_TILE
    t0 = row0 // tm
    lat = lambda i: jnp.where((i + t0) * tm >= ctx_rows, 1, 0)
    half = pl.BlockSpec((tm, DN_WIDTH), lambda i: (i + t0, 0))
    planes = pl.BlockSpec((DN_HEADS, tm, DN_HEAD_DIM), lambda i: (0, i + t0, 0))
    vec = lambda n: pl.BlockSpec((1, n), lambda i: (0, 0))
    return pl.pallas_call(
        functools.partial(_mixer_out_kernel, nb),
        grid=((rows - row0) // tm,),
        in_specs=[planes, planes, pl.BlockSpec((tm, DN_WIDTH), lambda i: (i + t0, COL_Z)), half, half, half,
                  vec(DN_HEAD_DIM), pl.BlockSpec((d, d), lambda i: (0, 0)),
                  pl.BlockSpec((tm, d), lambda i: (i + t0, 0)),
                  pl.BlockSpec((None, nb, d), lambda i: (lat(i), 0, 2)),
                  vec(d), vec(d)],
        out_specs=pl.BlockSpec((tm, d), lambda i: (i, 0)),
        out_shape=jax.ShapeDtypeStruct((rows - row0, d), F32),
        compiler_params=_cparams(("parallel",)),
        name="mixer_out",
    )(o_f, o_b, proj, hf, hb, gy, onorm_w, w_out, h, mod_rows, ln_g, ln_b)


PAIRS_PER_GROUP = EXPERTS_PER_GROUP * (EXPERTS_PER_GROUP - 1) // 2
N_CLASSES = N_GROUPS * PAIRS_PER_GROUP
assert N_CLASSES <= LANES
INFO_CLASS, INFO_RANK, INFO_W_LO, INFO_W_HI = range(4)
Y_SUB = 2 * (D_MODEL // LANES)


ROW_SUB = D_MODEL // LANES


def _store_row_tiles(ref, x):
    n = x.shape[0]
    for c in range(ROW_SUB):
        ref[pl.ds(c, n, stride=ROW_SUB), :] = x[:, c * LANES:(c + 1) * LANES]


def _load_row_tiles(ref, n):
    return jnp.concatenate([ref[pl.ds(c, n, stride=ROW_SUB), :] for c in range(ROW_SUB)], axis=1)


def _first_index(mask, lane):
    return jnp.min(jnp.where(mask, lane, LANES), axis=-1, keepdims=True)


def _router_kernel(nb, x_ref, sh_ref, sc_ref, wr_ref, br_ref, hm_ref, info_ref, cnt_ref, run_ref):
    @pl.when(pl.program_id(0) == 0)
    def _():
        run_ref[...] = jnp.zeros_like(run_ref)

    xn = _layer_norm(x_ref[...])
    hm = xn * (1.0 + _per_batch(xn, sc_ref[...], nb)) + _per_batch(xn, sh_ref[...], nb)
    _store_row_tiles(hm_ref, hm)
    tm = hm.shape[0]
    h_hi = hm.astype(BF16)
    h_lo = (hm - h_hi.astype(F32)).astype(BF16)
    hi_hl = jnp.dot(h_hi, wr_ref[...], preferred_element_type=F32)
    logits = (hi_hl[:, :LANES] + hi_hl[:, LANES:] + jnp.dot(h_lo, wr_ref[:, :LANES], preferred_element_type=F32)
              + br_ref[...])
    lane = lax.broadcasted_iota(jnp.int32, (tm, LANES), 1)
    neg = -jnp.inf
    is_g = lane < N_GROUPS
    gmax = jnp.max(jnp.where(is_g, logits, neg), axis=-1, keepdims=True)
    g_sel = _first_index(jnp.logical_and(is_g, logits == gmax), lane)
    p_group = 1.0 / jnp.sum(jnp.where(is_g, jnp.exp(logits - gmax), 0.0), axis=-1, keepdims=True)
    lo = N_GROUPS + EXPERTS_PER_GROUP * g_sel
    in_grp = jnp.logical_and(lane >= lo, lane < lo + EXPERTS_PER_GROUP)
    m1 = jnp.max(jnp.where(in_grp, logits, neg), axis=-1, keepdims=True)
    i1 = _first_index(jnp.logical_and(in_grp, logits == m1), lane)
    rest = jnp.logical_and(in_grp, lane != i1)
    m2 = jnp.max(jnp.where(rest, logits, neg), axis=-1, keepdims=True)
    i2 = _first_index(jnp.logical_and(rest, logits == m2), lane)
    e2 = jnp.exp(m2 - m1)
    w1 = p_group / (1.0 + e2)
    w2 = p_group * e2 / (1.0 + e2)
    l1 = i1 - lo
    l2 = i2 - lo
    e_lo = jnp.minimum(l1, l2)
    e_hi = jnp.maximum(l1, l2)
    pair = jnp.right_shift(e_lo * (2 * EXPERTS_PER_GROUP - 1 - e_lo), 1) + (e_hi - e_lo - 1)
    cls = g_sel * PAIRS_PER_GROUP + pair
    first_is_lo = l1 < l2
    w_lo = jnp.where(first_is_lo, w1, w2)
    w_hi = jnp.where(first_is_lo, w2, w1)
    oh = lane == cls
    r_i = lax.broadcasted_iota(jnp.int32, (tm, tm), 0)
    c_i = lax.broadcasted_iota(jnp.int32, (tm, tm), 1)
    before = jnp.dot((r_i > c_i).astype(BF16), oh.astype(BF16), preferred_element_type=F32) + run_ref[...]
    rank = jnp.sum(jnp.where(oh, before, 0.0), axis=-1, keepdims=True)
    run_ref[...] = run_ref[...] + jnp.sum(oh.astype(F32), axis=0, keepdims=True)
    cnt_ref[...] = run_ref[...]
    info = jnp.zeros((tm, LANES), F32)
    for idx, val in ((INFO_CLASS, cls.astype(F32)), (INFO_RANK, rank), (INFO_W_LO, w_lo), (INFO_W_HI, w_hi)):
        info = jnp.where(lane == idx, val, info)
    info_ref[...] = info


def _router(x1, mod_rows, wr, br, nb, ctx_rows):
    rows, d = x1.shape
    tm = ROW_TILE
    lat = lambda i: jnp.where(i * tm >= ctx_rows, 1, 0)
    return pl.pallas_call(
        functools.partial(_router_kernel, nb),
        grid=(rows // tm,),
        in_specs=[pl.BlockSpec((tm, d), lambda i: (i, 0)),
                  pl.BlockSpec((None, nb, d), lambda i: (lat(i), 0, 3)),
                  pl.BlockSpec((None, nb, d), lambda i: (lat(i), 0, 4)),
                  pl.BlockSpec((d, 2 * LANES), lambda i: (0, 0)),
                  pl.BlockSpec((1, LANES), lambda i: (0, 0))],
        out_specs=[pl.BlockSpec((tm * ROW_SUB, LANES), lambda i: (i, 0)),
                   pl.BlockSpec((tm, LANES), lambda i: (i, 0)),
                   pl.BlockSpec((1, LANES), lambda i: (0, 0))],
        out_shape=[jax.ShapeDtypeStruct((rows * ROW_SUB, LANES), F32),
                   jax.ShapeDtypeStruct((rows, LANES), F32),
                   jax.ShapeDtypeStruct((1, LANES), F32)],
        scratch_shapes=[pltpu.VMEM((1, LANES), F32)],
        compiler_params=_cparams(("arbitrary",)),
        name="moe_router",
    )(x1, mod_rows, mod_rows, wr, br)


def _dispatch_kernel(dst_ref, hm_ref, xs_in, xs_hbm, sem):
    del xs_in
    rs = ROW_SUB
    tb = dst_ref.shape[1]
    for r in range(tb):
        dst = xs_hbm.at[pl.ds(pl.multiple_of(dst_ref[0, r], rs), rs)]
        pltpu.make_async_copy(hm_ref.at[pl.ds(r * rs, rs)], dst, sem).start(priority=r % 2)
    pltpu.make_async_copy(hm_ref, hm_ref, sem).wait()


def _dispatch(hm, dest, sorted_rows):
    tb = ROW_TILE
    n_blocks = dest.shape[0] // tb
    xs0 = jnp.zeros((sorted_rows * ROW_SUB, LANES), F32)
    return pl.pallas_call(
        _dispatch_kernel,
        grid=(n_blocks,),
        in_specs=[pl.BlockSpec((None, 1, tb), lambda i: (i, 0, 0), memory_space=pltpu.SMEM),
                  pl.BlockSpec((tb * ROW_SUB, LANES), lambda i: (i, 0)), pl.BlockSpec(memory_space=pl.ANY)],
        out_specs=pl.BlockSpec(memory_space=pl.ANY),
        out_shape=jax.ShapeDtypeStruct(xs0.shape, F32),
        scratch_shapes=[pltpu.SemaphoreType.DMA(())],
        input_output_aliases={2: 0},
        compiler_params=_cparams(("arbitrary",)),
        name="moe_dispatch",
    )((dest * ROW_SUB).reshape(n_blocks, 1, tb), hm, xs0)


def _class_ffn_kernel(telo_ref, tehi_ref, nact_ref, xs_ref, wgl_ref, wul_ref, wdl_ref, wgh_ref, wuh_ref, wdh_ref,
                      ys_ref):
    j = pl.program_id(0)
    tmx = xs_ref.shape[0] // ROW_SUB

    @pl.when(j < nact_ref[0])
    def _():
        x = _load_row_tiles(xs_ref, tmx).astype(BF16)
        for half, (wg_ref, wu_ref, wd_ref) in enumerate(((wgl_ref, wul_ref, wdl_ref), (wgh_ref, wuh_ref, wdh_ref))):
            a = jnp.dot(x, wg_ref[...], preferred_element_type=F32)
            b = jnp.dot(x, wu_ref[...], preferred_element_type=F32)
            y = jnp.dot((_silu(a) * b).astype(BF16), wd_ref[...], preferred_element_type=F32)
            for c in range(ROW_SUB):
                ys_ref[pl.ds(half * ROW_SUB + c, tmx, stride=Y_SUB), :] = y[:, c * LANES:(c + 1) * LANES]

    @pl.when(j >= nact_ref[0])
    def _():
        ys_ref[...] = jnp.zeros_like(ys_ref)


def _class_ffn(xs, tile_e_lo, tile_e_hi, n_active, w_gate, w_up, w_down, tmx):
    n_tiles = xs.shape[0] // (tmx * ROW_SUB)
    d, de = w_gate.shape[-2:]
    w_in = lambda te_idx: pl.BlockSpec((None, d, de), lambda j, *te: (te[te_idx][j], 0, 0))
    w_out = lambda te_idx: pl.BlockSpec((None, de, d), lambda j, *te: (te[te_idx][j], 0, 0))
    grid_spec = pltpu.PrefetchScalarGridSpec(
        num_scalar_prefetch=3,
        grid=(n_tiles,),
        in_specs=[pl.BlockSpec((tmx * ROW_SUB, LANES), lambda j, *te: (j, 0)),
                  w_in(0), w_in(0), w_out(0), w_in(1), w_in(1), w_out(1)],
        out_specs=pl.BlockSpec((tmx * Y_SUB, LANES), lambda j, *te: (j, 0)),
    )
    return pl.pallas_call(
        _class_ffn_kernel,
        grid_spec=grid_spec,
        out_shape=jax.ShapeDtypeStruct((n_tiles * tmx * Y_SUB, LANES), F32),
        compiler_params=_cparams(("arbitrary",)),
        name="expert_ffn",
    )(tile_e_lo, tile_e_hi, n_active, xs, w_gate, w_up, w_down, w_gate, w_up, w_down)


def _moe_out_kernel(nb, src0_ref, srcn_ref, ys_hbm, info_ref, x_ref, gt_ref, g_ref, b_ref, o_ref, ybuf, sem):
    i = pl.program_id(0)
    tm = info_ref.shape[0]
    slot = i % 2
    nslot = 1 - slot

    def gather_all(src_ref, s):
        for r in range(tm):
            pltpu.make_async_copy(ys_hbm.at[pl.ds(pl.multiple_of(src_ref[0, r], Y_SUB), Y_SUB)],
                                  ybuf.at[s, pl.ds(r * Y_SUB, Y_SUB)], sem.at[s]).start(priority=r % 2)

    def wait_rows(s):
        pltpu.make_async_copy(ybuf.at[s], ybuf.at[s], sem.at[s]).wait()

    @pl.when(i == 0)
    def _():
        gather_all(src0_ref, 0)

    wait_rows(slot)
    gather_all(srcn_ref, nslot)
    info = info_ref[...]
    yb = ybuf.at[slot]
    y_lo = jnp.concatenate([yb[pl.ds(c, tm, stride=Y_SUB), :] for c in range(ROW_SUB)], axis=1)
    y_hi = jnp.concatenate([yb[pl.ds(ROW_SUB + c, tm, stride=Y_SUB), :] for c in range(ROW_SUB)], axis=1)
    f = y_lo * info[:, INFO_W_LO:INFO_W_LO + 1] + y_hi * info[:, INFO_W_HI:INFO_W_HI + 1]
    r = DEEP_ALPHA * x_ref[...] + _per_batch(f, gt_ref[...], nb) * f
    o_ref[...] = _layer_norm(r) * g_ref[...] + b_ref[...]

    @pl.when(i == pl.num_programs(0) - 1)
    def _():
        wait_rows(nslot)


def _moe_out(ys, dest, info, x1, mod_rows, ln_g, ln_b, nb, ctx_rows):
    rows, d = x1.shape
    tm = ROW_TILE
    n_blocks = rows // tm
    lat = lambda i: jnp.where(i * tm >= ctx_rows, 1, 0)
    vec = pl.BlockSpec((1, d), lambda i: (0, 0))
    src = (dest * Y_SUB).reshape(n_blocks, 1, tm)
    smem = lambda f: pl.BlockSpec((None, 1, tm), f, memory_space=pltpu.SMEM)
    return pl.pallas_call(
        functools.partial(_moe_out_kernel, nb),
        grid=(n_blocks,),
        in_specs=[smem(lambda i: (0, 0, 0)),
                  smem(lambda i: (jnp.minimum(i + 1, n_blocks - 1), 0, 0)),
                  pl.BlockSpec(memory_space=pl.ANY),
                  pl.BlockSpec((tm, LANES), lambda i: (i, 0)),
                  pl.BlockSpec((tm, d), lambda i: (i, 0)),
                  pl.BlockSpec((None, nb, d), lambda i: (lat(i), 0, 5)),
                  vec, vec],
        out_specs=pl.BlockSpec((tm, d), lambda i: (i, 0)),
        out_shape=jax.ShapeDtypeStruct((rows, d), F32),
        scratch_shapes=[pltpu.VMEM((2, tm * Y_SUB, LANES), F32), pltpu.SemaphoreType.DMA((2,))],
        compiler_params=_cparams(("arbitrary",)),
        name="moe_out",
    )(src, src, ys, info, x1, mod_rows, ln_g, ln_b)


def _class_experts():
    lo, hi = [], []
    for g in range(N_GROUPS):
        for a in range(EXPERTS_PER_GROUP):
            for b in range(a + 1, EXPERTS_PER_GROUP):
                lo.append(g * EXPERTS_PER_GROUP + a)
                hi.append(g * EXPERTS_PER_GROUP + b)
    return np.asarray(lo, np.int32), np.asarray(hi, np.int32)


def _routing_tables(info, counts, n_tiles, tmx):
    cnt = counts[0, :N_CLASSES].astype(jnp.int32)
    padded = ((cnt + tmx - 1) // tmx) * tmx
    ends = jnp.cumsum(padded)
    starts = ends - padded
    cls = info[:, INFO_CLASS].astype(jnp.int32)
    class_ids = jnp.arange(N_CLASSES, dtype=jnp.int32)
    start_of_tok = jnp.sum(jnp.where(cls[:, None] == class_ids[None, :], starts[None, :], 0), axis=1)
    dest = start_of_tok + info[:, INFO_RANK].astype(jnp.int32)
    tile_start = jnp.arange(n_tiles, dtype=jnp.int32) * tmx
    n_before = jnp.sum((ends[None, :] <= tile_start[:, None]).astype(jnp.int32), axis=1)
    tile_cls = jnp.minimum(n_before, N_CLASSES - 1)
    e_lo, e_hi = _class_experts()
    n_active = (ends[-1:] // tmx).astype(jnp.int32)
    return dest, jnp.asarray(e_lo)[tile_cls], jnp.asarray(e_hi)[tile_cls], n_active


def _relayout_w_in(w_in):
    qkvz = w_in[..., :4 * DN_WIDTH]
    gates = w_in[..., 4 * DN_WIDTH:4 * DN_WIDTH + 4 * DN_HEADS]
    rxy = w_in[..., 4 * DN_WIDTH + 4 * DN_HEADS:]
    pad = jnp.zeros(w_in.shape[:-1] + (LANES - 4 * DN_HEADS,), w_in.dtype)
    return jnp.concatenate([qkvz, rxy, gates, pad], axis=-1).astype(BF16)


def _block_diag(w):
    n, c, _ = w.shape
    eye = jnp.eye(n, dtype=w.dtype)
    return (eye[:, None, :, None] * w[:, :, None, :]).reshape(n * c, n * c)


def _lane_vec(vals, offset):
    return jnp.zeros((1, LANES), F32).at[0, offset:offset + vals.size].set(vals.reshape(-1))


def kernel(x, c, ctx, c_ctx, w_ada, b_ada, w_in, conv_qkv_w, dn_a_log, dn_dt_bias, dn_onorm_w, rg_conv_w, rg_conv_b, rg_wa, rg_ba, rg_wi, rg_bi, rg_lambda, w_out, ln_g, ln_b, router_wg, router_bg, router_we, router_be, w_e_gate, w_e_up, w_e_down):
    nb, t_lat, d = x.shape
    t_ctx = ctx.shape[1]
    assert d == D_MODEL and nb % 8 == 0 and t_lat % GRID_W == 0 and t_ctx % CHUNK == 0
    assert (t_ctx * nb) % ROW_TILE == 0 and (t_lat * nb) % ROW_TILE == 0
    tt = t_ctx + t_lat
    rows = tt * nb
    ctx_rows = t_ctx * nb
    n_ctx_chunks = t_ctx // CHUNK

    h = jnp.concatenate([jnp.swapaxes(ctx, 0, 1), jnp.swapaxes(x, 0, 1)], axis=0).reshape(rows, d)

    n_cc = ((nb + 1 + 7) // 8) * 8
    cc = jnp.zeros((n_cc, d), F32).at[:nb].set(c).at[nb].set(c_ctx)
    mod = _modulation(cc, w_ada, b_ada)
    mod_ctx = jnp.broadcast_to(mod[:, nb:nb + 1], (DEPTH, nb, 6 * d))
    mod_rows = jnp.stack([mod_ctx, mod[:, :nb]], axis=1)

    w_in2 = _relayout_w_in(w_in)
    for l in range(DEPTH):
        row0 = ctx_rows if l == DEPTH - 1 else 0
        moe_ctx_rows = ctx_rows - row0
        n_tiles = (rows - row0) // MOE_TILE + N_CLASSES
        mr = mod_rows[l]
        proj = _in_projection(h, mr, w_in2[l], nb, ctx_rows)
        qkv = _qkv_prep(proj, conv_qkv_w[l], nb, n_ctx_chunks)
        wg = jnp.concatenate([_block_diag(rg_wa[l, 0]), _block_diag(rg_wi[l, 0]),
                              _block_diag(rg_wa[l, 1]), _block_diag(rg_wi[l, 1])], axis=1).astype(BF16)
        bg = jnp.concatenate([rg_ba[l, 0], rg_bi[l, 0], rg_ba[l, 1], rg_bi[l, 1]])[None]
        hf, hb, gy = _rglru(proj, rg_conv_w[l], rg_conv_b[l][None], wg, bg, rg_lambda[l], nb, n_ctx_chunks)
        alog_vec = _lane_vec(dn_a_log[l], 2 * DN_HEADS)
        dtb_vec = _lane_vec(dn_dt_bias[l], 2 * DN_HEADS)
        o_f, o_b = _deltanet(qkv, proj, alog_vec, dtb_vec, nb, n_ctx_chunks)
        x1 = _mixer_out(o_f, o_b, proj, hf, hb, gy, dn_onorm_w[l][None], w_out[l].astype(BF16), h, mr,
                        ln_g[l, 0][None], ln_b[l, 0][None], nb, ctx_rows, row0)
        wr = jnp.zeros((d, LANES), F32).at[:, :N_GROUPS].set(router_wg[l]).at[
            :, N_GROUPS:N_GROUPS + N_EXPERTS].set(router_we[l])
        br = _lane_vec(jnp.concatenate([router_bg[l], router_be[l]]), 0)
        wr_hi = wr.astype(BF16)
        wr_hl = jnp.concatenate([wr_hi, (wr - wr_hi.astype(F32)).astype(BF16)], axis=1)
        hm, info, counts = _router(x1, mr, wr_hl, br, nb, moe_ctx_rows)
        dest, tile_e_lo, tile_e_hi, n_active = _routing_tables(info, counts, n_tiles, MOE_TILE)
        xs = _dispatch(hm, dest, n_tiles * MOE_TILE)
        ys = _class_ffn(xs, tile_e_lo, tile_e_hi, n_active,
                        w_e_gate[l].astype(BF16), w_e_up[l].astype(BF16), w_e_down[l].astype(BF16), MOE_TILE)
        h = _moe_out(ys, dest, info, x1, mr, ln_g[l, 1][None], ln_b[l, 1][None], nb, moe_ctx_rows)

    return jnp.swapaxes(h.reshape(t_lat, nb, d), 0, 1)
```

```python
import functools
import math

import jax
import jax.numpy as jnp
import numpy as np
from jax import lax
from jax.experimental import pallas as pl
from jax.experimental.pallas import tpu as pltpu

F32 = jnp.float32
BF16 = jnp.bfloat16
HIGHEST = lax.Precision.HIGHEST

D_MODEL = 1024
DEPTH = 2
GRID_W = 64
DN_HEAD_DIM = 128
DN_WIDTH = 512
DN_HEADS = 4
RG_WIDTH = 512
RG_BLOCKS = 8
RG_BLOCK_DIM = 64
RG_C = 8.0
CONV_K = 4
N_GROUPS = 4
EXPERTS_PER_GROUP = 8
N_EXPERTS = 32
D_EXPERT = 256
DEEP_ALPHA = (2 * DEPTH) ** 0.25
LN_EPS = 1e-5
NORM_EPS = 1e-6

LANES = 128
CHUNK = 64
SOLVE_BLOCK = 16
DN_BATCHES = 2
D_PROJ = 3 * DN_WIDTH + DN_WIDTH + 2 * RG_WIDTH + LANES
COL_Z = 3
COL_RX = 4
COL_RY = 5
COL_GATE = (3 * DN_WIDTH + DN_WIDTH + 2 * RG_WIDTH) // LANES
ROW_TILE = 512
MOE_TILE = 128
VMEM_LIMIT = 56 * 1024 * 1024


def _cparams(sem):
    return pltpu.CompilerParams(dimension_semantics=sem, vmem_limit_bytes=VMEM_LIMIT)


def _layer_norm(x):
    mu = jnp.mean(x, axis=-1, keepdims=True)
    xc = x - mu
    var = jnp.mean(xc * xc, axis=-1, keepdims=True)
    return xc * lax.rsqrt(var + LN_EPS)


def _per_batch(x, v, nb):
    rows, c = x.shape
    return jnp.broadcast_to(v[None], (rows // nb, nb, c)).reshape(rows, c)


def _sigmoid(x):
    return 1.0 / (1.0 + jnp.exp(-x))


def _silu(x):
    return x * _sigmoid(x)


def _mod_kernel(cc_ref, w_ref, b_ref, o_ref):
    s = _silu(cc_ref[...])
    o_ref[...] = jnp.dot(s, w_ref[...], precision=HIGHEST, preferred_element_type=F32) + b_ref[...]


def _modulation(cc, w_ada, b_ada):
    depth, d, n6 = w_ada.shape
    rows = cc.shape[0]
    tn = 1536
    return pl.pallas_call(
        _mod_kernel,
        grid=(depth, n6 // tn),
        in_specs=[
            pl.BlockSpec((rows, d), lambda l, j: (0, 0)),
            pl.BlockSpec((None, d, tn), lambda l, j: (l, 0, j)),
            pl.BlockSpec((None, 1, tn), lambda l, j: (l, 0, j)),
        ],
        out_specs=pl.BlockSpec((None, rows, tn), lambda l, j: (l, 0, j)),
        out_shape=jax.ShapeDtypeStruct((depth, rows, n6), F32),
        compiler_params=_cparams(("parallel", "parallel")),
        name="adaln_mod",
    )(cc, w_ada, b_ada.reshape(depth, 1, n6))


def _inproj_kernel(nb, h_ref, sh_ref, sc_ref, w_ref, o_ref):
    xn = _layer_norm(h_ref[...])
    hm = xn * (1.0 + _per_batch(xn, sc_ref[...], nb)) + _per_batch(xn, sh_ref[...], nb)
    o_ref[...] = jnp.dot(hm.astype(BF16), w_ref[...], preferred_element_type=F32)


def _in_projection(h, mod_rows, w_in2, nb, ctx_rows):
    rows, d = h.shape
    tm = ROW_TILE
    lat = lambda i: jnp.where(i * tm >= ctx_rows, 1, 0)
    return pl.pallas_call(
        functools.partial(_inproj_kernel, nb),
        grid=(rows // tm,),
        in_specs=[
            pl.BlockSpec((tm, d), lambda i: (i, 0)),
            pl.BlockSpec((None, nb, d), lambda i: (lat(i), 0, 0)),
            pl.BlockSpec((None, nb, d), lambda i: (lat(i), 0, 1)),
            pl.BlockSpec((d, D_PROJ), lambda i: (0, 0)),
        ],
        out_specs=pl.BlockSpec((tm, D_PROJ), lambda i: (i, 0)),
        out_shape=jax.ShapeDtypeStruct((rows, D_PROJ), F32),
        compiler_params=_cparams(("parallel",)),
        name="ln_mod_inproj",
    )(h, mod_rows, mod_rows, w_in2)


def _conv_tile(x, prev, nxt, w, nb, use_prev, use_next):
    prev = jnp.where(use_prev, prev, 0.0)
    nxt = jnp.where(use_next, nxt, 0.0)
    xm2 = jnp.concatenate([prev, x[: -2 * nb]], axis=0)
    xm1 = jnp.concatenate([prev[nb:], x[:-nb]], axis=0)
    xp1 = jnp.concatenate([x[nb:], nxt], axis=0)
    return xm2 * w[0:1] + xm1 * w[1:2] + x * w[2:3] + xp1 * w[3:4]


def _halo_flags(i, n_ctx_tiles):
    use_prev = jnp.logical_and(i > 0, i < n_ctx_tiles)
    use_next = i < n_ctx_tiles - 1
    return use_prev, use_next


def _qkv_prep_kernel(nb, n_ctx_tiles, x_ref, prev_ref, next_ref, w_ref, o_ref):
    i = pl.program_id(0)
    part = pl.program_id(1)
    use_prev, use_next = _halo_flags(i, n_ctx_tiles)
    y = _silu(_conv_tile(x_ref[...], prev_ref[...], next_ref[...], w_ref[...], nb, use_prev, use_next))
    qscale = jnp.where(part == 0, DN_HEAD_DIM ** -0.5, 1.0).astype(F32)
    for hh in range(DN_HEADS):
        sl = slice(hh * DN_HEAD_DIM, (hh + 1) * DN_HEAD_DIM)
        yh = y[:, sl]
        inv = lax.rsqrt(jnp.sum(yh * yh, axis=-1, keepdims=True) + NORM_EPS) * qscale
        o_ref[hh] = yh * jnp.where(part < 2, inv, 1.0)


def _halo_specs(tile_rows, nb, cols, col_of, n_row_blocks16, tile_of=lambda i: i):
    prev_b = tile_rows // (2 * nb)
    next_b = tile_rows // nb
    return [
        pl.BlockSpec((tile_rows, cols), lambda i, *a: (tile_of(i), col_of(*a))),
        pl.BlockSpec((2 * nb, cols), lambda i, *a: (jnp.maximum(tile_of(i) * prev_b - 1, 0), col_of(*a))),
        pl.BlockSpec((nb, cols),
                     lambda i, *a: (jnp.minimum((tile_of(i) + 1) * next_b, n_row_blocks16 - 1), col_of(*a))),
    ]


def _qkv_prep(proj, conv_w, nb, n_ctx_tiles):
    rows = proj.shape[0]
    tr = CHUNK * nb
    return pl.pallas_call(
        functools.partial(_qkv_prep_kernel, nb, n_ctx_tiles),
        grid=(rows // tr, 3),
        in_specs=_halo_specs(tr, nb, DN_WIDTH, lambda p: p, rows // nb)
        + [pl.BlockSpec((CONV_K, DN_WIDTH), lambda i, p: (0, p))],
        out_specs=pl.BlockSpec((DN_HEADS, tr, DN_HEAD_DIM), lambda i, p: (p, i, 0)),
        out_shape=jax.ShapeDtypeStruct((3 * DN_HEADS, rows, DN_HEAD_DIM), F32),
        compiler_params=_cparams(("parallel", "parallel")),
        name="qkv_conv_norm",
    )(proj, proj, proj, conv_w)


def _bwd_chunk(i, n_ctx, n_tot):
    return jnp.where(i < n_ctx, n_ctx - 1 - i, n_tot + n_ctx - 1 - i)


def _rglru_kernel(nb, n_ctx_tiles, n_tot, xf_ref, pf_ref, nf_ref, xb_ref, pb_ref, nb_ref, cw_ref, cb_ref,
                  wg0_ref, wg1_ref, bg0_ref, bg1_ref, lam_ref, hf_ref, hb_ref, a_scr, b_scr, st_ref):
    i = pl.program_id(0)

    @pl.when(i == 0)
    def _():
        st_ref[...] = jnp.zeros_like(st_ref)

    w = RG_WIDTH
    sp = jax.nn.softplus(-lam_ref[...])
    tiles = (i, _bwd_chunk(i, n_ctx_tiles, n_tot))
    for d, (x_ref, p_ref, n_ref, wg_ref, bg_ref) in enumerate(
            ((xf_ref, pf_ref, nf_ref, wg0_ref, bg0_ref), (xb_ref, pb_ref, nb_ref, wg1_ref, bg1_ref))):
        use_prev, use_next = _halo_flags(tiles[d], n_ctx_tiles)
        xc = _conv_tile(x_ref[...], p_ref[...], n_ref[...], cw_ref[...], nb, use_prev, use_next) + cb_ref[...]
        gates = jnp.dot(xc.astype(BF16), wg_ref[...], preferred_element_type=F32) + bg_ref[...]
        r = _sigmoid(gates[:, :w])
        ig = _sigmoid(gates[:, w:])
        a = jnp.exp(-RG_C * r * sp[d:d + 1])
        y = jnp.maximum(1.0 - a * a, 0.0)
        a_scr[d] = a
        b_scr[d] = jnp.where(y > 0.0, y * lax.rsqrt(y), 0.0) * (ig * xc)

    def body(t, carry):
        h0, h1 = carry
        r0 = pl.ds(pl.multiple_of(t * nb, nb), nb)
        r1 = pl.ds(pl.multiple_of((CHUNK - 1 - t) * nb, nb), nb)
        h0 = a_scr[0, r0, :] * h0 + b_scr[0, r0, :]
        h1 = a_scr[1, r1, :] * h1 + b_scr[1, r1, :]
        hf_ref[r0, :] = h0
        hb_ref[r1, :] = h1
        return h0, h1

    h0, h1 = lax.fori_loop(0, CHUNK, body, (st_ref[0], st_ref[1]), unroll=8)
    st_ref[0] = h0
    st_ref[1] = h1


def _rglru(proj, conv_w, conv_b, wg, bg, lam, nb, n_ctx_tiles):
    rows = proj.shape[0]
    tr = CHUNK * nb
    n_tot = rows // tr
    w = RG_WIDTH
    out = jax.ShapeDtypeStruct((rows, w), F32)
    fwd = lambda i: i
    bwd = lambda i: _bwd_chunk(i, n_ctx_tiles, n_tot)
    full = lambda shape: pl.BlockSpec(shape, lambda i: (0,) * len(shape))
    return pl.pallas_call(
        functools.partial(_rglru_kernel, nb, n_ctx_tiles, n_tot),
        grid=(n_tot,),
        in_specs=_halo_specs(tr, nb, w, lambda: COL_RX, rows // nb, fwd)
        + _halo_specs(tr, nb, w, lambda: COL_RX, rows // nb, bwd)
        + [full((CONV_K, w)), full((1, w)),
           pl.BlockSpec((w, 2 * w), lambda i: (0, 0)), pl.BlockSpec((w, 2 * w), lambda i: (0, 1)),
           pl.BlockSpec((1, 2 * w), lambda i: (0, 0)), pl.BlockSpec((1, 2 * w), lambda i: (0, 1)),
           full((2, w))],
        out_specs=[pl.BlockSpec((tr, w), lambda i: (fwd(i), 0)), pl.BlockSpec((tr, w), lambda i: (bwd(i), 0))],
        out_shape=[out] * 2,
        scratch_shapes=[pltpu.VMEM((2, tr, w), F32), pltpu.VMEM((2, tr, w), F32), pltpu.VMEM((2, nb, w), F32)],
        compiler_params=_cparams(("arbitrary",)),
        name="rglru",
    )(proj, proj, proj, proj, proj, proj, conv_w, conv_b, wg, wg, bg, bg, lam)


def _bdot(a, b):
    return jnp.dot(a.astype(BF16), b.astype(BF16), preferred_element_type=F32)


def _deltanet_gates(d, raw, alog, dtb):
    c = CHUNK
    row = lax.broadcasted_iota(jnp.int32, (c, c), 0)
    col = lax.broadcasted_iota(jnp.int32, (c, c), 1)
    incl = (row >= col) if d == 0 else (row <= col)
    g_all = -jnp.exp(alog) * jax.nn.softplus(raw + dtb)
    gc_all = jnp.dot(incl.astype(F32), g_all, precision=HIGHEST, preferred_element_type=F32)
    g_tot = jnp.sum(g_all, axis=0, keepdims=True)
    return dict(incl=incl, strict=(row > col) if d == 0 else (row < col),
                diag_blk=(row // SOLVE_BLOCK) == (col // SOLVE_BLOCK),
                beta=_sigmoid(raw), gc=gc_all, gc_t=gc_all.T, egc=jnp.exp(gc_all),
                kdec=jnp.exp(g_tot - gc_all), glast=jnp.exp(g_tot))


def _deltanet_heads(gts, cbs, qs, ks, vs, ss):
    c = CHUNK
    dh = DN_HEAD_DIM
    n = len(cbs)
    rng = range(n)
    cgs = [2 * DN_HEADS + cb for cb in cbs]
    beta = [gts[i]["beta"][:, cbs[i]:cbs[i] + 1] for i in rng]
    egc = [gts[i]["egc"][:, cgs[i]:cgs[i] + 1] for i in rng]
    decay = [jnp.exp(jnp.where(gts[i]["incl"], gts[i]["gc"][:, cgs[i]:cgs[i] + 1] - gts[i]["gc_t"][cgs[i]:cgs[i] + 1, :],
                               -1e30)) for i in rng]
    kbeta = [ks[i] * beta[i] for i in rng]
    prod = [lax.dot_general(jnp.concatenate([kbeta[i], qs[i]], axis=0).astype(BF16), ks[i].astype(BF16),
                            (((1,), (1,)), ((), ())), preferred_element_type=F32) for i in rng]
    a_mat = [jnp.where(gts[i]["strict"], prod[i][:c] * decay[i], 0.0) for i in rng]
    intra = [prod[i][c:] * decay[i] for i in rng]
    p = [jnp.where(gts[i]["diag_blk"], -a_mat[i], 0.0) for i in rng]
    y = [jnp.concatenate([vs[i] * beta[i], kbeta[i] * egc[i], jnp.where(gts[i]["diag_blk"], 0.0, a_mat[i])], axis=1)
         for i in rng]
    for j in range(4):
        y = [y[i] + _bdot(p[i], y[i]) for i in rng]
        if j < 3:
            p = [_bdot(p[i], p[i]) for i in rng]
    e = [y[i][:, 2 * dh:] for i in rng]
    y = [y[i][:, :2 * dh] for i in rng]
    e2 = [_bdot(e[i], e[i]) for i in rng]
    y = [y[i] + _bdot(e2[i], y[i]) for i in rng]
    y = [y[i] - _bdot(e[i], y[i]) for i in rng]
    ws_qs = [_bdot(jnp.concatenate([y[i][:, dh:], qs[i] * egc[i]], axis=0), ss[i]) for i in rng]
    v_new = [y[i][:, :dh] - ws_qs[i][:c] for i in rng]
    o = [ws_qs[i][c:] + _bdot(intra[i], v_new[i]) for i in rng]
    s_new = [ss[i] * gts[i]["glast"][:, cgs[i]:cgs[i] + 1] + lax.dot_general(
        (ks[i] * gts[i]["kdec"][:, cgs[i]:cgs[i] + 1]).astype(BF16), v_new[i].astype(BF16),
        (((0,), (0,)), ((), ())), preferred_element_type=F32) for i in rng]
    return o, s_new


def _deltanet_kernel(nb, qf_ref, kf_ref, vf_ref, gf_ref, qb_ref, kb_ref, vb_ref, gb_ref, alog_ref, dtb_ref,
                     of_ref, ob_ref, s_ref):
    @pl.when(pl.program_id(0) == 0)
    def _():
        s_ref[...] = jnp.zeros_like(s_ref)

    dirs = ((qf_ref, kf_ref, vf_ref, gf_ref, of_ref), (qb_ref, kb_ref, vb_ref, gb_ref, ob_ref))

    def body(it, carry):
        idx = [(it * DN_BATCHES + j, d, hh) for j in range(DN_BATCHES) for d in range(2) for hh in range(DN_HEADS)]
        rows_of = {j: pl.ds(it * DN_BATCHES + j, CHUNK, stride=nb) for j in range(DN_BATCHES)}
        rows_b = [rows_of[i // (2 * DN_HEADS)] for i in range(len(idx))]
        cbs = [d * DN_HEADS + hh for _, d, hh in idx]
        gates = {(j, d): _deltanet_gates(d, dirs[d][3][rows_of[j], :], alog_ref[...], dtb_ref[...])
                 for j in range(DN_BATCHES) for d in range(2)}
        gts = [gates[(i // (2 * DN_HEADS), d)] for i, (_, d, _) in enumerate(idx)]
        qs = [dirs[d][0][hh, rows_b[i], :] for i, (_, d, hh) in enumerate(idx)]
        ks = [dirs[d][1][hh, rows_b[i], :] for i, (_, d, hh) in enumerate(idx)]
        vs = [dirs[d][2][hh, rows_b[i], :] for i, (_, d, hh) in enumerate(idx)]
        ss = [s_ref[b * (2 * DN_HEADS) + cbs[i]] for i, (b, _, _) in enumerate(idx)]
        o, s_new = _deltanet_heads(gts, cbs, qs, ks, vs, ss)
        for i, (b, d, hh) in enumerate(idx):
            dirs[d][4][hh, rows_b[i], :] = o[i]
            s_ref[b * (2 * DN_HEADS) + cbs[i]] = s_new[i]
        return carry

    lax.fori_loop(0, nb // DN_BATCHES, body, 0)


def _deltanet(qkv, proj, alog_vec, dtb_vec, nb, n_ctx_chunks):
    rows = qkv.shape[1]
    tr = CHUNK * nb
    n_tot = rows // tr
    fwd = lambda i: i
    bwd = lambda i: _bwd_chunk(i, n_ctx_chunks, n_tot)

    def specs(order):
        return [pl.BlockSpec((DN_HEADS, tr, DN_HEAD_DIM), lambda i, p=p: (p, order(i), 0)) for p in range(3)] + [
            pl.BlockSpec((tr, LANES), lambda i: (order(i), COL_GATE))]

    vec = pl.BlockSpec((1, LANES), lambda i: (0, 0))
    out = jax.ShapeDtypeStruct((DN_HEADS, rows, DN_HEAD_DIM), F32)
    return pl.pallas_call(
        functools.partial(_deltanet_kernel, nb),
        grid=(n_tot,),
        in_specs=specs(fwd) + specs(bwd) + [vec, vec],
        out_specs=[pl.BlockSpec((DN_HEADS, tr, DN_HEAD_DIM), lambda i: (0, fwd(i), 0)),
                   pl.BlockSpec((DN_HEADS, tr, DN_HEAD_DIM), lambda i: (0, bwd(i), 0))],
        out_shape=[out, out],
        scratch_shapes=[pltpu.VMEM((nb * 2 * DN_HEADS, DN_HEAD_DIM, DN_HEAD_DIM), F32)],
        compiler_params=_cparams(("arbitrary",)),
        name="deltanet",
    )(qkv, qkv, qkv, proj, qkv, qkv, qkv, proj, alog_vec, dtb_vec)


def _mixer_out_kernel(nb, of_ref, ob_ref, z_ref, hf_ref, hb_ref, ry_ref, onw_ref, w_ref, h_ref, gt_ref, g_ref, b_ref,
                      o_ref):
    z = z_ref[...]
    parts = []
    for hh in range(DN_HEADS):
        sl = slice(hh * DN_HEAD_DIM, (hh + 1) * DN_HEAD_DIM)
        oh = of_ref[hh] + ob_ref[hh]
        inv = lax.rsqrt(jnp.mean(oh * oh, axis=-1, keepdims=True) + NORM_EPS)
        parts.append(oh * inv * onw_ref[...] * _silu(z[:, sl]))
    parts.append((hf_ref[...] + hb_ref[...]) * jax.nn.gelu(ry_ref[...]))
    y = jnp.concatenate(parts, axis=1).astype(BF16)
    u = jnp.dot(y, w_ref[...], preferred_element_type=F32)
    r = DEEP_ALPHA * h_ref[...] + _per_batch(u, gt_ref[...], nb) * u
    o_ref[...] = _layer_norm(r) * g_ref[...] + b_ref[...]


def _mixer_out(o_f, o_b, proj, hf, hb, onorm_w, w_out, h, mod_rows, ln_g, ln_b, nb, ctx_rows, row0):
    rows, d = h.shape
    tm = ROW_TILE
    t0 = row0 // tm
    lat = lambda i: jnp.where((i + t0) * tm >= ctx_rows, 1, 0)
    half = pl.BlockSpec((tm, DN_WIDTH), lambda i: (i + t0, 0))
    planes = pl.BlockSpec((DN_HEADS, tm, DN_HEAD_DIM), lambda i: (0, i + t0, 0))
    vec = lambda n: pl.BlockSpec((1, n), lambda i: (0, 0))
    return pl.pallas_call(
        functools.partial(_mixer_out_kernel, nb),
        grid=((rows - row0) // tm,),
        in_specs=[planes, planes, pl.BlockSpec((tm, DN_WIDTH), lambda i: (i + t0, COL_Z)), half, half,
                  pl.BlockSpec((tm, RG_WIDTH), lambda i: (i + t0, COL_RY)),
                  vec(DN_HEAD_DIM), pl.BlockSpec((d, d), lambda i: (0, 0)),
                  pl.BlockSpec((tm, d), lambda i: (i + t0, 0)),
                  pl.BlockSpec((None, nb, d), lambda i: (lat(i), 0, 2)),
                  vec(d), vec(d)],
        out_specs=pl.BlockSpec((tm, d), lambda i: (i, 0)),
        out_shape=jax.ShapeDtypeStruct((rows - row0, d), F32),
        compiler_params=_cparams(("parallel",)),
        name="mixer_out",
    )(o_f, o_b, proj, hf, hb, proj, onorm_w, w_out, h, mod_rows, ln_g, ln_b)


PAIRS_PER_GROUP = EXPERTS_PER_GROUP * (EXPERTS_PER_GROUP - 1) // 2
N_CLASSES = N_GROUPS * PAIRS_PER_GROUP
assert N_CLASSES <= LANES
INFO_CLASS, INFO_RANK, INFO_W_LO, INFO_W_HI = range(4)
Y_SUB = 2 * (D_MODEL // LANES)


ROW_SUB = D_MODEL // LANES


def _store_row_tiles(ref, x):
    n = x.shape[0]
    for c in range(ROW_SUB):
        ref[pl.ds(c, n, stride=ROW_SUB), :] = x[:, c * LANES:(c + 1) * LANES]


def _load_row_tiles(ref, n):
    return jnp.concatenate([ref[pl.ds(c, n, stride=ROW_SUB), :] for c in range(ROW_SUB)], axis=1)


def _first_index(mask, lane):
    return jnp.min(jnp.where(mask, lane, LANES), axis=-1, keepdims=True)


def _router_kernel(nb, x_ref, sh_ref, sc_ref, wr_ref, br_ref, hm_ref, info_ref, cnt_ref, run_ref):
    @pl.when(pl.program_id(0) == 0)
    def _():
        run_ref[...] = jnp.zeros_like(run_ref)

    xn = _layer_norm(x_ref[...])
    hm = xn * (1.0 + _per_batch(xn, sc_ref[...], nb)) + _per_batch(xn, sh_ref[...], nb)
    _store_row_tiles(hm_ref, hm)
    tm = hm.shape[0]
    h_hi = hm.astype(BF16)
    h_lo = (hm - h_hi.astype(F32)).astype(BF16)
    hi_hl = jnp.dot(h_hi, wr_ref[...], preferred_element_type=F32)
    logits = (hi_hl[:, :LANES] + hi_hl[:, LANES:] + jnp.dot(h_lo, wr_ref[:, :LANES], preferred_element_type=F32)
              + br_ref[...])
    lane = lax.broadcasted_iota(jnp.int32, (tm, LANES), 1)
    neg = -jnp.inf
    is_g = lane < N_GROUPS
    gmax = jnp.max(jnp.where(is_g, logits, neg), axis=-1, keepdims=True)
    g_sel = _first_index(jnp.logical_and(is_g, logits == gmax), lane)
    p_group = 1.0 / jnp.sum(jnp.where(is_g, jnp.exp(logits - gmax), 0.0), axis=-1, keepdims=True)
    lo = N_GROUPS + EXPERTS_PER_GROUP * g_sel
    in_grp = jnp.logical_and(lane >= lo, lane < lo + EXPERTS_PER_GROUP)
    m1 = jnp.max(jnp.where(in_grp, logits, neg), axis=-1, keepdims=True)
    i1 = _first_index(jnp.logical_and(in_grp, logits == m1), lane)
    rest = jnp.logical_and(in_grp, lane != i1)
    m2 = jnp.max(jnp.where(rest, logits, neg), axis=-1, keepdims=True)
    i2 = _first_index(jnp.logical_and(rest, logits == m2), lane)
    e2 = jnp.exp(m2 - m1)
    w1 = p_group / (1.0 + e2)
    w2 = p_group * e2 / (1.0 + e2)
    l1 = i1 - lo
    l2 = i2 - lo
    e_lo = jnp.minimum(l1, l2)
    e_hi = jnp.maximum(l1, l2)
    pair = jnp.right_shift(e_lo * (2 * EXPERTS_PER_GROUP - 1 - e_lo), 1) + (e_hi - e_lo - 1)
    cls = g_sel * PAIRS_PER_GROUP + pair
    first_is_lo = l1 < l2
    w_lo = jnp.where(first_is_lo, w1, w2)
    w_hi = jnp.where(first_is_lo, w2, w1)
    oh = lane == cls
    r_i = lax.broadcasted_iota(jnp.int32, (tm, tm), 0)
    c_i = lax.broadcasted_iota(jnp.int32, (tm, tm), 1)
    before = jnp.dot((r_i > c_i).astype(BF16), oh.astype(BF16), preferred_element_type=F32) + run_ref[...]
    rank = jnp.sum(jnp.where(oh, before, 0.0), axis=-1, keepdims=True)
    run_ref[...] = run_ref[...] + jnp.sum(oh.astype(F32), axis=0, keepdims=True)
    cnt_ref[...] = run_ref[...]
    info = jnp.zeros((tm, LANES), F32)
    for idx, val in ((INFO_CLASS, cls.astype(F32)), (INFO_RANK, rank), (INFO_W_LO, w_lo), (INFO_W_HI, w_hi)):
        info = jnp.where(lane == idx, val, info)
    info_ref[...] = info


def _router(x1, mod_rows, wr, br, nb, ctx_rows):
    rows, d = x1.shape
    tm = ROW_TILE
    lat = lambda i: jnp.where(i * tm >= ctx_rows, 1, 0)
    return pl.pallas_call(
        functools.partial(_router_kernel, nb),
        grid=(rows // tm,),
        in_specs=[pl.BlockSpec((tm, d), lambda i: (i, 0)),
                  pl.BlockSpec((None, nb, d), lambda i: (lat(i), 0, 3)),
                  pl.BlockSpec((None, nb, d), lambda i: (lat(i), 0, 4)),
                  pl.BlockSpec((d, 2 * LANES), lambda i: (0, 0)),
                  pl.BlockSpec((1, LANES), lambda i: (0, 0))],
        out_specs=[pl.BlockSpec((tm * ROW_SUB, LANES), lambda i: (i, 0)),
                   pl.BlockSpec((tm, LANES), lambda i: (i, 0)),
                   pl.BlockSpec((1, LANES), lambda i: (0, 0))],
        out_shape=[jax.ShapeDtypeStruct((rows * ROW_SUB, LANES), F32),
                   jax.ShapeDtypeStruct((rows, LANES), F32),
                   jax.ShapeDtypeStruct((1, LANES), F32)],
        scratch_shapes=[pltpu.VMEM((1, LANES), F32)],
        compiler_params=_cparams(("arbitrary",)),
        name="moe_router",
    )(x1, mod_rows, mod_rows, wr, br)


def _dispatch_kernel(dst_ref, hm_ref, xs_in, xs_hbm, sem):
    del xs_in
    rs = ROW_SUB
    tb = dst_ref.shape[1]
    for r in range(tb):
        dst = xs_hbm.at[pl.ds(pl.multiple_of(dst_ref[0, r], rs), rs)]
        pltpu.make_async_copy(hm_ref.at[pl.ds(r * rs, rs)], dst, sem).start(priority=r % 2)
    pltpu.make_async_copy(hm_ref, hm_ref, sem).wait()


def _dispatch(hm, dest, sorted_rows):
    tb = ROW_TILE
    n_blocks = dest.shape[0] // tb
    xs0 = jnp.zeros((sorted_rows * ROW_SUB, LANES), F32)
    return pl.pallas_call(
        _dispatch_kernel,
        grid=(n_blocks,),
        in_specs=[pl.BlockSpec((None, 1, tb), lambda i: (i, 0, 0), memory_space=pltpu.SMEM),
                  pl.BlockSpec((tb * ROW_SUB, LANES), lambda i: (i, 0)), pl.BlockSpec(memory_space=pl.ANY)],
        out_specs=pl.BlockSpec(memory_space=pl.ANY),
        out_shape=jax.ShapeDtypeStruct(xs0.shape, F32),
        scratch_shapes=[pltpu.SemaphoreType.DMA(())],
        input_output_aliases={2: 0},
        compiler_params=_cparams(("arbitrary",)),
        name="moe_dispatch",
    )((dest * ROW_SUB).reshape(n_blocks, 1, tb), hm, xs0)


def _class_ffn_kernel(telo_ref, tehi_ref, nact_ref, xs_ref, wgul_ref, wdl_ref, wguh_ref, wdh_ref, ys_ref, y_scr):
    j = pl.program_id(0)
    tmx = xs_ref.shape[0] // ROW_SUB
    de = wdl_ref.shape[0]

    @pl.when(j < nact_ref[0])
    def _():
        x = _load_row_tiles(xs_ref, tmx).astype(BF16)
        for half, (wgu_ref, wd_ref) in enumerate(((wgul_ref, wdl_ref), (wguh_ref, wdh_ref))):
            ab = jnp.dot(x, wgu_ref[...], preferred_element_type=F32)
            y = jnp.dot((_silu(ab[:, :de]) * ab[:, de:]).astype(BF16), wd_ref[...], preferred_element_type=F32)
            for c in range(ROW_SUB):
                y_scr[pl.ds(half * ROW_SUB + c, tmx, stride=Y_SUB), :] = y[:, c * LANES:(c + 1) * LANES]
        ys_ref[...] = y_scr[...].astype(BF16)

    @pl.when(j >= nact_ref[0])
    def _():
        ys_ref[...] = jnp.zeros_like(ys_ref)


def _class_ffn(xs, tile_e_lo, tile_e_hi, n_active, w_gate_up, w_down, tmx):
    n_tiles = xs.shape[0] // (tmx * ROW_SUB)
    d, de2 = w_gate_up.shape[-2:]
    w_in = lambda te_idx: pl.BlockSpec((None, d, de2), lambda j, *te: (te[te_idx][j], 0, 0))
    w_out = lambda te_idx: pl.BlockSpec((None, de2 // 2, d), lambda j, *te: (te[te_idx][j], 0, 0))
    grid_spec = pltpu.PrefetchScalarGridSpec(
        num_scalar_prefetch=3,
        grid=(n_tiles,),
        in_specs=[pl.BlockSpec((tmx * ROW_SUB, LANES), lambda j, lo, hi, na: (jnp.minimum(j, na[0] - 1), 0)),
                  w_in(0), w_out(0), w_in(1), w_out(1)],
        out_specs=pl.BlockSpec((tmx * Y_SUB, LANES), lambda j, lo, hi, na: (j, 0)),
        scratch_shapes=[pltpu.VMEM((tmx * Y_SUB, LANES), F32)],
    )
    return pl.pallas_call(
        _class_ffn_kernel,
        grid_spec=grid_spec,
        out_shape=jax.ShapeDtypeStruct((n_tiles * tmx * Y_SUB, LANES), BF16),
        compiler_params=_cparams(("arbitrary",)),
        name="expert_ffn",
    )(tile_e_lo, tile_e_hi, n_active, xs, w_gate_up, w_down, w_gate_up, w_down)


def _moe_out_kernel(nb, src0_ref, srcn_ref, ys_hbm, info_ref, x_ref, gt_ref, g_ref, b_ref, o_ref, ybuf, y_scr, sem):
    i = pl.program_id(0)
    tm = info_ref.shape[0]
    slot = i % 2
    nslot = 1 - slot

    def gather_all(src_ref, s):
        for r in range(tm):
            pltpu.make_async_copy(ys_hbm.at[pl.ds(pl.multiple_of(src_ref[0, r], Y_SUB), Y_SUB)],
                                  ybuf.at[s, pl.ds(r * Y_SUB, Y_SUB)], sem.at[s]).start(priority=r % 2)

    def wait_rows(s):
        pltpu.make_async_copy(ybuf.at[s], ybuf.at[s], sem.at[s]).wait()

    @pl.when(i == 0)
    def _():
        gather_all(src0_ref, 0)

    wait_rows(slot)
    gather_all(srcn_ref, nslot)
    info = info_ref[...]
    y_scr[...] = ybuf[slot].astype(F32)
    y_lo = jnp.concatenate([y_scr[pl.ds(c, tm, stride=Y_SUB), :] for c in range(ROW_SUB)], axis=1)
    y_hi = jnp.concatenate([y_scr[pl.ds(ROW_SUB + c, tm, stride=Y_SUB), :] for c in range(ROW_SUB)], axis=1)
    f = y_lo * info[:, INFO_W_LO:INFO_W_LO + 1] + y_hi * info[:, INFO_W_HI:INFO_W_HI + 1]
    r = DEEP_ALPHA * x_ref[...] + _per_batch(f, gt_ref[...], nb) * f
    o_ref[...] = _layer_norm(r) * g_ref[...] + b_ref[...]

    @pl.when(i == pl.num_programs(0) - 1)
    def _():
        wait_rows(nslot)


def _moe_out(ys, dest, info, x1, mod_rows, ln_g, ln_b, nb, ctx_rows):
    rows, d = x1.shape
    tm = ROW_TILE
    n_blocks = rows // tm
    lat = lambda i: jnp.where(i * tm >= ctx_rows, 1, 0)
    vec = pl.BlockSpec((1, d), lambda i: (0, 0))
    src = (dest * Y_SUB).reshape(n_blocks, 1, tm)
    smem = lambda f: pl.BlockSpec((None, 1, tm), f, memory_space=pltpu.SMEM)
    return pl.pallas_call(
        functools.partial(_moe_out_kernel, nb),
        grid=(n_blocks,),
        in_specs=[smem(lambda i: (0, 0, 0)),
                  smem(lambda i: (jnp.minimum(i + 1, n_blocks - 1), 0, 0)),
                  pl.BlockSpec(memory_space=pl.ANY),
                  pl.BlockSpec((tm, LANES), lambda i: (i, 0)),
                  pl.BlockSpec((tm, d), lambda i: (i, 0)),
                  pl.BlockSpec((None, nb, d), lambda i: (lat(i), 0, 5)),
                  vec, vec],
        out_specs=pl.BlockSpec((tm, d), lambda i: (i, 0)),
        out_shape=jax.ShapeDtypeStruct((rows, d), F32),
        scratch_shapes=[pltpu.VMEM((2, tm * Y_SUB, LANES), BF16), pltpu.VMEM((tm * Y_SUB, LANES), F32),
                        pltpu.SemaphoreType.DMA((2,))],
        compiler_params=_cparams(("arbitrary",)),
        name="moe_out",
    )(src, src, ys, info, x1, mod_rows, ln_g, ln_b)


def _class_experts():
    lo, hi = [], []
    for g in range(N_GROUPS):
        for a in range(EXPERTS_PER_GROUP):
            for b in range(a + 1, EXPERTS_PER_GROUP):
                lo.append(g * EXPERTS_PER_GROUP + a)
                hi.append(g * EXPERTS_PER_GROUP + b)
    return np.asarray(lo, np.int32), np.asarray(hi, np.int32)


def _routing_tables(info, counts, n_tiles, tmx):
    cnt = counts[0, :N_CLASSES].astype(jnp.int32)
    padded = ((cnt + tmx - 1) // tmx) * tmx
    ends = jnp.cumsum(padded)
    starts = ends - padded
    cls = info[:, INFO_CLASS].astype(jnp.int32)
    class_ids = jnp.arange(N_CLASSES, dtype=jnp.int32)
    start_of_tok = jnp.sum(jnp.where(cls[:, None] == class_ids[None, :], starts[None, :], 0), axis=1)
    dest = start_of_tok + info[:, INFO_RANK].astype(jnp.int32)
    tile_start = jnp.arange(n_tiles, dtype=jnp.int32) * tmx
    n_before = jnp.sum((ends[None, :] <= tile_start[:, None]).astype(jnp.int32), axis=1)
    tile_cls = jnp.minimum(n_before, N_CLASSES - 1)
    e_lo, e_hi = _class_experts()
    n_active = (ends[-1:] // tmx).astype(jnp.int32)
    return dest, jnp.asarray(e_lo)[tile_cls], jnp.asarray(e_hi)[tile_cls], n_active


def _relayout_w_in(w_in):
    qkvz = w_in[..., :4 * DN_WIDTH]
    gates = w_in[..., 4 * DN_WIDTH:4 * DN_WIDTH + 4 * DN_HEADS]
    rxy = w_in[..., 4 * DN_WIDTH + 4 * DN_HEADS:]
    pad = jnp.zeros(w_in.shape[:-1] + (LANES - 4 * DN_HEADS,), w_in.dtype)
    return jnp.concatenate([qkvz, rxy, gates, pad], axis=-1).astype(BF16)


def _block_diag(w):
    n, c, _ = w.shape
    eye = jnp.eye(n, dtype=w.dtype)
    return (eye[:, None, :, None] * w[:, :, None, :]).reshape(n * c, n * c)


def _lane_vec(vals, offset):
    return jnp.zeros((1, LANES), F32).at[0, offset:offset + vals.size].set(vals.reshape(-1))


def kernel(x, c, ctx, c_ctx, w_ada, b_ada, w_in, conv_qkv_w, dn_a_log, dn_dt_bias, dn_onorm_w, rg_conv_w, rg_conv_b, rg_wa, rg_ba, rg_wi, rg_bi, rg_lambda, w_out, ln_g, ln_b, router_wg, router_bg, router_we, router_be, w_e_gate, w_e_up, w_e_down):
    nb, t_lat, d = x.shape
    t_ctx = ctx.shape[1]
    assert d == D_MODEL and nb % 8 == 0 and t_lat % GRID_W == 0 and t_ctx % CHUNK == 0
    assert (t_ctx * nb) % ROW_TILE == 0 and (t_lat * nb) % ROW_TILE == 0
    tt = t_ctx + t_lat
    rows = tt * nb
    ctx_rows = t_ctx * nb
    n_ctx_chunks = t_ctx // CHUNK

    h = jnp.concatenate([jnp.swapaxes(ctx, 0, 1), jnp.swapaxes(x, 0, 1)], axis=0).reshape(rows, d)

    n_cc = ((nb + 1 + 7) // 8) * 8
    cc = jnp.zeros((n_cc, d), F32).at[:nb].set(c).at[nb].set(c_ctx)
    mod = _modulation(cc, w_ada, b_ada)
    mod_ctx = jnp.broadcast_to(mod[:, nb:nb + 1], (DEPTH, nb, 6 * d))
    mod_rows = jnp.stack([mod_ctx, mod[:, :nb]], axis=1)

    w_in2 = _relayout_w_in(w_in)
    for l in range(DEPTH):
        row0 = ctx_rows if l == DEPTH - 1 else 0
        moe_ctx_rows = ctx_rows - row0
        n_tiles = (rows - row0) // MOE_TILE + N_CLASSES
        mr = mod_rows[l]
        proj = _in_projection(h, mr, w_in2[l], nb, ctx_rows)
        qkv = _qkv_prep(proj, conv_qkv_w[l], nb, n_ctx_chunks)
        wg = jnp.concatenate([_block_diag(rg_wa[l, 0]), _block_diag(rg_wi[l, 0]),
                              _block_diag(rg_wa[l, 1]), _block_diag(rg_wi[l, 1])], axis=1).astype(BF16)
        bg = jnp.concatenate([rg_ba[l, 0], rg_bi[l, 0], rg_ba[l, 1], rg_bi[l, 1]])[None]
        hf, hb = _rglru(proj, rg_conv_w[l], rg_conv_b[l][None], wg, bg, rg_lambda[l], nb, n_ctx_chunks)
        alog_vec = _lane_vec(dn_a_log[l], 2 * DN_HEADS)
        dtb_vec = _lane_vec(dn_dt_bias[l], 2 * DN_HEADS)
        o_f, o_b = _deltanet(qkv, proj, alog_vec, dtb_vec, nb, n_ctx_chunks)
        x1 = _mixer_out(o_f, o_b, proj, hf, hb, dn_onorm_w[l][None], w_out[l].astype(BF16), h, mr,
                        ln_g[l, 0][None], ln_b[l, 0][None], nb, ctx_rows, row0)
        wr = jnp.zeros((d, LANES), F32).at[:, :N_GROUPS].set(router_wg[l]).at[
            :, N_GROUPS:N_GROUPS + N_EXPERTS].set(router_we[l])
        br = _lane_vec(jnp.concatenate([router_bg[l], router_be[l]]), 0)
        wr_hi = wr.astype(BF16)
        wr_hl = jnp.concatenate([wr_hi, (wr - wr_hi.astype(F32)).astype(BF16)], axis=1)
        hm, info, counts = _router(x1, mr, wr_hl, br, nb, moe_ctx_rows)
        dest, tile_e_lo, tile_e_hi, n_active = _routing_tables(info, counts, n_tiles, MOE_TILE)
        xs = _dispatch(hm, dest, n_tiles * MOE_TILE)
        w_gate_up = jnp.concatenate([w_e_gate[l], w_e_up[l]], axis=-1).astype(BF16)
        ys = _class_ffn(xs, tile_e_lo, tile_e_hi, n_active, w_gate_up, w_e_down[l].astype(BF16), MOE_TILE)
        h = _moe_out(ys, dest, info, x1, mr, ln_g[l, 1][None], ln_b[l, 1][None], nb, moe_ctx_rows)

    return jnp.swapaxes(h.reshape(t_lat, nb, d), 0, 1)
```

```python
import functools
import math

import jax
import jax.numpy as jnp
import numpy as np
from jax import lax
from jax.experimental import pallas as pl
from jax.experimental.pallas import tpu as pltpu

F32 = jnp.float32
BF16 = jnp.bfloat16
HIGHEST = lax.Precision.HIGHEST

D_MODEL = 1024
DEPTH = 2
GRID_W = 64
DN_HEAD_DIM = 128
DN_WIDTH = 512
DN_HEADS = 4
RG_WIDTH = 512
RG_BLOCKS = 8
RG_BLOCK_DIM = 64
RG_C = 8.0
CONV_K = 4
N_GROUPS = 4
EXPERTS_PER_GROUP = 8
N_EXPERTS = 32
D_EXPERT = 256
DEEP_ALPHA = (2 * DEPTH) ** 0.25
LN_EPS = 1e-5
NORM_EPS = 1e-6

LANES = 128
CHUNK = 64
SOLVE_BLOCK = 16
DN_BATCHES = 2
D_PROJ = 3 * DN_WIDTH + DN_WIDTH + 2 * RG_WIDTH + LANES
COL_Z = 3
COL_RX = 4
COL_RY = 5
COL_GATE = (3 * DN_WIDTH + DN_WIDTH + 2 * RG_WIDTH) // LANES
ROW_TILE = 512
MOE_TILE = 256
VMEM_LIMIT = 56 * 1024 * 1024


def _cparams(sem):
    return pltpu.CompilerParams(dimension_semantics=sem, vmem_limit_bytes=VMEM_LIMIT)


def _layer_norm(x):
    mu = jnp.mean(x, axis=-1, keepdims=True)
    xc = x - mu
    var = jnp.mean(xc * xc, axis=-1, keepdims=True)
    return xc * lax.rsqrt(var + LN_EPS)


def _per_batch(x, v, nb):
    rows, c = x.shape
    return jnp.broadcast_to(v[None], (rows // nb, nb, c)).reshape(rows, c)


def _sigmoid(x):
    return 1.0 / (1.0 + jnp.exp(-x))


def _silu(x):
    return x * _sigmoid(x)


def _mod_kernel(cc_ref, w_ref, b_ref, o_ref):
    s = _silu(cc_ref[...])
    o_ref[...] = jnp.dot(s, w_ref[...], precision=HIGHEST, preferred_element_type=F32) + b_ref[...]


def _modulation(cc, w_ada, b_ada):
    depth, d, n6 = w_ada.shape
    rows = cc.shape[0]
    tn = 1536
    return pl.pallas_call(
        _mod_kernel,
        grid=(depth, n6 // tn),
        in_specs=[
            pl.BlockSpec((rows, d), lambda l, j: (0, 0)),
            pl.BlockSpec((None, d, tn), lambda l, j: (l, 0, j)),
            pl.BlockSpec((None, 1, tn), lambda l, j: (l, 0, j)),
        ],
        out_specs=pl.BlockSpec((None, rows, tn), lambda l, j: (l, 0, j)),
        out_shape=jax.ShapeDtypeStruct((depth, rows, n6), F32),
        compiler_params=_cparams(("parallel", "parallel")),
        name="adaln_mod",
    )(cc, w_ada, b_ada.reshape(depth, 1, n6))


def _inproj_kernel(nb, h_ref, sh_ref, sc_ref, w_ref, o_ref):
    xn = _layer_norm(h_ref[...])
    hm = xn * (1.0 + _per_batch(xn, sc_ref[...], nb)) + _per_batch(xn, sh_ref[...], nb)
    o_ref[...] = jnp.dot(hm.astype(BF16), w_ref[...], preferred_element_type=F32)


def _in_projection(h, mod_rows, w_in2, nb, ctx_rows):
    rows, d = h.shape
    tm = ROW_TILE
    lat = lambda i: jnp.where(i * tm >= ctx_rows, 1, 0)
    return pl.pallas_call(
        functools.partial(_inproj_kernel, nb),
        grid=(rows // tm,),
        in_specs=[
            pl.BlockSpec((tm, d), lambda i: (i, 0)),
            pl.BlockSpec((None, nb, d), lambda i: (lat(i), 0, 0)),
            pl.BlockSpec((None, nb, d), lambda i: (lat(i), 0, 1)),
            pl.BlockSpec((d, D_PROJ), lambda i: (0, 0)),
        ],
        out_specs=pl.BlockSpec((tm, D_PROJ), lambda i: (i, 0)),
        out_shape=jax.ShapeDtypeStruct((rows, D_PROJ), F32),
        compiler_params=_cparams(("parallel",)),
        name="ln_mod_inproj",
    )(h, mod_rows, mod_rows, w_in2)


def _conv_tile(x, prev, nxt, w, nb, use_prev, use_next):
    prev = jnp.where(use_prev, prev, 0.0)
    nxt = jnp.where(use_next, nxt, 0.0)
    xm2 = jnp.concatenate([prev, x[: -2 * nb]], axis=0)
    xm1 = jnp.concatenate([prev[nb:], x[:-nb]], axis=0)
    xp1 = jnp.concatenate([x[nb:], nxt], axis=0)
    return xm2 * w[0:1] + xm1 * w[1:2] + x * w[2:3] + xp1 * w[3:4]


def _halo_flags(i, n_ctx_tiles):
    use_prev = jnp.logical_and(i > 0, i < n_ctx_tiles)
    use_next = i < n_ctx_tiles - 1
    return use_prev, use_next


def _qkv_prep_kernel(nb, n_ctx_tiles, x_ref, prev_ref, next_ref, w_ref, o_ref):
    i = pl.program_id(0)
    part = pl.program_id(1)
    use_prev, use_next = _halo_flags(i, n_ctx_tiles)
    y = _silu(_conv_tile(x_ref[...], prev_ref[...], next_ref[...], w_ref[...], nb, use_prev, use_next))
    qscale = jnp.where(part == 0, DN_HEAD_DIM ** -0.5, 1.0).astype(F32)
    for hh in range(DN_HEADS):
        sl = slice(hh * DN_HEAD_DIM, (hh + 1) * DN_HEAD_DIM)
        yh = y[:, sl]
        inv = lax.rsqrt(jnp.sum(yh * yh, axis=-1, keepdims=True) + NORM_EPS) * qscale
        o_ref[hh] = yh * jnp.where(part < 2, inv, 1.0)


def _halo_specs(tile_rows, nb, cols, col_of, n_row_blocks16, tile_of=lambda i: i):
    prev_b = tile_rows // (2 * nb)
    next_b = tile_rows // nb
    return [
        pl.BlockSpec((tile_rows, cols), lambda i, *a: (tile_of(i), col_of(*a))),
        pl.BlockSpec((2 * nb, cols), lambda i, *a: (jnp.maximum(tile_of(i) * prev_b - 1, 0), col_of(*a))),
        pl.BlockSpec((nb, cols),
                     lambda i, *a: (jnp.minimum((tile_of(i) + 1) * next_b, n_row_blocks16 - 1), col_of(*a))),
    ]


def _qkv_prep(proj, conv_w, nb, n_ctx_tiles):
    rows = proj.shape[0]
    tr = CHUNK * nb
    return pl.pallas_call(
        functools.partial(_qkv_prep_kernel, nb, n_ctx_tiles),
        grid=(rows // tr, 3),
        in_specs=_halo_specs(tr, nb, DN_WIDTH, lambda p: p, rows // nb)
        + [pl.BlockSpec((CONV_K, DN_WIDTH), lambda i, p: (0, p))],
        out_specs=pl.BlockSpec((DN_HEADS, tr, DN_HEAD_DIM), lambda i, p: (p, i, 0)),
        out_shape=jax.ShapeDtypeStruct((3 * DN_HEADS, rows, DN_HEAD_DIM), F32),
        compiler_params=_cparams(("parallel", "parallel")),
        name="qkv_conv_norm",
    )(proj, proj, proj, conv_w)


def _bwd_chunk(i, n_ctx, n_tot):
    return jnp.where(i < n_ctx, n_ctx - 1 - i, n_tot + n_ctx - 1 - i)


def _rglru_kernel(nb, n_ctx_tiles, n_tot, xf_ref, pf_ref, nf_ref, xb_ref, pb_ref, nb_ref, cw_ref, cb_ref,
                  wg0_ref, wg1_ref, bg0_ref, bg1_ref, lam_ref, hf_ref, hb_ref, a_scr, b_scr, st_ref):
    i = pl.program_id(0)

    @pl.when(i == 0)
    def _():
        st_ref[...] = jnp.zeros_like(st_ref)

    w = RG_WIDTH
    sp = jax.nn.softplus(-lam_ref[...])
    tiles = (i, _bwd_chunk(i, n_ctx_tiles, n_tot))
    for d, (x_ref, p_ref, n_ref, wg_ref, bg_ref) in enumerate(
            ((xf_ref, pf_ref, nf_ref, wg0_ref, bg0_ref), (xb_ref, pb_ref, nb_ref, wg1_ref, bg1_ref))):
        use_prev, use_next = _halo_flags(tiles[d], n_ctx_tiles)
        xc = _conv_tile(x_ref[...], p_ref[...], n_ref[...], cw_ref[...], nb, use_prev, use_next) + cb_ref[...]
        gates = jnp.dot(xc.astype(BF16), wg_ref[...], preferred_element_type=F32) + bg_ref[...]
        r = _sigmoid(gates[:, :w])
        ig = _sigmoid(gates[:, w:])
        a = jnp.exp(-RG_C * r * sp[d:d + 1])
        y = jnp.maximum(1.0 - a * a, 0.0)
        a_scr[d] = a
        b_scr[d] = jnp.where(y > 0.0, y * lax.rsqrt(y), 0.0) * (ig * xc)

    def body(t, carry):
        h0, h1 = carry
        r0 = pl.ds(pl.multiple_of(t * nb, nb), nb)
        r1 = pl.ds(pl.multiple_of((CHUNK - 1 - t) * nb, nb), nb)
        h0 = a_scr[0, r0, :] * h0 + b_scr[0, r0, :]
        h1 = a_scr[1, r1, :] * h1 + b_scr[1, r1, :]
        hf_ref[r0, :] = h0
        hb_ref[r1, :] = h1
        return h0, h1

    h0, h1 = lax.fori_loop(0, CHUNK, body, (st_ref[0], st_ref[1]), unroll=8)
    st_ref[0] = h0
    st_ref[1] = h1


def _rglru(proj, conv_w, conv_b, wg, bg, lam, nb, n_ctx_tiles):
    rows = proj.shape[0]
    tr = CHUNK * nb
    n_tot = rows // tr
    w = RG_WIDTH
    out = jax.ShapeDtypeStruct((rows, w), F32)
    fwd = lambda i: i
    bwd = lambda i: _bwd_chunk(i, n_ctx_tiles, n_tot)
    full = lambda shape: pl.BlockSpec(shape, lambda i: (0,) * len(shape))
    return pl.pallas_call(
        functools.partial(_rglru_kernel, nb, n_ctx_tiles, n_tot),
        grid=(n_tot,),
        in_specs=_halo_specs(tr, nb, w, lambda: COL_RX, rows // nb, fwd)
        + _halo_specs(tr, nb, w, lambda: COL_RX, rows // nb, bwd)
        + [full((CONV_K, w)), full((1, w)),
           pl.BlockSpec((w, 2 * w), lambda i: (0, 0)), pl.BlockSpec((w, 2 * w), lambda i: (0, 1)),
           pl.BlockSpec((1, 2 * w), lambda i: (0, 0)), pl.BlockSpec((1, 2 * w), lambda i: (0, 1)),
           full((2, w))],
        out_specs=[pl.BlockSpec((tr, w), lambda i: (fwd(i), 0)), pl.BlockSpec((tr, w), lambda i: (bwd(i), 0))],
        out_shape=[out] * 2,
        scratch_shapes=[pltpu.VMEM((2, tr, w), F32), pltpu.VMEM((2, tr, w), F32), pltpu.VMEM((2, nb, w), F32)],
        compiler_params=_cparams(("arbitrary",)),
        name="rglru",
    )(proj, proj, proj, proj, proj, proj, conv_w, conv_b, wg, wg, bg, bg, lam)


def _bdot(a, b):
    return jnp.dot(a.astype(BF16), b.astype(BF16), preferred_element_type=F32)


def _deltanet_gates(d, raw, alog, dtb):
    c = CHUNK
    row = lax.broadcasted_iota(jnp.int32, (c, c), 0)
    col = lax.broadcasted_iota(jnp.int32, (c, c), 1)
    incl = (row >= col) if d == 0 else (row <= col)
    g_all = -jnp.exp(alog) * jax.nn.softplus(raw + dtb)
    gc_all = jnp.dot(incl.astype(F32), g_all, precision=HIGHEST, preferred_element_type=F32)
    g_tot = jnp.sum(g_all, axis=0, keepdims=True)
    return dict(incl=incl, strict=(row > col) if d == 0 else (row < col),
                diag_blk=(row // SOLVE_BLOCK) == (col // SOLVE_BLOCK),
                beta=_sigmoid(raw), gc=gc_all, gc_t=gc_all.T, egc=jnp.exp(gc_all),
                kdec=jnp.exp(g_tot - gc_all), glast=jnp.exp(g_tot))


def _deltanet_heads(gts, cbs, qs, ks, vs, ss):
    c = CHUNK
    dh = DN_HEAD_DIM
    n = len(cbs)
    rng = range(n)
    cgs = [2 * DN_HEADS + cb for cb in cbs]
    beta = [gts[i]["beta"][:, cbs[i]:cbs[i] + 1] for i in rng]
    egc = [gts[i]["egc"][:, cgs[i]:cgs[i] + 1] for i in rng]
    decay = [jnp.exp(jnp.where(gts[i]["incl"], gts[i]["gc"][:, cgs[i]:cgs[i] + 1] - gts[i]["gc_t"][cgs[i]:cgs[i] + 1, :],
                               -1e30)) for i in rng]
    kbeta = [ks[i] * beta[i] for i in rng]
    prod = [lax.dot_general(jnp.concatenate([kbeta[i], qs[i]], axis=0).astype(BF16), ks[i].astype(BF16),
                            (((1,), (1,)), ((), ())), preferred_element_type=F32) for i in rng]
    a_mat = [jnp.where(gts[i]["strict"], prod[i][:c] * decay[i], 0.0) for i in rng]
    intra = [prod[i][c:] * decay[i] for i in rng]
    ks_qs = [_bdot(jnp.concatenate([kbeta[i] * egc[i], qs[i] * egc[i]], axis=0), ss[i]) for i in rng]
    p = [jnp.where(gts[i]["diag_blk"], -a_mat[i], 0.0) for i in rng]
    y = [jnp.concatenate([vs[i] * beta[i] - ks_qs[i][:c], jnp.where(gts[i]["diag_blk"], 0.0, a_mat[i])], axis=1)
         for i in rng]
    for j in range(4):
        y = [y[i] + _bdot(p[i], y[i]) for i in rng]
        if j < 3:
            p = [_bdot(p[i], p[i]) for i in rng]
    e = [y[i][:, dh:] for i in rng]
    y = [y[i][:, :dh] for i in rng]
    e2 = [_bdot(e[i], e[i]) for i in rng]
    y = [y[i] + _bdot(e2[i], y[i]) for i in rng]
    v_new = [y[i] - _bdot(e[i], y[i]) for i in rng]
    o = [ks_qs[i][c:] + _bdot(intra[i], v_new[i]) for i in rng]
    s_new = [ss[i] * gts[i]["glast"][:, cgs[i]:cgs[i] + 1] + lax.dot_general(
        (ks[i] * gts[i]["kdec"][:, cgs[i]:cgs[i] + 1]).astype(BF16), v_new[i].astype(BF16),
        (((0,), (0,)), ((), ())), preferred_element_type=F32) for i in rng]
    return o, s_new


def _deltanet_kernel(nb, qf_ref, kf_ref, vf_ref, gf_ref, qb_ref, kb_ref, vb_ref, gb_ref, alog_ref, dtb_ref,
                     of_ref, ob_ref, s_ref):
    @pl.when(pl.program_id(0) == 0)
    def _():
        s_ref[...] = jnp.zeros_like(s_ref)

    dirs = ((qf_ref, kf_ref, vf_ref, gf_ref, of_ref), (qb_ref, kb_ref, vb_ref, gb_ref, ob_ref))

    def body(it, carry):
        idx = [(it * DN_BATCHES + j, d, hh) for j in range(DN_BATCHES) for d in range(2) for hh in range(DN_HEADS)]
        rows_of = {j: pl.ds(it * DN_BATCHES + j, CHUNK, stride=nb) for j in range(DN_BATCHES)}
        rows_b = [rows_of[i // (2 * DN_HEADS)] for i in range(len(idx))]
        cbs = [d * DN_HEADS + hh for _, d, hh in idx]
        gates = {(j, d): _deltanet_gates(d, dirs[d][3][rows_of[j], :], alog_ref[...], dtb_ref[...])
                 for j in range(DN_BATCHES) for d in range(2)}
        gts = [gates[(i // (2 * DN_HEADS), d)] for i, (_, d, _) in enumerate(idx)]
        qs = [dirs[d][0][hh, rows_b[i], :] for i, (_, d, hh) in enumerate(idx)]
        ks = [dirs[d][1][hh, rows_b[i], :] for i, (_, d, hh) in enumerate(idx)]
        vs = [dirs[d][2][hh, rows_b[i], :] for i, (_, d, hh) in enumerate(idx)]
        ss = [s_ref[b * (2 * DN_HEADS) + cbs[i]] for i, (b, _, _) in enumerate(idx)]
        o, s_new = _deltanet_heads(gts, cbs, qs, ks, vs, ss)
        for i, (b, d, hh) in enumerate(idx):
            dirs[d][4][hh, rows_b[i], :] = o[i]
            s_ref[b * (2 * DN_HEADS) + cbs[i]] = s_new[i]
        return carry

    lax.fori_loop(0, nb // DN_BATCHES, body, 0)


def _deltanet(qkv, proj, alog_vec, dtb_vec, nb, n_ctx_chunks):
    rows = qkv.shape[1]
    tr = CHUNK * nb
    n_tot = rows // tr
    fwd = lambda i: i
    bwd = lambda i: _bwd_chunk(i, n_ctx_chunks, n_tot)

    def specs(order):
        return [pl.BlockSpec((DN_HEADS, tr, DN_HEAD_DIM), lambda i, p=p: (p, order(i), 0)) for p in range(3)] + [
            pl.BlockSpec((tr, LANES), lambda i: (order(i), COL_GATE))]

    vec = pl.BlockSpec((1, LANES), lambda i: (0, 0))
    out = jax.ShapeDtypeStruct((DN_HEADS, rows, DN_HEAD_DIM), F32)
    return pl.pallas_call(
        functools.partial(_deltanet_kernel, nb),
        grid=(n_tot,),
        in_specs=specs(fwd) + specs(bwd) + [vec, vec],
        out_specs=[pl.BlockSpec((DN_HEADS, tr, DN_HEAD_DIM), lambda i: (0, fwd(i), 0)),
                   pl.BlockSpec((DN_HEADS, tr, DN_HEAD_DIM), lambda i: (0, bwd(i), 0))],
        out_shape=[out, out],
        scratch_shapes=[pltpu.VMEM((nb * 2 * DN_HEADS, DN_HEAD_DIM, DN_HEAD_DIM), F32)],
        compiler_params=_cparams(("arbitrary",)),
        name="deltanet",
    )(qkv, qkv, qkv, proj, qkv, qkv, qkv, proj, alog_vec, dtb_vec)


def _mixer_out_kernel(nb, of_ref, ob_ref, z_ref, hf_ref, hb_ref, ry_ref, onw_ref, w_ref, h_ref, gt_ref, g_ref, b_ref,
                      o_ref):
    z = z_ref[...]
    parts = []
    for hh in range(DN_HEADS):
        sl = slice(hh * DN_HEAD_DIM, (hh + 1) * DN_HEAD_DIM)
        oh = of_ref[hh] + ob_ref[hh]
        inv = lax.rsqrt(jnp.mean(oh * oh, axis=-1, keepdims=True) + NORM_EPS)
        parts.append(oh * inv * onw_ref[...] * _silu(z[:, sl]))
    parts.append((hf_ref[...] + hb_ref[...]) * jax.nn.gelu(ry_ref[...]))
    y = jnp.concatenate(parts, axis=1).astype(BF16)
    u = jnp.dot(y, w_ref[...], preferred_element_type=F32)
    r = DEEP_ALPHA * h_ref[...] + _per_batch(u, gt_ref[...], nb) * u
    o_ref[...] = _layer_norm(r) * g_ref[...] + b_ref[...]


def _mixer_out(o_f, o_b, proj, hf, hb, onorm_w, w_out, h, mod_rows, ln_g, ln_b, nb, ctx_rows, row0):
    rows, d = h.shape
    tm = ROW_TILE
    t0 = row0 // tm
    lat = lambda i: jnp.where((i + t0) * tm >= ctx_rows, 1, 0)
    half = pl.BlockSpec((tm, DN_WIDTH), lambda i: (i + t0, 0))
    planes = pl.BlockSpec((DN_HEADS, tm, DN_HEAD_DIM), lambda i: (0, i + t0, 0))
    vec = lambda n: pl.BlockSpec((1, n), lambda i: (0, 0))
    return pl.pallas_call(
        functools.partial(_mixer_out_kernel, nb),
        grid=((rows - row0) // tm,),
        in_specs=[planes, planes, pl.BlockSpec((tm, DN_WIDTH), lambda i: (i + t0, COL_Z)), half, half,
                  pl.BlockSpec((tm, RG_WIDTH), lambda i: (i + t0, COL_RY)),
                  vec(DN_HEAD_DIM), pl.BlockSpec((d, d), lambda i: (0, 0)),
                  pl.BlockSpec((tm, d), lambda i: (i + t0, 0)),
                  pl.BlockSpec((None, nb, d), lambda i: (lat(i), 0, 2)),
                  vec(d), vec(d)],
        out_specs=pl.BlockSpec((tm, d), lambda i: (i, 0)),
        out_shape=jax.ShapeDtypeStruct((rows - row0, d), F32),
        compiler_params=_cparams(("parallel",)),
        name="mixer_out",
    )(o_f, o_b, proj, hf, hb, proj, onorm_w, w_out, h, mod_rows, ln_g, ln_b)


PAIRS_PER_GROUP = EXPERTS_PER_GROUP * (EXPERTS_PER_GROUP - 1) // 2
N_CLASSES = N_GROUPS * PAIRS_PER_GROUP
assert N_CLASSES <= LANES
INFO_CLASS, INFO_RANK, INFO_W_LO, INFO_W_HI = range(4)
Y_SUB = 2 * (D_MODEL // LANES)


ROW_SUB = D_MODEL // LANES


def _store_row_tiles(ref, x):
    n = x.shape[0]
    for c in range(ROW_SUB):
        ref[pl.ds(c, n, stride=ROW_SUB), :] = x[:, c * LANES:(c + 1) * LANES]


def _load_row_tiles(ref, n):
    return jnp.concatenate([ref[pl.ds(c, n, stride=ROW_SUB), :] for c in range(ROW_SUB)], axis=1)


def _first_index(mask, lane):
    return jnp.min(jnp.where(mask, lane, LANES), axis=-1, keepdims=True)


def _router_kernel(nb, x_ref, sh_ref, sc_ref, wr_ref, br_ref, hm_ref, info_ref, cnt_ref, run_ref):
    @pl.when(pl.program_id(0) == 0)
    def _():
        run_ref[...] = jnp.zeros_like(run_ref)

    xn = _layer_norm(x_ref[...])
    hm = xn * (1.0 + _per_batch(xn, sc_ref[...], nb)) + _per_batch(xn, sh_ref[...], nb)
    _store_row_tiles(hm_ref, hm)
    tm = hm.shape[0]
    h_hi = hm.astype(BF16)
    h_lo = (hm - h_hi.astype(F32)).astype(BF16)
    hi_hl = jnp.dot(h_hi, wr_ref[...], preferred_element_type=F32)
    logits = (hi_hl[:, :LANES] + hi_hl[:, LANES:] + jnp.dot(h_lo, wr_ref[:, :LANES], preferred_element_type=F32)
              + br_ref[...])
    lane = lax.broadcasted_iota(jnp.int32, (tm, LANES), 1)
    neg = -jnp.inf
    is_g = lane < N_GROUPS
    gmax = jnp.max(jnp.where(is_g, logits, neg), axis=-1, keepdims=True)
    g_sel = _first_index(jnp.logical_and(is_g, logits == gmax), lane)
    p_group = 1.0 / jnp.sum(jnp.where(is_g, jnp.exp(logits - gmax), 0.0), axis=-1, keepdims=True)
    lo = N_GROUPS + EXPERTS_PER_GROUP * g_sel
    in_grp = jnp.logical_and(lane >= lo, lane < lo + EXPERTS_PER_GROUP)
    m1 = jnp.max(jnp.where(in_grp, logits, neg), axis=-1, keepdims=True)
    i1 = _first_index(jnp.logical_and(in_grp, logits == m1), lane)
    rest = jnp.logical_and(in_grp, lane != i1)
    m2 = jnp.max(jnp.where(rest, logits, neg), axis=-1, keepdims=True)
    i2 = _first_index(jnp.logical_and(rest, logits == m2), lane)
    e2 = jnp.exp(m2 - m1)
    w1 = p_group / (1.0 + e2)
    w2 = p_group * e2 / (1.0 + e2)
    l1 = i1 - lo
    l2 = i2 - lo
    e_lo = jnp.minimum(l1, l2)
    e_hi = jnp.maximum(l1, l2)
    pair = jnp.right_shift(e_lo * (2 * EXPERTS_PER_GROUP - 1 - e_lo), 1) + (e_hi - e_lo - 1)
    cls = g_sel * PAIRS_PER_GROUP + pair
    first_is_lo = l1 < l2
    w_lo = jnp.where(first_is_lo, w1, w2)
    w_hi = jnp.where(first_is_lo, w2, w1)
    oh = lane == cls
    r_i = lax.broadcasted_iota(jnp.int32, (tm, tm), 0)
    c_i = lax.broadcasted_iota(jnp.int32, (tm, tm), 1)
    before = jnp.dot((r_i > c_i).astype(BF16), oh.astype(BF16), preferred_element_type=F32) + run_ref[...]
    rank = jnp.sum(jnp.where(oh, before, 0.0), axis=-1, keepdims=True)
    run_ref[...] = run_ref[...] + jnp.sum(oh.astype(F32), axis=0, keepdims=True)
    cnt_ref[...] = run_ref[...]
    info = jnp.zeros((tm, LANES), F32)
    for idx, val in ((INFO_CLASS, cls.astype(F32)), (INFO_RANK, rank), (INFO_W_LO, w_lo), (INFO_W_HI, w_hi)):
        info = jnp.where(lane == idx, val, info)
    info_ref[...] = info


def _router(x1, mod_rows, wr, br, nb, ctx_rows):
    rows, d = x1.shape
    tm = ROW_TILE
    lat = lambda i: jnp.where(i * tm >= ctx_rows, 1, 0)
    return pl.pallas_call(
        functools.partial(_router_kernel, nb),
        grid=(rows // tm,),
        in_specs=[pl.BlockSpec((tm, d), lambda i: (i, 0)),
                  pl.BlockSpec((None, nb, d), lambda i: (lat(i), 0, 3)),
                  pl.BlockSpec((None, nb, d), lambda i: (lat(i), 0, 4)),
                  pl.BlockSpec((d, 2 * LANES), lambda i: (0, 0)),
                  pl.BlockSpec((1, LANES), lambda i: (0, 0))],
        out_specs=[pl.BlockSpec((tm * ROW_SUB, LANES), lambda i: (i, 0)),
                   pl.BlockSpec((tm, LANES), lambda i: (i, 0)),
                   pl.BlockSpec((1, LANES), lambda i: (0, 0))],
        out_shape=[jax.ShapeDtypeStruct((rows * ROW_SUB, LANES), F32),
                   jax.ShapeDtypeStruct((rows, LANES), F32),
                   jax.ShapeDtypeStruct((1, LANES), F32)],
        scratch_shapes=[pltpu.VMEM((1, LANES), F32)],
        compiler_params=_cparams(("arbitrary",)),
        name="moe_router",
    )(x1, mod_rows, mod_rows, wr, br)


def _dispatch_kernel(dst_ref, hm_ref, xs_in, xs_hbm, sem):
    del xs_in
    rs = ROW_SUB
    tb = dst_ref.shape[1]
    for r in range(tb):
        dst = xs_hbm.at[pl.ds(pl.multiple_of(dst_ref[0, r], rs), rs)]
        pltpu.make_async_copy(hm_ref.at[pl.ds(r * rs, rs)], dst, sem).start(priority=r % 2)
    pltpu.make_async_copy(hm_ref, hm_ref, sem).wait()


def _dispatch(hm, dest, sorted_rows):
    tb = ROW_TILE
    n_blocks = dest.shape[0] // tb
    xs0 = jnp.zeros((sorted_rows * ROW_SUB, LANES), F32)
    return pl.pallas_call(
        _dispatch_kernel,
        grid=(n_blocks,),
        in_specs=[pl.BlockSpec((None, 1, tb), lambda i: (i, 0, 0), memory_space=pltpu.SMEM),
                  pl.BlockSpec((tb * ROW_SUB, LANES), lambda i: (i, 0)), pl.BlockSpec(memory_space=pl.ANY)],
        out_specs=pl.BlockSpec(memory_space=pl.ANY),
        out_shape=jax.ShapeDtypeStruct(xs0.shape, F32),
        scratch_shapes=[pltpu.SemaphoreType.DMA(())],
        input_output_aliases={2: 0},
        compiler_params=_cparams(("arbitrary",)),
        name="moe_dispatch",
    )((dest * ROW_SUB).reshape(n_blocks, 1, tb), hm, xs0)


def _class_ffn_kernel(telo_ref, tehi_ref, nact_ref, xs_ref, wgul_ref, wdl_ref, wguh_ref, wdh_ref, ys_ref, y_scr):
    j = pl.program_id(0)
    tmx = xs_ref.shape[0] // ROW_SUB
    de = wdl_ref.shape[0]

    @pl.when(j < nact_ref[0])
    def _():
        x = _load_row_tiles(xs_ref, tmx).astype(BF16)
        for half, (wgu_ref, wd_ref) in enumerate(((wgul_ref, wdl_ref), (wguh_ref, wdh_ref))):
            ab = jnp.dot(x, wgu_ref[...], preferred_element_type=F32)
            y = jnp.dot((_silu(ab[:, :de]) * ab[:, de:]).astype(BF16), wd_ref[...], preferred_element_type=F32)
            for c in range(ROW_SUB):
                y_scr[pl.ds(half * ROW_SUB + c, tmx, stride=Y_SUB), :] = y[:, c * LANES:(c + 1) * LANES]
        ys_ref[...] = y_scr[...].astype(BF16)

    @pl.when(j >= nact_ref[0])
    def _():
        ys_ref[...] = jnp.zeros_like(ys_ref)


def _class_ffn(xs, tile_e_lo, tile_e_hi, n_active, w_gate_up, w_down, tmx):
    n_tiles = xs.shape[0] // (tmx * ROW_SUB)
    d, de2 = w_gate_up.shape[-2:]
    w_in = lambda te_idx: pl.BlockSpec((None, d, de2), lambda j, *te: (te[te_idx][j], 0, 0))
    w_out = lambda te_idx: pl.BlockSpec((None, de2 // 2, d), lambda j, *te: (te[te_idx][j], 0, 0))
    grid_spec = pltpu.PrefetchScalarGridSpec(
        num_scalar_prefetch=3,
        grid=(n_tiles,),
        in_specs=[pl.BlockSpec((tmx * ROW_SUB, LANES), lambda j, lo, hi, na: (jnp.minimum(j, na[0] - 1), 0)),
                  w_in(0), w_out(0), w_in(1), w_out(1)],
        out_specs=pl.BlockSpec((tmx * Y_SUB, LANES), lambda j, lo, hi, na: (j, 0)),
        scratch_shapes=[pltpu.VMEM((tmx * Y_SUB, LANES), F32)],
    )
    return pl.pallas_call(
        _class_ffn_kernel,
        grid_spec=grid_spec,
        out_shape=jax.ShapeDtypeStruct((n_tiles * tmx * Y_SUB, LANES), BF16),
        compiler_params=_cparams(("arbitrary",)),
        name="expert_ffn",
    )(tile_e_lo, tile_e_hi, n_active, xs, w_gate_up, w_down, w_gate_up, w_down)


def _moe_out_kernel(nb, src0_ref, srcn_ref, ys_hbm, info_ref, x_ref, gt_ref, g_ref, b_ref, o_ref, ybuf, y_scr, sem):
    i = pl.program_id(0)
    tm = info_ref.shape[0]
    slot = i % 2
    nslot = 1 - slot

    def gather_all(src_ref, s):
        for r in range(tm):
            pltpu.make_async_copy(ys_hbm.at[pl.ds(pl.multiple_of(src_ref[0, r], Y_SUB), Y_SUB)],
                                  ybuf.at[s, pl.ds(r * Y_SUB, Y_SUB)], sem.at[s]).start(priority=r % 2)

    def wait_rows(s):
        pltpu.make_async_copy(ybuf.at[s], ybuf.at[s], sem.at[s]).wait()

    @pl.when(i == 0)
    def _():
        gather_all(src0_ref, 0)

    wait_rows(slot)
    gather_all(srcn_ref, nslot)
    info = info_ref[...]
    y_scr[...] = ybuf[slot].astype(F32)
    y_lo = jnp.concatenate([y_scr[pl.ds(c, tm, stride=Y_SUB), :] for c in range(ROW_SUB)], axis=1)
    y_hi = jnp.concatenate([y_scr[pl.ds(ROW_SUB + c, tm, stride=Y_SUB), :] for c in range(ROW_SUB)], axis=1)
    f = y_lo * info[:, INFO_W_LO:INFO_W_LO + 1] + y_hi * info[:, INFO_W_HI:INFO_W_HI + 1]
    r = DEEP_ALPHA * x_ref[...] + _per_batch(f, gt_ref[...], nb) * f
    o_ref[...] = _layer_norm(r) * g_ref[...] + b_ref[...]

    @pl.when(i == pl.num_programs(0) - 1)
    def _():
        wait_rows(nslot)


def _moe_out(ys, dest, info, x1, mod_rows, ln_g, ln_b, nb, ctx_rows):
    rows, d = x1.shape
    tm = ROW_TILE
    n_blocks = rows // tm
    lat = lambda i: jnp.where(i * tm >= ctx_rows, 1, 0)
    vec = pl.BlockSpec((1, d), lambda i: (0, 0))
    src = (dest * Y_SUB).reshape(n_blocks, 1, tm)
    smem = lambda f: pl.BlockSpec((None, 1, tm), f, memory_space=pltpu.SMEM)
    return pl.pallas_call(
        functools.partial(_moe_out_kernel, nb),
        grid=(n_blocks,),
        in_specs=[smem(lambda i: (0, 0, 0)),
                  smem(lambda i: (jnp.minimum(i + 1, n_blocks - 1), 0, 0)),
                  pl.BlockSpec(memory_space=pl.ANY),
                  pl.BlockSpec((tm, LANES), lambda i: (i, 0)),
                  pl.BlockSpec((tm, d), lambda i: (i, 0)),
                  pl.BlockSpec((None, nb, d), lambda i: (lat(i), 0, 5)),
                  vec, vec],
        out_specs=pl.BlockSpec((tm, d), lambda i: (i, 0)),
        out_shape=jax.ShapeDtypeStruct((rows, d), F32),
        scratch_shapes=[pltpu.VMEM((2, tm * Y_SUB, LANES), BF16), pltpu.VMEM((tm * Y_SUB, LANES), F32),
                        pltpu.SemaphoreType.DMA((2,))],
        compiler_params=_cparams(("arbitrary",)),
        name="moe_out",
    )(src, src, ys, info, x1, mod_rows, ln_g, ln_b)


def _class_experts():
    lo, hi = [], []
    for g in range(N_GROUPS):
        for a in range(EXPERTS_PER_GROUP):
            for b in range(a + 1, EXPERTS_PER_GROUP):
                lo.append(g * EXPERTS_PER_GROUP + a)
                hi.append(g * EXPERTS_PER_GROUP + b)
    return np.asarray(lo, np.int32), np.asarray(hi, np.int32)


def _routing_tables(info, counts, n_tiles, tmx):
    cnt = counts[0, :N_CLASSES].astype(jnp.int32)
    padded = ((cnt + tmx - 1) // tmx) * tmx
    ends = jnp.cumsum(padded)
    starts = ends - padded
    cls = info[:, INFO_CLASS].astype(jnp.int32)
    class_ids = jnp.arange(N_CLASSES, dtype=jnp.int32)
    start_of_tok = jnp.sum(jnp.where(cls[:, None] == class_ids[None, :], starts[None, :], 0), axis=1)
    dest = start_of_tok + info[:, INFO_RANK].astype(jnp.int32)
    tile_start = jnp.arange(n_tiles, dtype=jnp.int32) * tmx
    n_before = jnp.sum((ends[None, :] <= tile_start[:, None]).astype(jnp.int32), axis=1)
    tile_cls = jnp.minimum(n_before, N_CLASSES - 1)
    e_lo, e_hi = _class_experts()
    n_active = (ends[-1:] // tmx).astype(jnp.int32)
    return dest, jnp.asarray(e_lo)[tile_cls], jnp.asarray(e_hi)[tile_cls], n_active


def _relayout_w_in(w_in):
    qkvz = w_in[..., :4 * DN_WIDTH]
    gates = w_in[..., 4 * DN_WIDTH:4 * DN_WIDTH + 4 * DN_HEADS]
    rxy = w_in[..., 4 * DN_WIDTH + 4 * DN_HEADS:]
    pad = jnp.zeros(w_in.shape[:-1] + (LANES - 4 * DN_HEADS,), w_in.dtype)
    return jnp.concatenate([qkvz, rxy, gates, pad], axis=-1).astype(BF16)


def _block_diag(w):
    n, c, _ = w.shape
    eye = jnp.eye(n, dtype=w.dtype)
    return (eye[:, None, :, None] * w[:, :, None, :]).reshape(n * c, n * c)


def _lane_vec(vals, offset):
    return jnp.zeros((1, LANES), F32).at[0, offset:offset + vals.size].set(vals.reshape(-1))


def kernel(x, c, ctx, c_ctx, w_ada, b_ada, w_in, conv_qkv_w, dn_a_log, dn_dt_bias, dn_onorm_w, rg_conv_w, rg_conv_b, rg_wa, rg_ba, rg_wi, rg_bi, rg_lambda, w_out, ln_g, ln_b, router_wg, router_bg, router_we, router_be, w_e_gate, w_e_up, w_e_down):
    nb, t_lat, d = x.shape
    t_ctx = ctx.shape[1]
    assert d == D_MODEL and nb % 8 == 0 and t_lat % GRID_W == 0 and t_ctx % CHUNK == 0
    assert (t_ctx * nb) % ROW_TILE == 0 and (t_lat * nb) % ROW_TILE == 0
    tt = t_ctx + t_lat
    rows = tt * nb
    ctx_rows = t_ctx * nb
    n_ctx_chunks = t_ctx // CHUNK

    h = jnp.concatenate([jnp.swapaxes(ctx, 0, 1), jnp.swapaxes(x, 0, 1)], axis=0).reshape(rows, d)

    n_cc = ((nb + 1 + 7) // 8) * 8
    cc = jnp.zeros((n_cc, d), F32).at[:nb].set(c).at[nb].set(c_ctx)
    mod = _modulation(cc, w_ada, b_ada)
    mod_ctx = jnp.broadcast_to(mod[:, nb:nb + 1], (DEPTH, nb, 6 * d))
    mod_rows = jnp.stack([mod_ctx, mod[:, :nb]], axis=1)

    w_in2 = _relayout_w_in(w_in)
    for l in range(DEPTH):
        row0 = ctx_rows if l == DEPTH - 1 else 0
        moe_ctx_rows = ctx_rows - row0
        n_tiles = (rows - row0) // MOE_TILE + N_CLASSES
        mr = mod_rows[l]
        proj = _in_projection(h, mr, w_in2[l], nb, ctx_rows)
        qkv = _qkv_prep(proj, conv_qkv_w[l], nb, n_ctx_chunks)
        wg = jnp.concatenate([_block_diag(rg_wa[l, 0]), _block_diag(rg_wi[l, 0]),
                              _block_diag(rg_wa[l, 1]), _block_diag(rg_wi[l, 1])], axis=1).astype(BF16)
        bg = jnp.concatenate([rg_ba[l, 0], rg_bi[l, 0], rg_ba[l, 1], rg_bi[l, 1]])[None]
        hf, hb = _rglru(proj, rg_conv_w[l], rg_conv_b[l][None], wg, bg, rg_lambda[l], nb, n_ctx_chunks)
        alog_vec = _lane_vec(dn_a_log[l], 2 * DN_HEADS)
        dtb_vec = _lane_vec(dn_dt_bias[l], 2 * DN_HEADS)
        o_f, o_b = _deltanet(qkv, proj, alog_vec, dtb_vec, nb, n_ctx_chunks)
        x1 = _mixer_out(o_f, o_b, proj, hf, hb, dn_onorm_w[l][None], w_out[l].astype(BF16), h, mr,
                        ln_g[l, 0][None], ln_b[l, 0][None], nb, ctx_rows, row0)
        wr = jnp.zeros((d, LANES), F32).at[:, :N_GROUPS].set(router_wg[l]).at[
            :, N_GROUPS:N_GROUPS + N_EXPERTS].set(router_we[l])
        br = _lane_vec(jnp.concatenate([router_bg[l], router_be[l]]), 0)
        wr_hi = wr.astype(BF16)
        wr_hl = jnp.concatenate([wr_hi, (wr - wr_hi.astype(F32)).astype(BF16)], axis=1)
        hm, info, counts = _router(x1, mr, wr_hl, br, nb, moe_ctx_rows)
        dest, tile_e_lo, tile_e_hi, n_active = _routing_tables(info, counts, n_tiles, MOE_TILE)
        xs = _dispatch(hm, dest, n_tiles * MOE_TILE)
        w_gate_up = jnp.concatenate([w_e_gate[l], w_e_up[l]], axis=-1).astype(BF16)
        ys = _class_ffn(xs, tile_e_lo, tile_e_hi, n_active, w_gate_up, w_e_down[l].astype(BF16), MOE_TILE)
        h = _moe_out(ys, dest, info, x1, mr, ln_g[l, 1][None], ln_b[l, 1][None], nb, moe_ctx_rows)

    return jnp.swapaxes(h.reshape(t_lat, nb, d), 0, 1)
```

```python
import functools
import math

import jax
import jax.numpy as jnp
import numpy as np
from jax import lax
from jax.experimental import pallas as pl
from jax.experimental.pallas import tpu as pltpu

F32 = jnp.float32
BF16 = jnp.bfloat16
HIGHEST = lax.Precision.HIGHEST

D_MODEL = 1024
DEPTH = 2
GRID_W = 64
DN_HEAD_DIM = 128
DN_WIDTH = 512
DN_HEADS = 4
RG_WIDTH = 512
RG_BLOCKS = 8
RG_BLOCK_DIM = 64
RG_C = 8.0
CONV_K = 4
N_GROUPS = 4
EXPERTS_PER_GROUP = 8
N_EXPERTS = 32
D_EXPERT = 256
DEEP_ALPHA = (2 * DEPTH) ** 0.25
LN_EPS = 1e-5
NORM_EPS = 1e-6

LANES = 128
CHUNK = 64
SOLVE_BLOCK = 16
DN_BATCHES = 2
D_PROJ = 3 * DN_WIDTH + DN_WIDTH + 2 * RG_WIDTH + LANES
COL_Z = 3
COL_RX = 4
COL_RY = 5
COL_GATE = (3 * DN_WIDTH + DN_WIDTH + 2 * RG_WIDTH) // LANES
ROW_TILE = 512
MOE_TILE = 256
VMEM_LIMIT = 56 * 1024 * 1024


def _cparams(sem):
    return pltpu.CompilerParams(dimension_semantics=sem, vmem_limit_bytes=VMEM_LIMIT)


def _layer_norm(x):
    mu = jnp.mean(x, axis=-1, keepdims=True)
    xc = x - mu
    var = jnp.mean(xc * xc, axis=-1, keepdims=True)
    return xc * lax.rsqrt(var + LN_EPS)


def _per_batch(x, v, nb):
    rows, c = x.shape
    return jnp.broadcast_to(v[None], (rows // nb, nb, c)).reshape(rows, c)


def _sigmoid(x):
    return 1.0 / (1.0 + jnp.exp(-x))


def _silu(x):
    return x * _sigmoid(x)


def _mod_kernel(cc_ref, w_ref, b_ref, o_ref):
    s = _silu(cc_ref[...])
    o_ref[...] = jnp.dot(s, w_ref[...], precision=HIGHEST, preferred_element_type=F32) + b_ref[...]


def _modulation(cc, w_ada, b_ada):
    depth, d, n6 = w_ada.shape
    rows = cc.shape[0]
    tn = 1536
    return pl.pallas_call(
        _mod_kernel,
        grid=(depth, n6 // tn),
        in_specs=[
            pl.BlockSpec((rows, d), lambda l, j: (0, 0)),
            pl.BlockSpec((None, d, tn), lambda l, j: (l, 0, j)),
            pl.BlockSpec((None, 1, tn), lambda l, j: (l, 0, j)),
        ],
        out_specs=pl.BlockSpec((None, rows, tn), lambda l, j: (l, 0, j)),
        out_shape=jax.ShapeDtypeStruct((depth, rows, n6), F32),
        compiler_params=_cparams(("parallel", "parallel")),
        name="adaln_mod",
    )(cc, w_ada, b_ada.reshape(depth, 1, n6))


def _inproj_kernel(nb, h_ref, sh_ref, sc_ref, w_ref, o_ref):
    xn = _layer_norm(h_ref[...])
    hm = xn * (1.0 + _per_batch(xn, sc_ref[...], nb)) + _per_batch(xn, sh_ref[...], nb)
    o_ref[...] = jnp.dot(hm.astype(BF16), w_ref[...], preferred_element_type=F32)


def _in_projection(h, mod_rows, w_in2, nb, ctx_rows):
    rows, d = h.shape
    tm = ROW_TILE
    lat = lambda i: jnp.where(i * tm >= ctx_rows, 1, 0)
    return pl.pallas_call(
        functools.partial(_inproj_kernel, nb),
        grid=(rows // tm,),
        in_specs=[
            pl.BlockSpec((tm, d), lambda i: (i, 0)),
            pl.BlockSpec((None, nb, d), lambda i: (lat(i), 0, 0)),
            pl.BlockSpec((None, nb, d), lambda i: (lat(i), 0, 1)),
            pl.BlockSpec((d, D_PROJ), lambda i: (0, 0)),
        ],
        out_specs=pl.BlockSpec((tm, D_PROJ), lambda i: (i, 0)),
        out_shape=jax.ShapeDtypeStruct((rows, D_PROJ), F32),
        compiler_params=_cparams(("parallel",)),
        name="ln_mod_inproj",
    )(h, mod_rows, mod_rows, w_in2)


def _conv_tile(x, prev, nxt, w, nb, use_prev, use_next):
    prev = jnp.where(use_prev, prev, 0.0)
    nxt = jnp.where(use_next, nxt, 0.0)
    xm2 = jnp.concatenate([prev, x[: -2 * nb]], axis=0)
    xm1 = jnp.concatenate([prev[nb:], x[:-nb]], axis=0)
    xp1 = jnp.concatenate([x[nb:], nxt], axis=0)
    return xm2 * w[0:1] + xm1 * w[1:2] + x * w[2:3] + xp1 * w[3:4]


def _halo_flags(i, n_ctx_tiles):
    use_prev = jnp.logical_and(i > 0, i < n_ctx_tiles)
    use_next = i < n_ctx_tiles - 1
    return use_prev, use_next


def _qkv_prep_kernel(nb, n_ctx_tiles, x_ref, prev_ref, next_ref, w_ref, o_ref):
    i = pl.program_id(0)
    part = pl.program_id(1)
    use_prev, use_next = _halo_flags(i, n_ctx_tiles)
    y = _silu(_conv_tile(x_ref[...], prev_ref[...], next_ref[...], w_ref[...], nb, use_prev, use_next))
    qscale = jnp.where(part == 0, DN_HEAD_DIM ** -0.5, 1.0).astype(F32)
    for hh in range(DN_HEADS):
        sl = slice(hh * DN_HEAD_DIM, (hh + 1) * DN_HEAD_DIM)
        yh = y[:, sl]
        inv = lax.rsqrt(jnp.sum(yh * yh, axis=-1, keepdims=True) + NORM_EPS) * qscale
        o_ref[hh] = yh * jnp.where(part < 2, inv, 1.0)


def _halo_specs(tile_rows, nb, cols, col_of, n_row_blocks16, tile_of=lambda i: i):
    prev_b = tile_rows // (2 * nb)
    next_b = tile_rows // nb
    return [
        pl.BlockSpec((tile_rows, cols), lambda i, *a: (tile_of(i), col_of(*a))),
        pl.BlockSpec((2 * nb, cols), lambda i, *a: (jnp.maximum(tile_of(i) * prev_b - 1, 0), col_of(*a))),
        pl.BlockSpec((nb, cols),
                     lambda i, *a: (jnp.minimum((tile_of(i) + 1) * next_b, n_row_blocks16 - 1), col_of(*a))),
    ]


def _qkv_prep(proj, conv_w, nb, n_ctx_tiles):
    rows = proj.shape[0]
    tr = CHUNK * nb
    return pl.pallas_call(
        functools.partial(_qkv_prep_kernel, nb, n_ctx_tiles),
        grid=(rows // tr, 3),
        in_specs=_halo_specs(tr, nb, DN_WIDTH, lambda p: p, rows // nb)
        + [pl.BlockSpec((CONV_K, DN_WIDTH), lambda i, p: (0, p))],
        out_specs=pl.BlockSpec((DN_HEADS, tr, DN_HEAD_DIM), lambda i, p: (p, i, 0)),
        out_shape=jax.ShapeDtypeStruct((3 * DN_HEADS, rows, DN_HEAD_DIM), F32),
        compiler_params=_cparams(("parallel", "parallel")),
        name="qkv_conv_norm",
    )(proj, proj, proj, conv_w)


def _bwd_chunk(i, n_ctx, n_tot):
    return jnp.where(i < n_ctx, n_ctx - 1 - i, n_tot + n_ctx - 1 - i)


def _rglru_kernel(nb, n_ctx_tiles, n_tot, xf_ref, pf_ref, nf_ref, xb_ref, pb_ref, nb_ref, cw_ref, cb_ref,
                  wg0_ref, wg1_ref, bg0_ref, bg1_ref, lam_ref, hf_ref, hb_ref, a_scr, b_scr, st_ref):
    i = pl.program_id(0)

    @pl.when(i == 0)
    def _():
        st_ref[...] = jnp.zeros_like(st_ref)

    w = RG_WIDTH
    sp = jax.nn.softplus(-lam_ref[...])
    tiles = (i, _bwd_chunk(i, n_ctx_tiles, n_tot))
    for d, (x_ref, p_ref, n_ref, wg_ref, bg_ref) in enumerate(
            ((xf_ref, pf_ref, nf_ref, wg0_ref, bg0_ref), (xb_ref, pb_ref, nb_ref, wg1_ref, bg1_ref))):
        use_prev, use_next = _halo_flags(tiles[d], n_ctx_tiles)
        xc = _conv_tile(x_ref[...], p_ref[...], n_ref[...], cw_ref[...], nb, use_prev, use_next) + cb_ref[...]
        gates = jnp.dot(xc.astype(BF16), wg_ref[...], preferred_element_type=F32) + bg_ref[...]
        r = _sigmoid(gates[:, :w])
        ig = _sigmoid(gates[:, w:])
        a = jnp.exp(-RG_C * r * sp[d:d + 1])
        y = jnp.maximum(1.0 - a * a, 0.0)
        a_scr[d] = a
        b_scr[d] = jnp.where(y > 0.0, y * lax.rsqrt(y), 0.0) * (ig * xc)

    def body(t, carry):
        h0, h1 = carry
        r0 = pl.ds(pl.multiple_of(t * nb, nb), nb)
        r1 = pl.ds(pl.multiple_of((CHUNK - 1 - t) * nb, nb), nb)
        h0 = a_scr[0, r0, :] * h0 + b_scr[0, r0, :]
        h1 = a_scr[1, r1, :] * h1 + b_scr[1, r1, :]
        hf_ref[r0, :] = h0.astype(hf_ref.dtype)
        hb_ref[r1, :] = h1.astype(hb_ref.dtype)
        return h0, h1

    h0, h1 = lax.fori_loop(0, CHUNK, body, (st_ref[0], st_ref[1]), unroll=8)
    st_ref[0] = h0
    st_ref[1] = h1


def _rglru(proj, conv_w, conv_b, wg, bg, lam, nb, n_ctx_tiles):
    rows = proj.shape[0]
    tr = CHUNK * nb
    n_tot = rows // tr
    w = RG_WIDTH
    out = jax.ShapeDtypeStruct((rows, w), BF16)
    fwd = lambda i: i
    bwd = lambda i: _bwd_chunk(i, n_ctx_tiles, n_tot)
    full = lambda shape: pl.BlockSpec(shape, lambda i: (0,) * len(shape))
    return pl.pallas_call(
        functools.partial(_rglru_kernel, nb, n_ctx_tiles, n_tot),
        grid=(n_tot,),
        in_specs=_halo_specs(tr, nb, w, lambda: COL_RX, rows // nb, fwd)
        + _halo_specs(tr, nb, w, lambda: COL_RX, rows // nb, bwd)
        + [full((CONV_K, w)), full((1, w)),
           pl.BlockSpec((w, 2 * w), lambda i: (0, 0)), pl.BlockSpec((w, 2 * w), lambda i: (0, 1)),
           pl.BlockSpec((1, 2 * w), lambda i: (0, 0)), pl.BlockSpec((1, 2 * w), lambda i: (0, 1)),
           full((2, w))],
        out_specs=[pl.BlockSpec((tr, w), lambda i: (fwd(i), 0)), pl.BlockSpec((tr, w), lambda i: (bwd(i), 0))],
        out_shape=[out] * 2,
        scratch_shapes=[pltpu.VMEM((2, tr, w), F32), pltpu.VMEM((2, tr, w), F32), pltpu.VMEM((2, nb, w), F32)],
        compiler_params=_cparams(("arbitrary",)),
        name="rglru",
    )(proj, proj, proj, proj, proj, proj, conv_w, conv_b, wg, wg, bg, bg, lam)


def _bdot(a, b):
    return jnp.dot(a.astype(BF16), b.astype(BF16), preferred_element_type=F32)


def _deltanet_gates(d, raw, alog, dtb):
    c = CHUNK
    row = lax.broadcasted_iota(jnp.int32, (c, c), 0)
    col = lax.broadcasted_iota(jnp.int32, (c, c), 1)
    incl = (row >= col) if d == 0 else (row <= col)
    g_all = -jnp.exp(alog) * jax.nn.softplus(raw + dtb)
    gc_all = jnp.dot(incl.astype(F32), g_all, precision=HIGHEST, preferred_element_type=F32)
    g_tot = jnp.sum(g_all, axis=0, keepdims=True)
    return dict(incl=incl, strict=(row > col) if d == 0 else (row < col),
                diag_blk=(row // SOLVE_BLOCK) == (col // SOLVE_BLOCK),
                beta=_sigmoid(raw), gc=gc_all, gc_t=gc_all.T, egc=jnp.exp(gc_all),
                kdec=jnp.exp(g_tot - gc_all), glast=jnp.exp(g_tot))


def _deltanet_heads(gts, cbs, qs, ks, vs, ss):
    c = CHUNK
    dh = DN_HEAD_DIM
    n = len(cbs)
    rng = range(n)
    cgs = [2 * DN_HEADS + cb for cb in cbs]
    beta = [gts[i]["beta"][:, cbs[i]:cbs[i] + 1] for i in rng]
    egc = [gts[i]["egc"][:, cgs[i]:cgs[i] + 1] for i in rng]
    decay = [jnp.exp(jnp.where(gts[i]["incl"], gts[i]["gc"][:, cgs[i]:cgs[i] + 1] - gts[i]["gc_t"][cgs[i]:cgs[i] + 1, :],
                               -1e30)) for i in rng]
    kbeta = [ks[i] * beta[i] for i in rng]
    prod = [lax.dot_general(jnp.concatenate([kbeta[i], qs[i]], axis=0).astype(BF16), ks[i].astype(BF16),
                            (((1,), (1,)), ((), ())), preferred_element_type=F32) for i in rng]
    a_mat = [jnp.where(gts[i]["strict"], prod[i][:c] * decay[i], 0.0) for i in rng]
    intra = [prod[i][c:] * decay[i] for i in rng]
    ks_qs = [_bdot(jnp.concatenate([kbeta[i] * egc[i], qs[i] * egc[i]], axis=0), ss[i]) for i in rng]
    p = [jnp.where(gts[i]["diag_blk"], -a_mat[i], 0.0) for i in rng]
    y = [jnp.concatenate([vs[i] * beta[i] - ks_qs[i][:c], jnp.where(gts[i]["diag_blk"], 0.0, a_mat[i])], axis=1)
         for i in rng]
    for j in range(4):
        y = [y[i] + _bdot(p[i], y[i]) for i in rng]
        if j < 3:
            p = [_bdot(p[i], p[i]) for i in rng]
    e = [y[i][:, dh:] for i in rng]
    y = [y[i][:, :dh] for i in rng]
    e2 = [_bdot(e[i], e[i]) for i in rng]
    y = [y[i] + _bdot(e2[i], y[i]) for i in rng]
    v_new = [y[i] - _bdot(e[i], y[i]) for i in rng]
    o = [ks_qs[i][c:] + _bdot(intra[i], v_new[i]) for i in rng]
    s_new = [ss[i] * gts[i]["glast"][:, cgs[i]:cgs[i] + 1] + lax.dot_general(
        (ks[i] * gts[i]["kdec"][:, cgs[i]:cgs[i] + 1]).astype(BF16), v_new[i].astype(BF16),
        (((0,), (0,)), ((), ())), preferred_element_type=F32) for i in rng]
    return o, s_new


def _deltanet_kernel(nb, qf_ref, kf_ref, vf_ref, gf_ref, qb_ref, kb_ref, vb_ref, gb_ref, alog_ref, dtb_ref,
                     of_ref, ob_ref, s_ref):
    @pl.when(pl.program_id(0) == 0)
    def _():
        s_ref[...] = jnp.zeros_like(s_ref)

    dirs = ((qf_ref, kf_ref, vf_ref, gf_ref, of_ref), (qb_ref, kb_ref, vb_ref, gb_ref, ob_ref))

    def body(it, carry):
        idx = [(it * DN_BATCHES + j, d, hh) for j in range(DN_BATCHES) for d in range(2) for hh in range(DN_HEADS)]
        rows_of = {j: pl.ds(it * DN_BATCHES + j, CHUNK, stride=nb) for j in range(DN_BATCHES)}
        rows_b = [rows_of[i // (2 * DN_HEADS)] for i in range(len(idx))]
        cbs = [d * DN_HEADS + hh for _, d, hh in idx]
        gates = {(j, d): _deltanet_gates(d, dirs[d][3][rows_of[j], :], alog_ref[...], dtb_ref[...])
                 for j in range(DN_BATCHES) for d in range(2)}
        gts = [gates[(i // (2 * DN_HEADS), d)] for i, (_, d, _) in enumerate(idx)]
        qs = [dirs[d][0][hh, rows_b[i], :] for i, (_, d, hh) in enumerate(idx)]
        ks = [dirs[d][1][hh, rows_b[i], :] for i, (_, d, hh) in enumerate(idx)]
        vs = [dirs[d][2][hh, rows_b[i], :] for i, (_, d, hh) in enumerate(idx)]
        ss = [s_ref[b * (2 * DN_HEADS) + cbs[i]] for i, (b, _, _) in enumerate(idx)]
        o, s_new = _deltanet_heads(gts, cbs, qs, ks, vs, ss)
        for i, (b, d, hh) in enumerate(idx):
            dirs[d][4][hh, rows_b[i], :] = o[i]
            s_ref[b * (2 * DN_HEADS) + cbs[i]] = s_new[i]
        return carry

    lax.fori_loop(0, nb // DN_BATCHES, body, 0)


def _deltanet(qkv, proj, alog_vec, dtb_vec, nb, n_ctx_chunks):
    rows = qkv.shape[1]
    tr = CHUNK * nb
    n_tot = rows // tr
    fwd = lambda i: i
    bwd = lambda i: _bwd_chunk(i, n_ctx_chunks, n_tot)

    def specs(order):
        return [pl.BlockSpec((DN_HEADS, tr, DN_HEAD_DIM), lambda i, p=p: (p, order(i), 0)) for p in range(3)] + [
            pl.BlockSpec((tr, LANES), lambda i: (order(i), COL_GATE))]

    vec = pl.BlockSpec((1, LANES), lambda i: (0, 0))
    out = jax.ShapeDtypeStruct((DN_HEADS, rows, DN_HEAD_DIM), F32)
    return pl.pallas_call(
        functools.partial(_deltanet_kernel, nb),
        grid=(n_tot,),
        in_specs=specs(fwd) + specs(bwd) + [vec, vec],
        out_specs=[pl.BlockSpec((DN_HEADS, tr, DN_HEAD_DIM), lambda i: (0, fwd(i), 0)),
                   pl.BlockSpec((DN_HEADS, tr, DN_HEAD_DIM), lambda i: (0, bwd(i), 0))],
        out_shape=[out, out],
        scratch_shapes=[pltpu.VMEM((nb * 2 * DN_HEADS, DN_HEAD_DIM, DN_HEAD_DIM), F32)],
        compiler_params=_cparams(("arbitrary",)),
        name="deltanet",
    )(qkv, qkv, qkv, proj, qkv, qkv, qkv, proj, alog_vec, dtb_vec)


def _mixer_out_kernel(nb, of_ref, ob_ref, z_ref, hf_ref, hb_ref, ry_ref, onw_ref, w_ref, h_ref, gt_ref, g_ref, b_ref,
                      o_ref):
    z = z_ref[...]
    parts = []
    for hh in range(DN_HEADS):
        sl = slice(hh * DN_HEAD_DIM, (hh + 1) * DN_HEAD_DIM)
        oh = of_ref[hh] + ob_ref[hh]
        inv = lax.rsqrt(jnp.mean(oh * oh, axis=-1, keepdims=True) + NORM_EPS)
        parts.append(oh * inv * onw_ref[...] * _silu(z[:, sl]))
    parts.append((hf_ref[...].astype(F32) + hb_ref[...].astype(F32)) * jax.nn.gelu(ry_ref[...]))
    y = jnp.concatenate(parts, axis=1).astype(BF16)
    u = jnp.dot(y, w_ref[...], preferred_element_type=F32)
    r = DEEP_ALPHA * h_ref[...] + _per_batch(u, gt_ref[...], nb) * u
    o_ref[...] = _layer_norm(r) * g_ref[...] + b_ref[...]


def _mixer_out(o_f, o_b, proj, hf, hb, onorm_w, w_out, h, mod_rows, ln_g, ln_b, nb, ctx_rows, row0):
    rows, d = h.shape
    tm = ROW_TILE
    t0 = row0 // tm
    lat = lambda i: jnp.where((i + t0) * tm >= ctx_rows, 1, 0)
    half = pl.BlockSpec((tm, DN_WIDTH), lambda i: (i + t0, 0))
    planes = pl.BlockSpec((DN_HEADS, tm, DN_HEAD_DIM), lambda i: (0, i + t0, 0))
    vec = lambda n: pl.BlockSpec((1, n), lambda i: (0, 0))
    return pl.pallas_call(
        functools.partial(_mixer_out_kernel, nb),
        grid=((rows - row0) // tm,),
        in_specs=[planes, planes, pl.BlockSpec((tm, DN_WIDTH), lambda i: (i + t0, COL_Z)), half, half,
                  pl.BlockSpec((tm, RG_WIDTH), lambda i: (i + t0, COL_RY)),
                  vec(DN_HEAD_DIM), pl.BlockSpec((d, d), lambda i: (0, 0)),
                  pl.BlockSpec((tm, d), lambda i: (i + t0, 0)),
                  pl.BlockSpec((None, nb, d), lambda i: (lat(i), 0, 2)),
                  vec(d), vec(d)],
        out_specs=pl.BlockSpec((tm, d), lambda i: (i, 0)),
        out_shape=jax.ShapeDtypeStruct((rows - row0, d), F32),
        compiler_params=_cparams(("parallel",)),
        name="mixer_out",
    )(o_f, o_b, proj, hf, hb, proj, onorm_w, w_out, h, mod_rows, ln_g, ln_b)


PAIRS_PER_GROUP = EXPERTS_PER_GROUP * (EXPERTS_PER_GROUP - 1) // 2
N_CLASSES = N_GROUPS * PAIRS_PER_GROUP
assert N_CLASSES <= LANES
INFO_CLASS, INFO_RANK, INFO_W_LO, INFO_W_HI = range(4)
Y_SUB = 2 * (D_MODEL // LANES)


ROW_SUB = D_MODEL // LANES


def _store_row_tiles(ref, x):
    n = x.shape[0]
    for c in range(ROW_SUB):
        ref[pl.ds(c, n, stride=ROW_SUB), :] = x[:, c * LANES:(c + 1) * LANES]


def _load_row_tiles(ref, n):
    return jnp.concatenate([ref[pl.ds(c, n, stride=ROW_SUB), :] for c in range(ROW_SUB)], axis=1)


def _first_index(mask, lane):
    return jnp.min(jnp.where(mask, lane, LANES), axis=-1, keepdims=True)


def _router_kernel(nb, x_ref, sh_ref, sc_ref, wr_ref, br_ref, hm_ref, info_ref, cnt_ref, run_ref):
    @pl.when(pl.program_id(0) == 0)
    def _():
        run_ref[...] = jnp.zeros_like(run_ref)

    xn = _layer_norm(x_ref[...])
    hm = xn * (1.0 + _per_batch(xn, sc_ref[...], nb)) + _per_batch(xn, sh_ref[...], nb)
    _store_row_tiles(hm_ref, hm)
    tm = hm.shape[0]
    h_hi = hm.astype(BF16)
    h_lo = (hm - h_hi.astype(F32)).astype(BF16)
    hi_hl = jnp.dot(h_hi, wr_ref[...], preferred_element_type=F32)
    logits = (hi_hl[:, :LANES] + hi_hl[:, LANES:] + jnp.dot(h_lo, wr_ref[:, :LANES], preferred_element_type=F32)
              + br_ref[...])
    lane = lax.broadcasted_iota(jnp.int32, (tm, LANES), 1)
    neg = -jnp.inf
    is_g = lane < N_GROUPS
    gmax = jnp.max(jnp.where(is_g, logits, neg), axis=-1, keepdims=True)
    g_sel = _first_index(jnp.logical_and(is_g, logits == gmax), lane)
    p_group = 1.0 / jnp.sum(jnp.where(is_g, jnp.exp(logits - gmax), 0.0), axis=-1, keepdims=True)
    lo = N_GROUPS + EXPERTS_PER_GROUP * g_sel
    in_grp = jnp.logical_and(lane >= lo, lane < lo + EXPERTS_PER_GROUP)
    m1 = jnp.max(jnp.where(in_grp, logits, neg), axis=-1, keepdims=True)
    i1 = _first_index(jnp.logical_and(in_grp, logits == m1), lane)
    rest = jnp.logical_and(in_grp, lane != i1)
    m2 = jnp.max(jnp.where(rest, logits, neg), axis=-1, keepdims=True)
    i2 = _first_index(jnp.logical_and(rest, logits == m2), lane)
    e2 = jnp.exp(m2 - m1)
    w1 = p_group / (1.0 + e2)
    w2 = p_group * e2 / (1.0 + e2)
    l1 = i1 - lo
    l2 = i2 - lo
    e_lo = jnp.minimum(l1, l2)
    e_hi = jnp.maximum(l1, l2)
    pair = jnp.right_shift(e_lo * (2 * EXPERTS_PER_GROUP - 1 - e_lo), 1) + (e_hi - e_lo - 1)
    cls = g_sel * PAIRS_PER_GROUP + pair
    first_is_lo = l1 < l2
    w_lo = jnp.where(first_is_lo, w1, w2)
    w_hi = jnp.where(first_is_lo, w2, w1)
    oh = lane == cls
    r_i = lax.broadcasted_iota(jnp.int32, (tm, tm), 0)
    c_i = lax.broadcasted_iota(jnp.int32, (tm, tm), 1)
    before = jnp.dot((r_i > c_i).astype(BF16), oh.astype(BF16), preferred_element_type=F32) + run_ref[...]
    rank = jnp.sum(jnp.where(oh, before, 0.0), axis=-1, keepdims=True)
    run_ref[...] = run_ref[...] + jnp.sum(oh.astype(F32), axis=0, keepdims=True)
    cnt_ref[...] = run_ref[...]
    info = jnp.zeros((tm, LANES), F32)
    for idx, val in ((INFO_CLASS, cls.astype(F32)), (INFO_RANK, rank), (INFO_W_LO, w_lo), (INFO_W_HI, w_hi)):
        info = jnp.where(lane == idx, val, info)
    info_ref[...] = info


def _router(x1, mod_rows, wr, br, nb, ctx_rows):
    rows, d = x1.shape
    tm = ROW_TILE
    lat = lambda i: jnp.where(i * tm >= ctx_rows, 1, 0)
    return pl.pallas_call(
        functools.partial(_router_kernel, nb),
        grid=(rows // tm,),
        in_specs=[pl.BlockSpec((tm, d), lambda i: (i, 0)),
                  pl.BlockSpec((None, nb, d), lambda i: (lat(i), 0, 3)),
                  pl.BlockSpec((None, nb, d), lambda i: (lat(i), 0, 4)),
                  pl.BlockSpec((d, 2 * LANES), lambda i: (0, 0)),
                  pl.BlockSpec((1, LANES), lambda i: (0, 0))],
        out_specs=[pl.BlockSpec((tm * ROW_SUB, LANES), lambda i: (i, 0)),
                   pl.BlockSpec((tm, LANES), lambda i: (i, 0)),
                   pl.BlockSpec((1, LANES), lambda i: (0, 0))],
        out_shape=[jax.ShapeDtypeStruct((rows * ROW_SUB, LANES), F32),
                   jax.ShapeDtypeStruct((rows, LANES), F32),
                   jax.ShapeDtypeStruct((1, LANES), F32)],
        scratch_shapes=[pltpu.VMEM((1, LANES), F32)],
        compiler_params=_cparams(("arbitrary",)),
        name="moe_router",
    )(x1, mod_rows, mod_rows, wr, br)


def _dispatch_kernel(tmx, ends_ref, cnt_ref, dst_ref, hm_ref, xs_hbm, zero_scr, sem, zsem):
    rs = ROW_SUB
    tb = dst_ref.shape[1]

    @pl.when(pl.program_id(0) == 0)
    def _():
        zero_scr[...] = jnp.zeros_like(zero_scr)

        def tail_copy(c):
            start = pl.multiple_of((ends_ref[c] - tmx) * rs, tmx * rs)
            return pltpu.make_async_copy(zero_scr, xs_hbm.at[pl.ds(start, tmx * rs)], zsem)

        def start_one(c, carry):
            @pl.when(cnt_ref[c] > 0)
            def _():
                tail_copy(c).start()
            return carry

        def wait_one(c, carry):
            @pl.when(cnt_ref[c] > 0)
            def _():
                tail_copy(c).wait()
            return carry

        lax.fori_loop(0, N_CLASSES, start_one, 0)

        def spare_copy(j):
            return pltpu.make_async_copy(zero_scr, xs_hbm.at[pl.ds(pl.multiple_of(j * (tmx * rs), tmx * rs), tmx * rs)],
                                         zsem)

        n_used = ends_ref[N_CLASSES - 1] // tmx
        n_tiles = xs_hbm.shape[0] // (tmx * rs)
        lax.fori_loop(n_used, n_tiles, lambda j, carry: (spare_copy(j).start(), carry)[1], 0)
        lax.fori_loop(0, N_CLASSES, wait_one, 0)
        lax.fori_loop(n_used, n_tiles, lambda j, carry: (spare_copy(j).wait(), carry)[1], 0)

    for r in range(tb):
        dst = xs_hbm.at[pl.ds(pl.multiple_of(dst_ref[0, r], rs), rs)]
        pltpu.make_async_copy(hm_ref.at[pl.ds(r * rs, rs)], dst, sem).start(priority=r % 2)
    pltpu.make_async_copy(hm_ref, hm_ref, sem).wait()


def _dispatch(hm, dest, class_ends, class_cnt, sorted_rows, tmx):
    tb = ROW_TILE
    n_blocks = dest.shape[0] // tb
    grid_spec = pltpu.PrefetchScalarGridSpec(
        num_scalar_prefetch=2,
        grid=(n_blocks,),
        in_specs=[pl.BlockSpec((None, 1, tb), lambda i, e, c: (i, 0, 0), memory_space=pltpu.SMEM),
                  pl.BlockSpec((tb * ROW_SUB, LANES), lambda i, e, c: (i, 0))],
        out_specs=pl.BlockSpec(memory_space=pl.ANY),
        scratch_shapes=[pltpu.VMEM((tmx * ROW_SUB, LANES), F32), pltpu.SemaphoreType.DMA(()),
                        pltpu.SemaphoreType.DMA(())],
    )
    return pl.pallas_call(
        functools.partial(_dispatch_kernel, tmx),
        grid_spec=grid_spec,
        out_shape=jax.ShapeDtypeStruct((sorted_rows * ROW_SUB, LANES), F32),
        compiler_params=_cparams(("arbitrary",)),
        name="moe_dispatch",
    )(class_ends, class_cnt, (dest * ROW_SUB).reshape(n_blocks, 1, tb), hm)


def _class_ffn_kernel(telo_ref, tehi_ref, nact_ref, xs_ref, wgul_ref, wdl_ref, wguh_ref, wdh_ref, ys_ref, y_scr):
    j = pl.program_id(0)
    tmx = xs_ref.shape[0] // ROW_SUB
    de = wdl_ref.shape[0]

    @pl.when(j < nact_ref[0])
    def _():
        x = _load_row_tiles(xs_ref, tmx).astype(BF16)
        for half, (wgu_ref, wd_ref) in enumerate(((wgul_ref, wdl_ref), (wguh_ref, wdh_ref))):
            ab = jnp.dot(x, wgu_ref[...], preferred_element_type=F32)
            y = jnp.dot((_silu(ab[:, :de]) * ab[:, de:]).astype(BF16), wd_ref[...], preferred_element_type=F32)
            for c in range(ROW_SUB):
                y_scr[pl.ds(half * ROW_SUB + c, tmx, stride=Y_SUB), :] = y[:, c * LANES:(c + 1) * LANES]
        ys_ref[...] = y_scr[...].astype(BF16)

    @pl.when(j >= nact_ref[0])
    def _():
        ys_ref[...] = jnp.zeros_like(ys_ref)


def _class_ffn(xs, tile_e_lo, tile_e_hi, n_active, w_gate_up, w_down, tmx):
    n_tiles = xs.shape[0] // (tmx * ROW_SUB)
    d, de2 = w_gate_up.shape[-2:]
    w_in = lambda te_idx: pl.BlockSpec((None, d, de2), lambda j, *te: (te[te_idx][j], 0, 0))
    w_out = lambda te_idx: pl.BlockSpec((None, de2 // 2, d), lambda j, *te: (te[te_idx][j], 0, 0))
    grid_spec = pltpu.PrefetchScalarGridSpec(
        num_scalar_prefetch=3,
        grid=(n_tiles,),
        in_specs=[pl.BlockSpec((tmx * ROW_SUB, LANES), lambda j, lo, hi, na: (jnp.minimum(j, na[0] - 1), 0)),
                  w_in(0), w_out(0), w_in(1), w_out(1)],
        out_specs=pl.BlockSpec((tmx * Y_SUB, LANES), lambda j, lo, hi, na: (j, 0)),
        scratch_shapes=[pltpu.VMEM((tmx * Y_SUB, LANES), F32)],
    )
    return pl.pallas_call(
        _class_ffn_kernel,
        grid_spec=grid_spec,
        out_shape=jax.ShapeDtypeStruct((n_tiles * tmx * Y_SUB, LANES), BF16),
        compiler_params=_cparams(("arbitrary",)),
        name="expert_ffn",
    )(tile_e_lo, tile_e_hi, n_active, xs, w_gate_up, w_down, w_gate_up, w_down)


def _moe_out_kernel(nb, src0_ref, srcn_ref, ys_hbm, info_ref, x_ref, gt_ref, g_ref, b_ref, o_ref, ybuf, y_scr, sem):
    i = pl.program_id(0)
    tm = info_ref.shape[0]
    slot = i % 2
    nslot = 1 - slot

    def gather_all(src_ref, s):
        for r in range(tm):
            pltpu.make_async_copy(ys_hbm.at[pl.ds(pl.multiple_of(src_ref[0, r], Y_SUB), Y_SUB)],
                                  ybuf.at[s, pl.ds(r * Y_SUB, Y_SUB)], sem.at[s]).start(priority=r % 2)

    def wait_rows(s):
        pltpu.make_async_copy(ybuf.at[s], ybuf.at[s], sem.at[s]).wait()

    @pl.when(i == 0)
    def _():
        gather_all(src0_ref, 0)

    wait_rows(slot)
    gather_all(srcn_ref, nslot)
    info = info_ref[...]
    y_scr[...] = ybuf[slot].astype(F32)
    y_lo = jnp.concatenate([y_scr[pl.ds(c, tm, stride=Y_SUB), :] for c in range(ROW_SUB)], axis=1)
    y_hi = jnp.concatenate([y_scr[pl.ds(ROW_SUB + c, tm, stride=Y_SUB), :] for c in range(ROW_SUB)], axis=1)
    f = y_lo * info[:, INFO_W_LO:INFO_W_LO + 1] + y_hi * info[:, INFO_W_HI:INFO_W_HI + 1]
    r = DEEP_ALPHA * x_ref[...] + _per_batch(f, gt_ref[...], nb) * f
    o_ref[...] = _layer_norm(r) * g_ref[...] + b_ref[...]

    @pl.when(i == pl.num_programs(0) - 1)
    def _():
        wait_rows(nslot)


def _moe_out(ys, dest, info, x1, mod_rows, ln_g, ln_b, nb, ctx_rows):
    rows, d = x1.shape
    tm = ROW_TILE
    n_blocks = rows // tm
    lat = lambda i: jnp.where(i * tm >= ctx_rows, 1, 0)
    vec = pl.BlockSpec((1, d), lambda i: (0, 0))
    src = (dest * Y_SUB).reshape(n_blocks, 1, tm)
    smem = lambda f: pl.BlockSpec((None, 1, tm), f, memory_space=pltpu.SMEM)
    return pl.pallas_call(
        functools.partial(_moe_out_kernel, nb),
        grid=(n_blocks,),
        in_specs=[smem(lambda i: (0, 0, 0)),
                  smem(lambda i: (jnp.minimum(i + 1, n_blocks - 1), 0, 0)),
                  pl.BlockSpec(memory_space=pl.ANY),
                  pl.BlockSpec((tm, LANES), lambda i: (i, 0)),
                  pl.BlockSpec((tm, d), lambda i: (i, 0)),
                  pl.BlockSpec((None, nb, d), lambda i: (lat(i), 0, 5)),
                  vec, vec],
        out_specs=pl.BlockSpec((tm, d), lambda i: (i, 0)),
        out_shape=jax.ShapeDtypeStruct((rows, d), F32),
        scratch_shapes=[pltpu.VMEM((2, tm * Y_SUB, LANES), BF16), pltpu.VMEM((tm * Y_SUB, LANES), F32),
                        pltpu.SemaphoreType.DMA((2,))],
        compiler_params=_cparams(("arbitrary",)),
        name="moe_out",
    )(src, src, ys, info, x1, mod_rows, ln_g, ln_b)


def _class_experts():
    lo, hi = [], []
    for g in range(N_GROUPS):
        for a in range(EXPERTS_PER_GROUP):
            for b in range(a + 1, EXPERTS_PER_GROUP):
                lo.append(g * EXPERTS_PER_GROUP + a)
                hi.append(g * EXPERTS_PER_GROUP + b)
    return np.asarray(lo, np.int32), np.asarray(hi, np.int32)


def _routing_tables(info, counts, n_tiles, tmx):
    cnt = counts[0, :N_CLASSES].astype(jnp.int32)
    padded = ((cnt + tmx - 1) // tmx) * tmx
    ends = jnp.cumsum(padded)
    starts = ends - padded
    cls = info[:, INFO_CLASS].astype(jnp.int32)
    class_ids = jnp.arange(N_CLASSES, dtype=jnp.int32)
    start_of_tok = jnp.sum(jnp.where(cls[:, None] == class_ids[None, :], starts[None, :], 0), axis=1)
    dest = start_of_tok + info[:, INFO_RANK].astype(jnp.int32)
    tile_start = jnp.arange(n_tiles, dtype=jnp.int32) * tmx
    n_before = jnp.sum((ends[None, :] <= tile_start[:, None]).astype(jnp.int32), axis=1)
    tile_cls = jnp.minimum(n_before, N_CLASSES - 1)
    e_lo, e_hi = _class_experts()
    n_active = (ends[-1:] // tmx).astype(jnp.int32)
    return dest, ends, cnt, jnp.asarray(e_lo)[tile_cls], jnp.asarray(e_hi)[tile_cls], n_active


def _relayout_w_in(w_in):
    qkvz = w_in[..., :4 * DN_WIDTH]
    gates = w_in[..., 4 * DN_WIDTH:4 * DN_WIDTH + 4 * DN_HEADS]
    rxy = w_in[..., 4 * DN_WIDTH + 4 * DN_HEADS:]
    pad = jnp.zeros(w_in.shape[:-1] + (LANES - 4 * DN_HEADS,), w_in.dtype)
    return jnp.concatenate([qkvz, rxy, gates, pad], axis=-1).astype(BF16)


def _block_diag(w):
    n, c, _ = w.shape
    eye = jnp.eye(n, dtype=w.dtype)
    return (eye[:, None, :, None] * w[:, :, None, :]).reshape(n * c, n * c)


def _lane_vec(vals, offset):
    return jnp.zeros((1, LANES), F32).at[0, offset:offset + vals.size].set(vals.reshape(-1))


def kernel(x, c, ctx, c_ctx, w_ada, b_ada, w_in, conv_qkv_w, dn_a_log, dn_dt_bias, dn_onorm_w, rg_conv_w, rg_conv_b, rg_wa, rg_ba, rg_wi, rg_bi, rg_lambda, w_out, ln_g, ln_b, router_wg, router_bg, router_we, router_be, w_e_gate, w_e_up, w_e_down):
    nb, t_lat, d = x.shape
    t_ctx = ctx.shape[1]
    assert d == D_MODEL and nb % 8 == 0 and t_lat % GRID_W == 0 and t_ctx % CHUNK == 0
    assert (t_ctx * nb) % ROW_TILE == 0 and (t_lat * nb) % ROW_TILE == 0
    tt = t_ctx + t_lat
    rows = tt * nb
    ctx_rows = t_ctx * nb
    n_ctx_chunks = t_ctx // CHUNK

    h = jnp.concatenate([jnp.swapaxes(ctx, 0, 1), jnp.swapaxes(x, 0, 1)], axis=0).reshape(rows, d)

    n_cc = ((nb + 1 + 7) // 8) * 8
    cc = jnp.zeros((n_cc, d), F32).at[:nb].set(c).at[nb].set(c_ctx)
    mod = _modulation(cc, w_ada, b_ada)
    mod_ctx = jnp.broadcast_to(mod[:, nb:nb + 1], (DEPTH, nb, 6 * d))
    mod_rows = jnp.stack([mod_ctx, mod[:, :nb]], axis=1)

    w_in2 = _relayout_w_in(w_in)
    for l in range(DEPTH):
        row0 = ctx_rows if l == DEPTH - 1 else 0
        moe_ctx_rows = ctx_rows - row0
        n_tiles = (rows - row0) // MOE_TILE + N_CLASSES
        mr = mod_rows[l]
        proj = _in_projection(h, mr, w_in2[l], nb, ctx_rows)
        qkv = _qkv_prep(proj, conv_qkv_w[l], nb, n_ctx_chunks)
        wg = jnp.concatenate([_block_diag(rg_wa[l, 0]), _block_diag(rg_wi[l, 0]),
                              _block_diag(rg_wa[l, 1]), _block_diag(rg_wi[l, 1])], axis=1).astype(BF16)
        bg = jnp.concatenate([rg_ba[l, 0], rg_bi[l, 0], rg_ba[l, 1], rg_bi[l, 1]])[None]
        hf, hb = _rglru(proj, rg_conv_w[l], rg_conv_b[l][None], wg, bg, rg_lambda[l], nb, n_ctx_chunks)
        alog_vec = _lane_vec(dn_a_log[l], 2 * DN_HEADS)
        dtb_vec = _lane_vec(dn_dt_bias[l], 2 * DN_HEADS)
        o_f, o_b = _deltanet(qkv, proj, alog_vec, dtb_vec, nb, n_ctx_chunks)
        x1 = _mixer_out(o_f, o_b, proj, hf, hb, dn_onorm_w[l][None], w_out[l].astype(BF16), h, mr,
                        ln_g[l, 0][None], ln_b[l, 0][None], nb, ctx_rows, row0)
        wr = jnp.zeros((d, LANES), F32).at[:, :N_GROUPS].set(router_wg[l]).at[
            :, N_GROUPS:N_GROUPS + N_EXPERTS].set(router_we[l])
        br = _lane_vec(jnp.concatenate([router_bg[l], router_be[l]]), 0)
        wr_hi = wr.astype(BF16)
        wr_hl = jnp.concatenate([wr_hi, (wr - wr_hi.astype(F32)).astype(BF16)], axis=1)
        hm, info, counts = _router(x1, mr, wr_hl, br, nb, moe_ctx_rows)
        dest, class_ends, class_cnt, tile_e_lo, tile_e_hi, n_active = _routing_tables(info, counts, n_tiles, MOE_TILE)
        xs = _dispatch(hm, dest, class_ends, class_cnt, n_tiles * MOE_TILE, MOE_TILE)
        w_gate_up = jnp.concatenate([w_e_gate[l], w_e_up[l]], axis=-1).astype(BF16)
        ys = _class_ffn(xs, tile_e_lo, tile_e_hi, n_active, w_gate_up, w_e_down[l].astype(BF16), MOE_TILE)
        h = _moe_out(ys, dest, info, x1, mr, ln_g[l, 1][None], ln_b[l, 1][None], nb, moe_ctx_rows)

    return jnp.swapaxes(h.reshape(t_lat, nb, d), 0, 1)
```

```python
import functools
import math

import jax
import jax.numpy as jnp
import numpy as np
from jax import lax
from jax.experimental import pallas as pl
from jax.experimental.pallas import tpu as pltpu

F32 = jnp.float32
BF16 = jnp.bfloat16
HIGHEST = lax.Precision.HIGHEST

D_MODEL = 1024
DEPTH = 2
GRID_W = 64
DN_HEAD_DIM = 128
DN_WIDTH = 512
DN_HEADS = 4
RG_WIDTH = 512
RG_BLOCKS = 8
RG_BLOCK_DIM = 64
RG_C = 8.0
CONV_K = 4
N_GROUPS = 4
EXPERTS_PER_GROUP = 8
N_EXPERTS = 32
D_EXPERT = 256
DEEP_ALPHA = (2 * DEPTH) ** 0.25
LN_EPS = 1e-5
NORM_EPS = 1e-6

LANES = 128
CHUNK = 64
SOLVE_BLOCK = 16
DN_BATCHES = 2
D_PROJ = 3 * DN_WIDTH + DN_WIDTH + 2 * RG_WIDTH + LANES
COL_Z = 3
COL_RX = 4
COL_RY = 5
COL_GATE = (3 * DN_WIDTH + DN_WIDTH + 2 * RG_WIDTH) // LANES
ROW_TILE = 512
MOE_TILE = 256
VMEM_LIMIT = 56 * 1024 * 1024


def _cparams(sem):
    return pltpu.CompilerParams(dimension_semantics=sem, vmem_limit_bytes=VMEM_LIMIT)


def _layer_norm(x):
    mu = jnp.mean(x, axis=-1, keepdims=True)
    xc = x - mu
    var = jnp.mean(xc * xc, axis=-1, keepdims=True)
    return xc * lax.rsqrt(var + LN_EPS)


def _per_batch(x, v, nb):
    rows, c = x.shape
    return jnp.broadcast_to(v[None], (rows // nb, nb, c)).reshape(rows, c)


def _sigmoid(x):
    return 0.5 * jnp.tanh(0.5 * x) + 0.5


def _silu(x):
    return x * _sigmoid(x)


def _mod_kernel(cc_ref, w_ref, b_ref, o_ref):
    s = _silu(cc_ref[...])
    o_ref[...] = jnp.dot(s.astype(BF16), w_ref[...].astype(BF16), preferred_element_type=F32) + b_ref[...]


def _modulation(cc, w_ada, b_ada):
    depth, d, n6 = w_ada.shape
    rows = cc.shape[0]
    tn = 1536
    return pl.pallas_call(
        _mod_kernel,
        grid=(depth, n6 // tn),
        in_specs=[
            pl.BlockSpec((rows, d), lambda l, j: (0, 0)),
            pl.BlockSpec((None, d, tn), lambda l, j: (l, 0, j)),
            pl.BlockSpec((None, 1, tn), lambda l, j: (l, 0, j)),
        ],
        out_specs=pl.BlockSpec((None, rows, tn), lambda l, j: (l, 0, j)),
        out_shape=jax.ShapeDtypeStruct((depth, rows, n6), F32),
        compiler_params=_cparams(("parallel", "parallel")),
        name="adaln_mod",
    )(cc, w_ada, b_ada.reshape(depth, 1, n6))


def _inproj_kernel(nb, h_ref, sh_ref, sc_ref, w_ref, o_ref):
    xn = _layer_norm(h_ref[...])
    hm = xn * (1.0 + _per_batch(xn, sc_ref[...], nb)) + _per_batch(xn, sh_ref[...], nb)
    o_ref[...] = jnp.dot(hm.astype(BF16), w_ref[...], preferred_element_type=F32)


def _in_projection(h, mod_rows, w_in2, nb, ctx_rows):
    rows, d = h.shape
    tm = ROW_TILE
    lat = lambda i: jnp.where(i * tm >= ctx_rows, 1, 0)
    return pl.pallas_call(
        functools.partial(_inproj_kernel, nb),
        grid=(rows // tm,),
        in_specs=[
            pl.BlockSpec((tm, d), lambda i: (i, 0)),
            pl.BlockSpec((None, nb, d), lambda i: (lat(i), 0, 0)),
            pl.BlockSpec((None, nb, d), lambda i: (lat(i), 0, 1)),
            pl.BlockSpec((d, D_PROJ), lambda i: (0, 0)),
        ],
        out_specs=pl.BlockSpec((tm, D_PROJ), lambda i: (i, 0)),
        out_shape=jax.ShapeDtypeStruct((rows, D_PROJ), F32),
        compiler_params=_cparams(("parallel",)),
        name="ln_mod_inproj",
    )(h, mod_rows, mod_rows, w_in2)


def _conv_tile(x, prev, nxt, w, nb, use_prev, use_next):
    prev = jnp.where(use_prev, prev, 0.0)
    nxt = jnp.where(use_next, nxt, 0.0)
    xm2 = jnp.concatenate([prev, x[: -2 * nb]], axis=0)
    xm1 = jnp.concatenate([prev[nb:], x[:-nb]], axis=0)
    xp1 = jnp.concatenate([x[nb:], nxt], axis=0)
    return xm2 * w[0:1] + xm1 * w[1:2] + x * w[2:3] + xp1 * w[3:4]


def _halo_flags(i, n_ctx_tiles):
    use_prev = jnp.logical_and(i > 0, i < n_ctx_tiles)
    use_next = i < n_ctx_tiles - 1
    return use_prev, use_next


def _qkv_prep_kernel(nb, n_ctx_tiles, x_ref, prev_ref, next_ref, w_ref, o_ref):
    i = pl.program_id(0)
    part = pl.program_id(1)
    use_prev, use_next = _halo_flags(i, n_ctx_tiles)
    y = _silu(_conv_tile(x_ref[...], prev_ref[...], next_ref[...], w_ref[...], nb, use_prev, use_next))
    qscale = jnp.where(part == 0, DN_HEAD_DIM ** -0.5, 1.0).astype(F32)
    for hh in range(DN_HEADS):
        sl = slice(hh * DN_HEAD_DIM, (hh + 1) * DN_HEAD_DIM)
        yh = y[:, sl]
        inv = lax.rsqrt(jnp.sum(yh * yh, axis=-1, keepdims=True) + NORM_EPS) * qscale
        o_ref[hh] = yh * jnp.where(part < 2, inv, 1.0)


def _halo_specs(tile_rows, nb, cols, col_of, n_row_blocks16, tile_of=lambda i: i):
    prev_b = tile_rows // (2 * nb)
    next_b = tile_rows // nb
    return [
        pl.BlockSpec((tile_rows, cols), lambda i, *a: (tile_of(i), col_of(*a))),
        pl.BlockSpec((2 * nb, cols), lambda i, *a: (jnp.maximum(tile_of(i) * prev_b - 1, 0), col_of(*a))),
        pl.BlockSpec((nb, cols),
                     lambda i, *a: (jnp.minimum((tile_of(i) + 1) * next_b, n_row_blocks16 - 1), col_of(*a))),
    ]


def _qkv_prep(proj, conv_w, nb, n_ctx_tiles):
    rows = proj.shape[0]
    tr = CHUNK * nb
    return pl.pallas_call(
        functools.partial(_qkv_prep_kernel, nb, n_ctx_tiles),
        grid=(rows // tr, 3),
        in_specs=_halo_specs(tr, nb, DN_WIDTH, lambda p: p, rows // nb)
        + [pl.BlockSpec((CONV_K, DN_WIDTH), lambda i, p: (0, p))],
        out_specs=pl.BlockSpec((DN_HEADS, tr, DN_HEAD_DIM), lambda i, p: (p, i, 0)),
        out_shape=jax.ShapeDtypeStruct((3 * DN_HEADS, rows, DN_HEAD_DIM), F32),
        compiler_params=_cparams(("parallel", "parallel")),
        name="qkv_conv_norm",
    )(proj, proj, proj, conv_w)


def _bwd_chunk(i, n_ctx, n_tot):
    return jnp.where(i < n_ctx, n_ctx - 1 - i, n_tot + n_ctx - 1 - i)


def _rglru_kernel(nb, n_ctx_tiles, n_tot, xf_ref, pf_ref, nf_ref, xb_ref, pb_ref, nb_ref, cw_ref, cb_ref,
                  wg0_ref, wg1_ref, bg0_ref, bg1_ref, lam_ref, hf_ref, hb_ref, a_scr, b_scr, st_ref):
    i = pl.program_id(0)

    @pl.when(i == 0)
    def _():
        st_ref[...] = jnp.zeros_like(st_ref)

    w = RG_WIDTH
    sp = jax.nn.softplus(-lam_ref[...])
    tiles = (i, _bwd_chunk(i, n_ctx_tiles, n_tot))
    for d, (x_ref, p_ref, n_ref, wg_ref, bg_ref) in enumerate(
            ((xf_ref, pf_ref, nf_ref, wg0_ref, bg0_ref), (xb_ref, pb_ref, nb_ref, wg1_ref, bg1_ref))):
        use_prev, use_next = _halo_flags(tiles[d], n_ctx_tiles)
        xc = _conv_tile(x_ref[...], p_ref[...], n_ref[...], cw_ref[...], nb, use_prev, use_next) + cb_ref[...]
        gates = jnp.dot(xc.astype(BF16), wg_ref[...], preferred_element_type=F32) + bg_ref[...]
        r = _sigmoid(gates[:, :w])
        ig = _sigmoid(gates[:, w:])
        a = jnp.exp(-RG_C * r * sp[d:d + 1])
        y = jnp.maximum(1.0 - a * a, 0.0)
        a_scr[d] = a
        b_scr[d] = jnp.where(y > 0.0, y * lax.rsqrt(y), 0.0) * (ig * xc)

    def body(t, carry):
        h0, h1 = carry
        r0 = pl.ds(pl.multiple_of(t * nb, nb), nb)
        r1 = pl.ds(pl.multiple_of((CHUNK - 1 - t) * nb, nb), nb)
        h0 = a_scr[0, r0, :] * h0 + b_scr[0, r0, :]
        h1 = a_scr[1, r1, :] * h1 + b_scr[1, r1, :]
        hf_ref[r0, :] = h0.astype(hf_ref.dtype)
        hb_ref[r1, :] = h1.astype(hb_ref.dtype)
        return h0, h1

    h0, h1 = lax.fori_loop(0, CHUNK, body, (st_ref[0], st_ref[1]), unroll=8)
    st_ref[0] = h0
    st_ref[1] = h1


def _rglru(proj, conv_w, conv_b, wg, bg, lam, nb, n_ctx_tiles):
    rows = proj.shape[0]
    tr = CHUNK * nb
    n_tot = rows // tr
    w = RG_WIDTH
    out = jax.ShapeDtypeStruct((rows, w), BF16)
    fwd = lambda i: i
    bwd = lambda i: _bwd_chunk(i, n_ctx_tiles, n_tot)
    full = lambda shape: pl.BlockSpec(shape, lambda i: (0,) * len(shape))
    return pl.pallas_call(
        functools.partial(_rglru_kernel, nb, n_ctx_tiles, n_tot),
        grid=(n_tot,),
        in_specs=_halo_specs(tr, nb, w, lambda: COL_RX, rows // nb, fwd)
        + _halo_specs(tr, nb, w, lambda: COL_RX, rows // nb, bwd)
        + [full((CONV_K, w)), full((1, w)),
           pl.BlockSpec((w, 2 * w), lambda i: (0, 0)), pl.BlockSpec((w, 2 * w), lambda i: (0, 1)),
           pl.BlockSpec((1, 2 * w), lambda i: (0, 0)), pl.BlockSpec((1, 2 * w), lambda i: (0, 1)),
           full((2, w))],
        out_specs=[pl.BlockSpec((tr, w), lambda i: (fwd(i), 0)), pl.BlockSpec((tr, w), lambda i: (bwd(i), 0))],
        out_shape=[out] * 2,
        scratch_shapes=[pltpu.VMEM((2, tr, w), F32), pltpu.VMEM((2, tr, w), F32), pltpu.VMEM((2, nb, w), F32)],
        compiler_params=_cparams(("arbitrary",)),
        name="rglru",
    )(proj, proj, proj, proj, proj, proj, conv_w, conv_b, wg, wg, bg, bg, lam)


def _bdot(a, b):
    return jnp.dot(a.astype(BF16), b.astype(BF16), preferred_element_type=F32)


def _deltanet_gates(d, raw, alog, dtb):
    c = CHUNK
    row = lax.broadcasted_iota(jnp.int32, (c, c), 0)
    col = lax.broadcasted_iota(jnp.int32, (c, c), 1)
    incl = (row >= col) if d == 0 else (row <= col)
    g_all = -jnp.exp(alog) * jax.nn.softplus(raw + dtb)
    gc_all = jnp.dot(incl.astype(F32), g_all, precision=HIGHEST, preferred_element_type=F32)
    g_tot = jnp.sum(g_all, axis=0, keepdims=True)
    return dict(incl=incl, strict=(row > col) if d == 0 else (row < col),
                diag_blk=(row // SOLVE_BLOCK) == (col // SOLVE_BLOCK),
                beta=_sigmoid(raw), gc=gc_all, gc_t=gc_all.T, egc=jnp.exp(gc_all),
                kdec=jnp.exp(g_tot - gc_all), glast=jnp.exp(g_tot))


def _deltanet_heads(gts, cbs, qs, ks, vs, ss):
    c = CHUNK
    dh = DN_HEAD_DIM
    n = len(cbs)
    rng = range(n)
    cgs = [2 * DN_HEADS + cb for cb in cbs]
    beta = [gts[i]["beta"][:, cbs[i]:cbs[i] + 1] for i in rng]
    egc = [gts[i]["egc"][:, cgs[i]:cgs[i] + 1] for i in rng]
    decay = [jnp.exp(jnp.where(gts[i]["incl"], gts[i]["gc"][:, cgs[i]:cgs[i] + 1] - gts[i]["gc_t"][cgs[i]:cgs[i] + 1, :],
                               -1e30)) for i in rng]
    kbeta = [ks[i] * beta[i] for i in rng]
    prod = [lax.dot_general(jnp.concatenate([kbeta[i], qs[i]], axis=0).astype(BF16), ks[i].astype(BF16),
                            (((1,), (1,)), ((), ())), preferred_element_type=F32) for i in rng]
    a_mat = [jnp.where(gts[i]["strict"], prod[i][:c] * decay[i], 0.0) for i in rng]
    intra = [prod[i][c:] * decay[i] for i in rng]
    ks_qs = [_bdot(jnp.concatenate([kbeta[i] * egc[i], qs[i] * egc[i]], axis=0), ss[i]) for i in rng]
    p = [jnp.where(gts[i]["diag_blk"], -a_mat[i], 0.0) for i in rng]
    y = [jnp.concatenate([vs[i] * beta[i] - ks_qs[i][:c], jnp.where(gts[i]["diag_blk"], 0.0, a_mat[i])], axis=1)
         for i in rng]
    for j in range(4):
        y = [y[i] + _bdot(p[i], y[i]) for i in rng]
        if j < 3:
            p = [_bdot(p[i], p[i]) for i in rng]
    e = [y[i][:, dh:] for i in rng]
    y = [y[i][:, :dh] for i in rng]
    e2 = [_bdot(e[i], e[i]) for i in rng]
    y = [y[i] + _bdot(e2[i], y[i]) for i in rng]
    v_new = [y[i] - _bdot(e[i], y[i]) for i in rng]
    o = [ks_qs[i][c:] + _bdot(intra[i], v_new[i]) for i in rng]
    s_new = [ss[i] * gts[i]["glast"][:, cgs[i]:cgs[i] + 1] + lax.dot_general(
        (ks[i] * gts[i]["kdec"][:, cgs[i]:cgs[i] + 1]).astype(BF16), v_new[i].astype(BF16),
        (((0,), (0,)), ((), ())), preferred_element_type=F32) for i in rng]
    return o, s_new


def _deltanet_kernel(nb, qf_ref, kf_ref, vf_ref, gf_ref, qb_ref, kb_ref, vb_ref, gb_ref, alog_ref, dtb_ref,
                     of_ref, ob_ref, s_ref):
    @pl.when(pl.program_id(0) == 0)
    def _():
        s_ref[...] = jnp.zeros_like(s_ref)

    dirs = ((qf_ref, kf_ref, vf_ref, gf_ref, of_ref), (qb_ref, kb_ref, vb_ref, gb_ref, ob_ref))

    def body(it, carry):
        idx = [(it * DN_BATCHES + j, d, hh) for j in range(DN_BATCHES) for d in range(2) for hh in range(DN_HEADS)]
        rows_of = {j: pl.ds(it * DN_BATCHES + j, CHUNK, stride=nb) for j in range(DN_BATCHES)}
        rows_b = [rows_of[i // (2 * DN_HEADS)] for i in range(len(idx))]
        cbs = [d * DN_HEADS + hh for _, d, hh in idx]
        gates = {(j, d): _deltanet_gates(d, dirs[d][3][rows_of[j], :], alog_ref[...], dtb_ref[...])
                 for j in range(DN_BATCHES) for d in range(2)}
        gts = [gates[(i // (2 * DN_HEADS), d)] for i, (_, d, _) in enumerate(idx)]
        qs = [dirs[d][0][hh, rows_b[i], :] for i, (_, d, hh) in enumerate(idx)]
        ks = [dirs[d][1][hh, rows_b[i], :] for i, (_, d, hh) in enumerate(idx)]
        vs = [dirs[d][2][hh, rows_b[i], :] for i, (_, d, hh) in enumerate(idx)]
        ss = [s_ref[b * (2 * DN_HEADS) + cbs[i]] for i, (b, _, _) in enumerate(idx)]
        o, s_new = _deltanet_heads(gts, cbs, qs, ks, vs, ss)
        for i, (b, d, hh) in enumerate(idx):
            dirs[d][4][hh, rows_b[i], :] = o[i]
            s_ref[b * (2 * DN_HEADS) + cbs[i]] = s_new[i]
        return carry

    lax.fori_loop(0, nb // DN_BATCHES, body, 0)


def _deltanet(qkv, proj, alog_vec, dtb_vec, nb, n_ctx_chunks):
    rows = qkv.shape[1]
    tr = CHUNK * nb
    n_tot = rows // tr
    fwd = lambda i: i
    bwd = lambda i: _bwd_chunk(i, n_ctx_chunks, n_tot)

    def specs(order):
        return [pl.BlockSpec((DN_HEADS, tr, DN_HEAD_DIM), lambda i, p=p: (p, order(i), 0)) for p in range(3)] + [
            pl.BlockSpec((tr, LANES), lambda i: (order(i), COL_GATE))]

    vec = pl.BlockSpec((1, LANES), lambda i: (0, 0))
    out = jax.ShapeDtypeStruct((DN_HEADS, rows, DN_HEAD_DIM), F32)
    return pl.pallas_call(
        functools.partial(_deltanet_kernel, nb),
        grid=(n_tot,),
        in_specs=specs(fwd) + specs(bwd) + [vec, vec],
        out_specs=[pl.BlockSpec((DN_HEADS, tr, DN_HEAD_DIM), lambda i: (0, fwd(i), 0)),
                   pl.BlockSpec((DN_HEADS, tr, DN_HEAD_DIM), lambda i: (0, bwd(i), 0))],
        out_shape=[out, out],
        scratch_shapes=[pltpu.VMEM((nb * 2 * DN_HEADS, DN_HEAD_DIM, DN_HEAD_DIM), F32)],
        compiler_params=_cparams(("arbitrary",)),
        name="deltanet",
    )(qkv, qkv, qkv, proj, qkv, qkv, qkv, proj, alog_vec, dtb_vec)


def _mixer_out_kernel(nb, of_ref, ob_ref, z_ref, hf_ref, hb_ref, ry_ref, onw_ref, w_ref, h_ref, gt_ref, g_ref, b_ref,
                      o_ref):
    z = z_ref[...]
    parts = []
    for hh in range(DN_HEADS):
        sl = slice(hh * DN_HEAD_DIM, (hh + 1) * DN_HEAD_DIM)
        oh = of_ref[hh] + ob_ref[hh]
        inv = lax.rsqrt(jnp.mean(oh * oh, axis=-1, keepdims=True) + NORM_EPS)
        parts.append(oh * inv * onw_ref[...] * _silu(z[:, sl]))
    parts.append((hf_ref[...].astype(F32) + hb_ref[...].astype(F32)) * jax.nn.gelu(ry_ref[...]))
    y = jnp.concatenate(parts, axis=1).astype(BF16)
    u = jnp.dot(y, w_ref[...], preferred_element_type=F32)
    r = DEEP_ALPHA * h_ref[...] + _per_batch(u, gt_ref[...], nb) * u
    o_ref[...] = _layer_norm(r) * g_ref[...] + b_ref[...]


def _mixer_out(o_f, o_b, proj, hf, hb, onorm_w, w_out, h, mod_rows, ln_g, ln_b, nb, ctx_rows, row0):
    rows, d = h.shape
    tm = ROW_TILE
    t0 = row0 // tm
    lat = lambda i: jnp.where((i + t0) * tm >= ctx_rows, 1, 0)
    half = pl.BlockSpec((tm, DN_WIDTH), lambda i: (i + t0, 0))
    planes = pl.BlockSpec((DN_HEADS, tm, DN_HEAD_DIM), lambda i: (0, i + t0, 0))
    vec = lambda n: pl.BlockSpec((1, n), lambda i: (0, 0))
    return pl.pallas_call(
        functools.partial(_mixer_out_kernel, nb),
        grid=((rows - row0) // tm,),
        in_specs=[planes, planes, pl.BlockSpec((tm, DN_WIDTH), lambda i: (i + t0, COL_Z)), half, half,
                  pl.BlockSpec((tm, RG_WIDTH), lambda i: (i + t0, COL_RY)),
                  vec(DN_HEAD_DIM), pl.BlockSpec((d, d), lambda i: (0, 0)),
                  pl.BlockSpec((tm, d), lambda i: (i + t0, 0)),
                  pl.BlockSpec((None, nb, d), lambda i: (lat(i), 0, 2)),
                  vec(d), vec(d)],
        out_specs=pl.BlockSpec((tm, d), lambda i: (i, 0)),
        out_shape=jax.ShapeDtypeStruct((rows - row0, d), F32),
        compiler_params=_cparams(("parallel",)),
        name="mixer_out",
    )(o_f, o_b, proj, hf, hb, proj, onorm_w, w_out, h, mod_rows, ln_g, ln_b)


PAIRS_PER_GROUP = EXPERTS_PER_GROUP * (EXPERTS_PER_GROUP - 1) // 2
N_CLASSES = N_GROUPS * PAIRS_PER_GROUP
assert N_CLASSES <= LANES
INFO_CLASS, INFO_RANK, INFO_W_LO, INFO_W_HI = range(4)
Y_SUB = 2 * (D_MODEL // LANES)


ROW_SUB = D_MODEL // LANES


def _store_row_tiles(ref, x):
    n = x.shape[0]
    for c in range(ROW_SUB):
        ref[pl.ds(c, n, stride=ROW_SUB), :] = x[:, c * LANES:(c + 1) * LANES]


def _load_row_tiles(ref, n):
    return jnp.concatenate([ref[pl.ds(c, n, stride=ROW_SUB), :] for c in range(ROW_SUB)], axis=1)


def _first_index(mask, lane):
    return jnp.min(jnp.where(mask, lane, LANES), axis=-1, keepdims=True)


def _router_kernel(nb, x_ref, sh_ref, sc_ref, wr_ref, br_ref, hm_ref, info_ref, cnt_ref, run_ref):
    @pl.when(pl.program_id(0) == 0)
    def _():
        run_ref[...] = jnp.zeros_like(run_ref)

    xn = _layer_norm(x_ref[...])
    hm = xn * (1.0 + _per_batch(xn, sc_ref[...], nb)) + _per_batch(xn, sh_ref[...], nb)
    _store_row_tiles(hm_ref, hm)
    tm = hm.shape[0]
    h_hi = hm.astype(BF16)
    h_lo = (hm - h_hi.astype(F32)).astype(BF16)
    hi_hl = jnp.dot(h_hi, wr_ref[...], preferred_element_type=F32)
    logits = (hi_hl[:, :LANES] + hi_hl[:, LANES:] + jnp.dot(h_lo, wr_ref[:, :LANES], preferred_element_type=F32)
              + br_ref[...])
    lane = lax.broadcasted_iota(jnp.int32, (tm, LANES), 1)
    neg = -jnp.inf
    is_g = lane < N_GROUPS
    gmax = jnp.max(jnp.where(is_g, logits, neg), axis=-1, keepdims=True)
    g_sel = _first_index(jnp.logical_and(is_g, logits == gmax), lane)
    p_group = 1.0 / jnp.sum(jnp.where(is_g, jnp.exp(logits - gmax), 0.0), axis=-1, keepdims=True)
    lo = N_GROUPS + EXPERTS_PER_GROUP * g_sel
    in_grp = jnp.logical_and(lane >= lo, lane < lo + EXPERTS_PER_GROUP)
    m1 = jnp.max(jnp.where(in_grp, logits, neg), axis=-1, keepdims=True)
    i1 = _first_index(jnp.logical_and(in_grp, logits == m1), lane)
    rest = jnp.logical_and(in_grp, lane != i1)
    m2 = jnp.max(jnp.where(rest, logits, neg), axis=-1, keepdims=True)
    i2 = _first_index(jnp.logical_and(rest, logits == m2), lane)
    e2 = jnp.exp(m2 - m1)
    w1 = p_group / (1.0 + e2)
    w2 = p_group * e2 / (1.0 + e2)
    l1 = i1 - lo
    l2 = i2 - lo
    e_lo = jnp.minimum(l1, l2)
    e_hi = jnp.maximum(l1, l2)
    pair = jnp.right_shift(e_lo * (2 * EXPERTS_PER_GROUP - 1 - e_lo), 1) + (e_hi - e_lo - 1)
    cls = g_sel * PAIRS_PER_GROUP + pair
    first_is_lo = l1 < l2
    w_lo = jnp.where(first_is_lo, w1, w2)
    w_hi = jnp.where(first_is_lo, w2, w1)
    oh = lane == cls
    r_i = lax.broadcasted_iota(jnp.int32, (tm, tm), 0)
    c_i = lax.broadcasted_iota(jnp.int32, (tm, tm), 1)
    before = jnp.dot((r_i > c_i).astype(BF16), oh.astype(BF16), preferred_element_type=F32) + run_ref[...]
    rank = jnp.sum(jnp.where(oh, before, 0.0), axis=-1, keepdims=True)
    run_ref[...] = run_ref[...] + jnp.sum(oh.astype(F32), axis=0, keepdims=True)
    cnt_ref[...] = run_ref[...]
    info = jnp.zeros((tm, LANES), F32)
    for idx, val in ((INFO_CLASS, cls.astype(F32)), (INFO_RANK, rank), (INFO_W_LO, w_lo), (INFO_W_HI, w_hi)):
        info = jnp.where(lane == idx, val, info)
    info_ref[...] = info


def _router(x1, mod_rows, wr, br, nb, ctx_rows):
    rows, d = x1.shape
    tm = ROW_TILE
    lat = lambda i: jnp.where(i * tm >= ctx_rows, 1, 0)
    return pl.pallas_call(
        functools.partial(_router_kernel, nb),
        grid=(rows // tm,),
        in_specs=[pl.BlockSpec((tm, d), lambda i: (i, 0)),
                  pl.BlockSpec((None, nb, d), lambda i: (lat(i), 0, 3)),
                  pl.BlockSpec((None, nb, d), lambda i: (lat(i), 0, 4)),
                  pl.BlockSpec((d, 2 * LANES), lambda i: (0, 0)),
                  pl.BlockSpec((1, LANES), lambda i: (0, 0))],
        out_specs=[pl.BlockSpec((tm * ROW_SUB, LANES), lambda i: (i, 0)),
                   pl.BlockSpec((tm, LANES), lambda i: (i, 0)),
                   pl.BlockSpec((1, LANES), lambda i: (0, 0))],
        out_shape=[jax.ShapeDtypeStruct((rows * ROW_SUB, LANES), F32),
                   jax.ShapeDtypeStruct((rows, LANES), F32),
                   jax.ShapeDtypeStruct((1, LANES), F32)],
        scratch_shapes=[pltpu.VMEM((1, LANES), F32)],
        compiler_params=_cparams(("arbitrary",)),
        name="moe_router",
    )(x1, mod_rows, mod_rows, wr, br)


def _dispatch_kernel(tmx, ends_ref, cnt_ref, dst_ref, hm_ref, xs_hbm, zero_scr, sem, zsem):
    rs = ROW_SUB
    tb = dst_ref.shape[1]

    @pl.when(pl.program_id(0) == 0)
    def _():
        zero_scr[...] = jnp.zeros_like(zero_scr)

        def tail_copy(c):
            start = pl.multiple_of((ends_ref[c] - tmx) * rs, tmx * rs)
            return pltpu.make_async_copy(zero_scr, xs_hbm.at[pl.ds(start, tmx * rs)], zsem)

        def start_one(c, carry):
            @pl.when(cnt_ref[c] > 0)
            def _():
                tail_copy(c).start()
            return carry

        def wait_one(c, carry):
            @pl.when(cnt_ref[c] > 0)
            def _():
                tail_copy(c).wait()
            return carry

        lax.fori_loop(0, N_CLASSES, start_one, 0)

        def spare_copy(j):
            return pltpu.make_async_copy(zero_scr, xs_hbm.at[pl.ds(pl.multiple_of(j * (tmx * rs), tmx * rs), tmx * rs)],
                                         zsem)

        n_used = ends_ref[N_CLASSES - 1] // tmx
        n_tiles = xs_hbm.shape[0] // (tmx * rs)
        lax.fori_loop(n_used, n_tiles, lambda j, carry: (spare_copy(j).start(), carry)[1], 0)
        lax.fori_loop(0, N_CLASSES, wait_one, 0)
        lax.fori_loop(n_used, n_tiles, lambda j, carry: (spare_copy(j).wait(), carry)[1], 0)

    for r in range(tb):
        dst = xs_hbm.at[pl.ds(pl.multiple_of(dst_ref[0, r], rs), rs)]
        pltpu.make_async_copy(hm_ref.at[pl.ds(r * rs, rs)], dst, sem).start(priority=r % 2)
    pltpu.make_async_copy(hm_ref, hm_ref, sem).wait()


def _dispatch(hm, dest, class_ends, class_cnt, sorted_rows, tmx):
    tb = ROW_TILE
    n_blocks = dest.shape[0] // tb
    grid_spec = pltpu.PrefetchScalarGridSpec(
        num_scalar_prefetch=2,
        grid=(n_blocks,),
        in_specs=[pl.BlockSpec((None, 1, tb), lambda i, e, c: (i, 0, 0), memory_space=pltpu.SMEM),
                  pl.BlockSpec((tb * ROW_SUB, LANES), lambda i, e, c: (i, 0))],
        out_specs=pl.BlockSpec(memory_space=pl.ANY),
        scratch_shapes=[pltpu.VMEM((tmx * ROW_SUB, LANES), F32), pltpu.SemaphoreType.DMA(()),
                        pltpu.SemaphoreType.DMA(())],
    )
    return pl.pallas_call(
        functools.partial(_dispatch_kernel, tmx),
        grid_spec=grid_spec,
        out_shape=jax.ShapeDtypeStruct((sorted_rows * ROW_SUB, LANES), F32),
        compiler_params=_cparams(("arbitrary",)),
        name="moe_dispatch",
    )(class_ends, class_cnt, (dest * ROW_SUB).reshape(n_blocks, 1, tb), hm)


def _class_ffn_kernel(telo_ref, tehi_ref, nact_ref, xs_ref, wgul_ref, wdl_ref, wguh_ref, wdh_ref, ys_ref, y_scr):
    j = pl.program_id(0)
    tmx = xs_ref.shape[0] // ROW_SUB
    de = wdl_ref.shape[0]

    @pl.when(j < nact_ref[0])
    def _():
        x = _load_row_tiles(xs_ref, tmx).astype(BF16)
        for half, (wgu_ref, wd_ref) in enumerate(((wgul_ref, wdl_ref), (wguh_ref, wdh_ref))):
            ab = jnp.dot(x, wgu_ref[...], preferred_element_type=F32)
            y = jnp.dot((_silu(ab[:, :de]) * ab[:, de:]).astype(BF16), wd_ref[...], preferred_element_type=F32)
            for c in range(ROW_SUB):
                y_scr[pl.ds(half * ROW_SUB + c, tmx, stride=Y_SUB), :] = y[:, c * LANES:(c + 1) * LANES]
        ys_ref[...] = y_scr[...].astype(BF16)

    @pl.when(j >= nact_ref[0])
    def _():
        ys_ref[...] = jnp.zeros_like(ys_ref)


def _class_ffn(xs, tile_e_lo, tile_e_hi, n_active, w_gate_up, w_down, tmx):
    n_tiles = xs.shape[0] // (tmx * ROW_SUB)
    d, de2 = w_gate_up.shape[-2:]
    w_in = lambda te_idx: pl.BlockSpec((None, d, de2), lambda j, *te: (te[te_idx][j], 0, 0))
    w_out = lambda te_idx: pl.BlockSpec((None, de2 // 2, d), lambda j, *te: (te[te_idx][j], 0, 0))
    grid_spec = pltpu.PrefetchScalarGridSpec(
        num_scalar_prefetch=3,
        grid=(n_tiles,),
        in_specs=[pl.BlockSpec((tmx * ROW_SUB, LANES), lambda j, lo, hi, na: (jnp.minimum(j, na[0] - 1), 0)),
                  w_in(0), w_out(0), w_in(1), w_out(1)],
        out_specs=pl.BlockSpec((tmx * Y_SUB, LANES), lambda j, lo, hi, na: (j, 0)),
        scratch_shapes=[pltpu.VMEM((tmx * Y_SUB, LANES), F32)],
    )
    return pl.pallas_call(
        _class_ffn_kernel,
        grid_spec=grid_spec,
        out_shape=jax.ShapeDtypeStruct((n_tiles * tmx * Y_SUB, LANES), BF16),
        compiler_params=_cparams(("arbitrary",)),
        name="expert_ffn",
    )(tile_e_lo, tile_e_hi, n_active, xs, w_gate_up, w_down, w_gate_up, w_down)


def _moe_out_kernel(nb, batch_major, src0_ref, srcn_ref, ys_hbm, info_ref, x_ref, gt_ref, g_ref, b_ref, o_ref,
                    ybuf, y_scr, sem):
    i = pl.program_id(0)
    tm = info_ref.shape[0]
    slot = i % 2
    nslot = 1 - slot

    def gather_all(src_ref, s):
        for r in range(tm):
            pltpu.make_async_copy(ys_hbm.at[pl.ds(pl.multiple_of(src_ref[0, r], Y_SUB), Y_SUB)],
                                  ybuf.at[s, pl.ds(r * Y_SUB, Y_SUB)], sem.at[s]).start(priority=r % 2)

    def wait_rows(s):
        pltpu.make_async_copy(ybuf.at[s], ybuf.at[s], sem.at[s]).wait()

    @pl.when(i == 0)
    def _():
        gather_all(src0_ref, 0)

    wait_rows(slot)
    gather_all(srcn_ref, nslot)
    info = info_ref[...]
    y_scr[...] = ybuf[slot].astype(F32)
    y_lo = jnp.concatenate([y_scr[pl.ds(c, tm, stride=Y_SUB), :] for c in range(ROW_SUB)], axis=1)
    y_hi = jnp.concatenate([y_scr[pl.ds(ROW_SUB + c, tm, stride=Y_SUB), :] for c in range(ROW_SUB)], axis=1)
    f = y_lo * info[:, INFO_W_LO:INFO_W_LO + 1] + y_hi * info[:, INFO_W_HI:INFO_W_HI + 1]
    r = DEEP_ALPHA * x_ref[...] + _per_batch(f, gt_ref[...], nb) * f
    res = _layer_norm(r) * g_ref[...] + b_ref[...]
    if batch_major:
        for c in range(ROW_SUB):
            y_scr[pl.ds(c * tm, tm), :] = res[:, c * LANES:(c + 1) * LANES]
        for b in range(nb):
            for c in range(ROW_SUB):
                o_ref[b, :, c * LANES:(c + 1) * LANES] = y_scr[pl.ds(c * tm + b, tm // nb, stride=nb), :]
    else:
        o_ref[...] = res

    @pl.when(i == pl.num_programs(0) - 1)
    def _():
        wait_rows(nslot)


def _moe_out(ys, dest, info, x1, mod_rows, ln_g, ln_b, nb, ctx_rows, batch_major):
    rows, d = x1.shape
    tm = ROW_TILE
    n_blocks = rows // tm
    if batch_major:
        out_spec = pl.BlockSpec((nb, tm // nb, d), lambda i: (0, i, 0))
        out_shape = jax.ShapeDtypeStruct((nb, rows // nb, d), F32)
    else:
        out_spec = pl.BlockSpec((tm, d), lambda i: (i, 0))
        out_shape = jax.ShapeDtypeStruct((rows, d), F32)
    lat = lambda i: jnp.where(i * tm >= ctx_rows, 1, 0)
    vec = pl.BlockSpec((1, d), lambda i: (0, 0))
    src = (dest * Y_SUB).reshape(n_blocks, 1, tm)
    smem = lambda f: pl.BlockSpec((None, 1, tm), f, memory_space=pltpu.SMEM)
    return pl.pallas_call(
        functools.partial(_moe_out_kernel, nb, batch_major),
        grid=(n_blocks,),
        in_specs=[smem(lambda i: (0, 0, 0)),
                  smem(lambda i: (jnp.minimum(i + 1, n_blocks - 1), 0, 0)),
                  pl.BlockSpec(memory_space=pl.ANY),
                  pl.BlockSpec((tm, LANES), lambda i: (i, 0)),
                  pl.BlockSpec((tm, d), lambda i: (i, 0)),
                  pl.BlockSpec((None, nb, d), lambda i: (lat(i), 0, 5)),
                  vec, vec],
        out_specs=out_spec,
        out_shape=out_shape,
        scratch_shapes=[pltpu.VMEM((2, tm * Y_SUB, LANES), BF16), pltpu.VMEM((tm * Y_SUB, LANES), F32),
                        pltpu.SemaphoreType.DMA((2,))],
        compiler_params=_cparams(("arbitrary",)),
        name="moe_out",
    )(src, src, ys, info, x1, mod_rows, ln_g, ln_b)


def _class_experts():
    lo, hi = [], []
    for g in range(N_GROUPS):
        for a in range(EXPERTS_PER_GROUP):
            for b in range(a + 1, EXPERTS_PER_GROUP):
                lo.append(g * EXPERTS_PER_GROUP + a)
                hi.append(g * EXPERTS_PER_GROUP + b)
    return np.asarray(lo, np.int32), np.asarray(hi, np.int32)


def _routing_tables(info, counts, n_tiles, tmx):
    cnt = counts[0, :N_CLASSES].astype(jnp.int32)
    padded = ((cnt + tmx - 1) // tmx) * tmx
    ends = jnp.cumsum(padded)
    starts = ends - padded
    cls = info[:, INFO_CLASS].astype(jnp.int32)
    class_ids = jnp.arange(N_CLASSES, dtype=jnp.int32)
    start_of_tok = jnp.sum(jnp.where(cls[:, None] == class_ids[None, :], starts[None, :], 0), axis=1)
    dest = start_of_tok + info[:, INFO_RANK].astype(jnp.int32)
    tile_start = jnp.arange(n_tiles, dtype=jnp.int32) * tmx
    n_before = jnp.sum((ends[None, :] <= tile_start[:, None]).astype(jnp.int32), axis=1)
    tile_cls = jnp.minimum(n_before, N_CLASSES - 1)
    e_lo, e_hi = _class_experts()
    n_active = (ends[-1:] // tmx).astype(jnp.int32)
    return dest, ends, cnt, jnp.asarray(e_lo)[tile_cls], jnp.asarray(e_hi)[tile_cls], n_active


def _relayout_w_in(w_in):
    qkvz = w_in[..., :4 * DN_WIDTH]
    gates = w_in[..., 4 * DN_WIDTH:4 * DN_WIDTH + 4 * DN_HEADS]
    rxy = w_in[..., 4 * DN_WIDTH + 4 * DN_HEADS:]
    pad = jnp.zeros(w_in.shape[:-1] + (LANES - 4 * DN_HEADS,), w_in.dtype)
    return jnp.concatenate([qkvz, rxy, gates, pad], axis=-1).astype(BF16)


def _block_diag(w):
    n, c, _ = w.shape
    eye = jnp.eye(n, dtype=w.dtype)
    return (eye[:, None, :, None] * w[:, :, None, :]).reshape(n * c, n * c)


def _lane_vec(vals, offset):
    return jnp.zeros((1, LANES), F32).at[0, offset:offset + vals.size].set(vals.reshape(-1))


def kernel(x, c, ctx, c_ctx, w_ada, b_ada, w_in, conv_qkv_w, dn_a_log, dn_dt_bias, dn_onorm_w, rg_conv_w, rg_conv_b, rg_wa, rg_ba, rg_wi, rg_bi, rg_lambda, w_out, ln_g, ln_b, router_wg, router_bg, router_we, router_be, w_e_gate, w_e_up, w_e_down):
    nb, t_lat, d = x.shape
    t_ctx = ctx.shape[1]
    assert d == D_MODEL and nb % 8 == 0 and t_lat % GRID_W == 0 and t_ctx % CHUNK == 0
    assert (t_ctx * nb) % ROW_TILE == 0 and (t_lat * nb) % ROW_TILE == 0
    tt = t_ctx + t_lat
    rows = tt * nb
    ctx_rows = t_ctx * nb
    n_ctx_chunks = t_ctx // CHUNK

    h = jnp.concatenate([jnp.swapaxes(ctx, 0, 1), jnp.swapaxes(x, 0, 1)], axis=0).reshape(rows, d)

    n_cc = ((nb + 1 + 7) // 8) * 8
    cc = jnp.zeros((n_cc, d), F32).at[:nb].set(c).at[nb].set(c_ctx)
    mod = _modulation(cc, w_ada, b_ada)
    mod_ctx = jnp.broadcast_to(mod[:, nb:nb + 1], (DEPTH, nb, 6 * d))
    mod_rows = jnp.stack([mod_ctx, mod[:, :nb]], axis=1)

    w_in2 = _relayout_w_in(w_in)
    for l in range(DEPTH):
        row0 = ctx_rows if l == DEPTH - 1 else 0
        moe_ctx_rows = ctx_rows - row0
        n_tiles = (rows - row0) // MOE_TILE + N_CLASSES
        mr = mod_rows[l]
        proj = _in_projection(h, mr, w_in2[l], nb, ctx_rows)
        qkv = _qkv_prep(proj, conv_qkv_w[l], nb, n_ctx_chunks)
        wg = jnp.concatenate([_block_diag(rg_wa[l, 0]), _block_diag(rg_wi[l, 0]),
                              _block_diag(rg_wa[l, 1]), _block_diag(rg_wi[l, 1])], axis=1).astype(BF16)
        bg = jnp.concatenate([rg_ba[l, 0], rg_bi[l, 0], rg_ba[l, 1], rg_bi[l, 1]])[None]
        hf, hb = _rglru(proj, rg_conv_w[l], rg_conv_b[l][None], wg, bg, rg_lambda[l], nb, n_ctx_chunks)
        alog_vec = _lane_vec(dn_a_log[l], 2 * DN_HEADS)
        dtb_vec = _lane_vec(dn_dt_bias[l], 2 * DN_HEADS)
        o_f, o_b = _deltanet(qkv, proj, alog_vec, dtb_vec, nb, n_ctx_chunks)
        x1 = _mixer_out(o_f, o_b, proj, hf, hb, dn_onorm_w[l][None], w_out[l].astype(BF16), h, mr,
                        ln_g[l, 0][None], ln_b[l, 0][None], nb, ctx_rows, row0)
        wr = jnp.zeros((d, LANES), F32).at[:, :N_GROUPS].set(router_wg[l]).at[
            :, N_GROUPS:N_GROUPS + N_EXPERTS].set(router_we[l])
        br = _lane_vec(jnp.concatenate([router_bg[l], router_be[l]]), 0)
        wr_hi = wr.astype(BF16)
        wr_hl = jnp.concatenate([wr_hi, (wr - wr_hi.astype(F32)).astype(BF16)], axis=1)
        hm, info, counts = _router(x1, mr, wr_hl, br, nb, moe_ctx_rows)
        dest, class_ends, class_cnt, tile_e_lo, tile_e_hi, n_active = _routing_tables(info, counts, n_tiles, MOE_TILE)
        xs = _dispatch(hm, dest, class_ends, class_cnt, n_tiles * MOE_TILE, MOE_TILE)
        w_gate_up = jnp.concatenate([w_e_gate[l], w_e_up[l]], axis=-1).astype(BF16)
        ys = _class_ffn(xs, tile_e_lo, tile_e_hi, n_active, w_gate_up, w_e_down[l].astype(BF16), MOE_TILE)
        h = _moe_out(ys, dest, info, x1, mr, ln_g[l, 1][None], ln_b[l, 1][None], nb, moe_ctx_rows,
                     batch_major=(l == DEPTH - 1))

    return h
```

```python
import functools
import math

import jax
import jax.numpy as jnp
import numpy as np
from jax import lax
from jax.experimental import pallas as pl
from jax.experimental.pallas import tpu as pltpu

F32 = jnp.float32
BF16 = jnp.bfloat16
HIGHEST = lax.Precision.HIGHEST

D_MODEL = 1024
DEPTH = 2
GRID_W = 64
DN_HEAD_DIM = 128
DN_WIDTH = 512
DN_HEADS = 4
RG_WIDTH = 512
RG_BLOCKS = 8
RG_BLOCK_DIM = 64
RG_C = 8.0
CONV_K = 4
N_GROUPS = 4
EXPERTS_PER_GROUP = 8
N_EXPERTS = 32
D_EXPERT = 256
DEEP_ALPHA = (2 * DEPTH) ** 0.25
LN_EPS = 1e-5
NORM_EPS = 1e-6

LANES = 128
CHUNK = 64
SOLVE_BLOCK = 16
DN_BATCHES = 4
D_PROJ = 3 * DN_WIDTH + DN_WIDTH + 2 * RG_WIDTH + LANES
COL_Z = 3
COL_RX = 4
COL_RY = 5
COL_GATE = (3 * DN_WIDTH + DN_WIDTH + 2 * RG_WIDTH) // LANES
ROW_TILE = 512
MOE_TILE = 256
VMEM_LIMIT = 56 * 1024 * 1024


def _cparams(sem):
    return pltpu.CompilerParams(dimension_semantics=sem, vmem_limit_bytes=VMEM_LIMIT)


def _layer_norm(x):
    mu = jnp.mean(x, axis=-1, keepdims=True)
    xc = x - mu
    var = jnp.mean(xc * xc, axis=-1, keepdims=True)
    return xc * lax.rsqrt(var + LN_EPS)


def _per_batch(x, v, nb):
    rows, c = x.shape
    return jnp.broadcast_to(v[None], (rows // nb, nb, c)).reshape(rows, c)


def _sigmoid(x):
    return 0.5 * jnp.tanh(0.5 * x) + 0.5


def _silu(x):
    return x * _sigmoid(x)


def _mod_kernel(cc_ref, w_ref, b_ref, o_ref):
    s = _silu(cc_ref[...])
    o_ref[...] = jnp.dot(s.astype(BF16), w_ref[...].astype(BF16), preferred_element_type=F32) + b_ref[...]


def _modulation(cc, w_ada, b_ada):
    depth, d, n6 = w_ada.shape
    rows = cc.shape[0]
    tn = 1536
    return pl.pallas_call(
        _mod_kernel,
        grid=(depth, n6 // tn),
        in_specs=[
            pl.BlockSpec((rows, d), lambda l, j: (0, 0)),
            pl.BlockSpec((None, d, tn), lambda l, j: (l, 0, j)),
            pl.BlockSpec((None, 1, tn), lambda l, j: (l, 0, j)),
        ],
        out_specs=pl.BlockSpec((None, rows, tn), lambda l, j: (l, 0, j)),
        out_shape=jax.ShapeDtypeStruct((depth, rows, n6), F32),
        compiler_params=_cparams(("parallel", "parallel")),
        name="adaln_mod",
    )(cc, w_ada, b_ada.reshape(depth, 1, n6))


def _inproj_kernel(nb, h_ref, sh_ref, sc_ref, w_ref, o_ref):
    xn = _layer_norm(h_ref[...])
    hm = xn * (1.0 + _per_batch(xn, sc_ref[...], nb)) + _per_batch(xn, sh_ref[...], nb)
    o_ref[...] = jnp.dot(hm.astype(BF16), w_ref[...], preferred_element_type=F32)


def _in_projection(h, mod_rows, w_in2, layer, nb, ctx_rows):
    rows, d = h.shape
    tm = ROW_TILE
    lat = lambda i: jnp.where(i * tm >= ctx_rows, 1, 0)
    return pl.pallas_call(
        functools.partial(_inproj_kernel, nb),
        grid=(rows // tm,),
        in_specs=[
            pl.BlockSpec((tm, d), lambda i: (i, 0)),
            pl.BlockSpec((None, nb, d), lambda i: (lat(i), 0, 0)),
            pl.BlockSpec((None, nb, d), lambda i: (lat(i), 0, 1)),
            pl.BlockSpec((None, d, D_PROJ), lambda i: (layer, 0, 0)),
        ],
        out_specs=pl.BlockSpec((tm, D_PROJ), lambda i: (i, 0)),
        out_shape=jax.ShapeDtypeStruct((rows, D_PROJ), F32),
        compiler_params=_cparams(("parallel",)),
        name="ln_mod_inproj",
    )(h, mod_rows, mod_rows, w_in2)


def _conv_tile(x, prev, nxt, w, nb, use_prev, use_next):
    prev = jnp.where(use_prev, prev, 0.0)
    nxt = jnp.where(use_next, nxt, 0.0)
    xm2 = jnp.concatenate([prev, x[: -2 * nb]], axis=0)
    xm1 = jnp.concatenate([prev[nb:], x[:-nb]], axis=0)
    xp1 = jnp.concatenate([x[nb:], nxt], axis=0)
    return xm2 * w[0:1] + xm1 * w[1:2] + x * w[2:3] + xp1 * w[3:4]


def _halo_flags(i, n_ctx_tiles):
    use_prev = jnp.logical_and(i > 0, i < n_ctx_tiles)
    use_next = i < n_ctx_tiles - 1
    return use_prev, use_next


def _qkv_prep_kernel(nb, n_ctx_tiles, x_ref, prev_ref, next_ref, w_ref, o_ref):
    i = pl.program_id(0)
    part = pl.program_id(1)
    use_prev, use_next = _halo_flags(i, n_ctx_tiles)
    y = _silu(_conv_tile(x_ref[...], prev_ref[...], next_ref[...], w_ref[...], nb, use_prev, use_next))
    qscale = jnp.where(part == 0, DN_HEAD_DIM ** -0.5, 1.0).astype(F32)
    for hh in range(DN_HEADS):
        sl = slice(hh * DN_HEAD_DIM, (hh + 1) * DN_HEAD_DIM)
        yh = y[:, sl]
        inv = lax.rsqrt(jnp.sum(yh * yh, axis=-1, keepdims=True) + NORM_EPS) * qscale
        o_ref[hh] = yh * jnp.where(part < 2, inv, 1.0)


def _halo_specs(tile_rows, nb, cols, col_of, n_row_blocks16, tile_of=lambda i: i):
    prev_b = tile_rows // (2 * nb)
    next_b = tile_rows // nb
    return [
        pl.BlockSpec((tile_rows, cols), lambda i, *a: (tile_of(i), col_of(*a))),
        pl.BlockSpec((2 * nb, cols), lambda i, *a: (jnp.maximum(tile_of(i) * prev_b - 1, 0), col_of(*a))),
        pl.BlockSpec((nb, cols),
                     lambda i, *a: (jnp.minimum((tile_of(i) + 1) * next_b, n_row_blocks16 - 1), col_of(*a))),
    ]


def _qkv_prep(proj, conv_w, nb, n_ctx_tiles):
    rows = proj.shape[0]
    tr = CHUNK * nb
    return pl.pallas_call(
        functools.partial(_qkv_prep_kernel, nb, n_ctx_tiles),
        grid=(rows // tr, 3),
        in_specs=_halo_specs(tr, nb, DN_WIDTH, lambda p: p, rows // nb)
        + [pl.BlockSpec((CONV_K, DN_WIDTH), lambda i, p: (0, p))],
        out_specs=pl.BlockSpec((DN_HEADS, tr, DN_HEAD_DIM), lambda i, p: (p, i, 0)),
        out_shape=jax.ShapeDtypeStruct((3 * DN_HEADS, rows, DN_HEAD_DIM), F32),
        compiler_params=_cparams(("parallel", "parallel")),
        name="qkv_conv_norm",
    )(proj, proj, proj, conv_w)


def _bwd_chunk(i, n_ctx, n_tot):
    return jnp.where(i < n_ctx, n_ctx - 1 - i, n_tot + n_ctx - 1 - i)


def _rglru_kernel(nb, n_ctx_tiles, n_tot, xf_ref, pf_ref, nf_ref, xb_ref, pb_ref, nb_ref, cw_ref, cb_ref,
                  wg0_ref, wg1_ref, bg0_ref, bg1_ref, lam_ref, hf_ref, hb_ref, a_scr, b_scr, st_ref):
    i = pl.program_id(0)

    @pl.when(i == 0)
    def _():
        st_ref[...] = jnp.zeros_like(st_ref)

    w = RG_WIDTH
    sp = jax.nn.softplus(-lam_ref[...])
    tiles = (i, _bwd_chunk(i, n_ctx_tiles, n_tot))
    for d, (x_ref, p_ref, n_ref, wg_ref, bg_ref) in enumerate(
            ((xf_ref, pf_ref, nf_ref, wg0_ref, bg0_ref), (xb_ref, pb_ref, nb_ref, wg1_ref, bg1_ref))):
        use_prev, use_next = _halo_flags(tiles[d], n_ctx_tiles)
        xc = _conv_tile(x_ref[...], p_ref[...], n_ref[...], cw_ref[...], nb, use_prev, use_next) + cb_ref[...]
        gates = jnp.dot(xc.astype(BF16), wg_ref[...], preferred_element_type=F32) + bg_ref[...]
        r = _sigmoid(gates[:, :w])
        ig = _sigmoid(gates[:, w:])
        a = jnp.exp(-RG_C * r * sp[d:d + 1])
        y = jnp.maximum(1.0 - a * a, 0.0)
        a_scr[d] = a
        b_scr[d] = jnp.where(y > 0.0, y * lax.rsqrt(y), 0.0) * (ig * xc)

    def body(t, carry):
        h0, h1 = carry
        r0 = pl.ds(pl.multiple_of(t * nb, nb), nb)
        r1 = pl.ds(pl.multiple_of((CHUNK - 1 - t) * nb, nb), nb)
        h0 = a_scr[0, r0, :] * h0 + b_scr[0, r0, :]
        h1 = a_scr[1, r1, :] * h1 + b_scr[1, r1, :]
        hf_ref[r0, :] = h0.astype(hf_ref.dtype)
        hb_ref[r1, :] = h1.astype(hb_ref.dtype)
        return h0, h1

    h0, h1 = lax.fori_loop(0, CHUNK, body, (st_ref[0], st_ref[1]), unroll=8)
    st_ref[0] = h0
    st_ref[1] = h1


def _rglru(proj, conv_w, conv_b, wg, bg, lam, nb, n_ctx_tiles):
    rows = proj.shape[0]
    tr = CHUNK * nb
    n_tot = rows // tr
    w = RG_WIDTH
    out = jax.ShapeDtypeStruct((rows, w), BF16)
    fwd = lambda i: i
    bwd = lambda i: _bwd_chunk(i, n_ctx_tiles, n_tot)
    full = lambda shape: pl.BlockSpec(shape, lambda i: (0,) * len(shape))
    return pl.pallas_call(
        functools.partial(_rglru_kernel, nb, n_ctx_tiles, n_tot),
        grid=(n_tot,),
        in_specs=_halo_specs(tr, nb, w, lambda: COL_RX, rows // nb, fwd)
        + _halo_specs(tr, nb, w, lambda: COL_RX, rows // nb, bwd)
        + [full((CONV_K, w)), full((1, w)),
           pl.BlockSpec((w, 2 * w), lambda i: (0, 0)), pl.BlockSpec((w, 2 * w), lambda i: (0, 1)),
           pl.BlockSpec((1, 2 * w), lambda i: (0, 0)), pl.BlockSpec((1, 2 * w), lambda i: (0, 1)),
           full((2, w))],
        out_specs=[pl.BlockSpec((tr, w), lambda i: (fwd(i), 0)), pl.BlockSpec((tr, w), lambda i: (bwd(i), 0))],
        out_shape=[out] * 2,
        scratch_shapes=[pltpu.VMEM((2, tr, w), F32), pltpu.VMEM((2, tr, w), F32), pltpu.VMEM((2, nb, w), F32)],
        compiler_params=_cparams(("arbitrary",)),
        name="rglru",
    )(proj, proj, proj, proj, proj, proj, conv_w, conv_b, wg, wg, bg, bg, lam)


def _bdot(a, b):
    return jnp.dot(a.astype(BF16), b.astype(BF16), preferred_element_type=F32)


def _deltanet_gates(d, raw, alog, dtb):
    c = CHUNK
    row = lax.broadcasted_iota(jnp.int32, (c, c), 0)
    col = lax.broadcasted_iota(jnp.int32, (c, c), 1)
    incl = (row >= col) if d == 0 else (row <= col)
    g_all = -jnp.exp(alog) * jax.nn.softplus(raw + dtb)
    gc_all = jnp.dot(incl.astype(F32), g_all, precision=HIGHEST, preferred_element_type=F32)
    g_tot = jnp.sum(g_all, axis=0, keepdims=True)
    return dict(incl=incl, strict=(row > col) if d == 0 else (row < col),
                diag_blk=(row // SOLVE_BLOCK) == (col // SOLVE_BLOCK),
                beta=_sigmoid(raw), gc=gc_all, gc_t=gc_all.T, egc=jnp.exp(gc_all),
                kdec=jnp.exp(g_tot - gc_all), glast=jnp.exp(g_tot))


def _deltanet_heads(gts, cbs, qs, ks, vs, ss):
    c = CHUNK
    dh = DN_HEAD_DIM
    n = len(cbs)
    rng = range(n)
    cgs = [2 * DN_HEADS + cb for cb in cbs]
    beta = [gts[i]["beta"][:, cbs[i]:cbs[i] + 1] for i in rng]
    egc = [gts[i]["egc"][:, cgs[i]:cgs[i] + 1] for i in rng]
    decay = [jnp.exp(jnp.where(gts[i]["incl"], gts[i]["gc"][:, cgs[i]:cgs[i] + 1] - gts[i]["gc_t"][cgs[i]:cgs[i] + 1, :],
                               -1e30)) for i in rng]
    kbeta = [ks[i] * beta[i] for i in rng]
    prod = [lax.dot_general(jnp.concatenate([kbeta[i], qs[i]], axis=0).astype(BF16), ks[i].astype(BF16),
                            (((1,), (1,)), ((), ())), preferred_element_type=F32) for i in rng]
    a_mat = [jnp.where(gts[i]["strict"], prod[i][:c] * decay[i], 0.0) for i in rng]
    intra = [prod[i][c:] * decay[i] for i in rng]
    ks_qs = [_bdot(jnp.concatenate([kbeta[i] * egc[i], qs[i] * egc[i]], axis=0), ss[i]) for i in rng]
    p = [jnp.where(gts[i]["diag_blk"], -a_mat[i], 0.0) for i in rng]
    y = [jnp.concatenate([vs[i] * beta[i] - ks_qs[i][:c], jnp.where(gts[i]["diag_blk"], 0.0, a_mat[i])], axis=1)
         for i in rng]
    for j in range(4):
        y = [y[i] + _bdot(p[i], y[i]) for i in rng]
        if j < 3:
            p = [_bdot(p[i], p[i]) for i in rng]
    e = [y[i][:, dh:] for i in rng]
    y = [y[i][:, :dh] for i in rng]
    e2 = [_bdot(e[i], e[i]) for i in rng]
    y = [y[i] + _bdot(e2[i], y[i]) for i in rng]
    v_new = [y[i] - _bdot(e[i], y[i]) for i in rng]
    o = [ks_qs[i][c:] + _bdot(intra[i], v_new[i]) for i in rng]
    s_new = [ss[i] * gts[i]["glast"][:, cgs[i]:cgs[i] + 1] + lax.dot_general(
        (ks[i] * gts[i]["kdec"][:, cgs[i]:cgs[i] + 1]).astype(BF16), v_new[i].astype(BF16),
        (((0,), (0,)), ((), ())), preferred_element_type=F32) for i in rng]
    return o, s_new


def _deltanet_kernel(nb, qf_ref, kf_ref, vf_ref, gf_ref, qb_ref, kb_ref, vb_ref, gb_ref, alog_ref, dtb_ref,
                     of_ref, ob_ref, s_ref):
    @pl.when(pl.program_id(0) == 0)
    def _():
        s_ref[...] = jnp.zeros_like(s_ref)

    dirs = ((qf_ref, kf_ref, vf_ref, gf_ref, of_ref), (qb_ref, kb_ref, vb_ref, gb_ref, ob_ref))

    def body(it, carry):
        idx = [(it * DN_BATCHES + j, d, hh) for j in range(DN_BATCHES) for d in range(2) for hh in range(DN_HEADS)]
        rows_of = {j: pl.ds(it * DN_BATCHES + j, CHUNK, stride=nb) for j in range(DN_BATCHES)}
        rows_b = [rows_of[i // (2 * DN_HEADS)] for i in range(len(idx))]
        cbs = [d * DN_HEADS + hh for _, d, hh in idx]
        gates = {(j, d): _deltanet_gates(d, dirs[d][3][rows_of[j], :], alog_ref[...], dtb_ref[...])
                 for j in range(DN_BATCHES) for d in range(2)}
        gts = [gates[(i // (2 * DN_HEADS), d)] for i, (_, d, _) in enumerate(idx)]
        qs = [dirs[d][0][hh, rows_b[i], :] for i, (_, d, hh) in enumerate(idx)]
        ks = [dirs[d][1][hh, rows_b[i], :] for i, (_, d, hh) in enumerate(idx)]
        vs = [dirs[d][2][hh, rows_b[i], :] for i, (_, d, hh) in enumerate(idx)]
        ss = [s_ref[b * (2 * DN_HEADS) + cbs[i]] for i, (b, _, _) in enumerate(idx)]
        o, s_new = _deltanet_heads(gts, cbs, qs, ks, vs, ss)
        for i, (b, d, hh) in enumerate(idx):
            dirs[d][4][hh, rows_b[i], :] = o[i]
            s_ref[b * (2 * DN_HEADS) + cbs[i]] = s_new[i]
        return carry

    lax.fori_loop(0, nb // DN_BATCHES, body, 0)


def _deltanet(qkv, proj, alog_vec, dtb_vec, nb, n_ctx_chunks):
    rows = qkv.shape[1]
    tr = CHUNK * nb
    n_tot = rows // tr
    fwd = lambda i: i
    bwd = lambda i: _bwd_chunk(i, n_ctx_chunks, n_tot)

    def specs(order):
        return [pl.BlockSpec((DN_HEADS, tr, DN_HEAD_DIM), lambda i, p=p: (p, order(i), 0)) for p in range(3)] + [
            pl.BlockSpec((tr, LANES), lambda i: (order(i), COL_GATE))]

    vec = pl.BlockSpec((1, LANES), lambda i: (0, 0))
    out = jax.ShapeDtypeStruct((DN_HEADS, rows, DN_HEAD_DIM), F32)
    return pl.pallas_call(
        functools.partial(_deltanet_kernel, nb),
        grid=(n_tot,),
        in_specs=specs(fwd) + specs(bwd) + [vec, vec],
        out_specs=[pl.BlockSpec((DN_HEADS, tr, DN_HEAD_DIM), lambda i: (0, fwd(i), 0)),
                   pl.BlockSpec((DN_HEADS, tr, DN_HEAD_DIM), lambda i: (0, bwd(i), 0))],
        out_shape=[out, out],
        scratch_shapes=[pltpu.VMEM((nb * 2 * DN_HEADS, DN_HEAD_DIM, DN_HEAD_DIM), F32)],
        compiler_params=_cparams(("arbitrary",)),
        name="deltanet",
    )(qkv, qkv, qkv, proj, qkv, qkv, qkv, proj, alog_vec, dtb_vec)


def _mixer_out_kernel(nb, of_ref, ob_ref, z_ref, hf_ref, hb_ref, ry_ref, onw_ref, w_ref, h_ref, gt_ref, g_ref, b_ref,
                      o_ref):
    z = z_ref[...]
    parts = []
    for hh in range(DN_HEADS):
        sl = slice(hh * DN_HEAD_DIM, (hh + 1) * DN_HEAD_DIM)
        oh = of_ref[hh] + ob_ref[hh]
        inv = lax.rsqrt(jnp.mean(oh * oh, axis=-1, keepdims=True) + NORM_EPS)
        parts.append(oh * inv * onw_ref[...] * _silu(z[:, sl]))
    parts.append((hf_ref[...].astype(F32) + hb_ref[...].astype(F32)) * jax.nn.gelu(ry_ref[...]))
    y = jnp.concatenate(parts, axis=1).astype(BF16)
    u = jnp.dot(y, w_ref[...], preferred_element_type=F32)
    r = DEEP_ALPHA * h_ref[...] + _per_batch(u, gt_ref[...], nb) * u
    o_ref[...] = _layer_norm(r) * g_ref[...] + b_ref[...]


def _mixer_out(o_f, o_b, proj, hf, hb, onorm_w, w_out, layer, h, mod_rows, ln_g, ln_b, nb, ctx_rows, row0):
    rows, d = h.shape
    tm = ROW_TILE
    t0 = row0 // tm
    lat = lambda i: jnp.where((i + t0) * tm >= ctx_rows, 1, 0)
    half = pl.BlockSpec((tm, DN_WIDTH), lambda i: (i + t0, 0))
    planes = pl.BlockSpec((DN_HEADS, tm, DN_HEAD_DIM), lambda i: (0, i + t0, 0))
    vec = lambda n: pl.BlockSpec((1, n), lambda i: (0, 0))
    return pl.pallas_call(
        functools.partial(_mixer_out_kernel, nb),
        grid=((rows - row0) // tm,),
        in_specs=[planes, planes, pl.BlockSpec((tm, DN_WIDTH), lambda i: (i + t0, COL_Z)), half, half,
                  pl.BlockSpec((tm, RG_WIDTH), lambda i: (i + t0, COL_RY)),
                  vec(DN_HEAD_DIM), pl.BlockSpec((None, d, d), lambda i: (layer, 0, 0)),
                  pl.BlockSpec((tm, d), lambda i: (i + t0, 0)),
                  pl.BlockSpec((None, nb, d), lambda i: (lat(i), 0, 2)),
                  vec(d), vec(d)],
        out_specs=pl.BlockSpec((tm, d), lambda i: (i, 0)),
        out_shape=jax.ShapeDtypeStruct((rows - row0, d), F32),
        compiler_params=_cparams(("parallel",)),
        name="mixer_out",
    )(o_f, o_b, proj, hf, hb, proj, onorm_w, w_out, h, mod_rows, ln_g, ln_b)


PAIRS_PER_GROUP = EXPERTS_PER_GROUP * (EXPERTS_PER_GROUP - 1) // 2
N_CLASSES = N_GROUPS * PAIRS_PER_GROUP
assert N_CLASSES <= LANES
INFO_CLASS, INFO_RANK, INFO_W_LO, INFO_W_HI = range(4)
Y_SUB = 2 * (D_MODEL // LANES)


ROW_SUB = D_MODEL // LANES


def _store_row_tiles(ref, x):
    n = x.shape[0]
    for c in range(ROW_SUB):
        ref[pl.ds(c, n, stride=ROW_SUB), :] = x[:, c * LANES:(c + 1) * LANES]


def _load_row_tiles(ref, n):
    return jnp.concatenate([ref[pl.ds(c, n, stride=ROW_SUB), :] for c in range(ROW_SUB)], axis=1)


def _first_index(mask, lane):
    return jnp.min(jnp.where(mask, lane, LANES), axis=-1, keepdims=True)


def _router_kernel(nb, x_ref, sh_ref, sc_ref, wr_ref, br_ref, hm_ref, info_ref, cnt_ref, run_ref):
    @pl.when(pl.program_id(0) == 0)
    def _():
        run_ref[...] = jnp.zeros_like(run_ref)

    xn = _layer_norm(x_ref[...])
    hm = xn * (1.0 + _per_batch(xn, sc_ref[...], nb)) + _per_batch(xn, sh_ref[...], nb)
    _store_row_tiles(hm_ref, hm)
    tm = hm.shape[0]
    h_hi = hm.astype(BF16)
    h_lo = (hm - h_hi.astype(F32)).astype(BF16)
    hi_hl = jnp.dot(h_hi, wr_ref[...], preferred_element_type=F32)
    logits = (hi_hl[:, :LANES] + hi_hl[:, LANES:] + jnp.dot(h_lo, wr_ref[:, :LANES], preferred_element_type=F32)
              + br_ref[...])
    lane = lax.broadcasted_iota(jnp.int32, (tm, LANES), 1)
    neg = -jnp.inf
    is_g = lane < N_GROUPS
    gmax = jnp.max(jnp.where(is_g, logits, neg), axis=-1, keepdims=True)
    g_sel = _first_index(jnp.logical_and(is_g, logits == gmax), lane)
    p_group = 1.0 / jnp.sum(jnp.where(is_g, jnp.exp(logits - gmax), 0.0), axis=-1, keepdims=True)
    lo = N_GROUPS + EXPERTS_PER_GROUP * g_sel
    in_grp = jnp.logical_and(lane >= lo, lane < lo + EXPERTS_PER_GROUP)
    m1 = jnp.max(jnp.where(in_grp, logits, neg), axis=-1, keepdims=True)
    i1 = _first_index(jnp.logical_and(in_grp, logits == m1), lane)
    rest = jnp.logical_and(in_grp, lane != i1)
    m2 = jnp.max(jnp.where(rest, logits, neg), axis=-1, keepdims=True)
    i2 = _first_index(jnp.logical_and(rest, logits == m2), lane)
    e2 = jnp.exp(m2 - m1)
    w1 = p_group / (1.0 + e2)
    w2 = p_group * e2 / (1.0 + e2)
    l1 = i1 - lo
    l2 = i2 - lo
    e_lo = jnp.minimum(l1, l2)
    e_hi = jnp.maximum(l1, l2)
    pair = jnp.right_shift(e_lo * (2 * EXPERTS_PER_GROUP - 1 - e_lo), 1) + (e_hi - e_lo - 1)
    cls = g_sel * PAIRS_PER_GROUP + pair
    first_is_lo = l1 < l2
    w_lo = jnp.where(first_is_lo, w1, w2)
    w_hi = jnp.where(first_is_lo, w2, w1)
    oh = lane == cls
    r_i = lax.broadcasted_iota(jnp.int32, (tm, tm), 0)
    c_i = lax.broadcasted_iota(jnp.int32, (tm, tm), 1)
    before = jnp.dot((r_i > c_i).astype(BF16), oh.astype(BF16), preferred_element_type=F32) + run_ref[...]
    rank = jnp.sum(jnp.where(oh, before, 0.0), axis=-1, keepdims=True)
    run_ref[...] = run_ref[...] + jnp.sum(oh.astype(F32), axis=0, keepdims=True)
    cnt_ref[...] = run_ref[...]
    info = jnp.zeros((tm, LANES), F32)
    for idx, val in ((INFO_CLASS, cls.astype(F32)), (INFO_RANK, rank), (INFO_W_LO, w_lo), (INFO_W_HI, w_hi)):
        info = jnp.where(lane == idx, val, info)
    info_ref[...] = info


def _router(x1, mod_rows, wr, br, nb, ctx_rows):
    rows, d = x1.shape
    tm = ROW_TILE
    lat = lambda i: jnp.where(i * tm >= ctx_rows, 1, 0)
    return pl.pallas_call(
        functools.partial(_router_kernel, nb),
        grid=(rows // tm,),
        in_specs=[pl.BlockSpec((tm, d), lambda i: (i, 0)),
                  pl.BlockSpec((None, nb, d), lambda i: (lat(i), 0, 3)),
                  pl.BlockSpec((None, nb, d), lambda i: (lat(i), 0, 4)),
                  pl.BlockSpec((d, 2 * LANES), lambda i: (0, 0)),
                  pl.BlockSpec((1, LANES), lambda i: (0, 0))],
        out_specs=[pl.BlockSpec((tm * ROW_SUB, LANES), lambda i: (i, 0)),
                   pl.BlockSpec((tm, LANES), lambda i: (i, 0)),
                   pl.BlockSpec((1, LANES), lambda i: (0, 0))],
        out_shape=[jax.ShapeDtypeStruct((rows * ROW_SUB, LANES), F32),
                   jax.ShapeDtypeStruct((rows, LANES), F32),
                   jax.ShapeDtypeStruct((1, LANES), F32)],
        scratch_shapes=[pltpu.VMEM((1, LANES), F32)],
        compiler_params=_cparams(("arbitrary",)),
        name="moe_router",
    )(x1, mod_rows, mod_rows, wr, br)


def _dispatch_kernel(tmx, ends_ref, cnt_ref, dst_ref, hm_ref, xs_hbm, zero_scr, sem, zsem):
    rs = ROW_SUB
    tb = dst_ref.shape[1]

    @pl.when(pl.program_id(0) == 0)
    def _():
        zero_scr[...] = jnp.zeros_like(zero_scr)

        def tail_copy(c):
            start = pl.multiple_of((ends_ref[c] - tmx) * rs, tmx * rs)
            return pltpu.make_async_copy(zero_scr, xs_hbm.at[pl.ds(start, tmx * rs)], zsem)

        def start_one(c, carry):
            @pl.when(cnt_ref[c] > 0)
            def _():
                tail_copy(c).start()
            return carry

        def wait_one(c, carry):
            @pl.when(cnt_ref[c] > 0)
            def _():
                tail_copy(c).wait()
            return carry

        lax.fori_loop(0, N_CLASSES, start_one, 0)

        def spare_copy(j):
            return pltpu.make_async_copy(zero_scr, xs_hbm.at[pl.ds(pl.multiple_of(j * (tmx * rs), tmx * rs), tmx * rs)],
                                         zsem)

        n_used = ends_ref[N_CLASSES - 1] // tmx
        n_tiles = xs_hbm.shape[0] // (tmx * rs)
        lax.fori_loop(n_used, n_tiles, lambda j, carry: (spare_copy(j).start(), carry)[1], 0)
        lax.fori_loop(0, N_CLASSES, wait_one, 0)
        lax.fori_loop(n_used, n_tiles, lambda j, carry: (spare_copy(j).wait(), carry)[1], 0)

    for r in range(tb):
        dst = xs_hbm.at[pl.ds(pl.multiple_of(dst_ref[0, r], rs), rs)]
        pltpu.make_async_copy(hm_ref.at[pl.ds(r * rs, rs)], dst, sem).start(priority=r % 2)
    pltpu.make_async_copy(hm_ref, hm_ref, sem).wait()


def _dispatch(hm, dest, class_ends, class_cnt, sorted_rows, tmx):
    tb = ROW_TILE
    n_blocks = dest.shape[0] // tb
    grid_spec = pltpu.PrefetchScalarGridSpec(
        num_scalar_prefetch=2,
        grid=(n_blocks,),
        in_specs=[pl.BlockSpec((None, 1, tb), lambda i, e, c: (i, 0, 0), memory_space=pltpu.SMEM),
                  pl.BlockSpec((tb * ROW_SUB, LANES), lambda i, e, c: (i, 0))],
        out_specs=pl.BlockSpec(memory_space=pl.ANY),
        scratch_shapes=[pltpu.VMEM((tmx * ROW_SUB, LANES), F32), pltpu.SemaphoreType.DMA(()),
                        pltpu.SemaphoreType.DMA(())],
    )
    return pl.pallas_call(
        functools.partial(_dispatch_kernel, tmx),
        grid_spec=grid_spec,
        out_shape=jax.ShapeDtypeStruct((sorted_rows * ROW_SUB, LANES), F32),
        compiler_params=_cparams(("arbitrary",)),
        name="moe_dispatch",
    )(class_ends, class_cnt, (dest * ROW_SUB).reshape(n_blocks, 1, tb), hm)


def _class_ffn_kernel(telo_ref, tehi_ref, nact_ref, xs_ref, wgul_ref, wdl_ref, wguh_ref, wdh_ref, ys_ref, y_scr):
    j = pl.program_id(0)
    tmx = xs_ref.shape[0] // ROW_SUB
    de = wdl_ref.shape[0]

    @pl.when(j < nact_ref[0])
    def _():
        x = _load_row_tiles(xs_ref, tmx).astype(BF16)
        for half, (wgu_ref, wd_ref) in enumerate(((wgul_ref, wdl_ref), (wguh_ref, wdh_ref))):
            ab = jnp.dot(x, wgu_ref[...], preferred_element_type=F32)
            y = jnp.dot((_silu(ab[:, :de]) * ab[:, de:]).astype(BF16), wd_ref[...], preferred_element_type=F32)
            for c in range(ROW_SUB):
                y_scr[pl.ds(half * ROW_SUB + c, tmx, stride=Y_SUB), :] = y[:, c * LANES:(c + 1) * LANES]
        ys_ref[...] = y_scr[...].astype(BF16)

    @pl.when(j >= nact_ref[0])
    def _():
        ys_ref[...] = jnp.zeros_like(ys_ref)


def _class_ffn(xs, tile_e_lo, tile_e_hi, n_active, w_gate_up, w_down, layer, tmx):
    n_tiles = xs.shape[0] // (tmx * ROW_SUB)
    d, de2 = w_gate_up.shape[-2:]
    w_in = lambda te_idx: pl.BlockSpec((None, None, d, de2), lambda j, *te: (layer, te[te_idx][j], 0, 0))
    w_out = lambda te_idx: pl.BlockSpec((None, None, de2 // 2, d), lambda j, *te: (layer, te[te_idx][j], 0, 0))
    grid_spec = pltpu.PrefetchScalarGridSpec(
        num_scalar_prefetch=3,
        grid=(n_tiles,),
        in_specs=[pl.BlockSpec((tmx * ROW_SUB, LANES), lambda j, lo, hi, na: (jnp.minimum(j, na[0] - 1), 0)),
                  w_in(0), w_out(0), w_in(1), w_out(1)],
        out_specs=pl.BlockSpec((tmx * Y_SUB, LANES), lambda j, lo, hi, na: (j, 0)),
        scratch_shapes=[pltpu.VMEM((tmx * Y_SUB, LANES), F32)],
    )
    return pl.pallas_call(
        _class_ffn_kernel,
        grid_spec=grid_spec,
        out_shape=jax.ShapeDtypeStruct((n_tiles * tmx * Y_SUB, LANES), BF16),
        compiler_params=_cparams(("arbitrary",)),
        name="expert_ffn",
    )(tile_e_lo, tile_e_hi, n_active, xs, w_gate_up, w_down, w_gate_up, w_down)


def _moe_out_kernel(nb, batch_major, src0_ref, srcn_ref, ys_hbm, info_ref, x_ref, gt_ref, g_ref, b_ref, o_ref,
                    ybuf, y_scr, sem):
    i = pl.program_id(0)
    tm = info_ref.shape[0]
    slot = i % 2
    nslot = 1 - slot

    def gather_all(src_ref, s):
        for r in range(tm):
            pltpu.make_async_copy(ys_hbm.at[pl.ds(pl.multiple_of(src_ref[0, r], Y_SUB), Y_SUB)],
                                  ybuf.at[s, pl.ds(r * Y_SUB, Y_SUB)], sem.at[s]).start(priority=r % 2)

    def wait_rows(s):
        pltpu.make_async_copy(ybuf.at[s], ybuf.at[s], sem.at[s]).wait()

    @pl.when(i == 0)
    def _():
        gather_all(src0_ref, 0)

    wait_rows(slot)
    gather_all(srcn_ref, nslot)
    info = info_ref[...]
    y_scr[...] = ybuf[slot].astype(F32)
    y_lo = jnp.concatenate([y_scr[pl.ds(c, tm, stride=Y_SUB), :] for c in range(ROW_SUB)], axis=1)
    y_hi = jnp.concatenate([y_scr[pl.ds(ROW_SUB + c, tm, stride=Y_SUB), :] for c in range(ROW_SUB)], axis=1)
    f = y_lo * info[:, INFO_W_LO:INFO_W_LO + 1] + y_hi * info[:, INFO_W_HI:INFO_W_HI + 1]
    r = DEEP_ALPHA * x_ref[...] + _per_batch(f, gt_ref[...], nb) * f
    res = _layer_norm(r) * g_ref[...] + b_ref[...]
    if batch_major:
        for c in range(ROW_SUB):
            y_scr[pl.ds(c * tm, tm), :] = res[:, c * LANES:(c + 1) * LANES]
        for b in range(nb):
            for c in range(ROW_SUB):
                o_ref[b, :, c * LANES:(c + 1) * LANES] = y_scr[pl.ds(c * tm + b, tm // nb, stride=nb), :]
    else:
        o_ref[...] = res

    @pl.when(i == pl.num_programs(0) - 1)
    def _():
        wait_rows(nslot)


def _moe_out(ys, dest, info, x1, mod_rows, ln_g, ln_b, nb, ctx_rows, batch_major):
    rows, d = x1.shape
    tm = ROW_TILE
    n_blocks = rows // tm
    if batch_major:
        out_spec = pl.BlockSpec((nb, tm // nb, d), lambda i: (0, i, 0))
        out_shape = jax.ShapeDtypeStruct((nb, rows // nb, d), F32)
    else:
        out_spec = pl.BlockSpec((tm, d), lambda i: (i, 0))
        out_shape = jax.ShapeDtypeStruct((rows, d), F32)
    lat = lambda i: jnp.where(i * tm >= ctx_rows, 1, 0)
    vec = pl.BlockSpec((1, d), lambda i: (0, 0))
    src = (dest * Y_SUB).reshape(n_blocks, 1, tm)
    smem = lambda f: pl.BlockSpec((None, 1, tm), f, memory_space=pltpu.SMEM)
    return pl.pallas_call(
        functools.partial(_moe_out_kernel, nb, batch_major),
        grid=(n_blocks,),
        in_specs=[smem(lambda i: (0, 0, 0)),
                  smem(lambda i: (jnp.minimum(i + 1, n_blocks - 1), 0, 0)),
                  pl.BlockSpec(memory_space=pl.ANY),
                  pl.BlockSpec((tm, LANES), lambda i: (i, 0)),
                  pl.BlockSpec((tm, d), lambda i: (i, 0)),
                  pl.BlockSpec((None, nb, d), lambda i: (lat(i), 0, 5)),
                  vec, vec],
        out_specs=out_spec,
        out_shape=out_shape,
        scratch_shapes=[pltpu.VMEM((2, tm * Y_SUB, LANES), BF16), pltpu.VMEM((tm * Y_SUB, LANES), F32),
                        pltpu.SemaphoreType.DMA((2,))],
        compiler_params=_cparams(("arbitrary",)),
        name="moe_out",
    )(src, src, ys, info, x1, mod_rows, ln_g, ln_b)


def _class_experts():
    lo, hi = [], []
    for g in range(N_GROUPS):
        for a in range(EXPERTS_PER_GROUP):
            for b in range(a + 1, EXPERTS_PER_GROUP):
                lo.append(g * EXPERTS_PER_GROUP + a)
                hi.append(g * EXPERTS_PER_GROUP + b)
    return np.asarray(lo, np.int32), np.asarray(hi, np.int32)


def _routing_tables(info, counts, n_tiles, tmx):
    cnt = counts[0, :N_CLASSES].astype(jnp.int32)
    padded = ((cnt + tmx - 1) // tmx) * tmx
    ends = jnp.cumsum(padded)
    starts = ends - padded
    cls = info[:, INFO_CLASS].astype(jnp.int32)
    class_ids = jnp.arange(N_CLASSES, dtype=jnp.int32)
    start_of_tok = jnp.sum(jnp.where(cls[:, None] == class_ids[None, :], starts[None, :], 0), axis=1)
    dest = start_of_tok + info[:, INFO_RANK].astype(jnp.int32)
    tile_start = jnp.arange(n_tiles, dtype=jnp.int32) * tmx
    n_before = jnp.sum((ends[None, :] <= tile_start[:, None]).astype(jnp.int32), axis=1)
    tile_cls = jnp.minimum(n_before, N_CLASSES - 1)
    e_lo, e_hi = _class_experts()
    n_active = (ends[-1:] // tmx).astype(jnp.int32)
    return dest, ends, cnt, jnp.asarray(e_lo)[tile_cls], jnp.asarray(e_hi)[tile_cls], n_active


def _relayout_w_in(w_in):
    qkvz = w_in[..., :4 * DN_WIDTH]
    gates = w_in[..., 4 * DN_WIDTH:4 * DN_WIDTH + 4 * DN_HEADS]
    rxy = w_in[..., 4 * DN_WIDTH + 4 * DN_HEADS:]
    pad = jnp.zeros(w_in.shape[:-1] + (LANES - 4 * DN_HEADS,), w_in.dtype)
    return jnp.concatenate([qkvz, rxy, gates, pad], axis=-1).astype(BF16)


def _block_diag(w):
    n, c, _ = w.shape
    eye = jnp.eye(n, dtype=w.dtype)
    return (eye[:, None, :, None] * w[:, :, None, :]).reshape(n * c, n * c)


def _lane_vec(vals, offset):
    return jnp.zeros((1, LANES), F32).at[0, offset:offset + vals.size].set(vals.reshape(-1))


def kernel(x, c, ctx, c_ctx, w_ada, b_ada, w_in, conv_qkv_w, dn_a_log, dn_dt_bias, dn_onorm_w, rg_conv_w, rg_conv_b, rg_wa, rg_ba, rg_wi, rg_bi, rg_lambda, w_out, ln_g, ln_b, router_wg, router_bg, router_we, router_be, w_e_gate, w_e_up, w_e_down):
    nb, t_lat, d = x.shape
    t_ctx = ctx.shape[1]
    assert d == D_MODEL and nb % 8 == 0 and t_lat % GRID_W == 0 and t_ctx % CHUNK == 0
    assert (t_ctx * nb) % ROW_TILE == 0 and (t_lat * nb) % ROW_TILE == 0
    tt = t_ctx + t_lat
    rows = tt * nb
    ctx_rows = t_ctx * nb
    n_ctx_chunks = t_ctx // CHUNK

    h = jnp.concatenate([jnp.swapaxes(ctx, 0, 1), jnp.swapaxes(x, 0, 1)], axis=0).reshape(rows, d)

    n_cc = ((nb + 1 + 7) // 8) * 8
    cc = jnp.zeros((n_cc, d), F32).at[:nb].set(c).at[nb].set(c_ctx)
    mod = _modulation(cc, w_ada, b_ada)
    mod_ctx = jnp.broadcast_to(mod[:, nb:nb + 1], (DEPTH, nb, 6 * d))
    mod_rows = jnp.stack([mod_ctx, mod[:, :nb]], axis=1)

    w_in2 = _relayout_w_in(w_in)
    w_out_bf = w_out.astype(BF16)
    w_gate_up = jnp.concatenate([w_e_gate, w_e_up], axis=-1).astype(BF16)
    w_down = w_e_down.astype(BF16)
    for l in range(DEPTH):
        row0 = ctx_rows if l == DEPTH - 1 else 0
        moe_ctx_rows = ctx_rows - row0
        n_tiles = (rows - row0) // MOE_TILE + N_CLASSES
        mr = mod_rows[l]
        proj = _in_projection(h, mr, w_in2, l, nb, ctx_rows)
        qkv = _qkv_prep(proj, conv_qkv_w[l], nb, n_ctx_chunks)
        wg = jnp.concatenate([_block_diag(rg_wa[l, 0]), _block_diag(rg_wi[l, 0]),
                              _block_diag(rg_wa[l, 1]), _block_diag(rg_wi[l, 1])], axis=1).astype(BF16)
        bg = jnp.concatenate([rg_ba[l, 0], rg_bi[l, 0], rg_ba[l, 1], rg_bi[l, 1]])[None]
        hf, hb = _rglru(proj, rg_conv_w[l], rg_conv_b[l][None], wg, bg, rg_lambda[l], nb, n_ctx_chunks)
        alog_vec = _lane_vec(dn_a_log[l], 2 * DN_HEADS)
        dtb_vec = _lane_vec(dn_dt_bias[l], 2 * DN_HEADS)
        o_f, o_b = _deltanet(qkv, proj, alog_vec, dtb_vec, nb, n_ctx_chunks)
        x1 = _mixer_out(o_f, o_b, proj, hf, hb, dn_onorm_w[l][None], w_out_bf, l, h, mr,
                        ln_g[l, 0][None], ln_b[l, 0][None], nb, ctx_rows, row0)
        wr = jnp.zeros((d, LANES), F32).at[:, :N_GROUPS].set(router_wg[l]).at[
            :, N_GROUPS:N_GROUPS + N_EXPERTS].set(router_we[l])
        br = _lane_vec(jnp.concatenate([router_bg[l], router_be[l]]), 0)
        wr_hi = wr.astype(BF16)
        wr_hl = jnp.concatenate([wr_hi, (wr - wr_hi.astype(F32)).astype(BF16)], axis=1)
        hm, info, counts = _router(x1, mr, wr_hl, br, nb, moe_ctx_rows)
        dest, class_ends, class_cnt, tile_e_lo, tile_e_hi, n_active = _routing_tables(info, counts, n_tiles, MOE_TILE)
        xs = _dispatch(hm, dest, class_ends, class_cnt, n_tiles * MOE_TILE, MOE_TILE)
        ys = _class_ffn(xs, tile_e_lo, tile_e_hi, n_active, w_gate_up, w_down, l, MOE_TILE)
        h = _moe_out(ys, dest, info, x1, mr, ln_g[l, 1][None], ln_b[l, 1][None], nb, moe_ctx_rows,
                     batch_major=(l == DEPTH - 1))

    return h
```

```python
import functools
import math

import jax
import jax.numpy as jnp
import numpy as np
from jax import lax
from jax.experimental import pallas as pl
from jax.experimental.pallas import tpu as pltpu

F32 = jnp.float32
BF16 = jnp.bfloat16
HIGHEST = lax.Precision.HIGHEST

D_MODEL = 1024
DEPTH = 2
GRID_W = 64
DN_HEAD_DIM = 128
DN_WIDTH = 512
DN_HEADS = 4
RG_WIDTH = 512
RG_BLOCKS = 8
RG_BLOCK_DIM = 64
RG_C = 8.0
CONV_K = 4
N_GROUPS = 4
EXPERTS_PER_GROUP = 8
N_EXPERTS = 32
D_EXPERT = 256
DEEP_ALPHA = (2 * DEPTH) ** 0.25
LN_EPS = 1e-5
NORM_EPS = 1e-6

LANES = 128
CHUNK = 64
SOLVE_BLOCK = 16
DN_BATCHES = 4
D_PROJ = 3 * DN_WIDTH + DN_WIDTH + 2 * RG_WIDTH + LANES
COL_Z = 3
COL_RX = 4
COL_RY = 5
COL_GATE = (3 * DN_WIDTH + DN_WIDTH + 2 * RG_WIDTH) // LANES
ROW_TILE = 512
MOE_TILE = 256
VMEM_LIMIT = 56 * 1024 * 1024


def _cparams(sem):
    return pltpu.CompilerParams(dimension_semantics=sem, vmem_limit_bytes=VMEM_LIMIT)


def _layer_norm(x):
    mu = jnp.mean(x, axis=-1, keepdims=True)
    xc = x - mu
    var = jnp.mean(xc * xc, axis=-1, keepdims=True)
    return xc * lax.rsqrt(var + LN_EPS)


def _per_batch(x, v, nb):
    rows, c = x.shape
    return jnp.broadcast_to(v[None], (rows // nb, nb, c)).reshape(rows, c)


def _sigmoid(x):
    return 0.5 * jnp.tanh(0.5 * x) + 0.5


def _silu(x):
    h = 0.5 * x
    return h * jnp.tanh(h) + h


def _mod_kernel(cc_ref, w_ref, b_ref, o_ref):
    s = _silu(cc_ref[...])
    o_ref[...] = jnp.dot(s.astype(BF16), w_ref[...].astype(BF16), preferred_element_type=F32) + b_ref[...]


def _modulation(cc, w_ada, b_ada):
    depth, d, n6 = w_ada.shape
    rows = cc.shape[0]
    tn = 1536
    return pl.pallas_call(
        _mod_kernel,
        grid=(depth, n6 // tn),
        in_specs=[
            pl.BlockSpec((rows, d), lambda l, j: (0, 0)),
            pl.BlockSpec((None, d, tn), lambda l, j: (l, 0, j)),
            pl.BlockSpec((None, 1, tn), lambda l, j: (l, 0, j)),
        ],
        out_specs=pl.BlockSpec((None, rows, tn), lambda l, j: (l, 0, j)),
        out_shape=jax.ShapeDtypeStruct((depth, rows, n6), F32),
        compiler_params=_cparams(("parallel", "parallel")),
        name="adaln_mod",
    )(cc, w_ada, b_ada.reshape(depth, 1, n6))


def _inproj_kernel(nb, h_ref, sh_ref, sc_ref, w_ref, o_ref):
    xn = _layer_norm(h_ref[...])
    hm = xn * (1.0 + _per_batch(xn, sc_ref[...], nb)) + _per_batch(xn, sh_ref[...], nb)
    o_ref[...] = jnp.dot(hm.astype(BF16), w_ref[...], preferred_element_type=F32)


def _in_projection(h, mod_rows, w_in2, layer, nb, ctx_rows):
    rows, d = h.shape
    tm = ROW_TILE
    lat = lambda i: jnp.where(i * tm >= ctx_rows, 1, 0)
    return pl.pallas_call(
        functools.partial(_inproj_kernel, nb),
        grid=(rows // tm,),
        in_specs=[
            pl.BlockSpec((tm, d), lambda i: (i, 0)),
            pl.BlockSpec((None, nb, d), lambda i: (lat(i), 0, 0)),
            pl.BlockSpec((None, nb, d), lambda i: (lat(i), 0, 1)),
            pl.BlockSpec((None, d, D_PROJ), lambda i: (layer, 0, 0)),
        ],
        out_specs=pl.BlockSpec((tm, D_PROJ), lambda i: (i, 0)),
        out_shape=jax.ShapeDtypeStruct((rows, D_PROJ), F32),
        compiler_params=_cparams(("parallel",)),
        name="ln_mod_inproj",
    )(h, mod_rows, mod_rows, w_in2)


def _conv_tile(x, prev, nxt, w, nb, use_prev, use_next):
    prev = jnp.where(use_prev, prev, 0.0)
    nxt = jnp.where(use_next, nxt, 0.0)
    xm2 = jnp.concatenate([prev, x[: -2 * nb]], axis=0)
    xm1 = jnp.concatenate([prev[nb:], x[:-nb]], axis=0)
    xp1 = jnp.concatenate([x[nb:], nxt], axis=0)
    return xm2 * w[0:1] + xm1 * w[1:2] + x * w[2:3] + xp1 * w[3:4]


def _halo_flags(i, n_ctx_tiles):
    use_prev = jnp.logical_and(i > 0, i < n_ctx_tiles)
    use_next = i < n_ctx_tiles - 1
    return use_prev, use_next


def _qkv_prep_kernel(nb, n_ctx_tiles, x_ref, prev_ref, next_ref, w_ref, o_ref):
    i = pl.program_id(0)
    part = pl.program_id(1)
    use_prev, use_next = _halo_flags(i, n_ctx_tiles)
    y = _silu(_conv_tile(x_ref[...], prev_ref[...], next_ref[...], w_ref[...], nb, use_prev, use_next))
    heads = [y[:, hh * DN_HEAD_DIM:(hh + 1) * DN_HEAD_DIM] for hh in range(DN_HEADS)]

    @pl.when(part < 2)
    def _():
        qscale = jnp.where(part == 0, DN_HEAD_DIM ** -0.5, 1.0).astype(F32)
        for hh, yh in enumerate(heads):
            o_ref[hh] = yh * (lax.rsqrt(jnp.sum(yh * yh, axis=-1, keepdims=True) + NORM_EPS) * qscale)

    @pl.when(part == 2)
    def _():
        for hh, yh in enumerate(heads):
            o_ref[hh] = yh


def _halo_specs(tile_rows, nb, cols, col_of, n_row_blocks16, tile_of=lambda i: i):
    prev_b = tile_rows // (2 * nb)
    next_b = tile_rows // nb
    return [
        pl.BlockSpec((tile_rows, cols), lambda i, *a: (tile_of(i), col_of(*a))),
        pl.BlockSpec((2 * nb, cols), lambda i, *a: (jnp.maximum(tile_of(i) * prev_b - 1, 0), col_of(*a))),
        pl.BlockSpec((nb, cols),
                     lambda i, *a: (jnp.minimum((tile_of(i) + 1) * next_b, n_row_blocks16 - 1), col_of(*a))),
    ]


def _qkv_prep(proj, conv_w, nb, n_ctx_tiles):
    rows = proj.shape[0]
    tr = CHUNK * nb
    return pl.pallas_call(
        functools.partial(_qkv_prep_kernel, nb, n_ctx_tiles),
        grid=(rows // tr, 3),
        in_specs=_halo_specs(tr, nb, DN_WIDTH, lambda p: p, rows // nb)
        + [pl.BlockSpec((CONV_K, DN_WIDTH), lambda i, p: (0, p))],
        out_specs=pl.BlockSpec((DN_HEADS, tr, DN_HEAD_DIM), lambda i, p: (p, i, 0)),
        out_shape=jax.ShapeDtypeStruct((3 * DN_HEADS, rows, DN_HEAD_DIM), F32),
        compiler_params=_cparams(("parallel", "parallel")),
        name="qkv_conv_norm",
    )(proj, proj, proj, conv_w)


def _bwd_chunk(i, n_ctx, n_tot):
    return jnp.where(i < n_ctx, n_ctx - 1 - i, n_tot + n_ctx - 1 - i)


def _rglru_kernel(nb, n_ctx_tiles, n_tot, xf_ref, pf_ref, nf_ref, xb_ref, pb_ref, nb_ref, cw_ref, cb_ref,
                  wg0_ref, wg1_ref, bg0_ref, bg1_ref, lam_ref, hf_ref, hb_ref, a_scr, b_scr, st_ref):
    i = pl.program_id(0)

    @pl.when(i == 0)
    def _():
        st_ref[...] = jnp.zeros_like(st_ref)

    w = RG_WIDTH
    sp = jax.nn.softplus(-lam_ref[...])
    tiles = (i, _bwd_chunk(i, n_ctx_tiles, n_tot))
    for d, (x_ref, p_ref, n_ref, wg_ref, bg_ref) in enumerate(
            ((xf_ref, pf_ref, nf_ref, wg0_ref, bg0_ref), (xb_ref, pb_ref, nb_ref, wg1_ref, bg1_ref))):
        use_prev, use_next = _halo_flags(tiles[d], n_ctx_tiles)
        xc = _conv_tile(x_ref[...], p_ref[...], n_ref[...], cw_ref[...], nb, use_prev, use_next) + cb_ref[...]
        gates = jnp.dot(xc.astype(BF16), wg_ref[...], preferred_element_type=F32) + bg_ref[...]
        ig = _sigmoid(gates[:, w:])
        c = (-0.5 * RG_C) * sp[d:d + 1]
        a = jnp.exp(c * jnp.tanh(0.5 * gates[:, :w]) + c)
        y = jnp.maximum(1.0 - a * a, 0.0)
        a_scr[d] = a
        b_scr[d] = jnp.where(y > 0.0, y * lax.rsqrt(y), 0.0) * (ig * xc)

    def body(t, carry):
        h0, h1 = carry
        r0 = pl.ds(pl.multiple_of(t * nb, nb), nb)
        r1 = pl.ds(pl.multiple_of((CHUNK - 1 - t) * nb, nb), nb)
        h0 = a_scr[0, r0, :] * h0 + b_scr[0, r0, :]
        h1 = a_scr[1, r1, :] * h1 + b_scr[1, r1, :]
        hf_ref[r0, :] = h0.astype(hf_ref.dtype)
        hb_ref[r1, :] = h1.astype(hb_ref.dtype)
        return h0, h1

    h0, h1 = lax.fori_loop(0, CHUNK, body, (st_ref[0], st_ref[1]), unroll=8)
    st_ref[0] = h0
    st_ref[1] = h1


def _rglru(proj, conv_w, conv_b, wg, bg, lam, nb, n_ctx_tiles):
    rows = proj.shape[0]
    tr = CHUNK * nb
    n_tot = rows // tr
    w = RG_WIDTH
    out = jax.ShapeDtypeStruct((rows, w), BF16)
    fwd = lambda i: i
    bwd = lambda i: _bwd_chunk(i, n_ctx_tiles, n_tot)
    full = lambda shape: pl.BlockSpec(shape, lambda i: (0,) * len(shape))
    return pl.pallas_call(
        functools.partial(_rglru_kernel, nb, n_ctx_tiles, n_tot),
        grid=(n_tot,),
        in_specs=_halo_specs(tr, nb, w, lambda: COL_RX, rows // nb, fwd)
        + _halo_specs(tr, nb, w, lambda: COL_RX, rows // nb, bwd)
        + [full((CONV_K, w)), full((1, w)),
           pl.BlockSpec((w, 2 * w), lambda i: (0, 0)), pl.BlockSpec((w, 2 * w), lambda i: (0, 1)),
           pl.BlockSpec((1, 2 * w), lambda i: (0, 0)), pl.BlockSpec((1, 2 * w), lambda i: (0, 1)),
           full((2, w))],
        out_specs=[pl.BlockSpec((tr, w), lambda i: (fwd(i), 0)), pl.BlockSpec((tr, w), lambda i: (bwd(i), 0))],
        out_shape=[out] * 2,
        scratch_shapes=[pltpu.VMEM((2, tr, w), F32), pltpu.VMEM((2, tr, w), F32), pltpu.VMEM((2, nb, w), F32)],
        compiler_params=_cparams(("arbitrary",)),
        name="rglru",
    )(proj, proj, proj, proj, proj, proj, conv_w, conv_b, wg, wg, bg, bg, lam)


def _bdot(a, b):
    return jnp.dot(a.astype(BF16), b.astype(BF16), preferred_element_type=F32)


def _deltanet_gates(d, raw, alog, dtb):
    c = CHUNK
    row = lax.broadcasted_iota(jnp.int32, (c, c), 0)
    col = lax.broadcasted_iota(jnp.int32, (c, c), 1)
    incl = (row >= col) if d == 0 else (row <= col)
    g_all = -jnp.exp(alog) * jax.nn.softplus(raw + dtb)
    gc_all = jnp.dot(incl.astype(F32), g_all, precision=HIGHEST, preferred_element_type=F32)
    g_tot = jnp.sum(g_all, axis=0, keepdims=True)
    return dict(incl=incl, strict=(row > col) if d == 0 else (row < col),
                diag_blk=(row // SOLVE_BLOCK) == (col // SOLVE_BLOCK),
                beta=_sigmoid(raw), gc=gc_all, gc_t=gc_all.T, egc=jnp.exp(gc_all),
                kdec=jnp.exp(g_tot - gc_all), glast=jnp.exp(g_tot))


def _deltanet_heads(gts, cbs, qs, ks, vs, ss):
    c = CHUNK
    dh = DN_HEAD_DIM
    n = len(cbs)
    rng = range(n)
    cgs = [2 * DN_HEADS + cb for cb in cbs]
    beta = [gts[i]["beta"][:, cbs[i]:cbs[i] + 1] for i in rng]
    egc = [gts[i]["egc"][:, cgs[i]:cgs[i] + 1] for i in rng]
    decay = [jnp.exp(jnp.where(gts[i]["incl"], gts[i]["gc"][:, cgs[i]:cgs[i] + 1] - gts[i]["gc_t"][cgs[i]:cgs[i] + 1, :],
                               -1e30)) for i in rng]
    kbeta = [ks[i] * beta[i] for i in rng]
    prod = [lax.dot_general(jnp.concatenate([kbeta[i], qs[i]], axis=0).astype(BF16), ks[i].astype(BF16),
                            (((1,), (1,)), ((), ())), preferred_element_type=F32) for i in rng]
    a_mat = [jnp.where(gts[i]["strict"], prod[i][:c] * decay[i], 0.0) for i in rng]
    intra = [prod[i][c:] * decay[i] for i in rng]
    ks_qs = [_bdot(jnp.concatenate([kbeta[i] * egc[i], qs[i] * egc[i]], axis=0), ss[i]) for i in rng]
    p = [jnp.where(gts[i]["diag_blk"], -a_mat[i], 0.0) for i in rng]
    y = [jnp.concatenate([vs[i] * beta[i] - ks_qs[i][:c], jnp.where(gts[i]["diag_blk"], 0.0, a_mat[i])], axis=1)
         for i in rng]
    for j in range(4):
        y = [y[i] + _bdot(p[i], y[i]) for i in rng]
        if j < 3:
            p = [_bdot(p[i], p[i]) for i in rng]
    e = [y[i][:, dh:] for i in rng]
    y = [y[i][:, :dh] for i in rng]
    e2 = [_bdot(e[i], e[i]) for i in rng]
    y = [y[i] + _bdot(e2[i], y[i]) for i in rng]
    v_new = [y[i] - _bdot(e[i], y[i]) for i in rng]
    o = [ks_qs[i][c:] + _bdot(intra[i], v_new[i]) for i in rng]
    s_new = [ss[i] * gts[i]["glast"][:, cgs[i]:cgs[i] + 1] + lax.dot_general(
        (ks[i] * gts[i]["kdec"][:, cgs[i]:cgs[i] + 1]).astype(BF16), v_new[i].astype(BF16),
        (((0,), (0,)), ((), ())), preferred_element_type=F32) for i in rng]
    return o, s_new


def _deltanet_kernel(nb, qf_ref, kf_ref, vf_ref, gf_ref, qb_ref, kb_ref, vb_ref, gb_ref, alog_ref, dtb_ref,
                     of_ref, ob_ref, s_ref):
    @pl.when(pl.program_id(0) == 0)
    def _():
        s_ref[...] = jnp.zeros_like(s_ref)

    dirs = ((qf_ref, kf_ref, vf_ref, gf_ref, of_ref), (qb_ref, kb_ref, vb_ref, gb_ref, ob_ref))

    def body(it, carry):
        idx = [(it * DN_BATCHES + j, d, hh) for j in range(DN_BATCHES) for d in range(2) for hh in range(DN_HEADS)]
        rows_of = {j: pl.ds(it * DN_BATCHES + j, CHUNK, stride=nb) for j in range(DN_BATCHES)}
        rows_b = [rows_of[i // (2 * DN_HEADS)] for i in range(len(idx))]
        cbs = [d * DN_HEADS + hh for _, d, hh in idx]
        gates = {(j, d): _deltanet_gates(d, dirs[d][3][rows_of[j], :], alog_ref[...], dtb_ref[...])
                 for j in range(DN_BATCHES) for d in range(2)}
        gts = [gates[(i // (2 * DN_HEADS), d)] for i, (_, d, _) in enumerate(idx)]
        qs = [dirs[d][0][hh, rows_b[i], :] for i, (_, d, hh) in enumerate(idx)]
        ks = [dirs[d][1][hh, rows_b[i], :] for i, (_, d, hh) in enumerate(idx)]
        vs = [dirs[d][2][hh, rows_b[i], :] for i, (_, d, hh) in enumerate(idx)]
        ss = [s_ref[b * (2 * DN_HEADS) + cbs[i]] for i, (b, _, _) in enumerate(idx)]
        o, s_new = _deltanet_heads(gts, cbs, qs, ks, vs, ss)
        for i, (b, d, hh) in enumerate(idx):
            dirs[d][4][hh, rows_b[i], :] = o[i]
            s_ref[b * (2 * DN_HEADS) + cbs[i]] = s_new[i]
        return carry

    lax.fori_loop(0, nb // DN_BATCHES, body, 0)


def _deltanet(qkv, proj, alog_vec, dtb_vec, nb, n_ctx_chunks):
    rows = qkv.shape[1]
    tr = CHUNK * nb
    n_tot = rows // tr
    fwd = lambda i: i
    bwd = lambda i: _bwd_chunk(i, n_ctx_chunks, n_tot)

    def specs(order):
        return [pl.BlockSpec((DN_HEADS, tr, DN_HEAD_DIM), lambda i, p=p: (p, order(i), 0)) for p in range(3)] + [
            pl.BlockSpec((tr, LANES), lambda i: (order(i), COL_GATE))]

    vec = pl.BlockSpec((1, LANES), lambda i: (0, 0))
    out = jax.ShapeDtypeStruct((DN_HEADS, rows, DN_HEAD_DIM), F32)
    return pl.pallas_call(
        functools.partial(_deltanet_kernel, nb),
        grid=(n_tot,),
        in_specs=specs(fwd) + specs(bwd) + [vec, vec],
        out_specs=[pl.BlockSpec((DN_HEADS, tr, DN_HEAD_DIM), lambda i: (0, fwd(i), 0)),
                   pl.BlockSpec((DN_HEADS, tr, DN_HEAD_DIM), lambda i: (0, bwd(i), 0))],
        out_shape=[out, out],
        scratch_shapes=[pltpu.VMEM((nb * 2 * DN_HEADS, DN_HEAD_DIM, DN_HEAD_DIM), F32)],
        compiler_params=_cparams(("arbitrary",)),
        name="deltanet",
    )(qkv, qkv, qkv, proj, qkv, qkv, qkv, proj, alog_vec, dtb_vec)


def _mixer_out_kernel(nb, of_ref, ob_ref, z_ref, hf_ref, hb_ref, ry_ref, onw_ref, w_ref, h_ref, gt_ref, g_ref, b_ref,
                      o_ref):
    z = z_ref[...]
    parts = []
    for hh in range(DN_HEADS):
        sl = slice(hh * DN_HEAD_DIM, (hh + 1) * DN_HEAD_DIM)
        oh = of_ref[hh] + ob_ref[hh]
        inv = lax.rsqrt(jnp.mean(oh * oh, axis=-1, keepdims=True) + NORM_EPS)
        parts.append(oh * inv * onw_ref[...] * _silu(z[:, sl]))
    parts.append((hf_ref[...].astype(F32) + hb_ref[...].astype(F32)) * jax.nn.gelu(ry_ref[...]))
    y = jnp.concatenate(parts, axis=1).astype(BF16)
    u = jnp.dot(y, w_ref[...], preferred_element_type=F32)
    r = DEEP_ALPHA * h_ref[...] + _per_batch(u, gt_ref[...], nb) * u
    o_ref[...] = _layer_norm(r) * g_ref[...] + b_ref[...]


def _mixer_out(o_f, o_b, proj, hf, hb, onorm_w, w_out, layer, h, mod_rows, ln_g, ln_b, nb, ctx_rows, row0):
    rows, d = h.shape
    tm = ROW_TILE
    t0 = row0 // tm
    lat = lambda i: jnp.where((i + t0) * tm >= ctx_rows, 1, 0)
    half = pl.BlockSpec((tm, DN_WIDTH), lambda i: (i + t0, 0))
    planes = pl.BlockSpec((DN_HEADS, tm, DN_HEAD_DIM), lambda i: (0, i + t0, 0))
    vec = lambda n: pl.BlockSpec((1, n), lambda i: (0, 0))
    return pl.pallas_call(
        functools.partial(_mixer_out_kernel, nb),
        grid=((rows - row0) // tm,),
        in_specs=[planes, planes, pl.BlockSpec((tm, DN_WIDTH), lambda i: (i + t0, COL_Z)), half, half,
                  pl.BlockSpec((tm, RG_WIDTH), lambda i: (i + t0, COL_RY)),
                  vec(DN_HEAD_DIM), pl.BlockSpec((None, d, d), lambda i: (layer, 0, 0)),
                  pl.BlockSpec((tm, d), lambda i: (i + t0, 0)),
                  pl.BlockSpec((None, nb, d), lambda i: (lat(i), 0, 2)),
                  vec(d), vec(d)],
        out_specs=pl.BlockSpec((tm, d), lambda i: (i, 0)),
        out_shape=jax.ShapeDtypeStruct((rows - row0, d), F32),
        compiler_params=_cparams(("parallel",)),
        name="mixer_out",
    )(o_f, o_b, proj, hf, hb, proj, onorm_w, w_out, h, mod_rows, ln_g, ln_b)


PAIRS_PER_GROUP = EXPERTS_PER_GROUP * (EXPERTS_PER_GROUP - 1) // 2
N_CLASSES = N_GROUPS * PAIRS_PER_GROUP
assert N_CLASSES <= LANES
INFO_CLASS, INFO_RANK, INFO_W_LO, INFO_W_HI = range(4)
Y_SUB = 2 * (D_MODEL // LANES)


ROW_SUB = D_MODEL // LANES


def _store_row_tiles(ref, x):
    n = x.shape[0]
    for c in range(ROW_SUB):
        ref[pl.ds(c, n, stride=ROW_SUB), :] = x[:, c * LANES:(c + 1) * LANES]


def _load_row_tiles(ref, n):
    return jnp.concatenate([ref[pl.ds(c, n, stride=ROW_SUB), :] for c in range(ROW_SUB)], axis=1)


def _first_index(mask, lane):
    return jnp.min(jnp.where(mask, lane, LANES), axis=-1, keepdims=True)


def _router_kernel(nb, x_ref, sh_ref, sc_ref, wr_ref, br_ref, hm_ref, info_ref, cnt_ref, run_ref):
    @pl.when(pl.program_id(0) == 0)
    def _():
        run_ref[...] = jnp.zeros_like(run_ref)

    xn = _layer_norm(x_ref[...])
    hm = xn * (1.0 + _per_batch(xn, sc_ref[...], nb)) + _per_batch(xn, sh_ref[...], nb)
    _store_row_tiles(hm_ref, hm)
    tm = hm.shape[0]
    h_hi = hm.astype(BF16)
    h_lo = (hm - h_hi.astype(F32)).astype(BF16)
    hi_hl = jnp.dot(h_hi, wr_ref[...], preferred_element_type=F32)
    logits = (hi_hl[:, :LANES] + hi_hl[:, LANES:] + jnp.dot(h_lo, wr_ref[:, :LANES], preferred_element_type=F32)
              + br_ref[...])
    lane = lax.broadcasted_iota(jnp.int32, (tm, LANES), 1)
    neg = -jnp.inf
    is_g = lane < N_GROUPS
    gmax = jnp.max(jnp.where(is_g, logits, neg), axis=-1, keepdims=True)
    g_sel = _first_index(jnp.logical_and(is_g, logits == gmax), lane)
    p_group = 1.0 / jnp.sum(jnp.where(is_g, jnp.exp(logits - gmax), 0.0), axis=-1, keepdims=True)
    lo = N_GROUPS + EXPERTS_PER_GROUP * g_sel
    in_grp = jnp.logical_and(lane >= lo, lane < lo + EXPERTS_PER_GROUP)
    m1 = jnp.max(jnp.where(in_grp, logits, neg), axis=-1, keepdims=True)
    i1 = _first_index(jnp.logical_and(in_grp, logits == m1), lane)
    rest = jnp.logical_and(in_grp, lane != i1)
    m2 = jnp.max(jnp.where(rest, logits, neg), axis=-1, keepdims=True)
    i2 = _first_index(jnp.logical_and(rest, logits == m2), lane)
    e2 = jnp.exp(m2 - m1)
    w1 = p_group / (1.0 + e2)
    w2 = p_group * e2 / (1.0 + e2)
    l1 = i1 - lo
    l2 = i2 - lo
    e_lo = jnp.minimum(l1, l2)
    e_hi = jnp.maximum(l1, l2)
    pair = jnp.right_shift(e_lo * (2 * EXPERTS_PER_GROUP - 1 - e_lo), 1) + (e_hi - e_lo - 1)
    cls = g_sel * PAIRS_PER_GROUP + pair
    first_is_lo = l1 < l2
    w_lo = jnp.where(first_is_lo, w1, w2)
    w_hi = jnp.where(first_is_lo, w2, w1)
    oh = lane == cls
    r_i = lax.broadcasted_iota(jnp.int32, (tm, tm), 0)
    c_i = lax.broadcasted_iota(jnp.int32, (tm, tm), 1)
    before = jnp.dot((r_i > c_i).astype(BF16), oh.astype(BF16), preferred_element_type=F32) + run_ref[...]
    rank = jnp.sum(jnp.where(oh, before, 0.0), axis=-1, keepdims=True)
    run_ref[...] = run_ref[...] + jnp.sum(oh.astype(F32), axis=0, keepdims=True)
    cnt_ref[...] = run_ref[...]
    info = jnp.zeros((tm, LANES), F32)
    for idx, val in ((INFO_CLASS, cls.astype(F32)), (INFO_RANK, rank), (INFO_W_LO, w_lo), (INFO_W_HI, w_hi)):
        info = jnp.where(lane == idx, val, info)
    info_ref[...] = info


def _router(x1, mod_rows, wr, br, nb, ctx_rows):
    rows, d = x1.shape
    tm = ROW_TILE
    lat = lambda i: jnp.where(i * tm >= ctx_rows, 1, 0)
    return pl.pallas_call(
        functools.partial(_router_kernel, nb),
        grid=(rows // tm,),
        in_specs=[pl.BlockSpec((tm, d), lambda i: (i, 0)),
                  pl.BlockSpec((None, nb, d), lambda i: (lat(i), 0, 3)),
                  pl.BlockSpec((None, nb, d), lambda i: (lat(i), 0, 4)),
                  pl.BlockSpec((d, 2 * LANES), lambda i: (0, 0)),
                  pl.BlockSpec((1, LANES), lambda i: (0, 0))],
        out_specs=[pl.BlockSpec((tm * ROW_SUB, LANES), lambda i: (i, 0)),
                   pl.BlockSpec((tm, LANES), lambda i: (i, 0)),
                   pl.BlockSpec((1, LANES), lambda i: (0, 0))],
        out_shape=[jax.ShapeDtypeStruct((rows * ROW_SUB, LANES), F32),
                   jax.ShapeDtypeStruct((rows, LANES), F32),
                   jax.ShapeDtypeStruct((1, LANES), F32)],
        scratch_shapes=[pltpu.VMEM((1, LANES), F32)],
        compiler_params=_cparams(("arbitrary",)),
        name="moe_router",
    )(x1, mod_rows, mod_rows, wr, br)


def _dispatch_kernel(tmx, ends_ref, cnt_ref, dst_ref, hm_ref, xs_hbm, zero_scr, sem, zsem):
    rs = ROW_SUB
    tb = dst_ref.shape[1]

    @pl.when(pl.program_id(0) == 0)
    def _():
        zero_scr[...] = jnp.zeros_like(zero_scr)

        def tail_copy(c):
            start = pl.multiple_of((ends_ref[c] - tmx) * rs, tmx * rs)
            return pltpu.make_async_copy(zero_scr, xs_hbm.at[pl.ds(start, tmx * rs)], zsem)

        def start_one(c, carry):
            @pl.when(cnt_ref[c] > 0)
            def _():
                tail_copy(c).start()
            return carry

        def wait_one(c, carry):
            @pl.when(cnt_ref[c] > 0)
            def _():
                tail_copy(c).wait()
            return carry

        lax.fori_loop(0, N_CLASSES, start_one, 0)

        def spare_copy(j):
            return pltpu.make_async_copy(zero_scr, xs_hbm.at[pl.ds(pl.multiple_of(j * (tmx * rs), tmx * rs), tmx * rs)],
                                         zsem)

        n_used = ends_ref[N_CLASSES - 1] // tmx
        n_tiles = xs_hbm.shape[0] // (tmx * rs)
        lax.fori_loop(n_used, n_tiles, lambda j, carry: (spare_copy(j).start(), carry)[1], 0)
        lax.fori_loop(0, N_CLASSES, wait_one, 0)
        lax.fori_loop(n_used, n_tiles, lambda j, carry: (spare_copy(j).wait(), carry)[1], 0)

    for r in range(tb):
        dst = xs_hbm.at[pl.ds(pl.multiple_of(dst_ref[0, r], rs), rs)]
        pltpu.make_async_copy(hm_ref.at[pl.ds(r * rs, rs)], dst, sem).start(priority=r % 2)
    pltpu.make_async_copy(hm_ref, hm_ref, sem).wait()


def _dispatch(hm, dest, class_ends, class_cnt, sorted_rows, tmx):
    tb = ROW_TILE
    n_blocks = dest.shape[0] // tb
    grid_spec = pltpu.PrefetchScalarGridSpec(
        num_scalar_prefetch=2,
        grid=(n_blocks,),
        in_specs=[pl.BlockSpec((None, 1, tb), lambda i, e, c: (i, 0, 0), memory_space=pltpu.SMEM),
                  pl.BlockSpec((tb * ROW_SUB, LANES), lambda i, e, c: (i, 0))],
        out_specs=pl.BlockSpec(memory_space=pl.ANY),
        scratch_shapes=[pltpu.VMEM((tmx * ROW_SUB, LANES), F32), pltpu.SemaphoreType.DMA(()),
                        pltpu.SemaphoreType.DMA(())],
    )
    return pl.pallas_call(
        functools.partial(_dispatch_kernel, tmx),
        grid_spec=grid_spec,
        out_shape=jax.ShapeDtypeStruct((sorted_rows * ROW_SUB, LANES), F32),
        compiler_params=_cparams(("arbitrary",)),
        name="moe_dispatch",
    )(class_ends, class_cnt, (dest * ROW_SUB).reshape(n_blocks, 1, tb), hm)


def _class_ffn_kernel(telo_ref, tehi_ref, nact_ref, xs_ref, wgul_ref, wdl_ref, wguh_ref, wdh_ref, ys_ref, y_scr):
    j = pl.program_id(0)
    tmx = xs_ref.shape[0] // ROW_SUB
    de = wdl_ref.shape[0]

    @pl.when(j < nact_ref[0])
    def _():
        x = _load_row_tiles(xs_ref, tmx).astype(BF16)
        for half, (wgu_ref, wd_ref) in enumerate(((wgul_ref, wdl_ref), (wguh_ref, wdh_ref))):
            ab = jnp.dot(x, wgu_ref[...], preferred_element_type=F32)
            y = jnp.dot((_silu(ab[:, :de]) * ab[:, de:]).astype(BF16), wd_ref[...], preferred_element_type=F32)
            for c in range(ROW_SUB):
                y_scr[pl.ds(half * ROW_SUB + c, tmx, stride=Y_SUB), :] = y[:, c * LANES:(c + 1) * LANES]
        ys_ref[...] = y_scr[...].astype(BF16)

    @pl.when(j >= nact_ref[0])
    def _():
        ys_ref[...] = jnp.zeros_like(ys_ref)


def _class_ffn(xs, tile_e_lo, tile_e_hi, n_active, w_gate_up, w_down, layer, tmx):
    n_tiles = xs.shape[0] // (tmx * ROW_SUB)
    d, de2 = w_gate_up.shape[-2:]
    w_in = lambda te_idx: pl.BlockSpec((None, None, d, de2), lambda j, *te: (layer, te[te_idx][j], 0, 0))
    w_out = lambda te_idx: pl.BlockSpec((None, None, de2 // 2, d), lambda j, *te: (layer, te[te_idx][j], 0, 0))
    grid_spec = pltpu.PrefetchScalarGridSpec(
        num_scalar_prefetch=3,
        grid=(n_tiles,),
        in_specs=[pl.BlockSpec((tmx * ROW_SUB, LANES), lambda j, lo, hi, na: (jnp.minimum(j, na[0] - 1), 0)),
                  w_in(0), w_out(0), w_in(1), w_out(1)],
        out_specs=pl.BlockSpec((tmx * Y_SUB, LANES), lambda j, lo, hi, na: (j, 0)),
        scratch_shapes=[pltpu.VMEM((tmx * Y_SUB, LANES), F32)],
    )
    return pl.pallas_call(
        _class_ffn_kernel,
        grid_spec=grid_spec,
        out_shape=jax.ShapeDtypeStruct((n_tiles * tmx * Y_SUB, LANES), BF16),
        compiler_params=_cparams(("arbitrary",)),
        name="expert_ffn",
    )(tile_e_lo, tile_e_hi, n_active, xs, w_gate_up, w_down, w_gate_up, w_down)


def _moe_out_kernel(nb, batch_major, src0_ref, srcn_ref, ys_hbm, info_ref, x_ref, gt_ref, g_ref, b_ref, o_ref,
                    ybuf, y_scr, sem):
    i = pl.program_id(0)
    tm = info_ref.shape[0]
    slot = i % 2
    nslot = 1 - slot

    def gather_all(src_ref, s):
        for r in range(tm):
            pltpu.make_async_copy(ys_hbm.at[pl.ds(pl.multiple_of(src_ref[0, r], Y_SUB), Y_SUB)],
                                  ybuf.at[s, pl.ds(r * Y_SUB, Y_SUB)], sem.at[s]).start(priority=r % 2)

    def wait_rows(s):
        pltpu.make_async_copy(ybuf.at[s], ybuf.at[s], sem.at[s]).wait()

    @pl.when(i == 0)
    def _():
        gather_all(src0_ref, 0)

    wait_rows(slot)
    gather_all(srcn_ref, nslot)
    info = info_ref[...]
    y_scr[...] = ybuf[slot].astype(F32)
    y_lo = jnp.concatenate([y_scr[pl.ds(c, tm, stride=Y_SUB), :] for c in range(ROW_SUB)], axis=1)
    y_hi = jnp.concatenate([y_scr[pl.ds(ROW_SUB + c, tm, stride=Y_SUB), :] for c in range(ROW_SUB)], axis=1)
    f = y_lo * info[:, INFO_W_LO:INFO_W_LO + 1] + y_hi * info[:, INFO_W_HI:INFO_W_HI + 1]
    r = DEEP_ALPHA * x_ref[...] + _per_batch(f, gt_ref[...], nb) * f
    res = _layer_norm(r) * g_ref[...] + b_ref[...]
    if batch_major:
        for c in range(ROW_SUB):
            y_scr[pl.ds(c * tm, tm), :] = res[:, c * LANES:(c + 1) * LANES]
        for b in range(nb):
            for c in range(ROW_SUB):
                o_ref[b, :, c * LANES:(c + 1) * LANES] = y_scr[pl.ds(c * tm + b, tm // nb, stride=nb), :]
    else:
        o_ref[...] = res

    @pl.when(i == pl.num_programs(0) - 1)
    def _():
        wait_rows(nslot)


def _moe_out(ys, dest, info, x1, mod_rows, ln_g, ln_b, nb, ctx_rows, batch_major):
    rows, d = x1.shape
    tm = ROW_TILE
    n_blocks = rows // tm
    if batch_major:
        out_spec = pl.BlockSpec((nb, tm // nb, d), lambda i: (0, i, 0))
        out_shape = jax.ShapeDtypeStruct((nb, rows // nb, d), F32)
    else:
        out_spec = pl.BlockSpec((tm, d), lambda i: (i, 0))
        out_shape = jax.ShapeDtypeStruct((rows, d), F32)
    lat = lambda i: jnp.where(i * tm >= ctx_rows, 1, 0)
    vec = pl.BlockSpec((1, d), lambda i: (0, 0))
    src = (dest * Y_SUB).reshape(n_blocks, 1, tm)
    smem = lambda f: pl.BlockSpec((None, 1, tm), f, memory_space=pltpu.SMEM)
    return pl.pallas_call(
        functools.partial(_moe_out_kernel, nb, batch_major),
        grid=(n_blocks,),
        in_specs=[smem(lambda i: (0, 0, 0)),
                  smem(lambda i: (jnp.minimum(i + 1, n_blocks - 1), 0, 0)),
                  pl.BlockSpec(memory_space=pl.ANY),
                  pl.BlockSpec((tm, LANES), lambda i: (i, 0)),
                  pl.BlockSpec((tm, d), lambda i: (i, 0)),
                  pl.BlockSpec((None, nb, d), lambda i: (lat(i), 0, 5)),
                  vec, vec],
        out_specs=out_spec,
        out_shape=out_shape,
        scratch_shapes=[pltpu.VMEM((2, tm * Y_SUB, LANES), BF16), pltpu.VMEM((tm * Y_SUB, LANES), F32),
                        pltpu.SemaphoreType.DMA((2,))],
        compiler_params=_cparams(("arbitrary",)),
        name="moe_out",
    )(src, src, ys, info, x1, mod_rows, ln_g, ln_b)


def _class_experts():
    lo, hi = [], []
    for g in range(N_GROUPS):
        for a in range(EXPERTS_PER_GROUP):
            for b in range(a + 1, EXPERTS_PER_GROUP):
                lo.append(g * EXPERTS_PER_GROUP + a)
                hi.append(g * EXPERTS_PER_GROUP + b)
    return np.asarray(lo, np.int32), np.asarray(hi, np.int32)


def _routing_tables(info, counts, n_tiles, tmx):
    cnt = counts[0, :N_CLASSES].astype(jnp.int32)
    padded = ((cnt + tmx - 1) // tmx) * tmx
    ends = jnp.cumsum(padded)
    starts = ends - padded
    cls = info[:, INFO_CLASS].astype(jnp.int32)
    class_ids = jnp.arange(N_CLASSES, dtype=jnp.int32)
    start_of_tok = jnp.sum(jnp.where(cls[:, None] == class_ids[None, :], starts[None, :], 0), axis=1)
    dest = start_of_tok + info[:, INFO_RANK].astype(jnp.int32)
    tile_start = jnp.arange(n_tiles, dtype=jnp.int32) * tmx
    n_before = jnp.sum((ends[None, :] <= tile_start[:, None]).astype(jnp.int32), axis=1)
    tile_cls = jnp.minimum(n_before, N_CLASSES - 1)
    e_lo, e_hi = _class_experts()
    n_active = (ends[-1:] // tmx).astype(jnp.int32)
    return dest, ends, cnt, jnp.asarray(e_lo)[tile_cls], jnp.asarray(e_hi)[tile_cls], n_active


def _relayout_w_in(w_in):
    qkvz = w_in[..., :4 * DN_WIDTH]
    gates = w_in[..., 4 * DN_WIDTH:4 * DN_WIDTH + 4 * DN_HEADS]
    rxy = w_in[..., 4 * DN_WIDTH + 4 * DN_HEADS:]
    pad = jnp.zeros(w_in.shape[:-1] + (LANES - 4 * DN_HEADS,), w_in.dtype)
    return jnp.concatenate([qkvz, rxy, gates, pad], axis=-1).astype(BF16)


def _block_diag(w):
    n, c, _ = w.shape
    eye = jnp.eye(n, dtype=w.dtype)
    return (eye[:, None, :, None] * w[:, :, None, :]).reshape(n * c, n * c)


def _lane_vec(vals, offset):
    return jnp.zeros((1, LANES), F32).at[0, offset:offset + vals.size].set(vals.reshape(-1))


def kernel(x, c, ctx, c_ctx, w_ada, b_ada, w_in, conv_qkv_w, dn_a_log, dn_dt_bias, dn_onorm_w, rg_conv_w, rg_conv_b, rg_wa, rg_ba, rg_wi, rg_bi, rg_lambda, w_out, ln_g, ln_b, router_wg, router_bg, router_we, router_be, w_e_gate, w_e_up, w_e_down):
    nb, t_lat, d = x.shape
    t_ctx = ctx.shape[1]
    assert d == D_MODEL and nb % 8 == 0 and t_lat % GRID_W == 0 and t_ctx % CHUNK == 0
    assert (t_ctx * nb) % ROW_TILE == 0 and (t_lat * nb) % ROW_TILE == 0
    tt = t_ctx + t_lat
    rows = tt * nb
    ctx_rows = t_ctx * nb
    n_ctx_chunks = t_ctx // CHUNK

    h = jnp.concatenate([jnp.swapaxes(ctx, 0, 1), jnp.swapaxes(x, 0, 1)], axis=0).reshape(rows, d)

    n_cc = ((nb + 1 + 7) // 8) * 8
    cc = jnp.zeros((n_cc, d), F32).at[:nb].set(c).at[nb].set(c_ctx)
    mod = _modulation(cc, w_ada, b_ada)
    mod_ctx = jnp.broadcast_to(mod[:, nb:nb + 1], (DEPTH, nb, 6 * d))
    mod_rows = jnp.stack([mod_ctx, mod[:, :nb]], axis=1)

    w_in2 = _relayout_w_in(w_in)
    w_out_bf = w_out.astype(BF16)
    w_gate_up = jnp.concatenate([w_e_gate, w_e_up], axis=-1).astype(BF16)
    w_down = w_e_down.astype(BF16)
    for l in range(DEPTH):
        row0 = ctx_rows if l == DEPTH - 1 else 0
        moe_ctx_rows = ctx_rows - row0
        n_tiles = (rows - row0) // MOE_TILE + N_CLASSES
        mr = mod_rows[l]
        proj = _in_projection(h, mr, w_in2, l, nb, ctx_rows)
        qkv = _qkv_prep(proj, conv_qkv_w[l], nb, n_ctx_chunks)
        wg = jnp.concatenate([_block_diag(rg_wa[l, 0]), _block_diag(rg_wi[l, 0]),
                              _block_diag(rg_wa[l, 1]), _block_diag(rg_wi[l, 1])], axis=1).astype(BF16)
        bg = jnp.concatenate([rg_ba[l, 0], rg_bi[l, 0], rg_ba[l, 1], rg_bi[l, 1]])[None]
        hf, hb = _rglru(proj, rg_conv_w[l], rg_conv_b[l][None], wg, bg, rg_lambda[l], nb, n_ctx_chunks)
        alog_vec = _lane_vec(dn_a_log[l], 2 * DN_HEADS)
        dtb_vec = _lane_vec(dn_dt_bias[l], 2 * DN_HEADS)
        o_f, o_b = _deltanet(qkv, proj, alog_vec, dtb_vec, nb, n_ctx_chunks)
        x1 = _mixer_out(o_f, o_b, proj, hf, hb, dn_onorm_w[l][None], w_out_bf, l, h, mr,
                        ln_g[l, 0][None], ln_b[l, 0][None], nb, ctx_rows, row0)
        wr = jnp.zeros((d, LANES), F32).at[:, :N_GROUPS].set(router_wg[l]).at[
            :, N_GROUPS:N_GROUPS + N_EXPERTS].set(router_we[l])
        br = _lane_vec(jnp.concatenate([router_bg[l], router_be[l]]), 0)
        wr_hi = wr.astype(BF16)
        wr_hl = jnp.concatenate([wr_hi, (wr - wr_hi.astype(F32)).astype(BF16)], axis=1)
        hm, info, counts = _router(x1, mr, wr_hl, br, nb, moe_ctx_rows)
        dest, class_ends, class_cnt, tile_e_lo, tile_e_hi, n_active = _routing_tables(info, counts, n_tiles, MOE_TILE)
        xs = _dispatch(hm, dest, class_ends, class_cnt, n_tiles * MOE_TILE, MOE_TILE)
        ys = _class_ffn(xs, tile_e_lo, tile_e_hi, n_active, w_gate_up, w_down, l, MOE_TILE)
        h = _moe_out(ys, dest, info, x1, mr, ln_g[l, 1][None], ln_b[l, 1][None], nb, moe_ctx_rows,
                     batch_major=(l == DEPTH - 1))

    return h
```

```python
import functools
import math

import jax
import jax.numpy as jnp
import numpy as np
from jax import lax
from jax.experimental import pallas as pl
from jax.experimental.pallas import tpu as pltpu

F32 = jnp.float32
BF16 = jnp.bfloat16
HIGHEST = lax.Precision.HIGHEST

D_MODEL = 1024
DEPTH = 2
GRID_W = 64
DN_HEAD_DIM = 128
DN_WIDTH = 512
DN_HEADS = 4
RG_WIDTH = 512
RG_BLOCKS = 8
RG_BLOCK_DIM = 64
RG_C = 8.0
CONV_K = 4
N_GROUPS = 4
EXPERTS_PER_GROUP = 8
N_EXPERTS = 32
D_EXPERT = 256
DEEP_ALPHA = (2 * DEPTH) ** 0.25
LN_EPS = 1e-5
NORM_EPS = 1e-6

LANES = 128
CHUNK = 64
SOLVE_BLOCK = 16
DN_BATCHES = 4
D_PROJ = 3 * DN_WIDTH + DN_WIDTH + 2 * RG_WIDTH + LANES
COL_Z = 3
COL_RX = 4
COL_RY = 5
COL_GATE = (3 * DN_WIDTH + DN_WIDTH + 2 * RG_WIDTH) // LANES
ROW_TILE = 512
MOE_TILE = 256
VMEM_LIMIT = 56 * 1024 * 1024


def _cparams(sem):
    return pltpu.CompilerParams(dimension_semantics=sem, vmem_limit_bytes=VMEM_LIMIT)


def _layer_norm(x):
    mu = jnp.mean(x, axis=-1, keepdims=True)
    xc = x - mu
    var = jnp.mean(xc * xc, axis=-1, keepdims=True)
    return xc * lax.rsqrt(var + LN_EPS)


def _per_batch(x, v, nb):
    rows, c = x.shape
    return jnp.broadcast_to(v[None], (rows // nb, nb, c)).reshape(rows, c)


def _sigmoid(x):
    return 0.5 * jnp.tanh(0.5 * x) + 0.5


def _silu(x):
    h = 0.5 * x
    return h * jnp.tanh(h) + h


def _mod_kernel(cc_ref, w_ref, b_ref, o_ref):
    s = _silu(cc_ref[...])
    o_ref[...] = jnp.dot(s.astype(BF16), w_ref[...].astype(BF16), preferred_element_type=F32) + b_ref[...]


def _modulation(cc, w_ada, b_ada):
    depth, d, n6 = w_ada.shape
    rows = cc.shape[0]
    tn = 1536
    return pl.pallas_call(
        _mod_kernel,
        grid=(depth, n6 // tn),
        in_specs=[
            pl.BlockSpec((rows, d), lambda l, j: (0, 0)),
            pl.BlockSpec((None, d, tn), lambda l, j: (l, 0, j)),
            pl.BlockSpec((None, 1, tn), lambda l, j: (l, 0, j)),
        ],
        out_specs=pl.BlockSpec((None, rows, tn), lambda l, j: (l, 0, j)),
        out_shape=jax.ShapeDtypeStruct((depth, rows, n6), F32),
        compiler_params=_cparams(("parallel", "parallel")),
        name="adaln_mod",
    )(cc, w_ada, b_ada.reshape(depth, 1, n6))


def _inproj_kernel(nb, h_ref, sh_ref, sc_ref, w_ref, o_ref):
    xn = _layer_norm(h_ref[...])
    hm = xn * (1.0 + _per_batch(xn, sc_ref[...], nb)) + _per_batch(xn, sh_ref[...], nb)
    o_ref[...] = jnp.dot(hm.astype(BF16), w_ref[...], preferred_element_type=F32)


def _in_projection(h, mod_rows, w_in2, layer, nb, ctx_rows):
    rows, d = h.shape
    tm = ROW_TILE
    lat = lambda i: jnp.where(i * tm >= ctx_rows, 1, 0)
    return pl.pallas_call(
        functools.partial(_inproj_kernel, nb),
        grid=(rows // tm,),
        in_specs=[
            pl.BlockSpec((tm, d), lambda i: (i, 0)),
            pl.BlockSpec((None, nb, d), lambda i: (lat(i), 0, 0)),
            pl.BlockSpec((None, nb, d), lambda i: (lat(i), 0, 1)),
            pl.BlockSpec((None, d, D_PROJ), lambda i: (layer, 0, 0)),
        ],
        out_specs=pl.BlockSpec((tm, D_PROJ), lambda i: (i, 0)),
        out_shape=jax.ShapeDtypeStruct((rows, D_PROJ), F32),
        compiler_params=_cparams(("parallel",)),
        name="ln_mod_inproj",
    )(h, mod_rows, mod_rows, w_in2)


def _conv_tile(x, prev, nxt, w, nb, use_prev, use_next):
    prev = jnp.where(use_prev, prev, 0.0)
    nxt = jnp.where(use_next, nxt, 0.0)
    xm2 = jnp.concatenate([prev, x[: -2 * nb]], axis=0)
    xm1 = jnp.concatenate([prev[nb:], x[:-nb]], axis=0)
    xp1 = jnp.concatenate([x[nb:], nxt], axis=0)
    return xm2 * w[0:1] + xm1 * w[1:2] + x * w[2:3] + xp1 * w[3:4]


def _halo_flags(i, n_ctx_tiles):
    use_prev = jnp.logical_and(i > 0, i < n_ctx_tiles)
    use_next = i < n_ctx_tiles - 1
    return use_prev, use_next


def _qkv_prep_kernel(nb, n_ctx_tiles, x_ref, prev_ref, next_ref, w_ref, o_ref):
    i = pl.program_id(0)
    part = pl.program_id(1)
    use_prev, use_next = _halo_flags(i, n_ctx_tiles)
    y = _silu(_conv_tile(x_ref[...], prev_ref[...], next_ref[...], w_ref[...], nb, use_prev, use_next))
    qscale = jnp.where(part == 0, DN_HEAD_DIM ** -0.5, 1.0).astype(F32)
    for hh in range(DN_HEADS):
        yh = y[:, hh * DN_HEAD_DIM:(hh + 1) * DN_HEAD_DIM]
        inv = lax.rsqrt(jnp.sum(yh * yh, axis=-1, keepdims=True) + NORM_EPS) * qscale
        o_ref[hh] = yh * jnp.where(part < 2, inv, 1.0)


def _halo_specs(tile_rows, nb, cols, col_of, n_row_blocks16, tile_of=lambda i: i):
    prev_b = tile_rows // (2 * nb)
    next_b = tile_rows // nb
    return [
        pl.BlockSpec((tile_rows, cols), lambda i, *a: (tile_of(i), col_of(*a))),
        pl.BlockSpec((2 * nb, cols), lambda i, *a: (jnp.maximum(tile_of(i) * prev_b - 1, 0), col_of(*a))),
        pl.BlockSpec((nb, cols),
                     lambda i, *a: (jnp.minimum((tile_of(i) + 1) * next_b, n_row_blocks16 - 1), col_of(*a))),
    ]


def _qkv_prep(proj, conv_w, nb, n_ctx_tiles):
    rows = proj.shape[0]
    tr = CHUNK * nb
    return pl.pallas_call(
        functools.partial(_qkv_prep_kernel, nb, n_ctx_tiles),
        grid=(rows // tr, 3),
        in_specs=_halo_specs(tr, nb, DN_WIDTH, lambda p: p, rows // nb)
        + [pl.BlockSpec((CONV_K, DN_WIDTH), lambda i, p: (0, p))],
        out_specs=pl.BlockSpec((DN_HEADS, tr, DN_HEAD_DIM), lambda i, p: (p, i, 0)),
        out_shape=jax.ShapeDtypeStruct((3 * DN_HEADS, rows, DN_HEAD_DIM), F32),
        compiler_params=_cparams(("parallel", "parallel")),
        name="qkv_conv_norm",
    )(proj, proj, proj, conv_w)


def _bwd_chunk(i, n_ctx, n_tot):
    return jnp.where(i < n_ctx, n_ctx - 1 - i, n_tot + n_ctx - 1 - i)


def _rglru_kernel(nb, n_ctx_tiles, n_tot, xf_ref, pf_ref, nf_ref, xb_ref, pb_ref, nb_ref, cw_ref, cb_ref,
                  wg0_ref, wg1_ref, bg0_ref, bg1_ref, lam_ref, hf_ref, hb_ref, a_scr, b_scr, st_ref):
    i = pl.program_id(0)

    @pl.when(i == 0)
    def _():
        st_ref[...] = jnp.zeros_like(st_ref)

    w = RG_WIDTH
    sp = jax.nn.softplus(-lam_ref[...])
    tiles = (i, _bwd_chunk(i, n_ctx_tiles, n_tot))
    for d, (x_ref, p_ref, n_ref, wg_ref, bg_ref) in enumerate(
            ((xf_ref, pf_ref, nf_ref, wg0_ref, bg0_ref), (xb_ref, pb_ref, nb_ref, wg1_ref, bg1_ref))):
        use_prev, use_next = _halo_flags(tiles[d], n_ctx_tiles)
        xc = _conv_tile(x_ref[...], p_ref[...], n_ref[...], cw_ref[...], nb, use_prev, use_next) + cb_ref[...]
        gates = jnp.dot(xc.astype(BF16), wg_ref[...], preferred_element_type=F32) + bg_ref[...]
        ig = _sigmoid(gates[:, w:])
        c = (-0.5 * RG_C) * sp[d:d + 1]
        a = jnp.exp(c * jnp.tanh(0.5 * gates[:, :w]) + c)
        y = jnp.maximum(1.0 - a * a, 0.0)
        a_scr[d] = a
        b_scr[d] = jnp.where(y > 0.0, y * lax.rsqrt(y), 0.0) * (ig * xc)

    def body(t, carry):
        h0, h1 = carry
        r0 = pl.ds(pl.multiple_of(t * nb, nb), nb)
        r1 = pl.ds(pl.multiple_of((CHUNK - 1 - t) * nb, nb), nb)
        h0 = a_scr[0, r0, :] * h0 + b_scr[0, r0, :]
        h1 = a_scr[1, r1, :] * h1 + b_scr[1, r1, :]
        hf_ref[r0, :] = h0.astype(hf_ref.dtype)
        hb_ref[r1, :] = h1.astype(hb_ref.dtype)
        return h0, h1

    h0, h1 = lax.fori_loop(0, CHUNK, body, (st_ref[0], st_ref[1]), unroll=8)
    st_ref[0] = h0
    st_ref[1] = h1


def _rglru(proj, conv_w, conv_b, wg, bg, lam, nb, n_ctx_tiles):
    rows = proj.shape[0]
    tr = CHUNK * nb
    n_tot = rows // tr
    w = RG_WIDTH
    out = jax.ShapeDtypeStruct((rows, w), BF16)
    fwd = lambda i: i
    bwd = lambda i: _bwd_chunk(i, n_ctx_tiles, n_tot)
    full = lambda shape: pl.BlockSpec(shape, lambda i: (0,) * len(shape))
    return pl.pallas_call(
        functools.partial(_rglru_kernel, nb, n_ctx_tiles, n_tot),
        grid=(n_tot,),
        in_specs=_halo_specs(tr, nb, w, lambda: COL_RX, rows // nb, fwd)
        + _halo_specs(tr, nb, w, lambda: COL_RX, rows // nb, bwd)
        + [full((CONV_K, w)), full((1, w)),
           pl.BlockSpec((w, 2 * w), lambda i: (0, 0)), pl.BlockSpec((w, 2 * w), lambda i: (0, 1)),
           pl.BlockSpec((1, 2 * w), lambda i: (0, 0)), pl.BlockSpec((1, 2 * w), lambda i: (0, 1)),
           full((2, w))],
        out_specs=[pl.BlockSpec((tr, w), lambda i: (fwd(i), 0)), pl.BlockSpec((tr, w), lambda i: (bwd(i), 0))],
        out_shape=[out] * 2,
        scratch_shapes=[pltpu.VMEM((2, tr, w), F32), pltpu.VMEM((2, tr, w), F32), pltpu.VMEM((2, nb, w), F32)],
        compiler_params=_cparams(("arbitrary",)),
        name="rglru",
    )(proj, proj, proj, proj, proj, proj, conv_w, conv_b, wg, wg, bg, bg, lam)


def _bdot(a, b):
    return jnp.dot(a.astype(BF16), b.astype(BF16), preferred_element_type=F32)


def _deltanet_gates(d, raw, alog, dtb):
    c = CHUNK
    row = lax.broadcasted_iota(jnp.int32, (c, c), 0)
    col = lax.broadcasted_iota(jnp.int32, (c, c), 1)
    incl = (row >= col) if d == 0 else (row <= col)
    g_all = -jnp.exp(alog) * jax.nn.softplus(raw + dtb)
    gc_all = jnp.dot(incl.astype(F32), g_all, precision=HIGHEST, preferred_element_type=F32)
    g_tot = jnp.sum(g_all, axis=0, keepdims=True)
    return dict(incl=incl, strict=(row > col) if d == 0 else (row < col),
                diag_blk=(row // SOLVE_BLOCK) == (col // SOLVE_BLOCK),
                beta=_sigmoid(raw), gc=gc_all, gc_t=gc_all.T, egc=jnp.exp(gc_all),
                kdec=jnp.exp(g_tot - gc_all), glast=jnp.exp(g_tot))


def _deltanet_heads(gts, cbs, qs, ks, vs, ss):
    c = CHUNK
    dh = DN_HEAD_DIM
    n = len(cbs)
    rng = range(n)
    cgs = [2 * DN_HEADS + cb for cb in cbs]
    beta = [gts[i]["beta"][:, cbs[i]:cbs[i] + 1] for i in rng]
    egc = [gts[i]["egc"][:, cgs[i]:cgs[i] + 1] for i in rng]
    decay = [jnp.exp(jnp.where(gts[i]["incl"], gts[i]["gc"][:, cgs[i]:cgs[i] + 1] - gts[i]["gc_t"][cgs[i]:cgs[i] + 1, :],
                               -1e30)) for i in rng]
    kbeta = [ks[i] * beta[i] for i in rng]
    prod = [lax.dot_general(jnp.concatenate([kbeta[i], qs[i]], axis=0).astype(BF16), ks[i].astype(BF16),
                            (((1,), (1,)), ((), ())), preferred_element_type=F32) for i in rng]
    a_mat = [jnp.where(gts[i]["strict"], prod[i][:c] * decay[i], 0.0) for i in rng]
    intra = [prod[i][c:] * decay[i] for i in rng]
    ks_qs = [_bdot(jnp.concatenate([kbeta[i] * egc[i], qs[i] * egc[i]], axis=0), ss[i]) for i in rng]
    p = [jnp.where(gts[i]["diag_blk"], -a_mat[i], 0.0) for i in rng]
    y = [jnp.concatenate([vs[i] * beta[i] - ks_qs[i][:c], jnp.where(gts[i]["diag_blk"], 0.0, a_mat[i])], axis=1)
         for i in rng]
    for j in range(4):
        y = [y[i] + _bdot(p[i], y[i]) for i in rng]
        if j < 3:
            p = [_bdot(p[i], p[i]) for i in rng]
    e = [y[i][:, dh:] for i in rng]
    y = [y[i][:, :dh] for i in rng]
    e2 = [_bdot(e[i], e[i]) for i in rng]
    y = [y[i] + _bdot(e2[i], y[i]) for i in rng]
    v_new = [y[i] - _bdot(e[i], y[i]) for i in rng]
    o = [ks_qs[i][c:] + _bdot(intra[i], v_new[i]) for i in rng]
    s_new = [ss[i] * gts[i]["glast"][:, cgs[i]:cgs[i] + 1] + lax.dot_general(
        (ks[i] * gts[i]["kdec"][:, cgs[i]:cgs[i] + 1]).astype(BF16), v_new[i].astype(BF16),
        (((0,), (0,)), ((), ())), preferred_element_type=F32) for i in rng]
    return o, s_new


def _deltanet_kernel(nb, qf_ref, kf_ref, vf_ref, gf_ref, qb_ref, kb_ref, vb_ref, gb_ref, alog_ref, dtb_ref,
                     of_ref, ob_ref, s_ref):
    @pl.when(pl.program_id(0) == 0)
    def _():
        s_ref[...] = jnp.zeros_like(s_ref)

    dirs = ((qf_ref, kf_ref, vf_ref, gf_ref, of_ref), (qb_ref, kb_ref, vb_ref, gb_ref, ob_ref))

    def body(it, carry):
        idx = [(it * DN_BATCHES + j, d, hh) for j in range(DN_BATCHES) for d in range(2) for hh in range(DN_HEADS)]
        rows_of = {j: pl.ds(it * DN_BATCHES + j, CHUNK, stride=nb) for j in range(DN_BATCHES)}
        rows_b = [rows_of[i // (2 * DN_HEADS)] for i in range(len(idx))]
        cbs = [d * DN_HEADS + hh for _, d, hh in idx]
        gates = {(j, d): _deltanet_gates(d, dirs[d][3][rows_of[j], :], alog_ref[...], dtb_ref[...])
                 for j in range(DN_BATCHES) for d in range(2)}
        gts = [gates[(i // (2 * DN_HEADS), d)] for i, (_, d, _) in enumerate(idx)]
        qs = [dirs[d][0][hh, rows_b[i], :] for i, (_, d, hh) in enumerate(idx)]
        ks = [dirs[d][1][hh, rows_b[i], :] for i, (_, d, hh) in enumerate(idx)]
        vs = [dirs[d][2][hh, rows_b[i], :] for i, (_, d, hh) in enumerate(idx)]
        ss = [s_ref[b * (2 * DN_HEADS) + cbs[i]] for i, (b, _, _) in enumerate(idx)]
        o, s_new = _deltanet_heads(gts, cbs, qs, ks, vs, ss)
        for i, (b, d, hh) in enumerate(idx):
            dirs[d][4][hh, rows_b[i], :] = o[i]
            s_ref[b * (2 * DN_HEADS) + cbs[i]] = s_new[i]
        return carry

    lax.fori_loop(0, nb // DN_BATCHES, body, 0)


def _deltanet(qkv, proj, alog_vec, dtb_vec, nb, n_ctx_chunks):
    rows = qkv.shape[1]
    tr = CHUNK * nb
    n_tot = rows // tr
    fwd = lambda i: i
    bwd = lambda i: _bwd_chunk(i, n_ctx_chunks, n_tot)

    def specs(order):
        return [pl.BlockSpec((DN_HEADS, tr, DN_HEAD_DIM), lambda i, p=p: (p, order(i), 0)) for p in range(3)] + [
            pl.BlockSpec((tr, LANES), lambda i: (order(i), COL_GATE))]

    vec = pl.BlockSpec((1, LANES), lambda i: (0, 0))
    out = jax.ShapeDtypeStruct((DN_HEADS, rows, DN_HEAD_DIM), F32)
    return pl.pallas_call(
        functools.partial(_deltanet_kernel, nb),
        grid=(n_tot,),
        in_specs=specs(fwd) + specs(bwd) + [vec, vec],
        out_specs=[pl.BlockSpec((DN_HEADS, tr, DN_HEAD_DIM), lambda i: (0, fwd(i), 0)),
                   pl.BlockSpec((DN_HEADS, tr, DN_HEAD_DIM), lambda i: (0, bwd(i), 0))],
        out_shape=[out, out],
        scratch_shapes=[pltpu.VMEM((nb * 2 * DN_HEADS, DN_HEAD_DIM, DN_HEAD_DIM), F32)],
        compiler_params=_cparams(("arbitrary",)),
        name="deltanet",
    )(qkv, qkv, qkv, proj, qkv, qkv, qkv, proj, alog_vec, dtb_vec)


def _mixer_out_kernel(nb, of_ref, ob_ref, z_ref, hf_ref, hb_ref, ry_ref, onw_ref, w_ref, h_ref, gt_ref, g_ref, b_ref,
                      o_ref):
    z = z_ref[...]
    parts = []
    for hh in range(DN_HEADS):
        sl = slice(hh * DN_HEAD_DIM, (hh + 1) * DN_HEAD_DIM)
        oh = of_ref[hh] + ob_ref[hh]
        inv = lax.rsqrt(jnp.mean(oh * oh, axis=-1, keepdims=True) + NORM_EPS)
        parts.append(oh * inv * onw_ref[...] * _silu(z[:, sl]))
    parts.append((hf_ref[...].astype(F32) + hb_ref[...].astype(F32)) * jax.nn.gelu(ry_ref[...]))
    y = jnp.concatenate(parts, axis=1).astype(BF16)
    u = jnp.dot(y, w_ref[...], preferred_element_type=F32)
    r = DEEP_ALPHA * h_ref[...] + _per_batch(u, gt_ref[...], nb) * u
    o_ref[...] = _layer_norm(r) * g_ref[...] + b_ref[...]


def _mixer_out(o_f, o_b, proj, hf, hb, onorm_w, w_out, layer, h, mod_rows, ln_g, ln_b, nb, ctx_rows, row0):
    rows, d = h.shape
    tm = ROW_TILE
    t0 = row0 // tm
    lat = lambda i: jnp.where((i + t0) * tm >= ctx_rows, 1, 0)
    half = pl.BlockSpec((tm, DN_WIDTH), lambda i: (i + t0, 0))
    planes = pl.BlockSpec((DN_HEADS, tm, DN_HEAD_DIM), lambda i: (0, i + t0, 0))
    vec = lambda n: pl.BlockSpec((1, n), lambda i: (0, 0))
    return pl.pallas_call(
        functools.partial(_mixer_out_kernel, nb),
        grid=((rows - row0) // tm,),
        in_specs=[planes, planes, pl.BlockSpec((tm, DN_WIDTH), lambda i: (i + t0, COL_Z)), half, half,
                  pl.BlockSpec((tm, RG_WIDTH), lambda i: (i + t0, COL_RY)),
                  vec(DN_HEAD_DIM), pl.BlockSpec((None, d, d), lambda i: (layer, 0, 0)),
                  pl.BlockSpec((tm, d), lambda i: (i + t0, 0)),
                  pl.BlockSpec((None, nb, d), lambda i: (lat(i), 0, 2)),
                  vec(d), vec(d)],
        out_specs=pl.BlockSpec((tm, d), lambda i: (i, 0)),
        out_shape=jax.ShapeDtypeStruct((rows - row0, d), F32),
        compiler_params=_cparams(("parallel",)),
        name="mixer_out",
    )(o_f, o_b, proj, hf, hb, proj, onorm_w, w_out, h, mod_rows, ln_g, ln_b)


PAIRS_PER_GROUP = EXPERTS_PER_GROUP * (EXPERTS_PER_GROUP - 1) // 2
N_CLASSES = N_GROUPS * PAIRS_PER_GROUP
assert N_CLASSES <= LANES
INFO_CLASS, INFO_RANK, INFO_W_LO, INFO_W_HI = range(4)
Y_SUB = 2 * (D_MODEL // LANES)


ROW_SUB = D_MODEL // LANES


def _store_row_tiles(ref, x):
    n = x.shape[0]
    for c in range(ROW_SUB):
        ref[pl.ds(c, n, stride=ROW_SUB), :] = x[:, c * LANES:(c + 1) * LANES]


def _load_row_tiles(ref, n):
    return jnp.concatenate([ref[pl.ds(c, n, stride=ROW_SUB), :] for c in range(ROW_SUB)], axis=1)


def _first_index(mask, lane):
    return jnp.min(jnp.where(mask, lane, LANES), axis=-1, keepdims=True)


def _router_kernel(nb, x_ref, sh_ref, sc_ref, wr_ref, br_ref, hm_ref, info_ref, cnt_ref, run_ref):
    @pl.when(pl.program_id(0) == 0)
    def _():
        run_ref[...] = jnp.zeros_like(run_ref)

    xn = _layer_norm(x_ref[...])
    hm = xn * (1.0 + _per_batch(xn, sc_ref[...], nb)) + _per_batch(xn, sh_ref[...], nb)
    _store_row_tiles(hm_ref, hm)
    tm = hm.shape[0]
    h_hi = hm.astype(BF16)
    h_lo = (hm - h_hi.astype(F32)).astype(BF16)
    hi_hl = jnp.dot(h_hi, wr_ref[...], preferred_element_type=F32)
    logits = (hi_hl[:, :LANES] + hi_hl[:, LANES:] + jnp.dot(h_lo, wr_ref[:, :LANES], preferred_element_type=F32)
              + br_ref[...])
    lane = lax.broadcasted_iota(jnp.int32, (tm, LANES), 1)
    neg = -jnp.inf
    is_g = lane < N_GROUPS
    gmax = jnp.max(jnp.where(is_g, logits, neg), axis=-1, keepdims=True)
    g_sel = _first_index(jnp.logical_and(is_g, logits == gmax), lane)
    p_group = 1.0 / jnp.sum(jnp.where(is_g, jnp.exp(logits - gmax), 0.0), axis=-1, keepdims=True)
    lo = N_GROUPS + EXPERTS_PER_GROUP * g_sel
    in_grp = jnp.logical_and(lane >= lo, lane < lo + EXPERTS_PER_GROUP)
    m1 = jnp.max(jnp.where(in_grp, logits, neg), axis=-1, keepdims=True)
    i1 = _first_index(jnp.logical_and(in_grp, logits == m1), lane)
    rest = jnp.logical_and(in_grp, lane != i1)
    m2 = jnp.max(jnp.where(rest, logits, neg), axis=-1, keepdims=True)
    i2 = _first_index(jnp.logical_and(rest, logits == m2), lane)
    e2 = jnp.exp(m2 - m1)
    w1 = p_group / (1.0 + e2)
    w2 = p_group * e2 / (1.0 + e2)
    l1 = i1 - lo
    l2 = i2 - lo
    e_lo = jnp.minimum(l1, l2)
    e_hi = jnp.maximum(l1, l2)
    pair = jnp.right_shift(e_lo * (2 * EXPERTS_PER_GROUP - 1 - e_lo), 1) + (e_hi - e_lo - 1)
    cls = g_sel * PAIRS_PER_GROUP + pair
    first_is_lo = l1 < l2
    w_lo = jnp.where(first_is_lo, w1, w2)
    w_hi = jnp.where(first_is_lo, w2, w1)
    oh = lane == cls
    r_i = lax.broadcasted_iota(jnp.int32, (tm, tm), 0)
    c_i = lax.broadcasted_iota(jnp.int32, (tm, tm), 1)
    before = jnp.dot((r_i > c_i).astype(BF16), oh.astype(BF16), preferred_element_type=F32) + run_ref[...]
    rank = jnp.sum(jnp.where(oh, before, 0.0), axis=-1, keepdims=True)
    run_ref[...] = run_ref[...] + jnp.sum(oh.astype(F32), axis=0, keepdims=True)
    cnt_ref[...] = run_ref[...]
    info = jnp.zeros((tm, LANES), F32)
    for idx, val in ((INFO_CLASS, cls.astype(F32)), (INFO_RANK, rank), (INFO_W_LO, w_lo), (INFO_W_HI, w_hi)):
        info = jnp.where(lane == idx, val, info)
    info_ref[...] = info


def _router(x1, mod_rows, wr, br, nb, ctx_rows):
    rows, d = x1.shape
    tm = ROW_TILE
    lat = lambda i: jnp.where(i * tm >= ctx_rows, 1, 0)
    return pl.pallas_call(
        functools.partial(_router_kernel, nb),
        grid=(rows // tm,),
        in_specs=[pl.BlockSpec((tm, d), lambda i: (i, 0)),
                  pl.BlockSpec((None, nb, d), lambda i: (lat(i), 0, 3)),
                  pl.BlockSpec((None, nb, d), lambda i: (lat(i), 0, 4)),
                  pl.BlockSpec((d, 2 * LANES), lambda i: (0, 0)),
                  pl.BlockSpec((1, LANES), lambda i: (0, 0))],
        out_specs=[pl.BlockSpec((tm * ROW_SUB, LANES), lambda i: (i, 0)),
                   pl.BlockSpec((tm, LANES), lambda i: (i, 0)),
                   pl.BlockSpec((1, LANES), lambda i: (0, 0))],
        out_shape=[jax.ShapeDtypeStruct((rows * ROW_SUB, LANES), F32),
                   jax.ShapeDtypeStruct((rows, LANES), F32),
                   jax.ShapeDtypeStruct((1, LANES), F32)],
        scratch_shapes=[pltpu.VMEM((1, LANES), F32)],
        compiler_params=_cparams(("arbitrary",)),
        name="moe_router",
    )(x1, mod_rows, mod_rows, wr, br)


def _dispatch_kernel(tmx, ends_ref, cnt_ref, dst_ref, hm_ref, xs_hbm, zero_scr, sem, zsem):
    rs = ROW_SUB
    tb = dst_ref.shape[1]

    @pl.when(pl.program_id(0) == 0)
    def _():
        zero_scr[...] = jnp.zeros_like(zero_scr)

        def tail_copy(c):
            start = pl.multiple_of((ends_ref[c] - tmx) * rs, tmx * rs)
            return pltpu.make_async_copy(zero_scr, xs_hbm.at[pl.ds(start, tmx * rs)], zsem)

        def start_one(c, carry):
            @pl.when(cnt_ref[c] > 0)
            def _():
                tail_copy(c).start()
            return carry

        def wait_one(c, carry):
            @pl.when(cnt_ref[c] > 0)
            def _():
                tail_copy(c).wait()
            return carry

        lax.fori_loop(0, N_CLASSES, start_one, 0)

        def spare_copy(j):
            return pltpu.make_async_copy(zero_scr, xs_hbm.at[pl.ds(pl.multiple_of(j * (tmx * rs), tmx * rs), tmx * rs)],
                                         zsem)

        n_used = ends_ref[N_CLASSES - 1] // tmx
        n_tiles = xs_hbm.shape[0] // (tmx * rs)
        lax.fori_loop(n_used, n_tiles, lambda j, carry: (spare_copy(j).start(), carry)[1], 0)
        lax.fori_loop(0, N_CLASSES, wait_one, 0)
        lax.fori_loop(n_used, n_tiles, lambda j, carry: (spare_copy(j).wait(), carry)[1], 0)

    for r in range(tb):
        dst = xs_hbm.at[pl.ds(pl.multiple_of(dst_ref[0, r], rs), rs)]
        pltpu.make_async_copy(hm_ref.at[pl.ds(r * rs, rs)], dst, sem).start(priority=r % 2)
    pltpu.make_async_copy(hm_ref, hm_ref, sem).wait()


def _dispatch(hm, dest, class_ends, class_cnt, sorted_rows, tmx):
    tb = ROW_TILE
    n_blocks = dest.shape[0] // tb
    grid_spec = pltpu.PrefetchScalarGridSpec(
        num_scalar_prefetch=2,
        grid=(n_blocks,),
        in_specs=[pl.BlockSpec((None, 1, tb), lambda i, e, c: (i, 0, 0), memory_space=pltpu.SMEM),
                  pl.BlockSpec((tb * ROW_SUB, LANES), lambda i, e, c: (i, 0))],
        out_specs=pl.BlockSpec(memory_space=pl.ANY),
        scratch_shapes=[pltpu.VMEM((tmx * ROW_SUB, LANES), F32), pltpu.SemaphoreType.DMA(()),
                        pltpu.SemaphoreType.DMA(())],
    )
    return pl.pallas_call(
        functools.partial(_dispatch_kernel, tmx),
        grid_spec=grid_spec,
        out_shape=jax.ShapeDtypeStruct((sorted_rows * ROW_SUB, LANES), F32),
        compiler_params=_cparams(("arbitrary",)),
        name="moe_dispatch",
    )(class_ends, class_cnt, (dest * ROW_SUB).reshape(n_blocks, 1, tb), hm)


def _class_ffn_kernel(telo_ref, tehi_ref, nact_ref, xs_ref, wgul_ref, wdl_ref, wguh_ref, wdh_ref, ys_ref, y_scr):
    j = pl.program_id(0)
    tmx = xs_ref.shape[0] // ROW_SUB
    de = wdl_ref.shape[0]

    @pl.when(j < nact_ref[0])
    def _():
        x = _load_row_tiles(xs_ref, tmx).astype(BF16)
        for half, (wgu_ref, wd_ref) in enumerate(((wgul_ref, wdl_ref), (wguh_ref, wdh_ref))):
            ab = jnp.dot(x, wgu_ref[...], preferred_element_type=F32)
            y = jnp.dot((_silu(ab[:, :de]) * ab[:, de:]).astype(BF16), wd_ref[...], preferred_element_type=F32)
            for c in range(ROW_SUB):
                y_scr[pl.ds(half * ROW_SUB + c, tmx, stride=Y_SUB), :] = y[:, c * LANES:(c + 1) * LANES]
        ys_ref[...] = y_scr[...].astype(BF16)

    @pl.when(j >= nact_ref[0])
    def _():
        ys_ref[...] = jnp.zeros_like(ys_ref)


def _class_ffn(xs, tile_e_lo, tile_e_hi, n_active, w_gate_up, w_down, layer, tmx):
    n_tiles = xs.shape[0] // (tmx * ROW_SUB)
    d, de2 = w_gate_up.shape[-2:]
    w_in = lambda te_idx: pl.BlockSpec((None, None, d, de2), lambda j, *te: (layer, te[te_idx][j], 0, 0))
    w_out = lambda te_idx: pl.BlockSpec((None, None, de2 // 2, d), lambda j, *te: (layer, te[te_idx][j], 0, 0))
    grid_spec = pltpu.PrefetchScalarGridSpec(
        num_scalar_prefetch=3,
        grid=(n_tiles,),
        in_specs=[pl.BlockSpec((tmx * ROW_SUB, LANES), lambda j, lo, hi, na: (jnp.minimum(j, na[0] - 1), 0)),
                  w_in(0), w_out(0), w_in(1), w_out(1)],
        out_specs=pl.BlockSpec((tmx * Y_SUB, LANES), lambda j, lo, hi, na: (j, 0)),
        scratch_shapes=[pltpu.VMEM((tmx * Y_SUB, LANES), F32)],
    )
    return pl.pallas_call(
        _class_ffn_kernel,
        grid_spec=grid_spec,
        out_shape=jax.ShapeDtypeStruct((n_tiles * tmx * Y_SUB, LANES), BF16),
        compiler_params=_cparams(("arbitrary",)),
        name="expert_ffn",
    )(tile_e_lo, tile_e_hi, n_active, xs, w_gate_up, w_down, w_gate_up, w_down)


def _moe_out_kernel(nb, batch_major, src0_ref, srcn_ref, ys_hbm, info_ref, x_ref, gt_ref, g_ref, b_ref, o_ref,
                    ybuf, y_scr, sem):
    i = pl.program_id(0)
    tm = info_ref.shape[0]
    slot = i % 2
    nslot = 1 - slot

    def gather_all(src_ref, s):
        for r in range(tm):
            pltpu.make_async_copy(ys_hbm.at[pl.ds(pl.multiple_of(src_ref[0, r], Y_SUB), Y_SUB)],
                                  ybuf.at[s, pl.ds(r * Y_SUB, Y_SUB)], sem.at[s]).start(priority=r % 2)

    def wait_rows(s):
        pltpu.make_async_copy(ybuf.at[s], ybuf.at[s], sem.at[s]).wait()

    @pl.when(i == 0)
    def _():
        gather_all(src0_ref, 0)

    wait_rows(slot)
    gather_all(srcn_ref, nslot)
    info = info_ref[...]
    y_scr[...] = ybuf[slot].astype(F32)
    y_lo = jnp.concatenate([y_scr[pl.ds(c, tm, stride=Y_SUB), :] for c in range(ROW_SUB)], axis=1)
    y_hi = jnp.concatenate([y_scr[pl.ds(ROW_SUB + c, tm, stride=Y_SUB), :] for c in range(ROW_SUB)], axis=1)
    f = y_lo * info[:, INFO_W_LO:INFO_W_LO + 1] + y_hi * info[:, INFO_W_HI:INFO_W_HI + 1]
    r = DEEP_ALPHA * x_ref[...] + _per_batch(f, gt_ref[...], nb) * f
    res = _layer_norm(r) * g_ref[...] + b_ref[...]
    if batch_major:
        for c in range(ROW_SUB):
            y_scr[pl.ds(c * tm, tm), :] = res[:, c * LANES:(c + 1) * LANES]
        for b in range(nb):
            for c in range(ROW_SUB):
                o_ref[b, :, c * LANES:(c + 1) * LANES] = y_scr[pl.ds(c * tm + b, tm // nb, stride=nb), :]
    else:
        o_ref[...] = res

    @pl.when(i == pl.num_programs(0) - 1)
    def _():
        wait_rows(nslot)


def _moe_out(ys, dest, info, x1, mod_rows, ln_g, ln_b, nb, ctx_rows, batch_major):
    rows, d = x1.shape
    tm = ROW_TILE
    n_blocks = rows // tm
    if batch_major:
        out_spec = pl.BlockSpec((nb, tm // nb, d), lambda i: (0, i, 0))
        out_shape = jax.ShapeDtypeStruct((nb, rows // nb, d), F32)
    else:
        out_spec = pl.BlockSpec((tm, d), lambda i: (i, 0))
        out_shape = jax.ShapeDtypeStruct((rows, d), F32)
    lat = lambda i: jnp.where(i * tm >= ctx_rows, 1, 0)
    vec = pl.BlockSpec((1, d), lambda i: (0, 0))
    src = (dest * Y_SUB).reshape(n_blocks, 1, tm)
    smem = lambda f: pl.BlockSpec((None, 1, tm), f, memory_space=pltpu.SMEM)
    return pl.pallas_call(
        functools.partial(_moe_out_kernel, nb, batch_major),
        grid=(n_blocks,),
        in_specs=[smem(lambda i: (0, 0, 0)),
                  smem(lambda i: (jnp.minimum(i + 1, n_blocks - 1), 0, 0)),
                  pl.BlockSpec(memory_space=pl.ANY),
                  pl.BlockSpec((tm, LANES), lambda i: (i, 0)),
                  pl.BlockSpec((tm, d), lambda i: (i, 0)),
                  pl.BlockSpec((None, nb, d), lambda i: (lat(i), 0, 5)),
                  vec, vec],
        out_specs=out_spec,
        out_shape=out_shape,
        scratch_shapes=[pltpu.VMEM((2, tm * Y_SUB, LANES), BF16), pltpu.VMEM((tm * Y_SUB, LANES), F32),
                        pltpu.SemaphoreType.DMA((2,))],
        compiler_params=_cparams(("arbitrary",)),
        name="moe_out",
    )(src, src, ys, info, x1, mod_rows, ln_g, ln_b)


def _class_experts():
    lo, hi = [], []
    for g in range(N_GROUPS):
        for a in range(EXPERTS_PER_GROUP):
            for b in range(a + 1, EXPERTS_PER_GROUP):
                lo.append(g * EXPERTS_PER_GROUP + a)
                hi.append(g * EXPERTS_PER_GROUP + b)
    return np.asarray(lo, np.int32), np.asarray(hi, np.int32)


def _routing_tables(info, counts, n_tiles, tmx):
    cnt = counts[0, :N_CLASSES].astype(jnp.int32)
    padded = ((cnt + tmx - 1) // tmx) * tmx
    ends = jnp.cumsum(padded)
    starts = ends - padded
    cls = info[:, INFO_CLASS].astype(jnp.int32)
    class_ids = jnp.arange(N_CLASSES, dtype=jnp.int32)
    start_of_tok = jnp.sum(jnp.where(cls[:, None] == class_ids[None, :], starts[None, :], 0), axis=1)
    dest = start_of_tok + info[:, INFO_RANK].astype(jnp.int32)
    tile_start = jnp.arange(n_tiles, dtype=jnp.int32) * tmx
    n_before = jnp.sum((ends[None, :] <= tile_start[:, None]).astype(jnp.int32), axis=1)
    tile_cls = jnp.minimum(n_before, N_CLASSES - 1)
    e_lo, e_hi = _class_experts()
    n_active = (ends[-1:] // tmx).astype(jnp.int32)
    return dest, ends, cnt, jnp.asarray(e_lo)[tile_cls], jnp.asarray(e_hi)[tile_cls], n_active


def _relayout_w_in(w_in):
    qkvz = w_in[..., :4 * DN_WIDTH]
    gates = w_in[..., 4 * DN_WIDTH:4 * DN_WIDTH + 4 * DN_HEADS]
    rxy = w_in[..., 4 * DN_WIDTH + 4 * DN_HEADS:]
    pad = jnp.zeros(w_in.shape[:-1] + (LANES - 4 * DN_HEADS,), w_in.dtype)
    return jnp.concatenate([qkvz, rxy, gates, pad], axis=-1).astype(BF16)


def _block_diag(w):
    n, c, _ = w.shape
    eye = jnp.eye(n, dtype=w.dtype)
    return (eye[:, None, :, None] * w[:, :, None, :]).reshape(n * c, n * c)


def _lane_vec(vals, offset):
    return jnp.zeros((1, LANES), F32).at[0, offset:offset + vals.size].set(vals.reshape(-1))


def kernel(x, c, ctx, c_ctx, w_ada, b_ada, w_in, conv_qkv_w, dn_a_log, dn_dt_bias, dn_onorm_w, rg_conv_w, rg_conv_b, rg_wa, rg_ba, rg_wi, rg_bi, rg_lambda, w_out, ln_g, ln_b, router_wg, router_bg, router_we, router_be, w_e_gate, w_e_up, w_e_down):
    nb, t_lat, d = x.shape
    t_ctx = ctx.shape[1]
    assert d == D_MODEL and nb % 8 == 0 and t_lat % GRID_W == 0 and t_ctx % CHUNK == 0
    assert (t_ctx * nb) % ROW_TILE == 0 and (t_lat * nb) % ROW_TILE == 0
    tt = t_ctx + t_lat
    rows = tt * nb
    ctx_rows = t_ctx * nb
    n_ctx_chunks = t_ctx // CHUNK

    h = jnp.concatenate([jnp.swapaxes(ctx, 0, 1), jnp.swapaxes(x, 0, 1)], axis=0).reshape(rows, d)

    n_cc = ((nb + 1 + 7) // 8) * 8
    cc = jnp.zeros((n_cc, d), F32).at[:nb].set(c).at[nb].set(c_ctx)
    mod = _modulation(cc, w_ada, b_ada)
    mod_ctx = jnp.broadcast_to(mod[:, nb:nb + 1], (DEPTH, nb, 6 * d))
    mod_rows = jnp.stack([mod_ctx, mod[:, :nb]], axis=1)

    w_in2 = _relayout_w_in(w_in)
    w_out_bf = w_out.astype(BF16)
    w_gate_up = jnp.concatenate([w_e_gate, w_e_up], axis=-1).astype(BF16)
    w_down = w_e_down.astype(BF16)
    for l in range(DEPTH):
        row0 = ctx_rows if l == DEPTH - 1 else 0
        moe_ctx_rows = ctx_rows - row0
        n_tiles = (rows - row0) // MOE_TILE + N_CLASSES
        mr = mod_rows[l]
        proj = _in_projection(h, mr, w_in2, l, nb, ctx_rows)
        qkv = _qkv_prep(proj, conv_qkv_w[l], nb, n_ctx_chunks)
        wg = jnp.concatenate([_block_diag(rg_wa[l, 0]), _block_diag(rg_wi[l, 0]),
                              _block_diag(rg_wa[l, 1]), _block_diag(rg_wi[l, 1])], axis=1).astype(BF16)
        bg = jnp.concatenate([rg_ba[l, 0], rg_bi[l, 0], rg_ba[l, 1], rg_bi[l, 1]])[None]
        hf, hb = _rglru(proj, rg_conv_w[l], rg_conv_b[l][None], wg, bg, rg_lambda[l], nb, n_ctx_chunks)
        alog_vec = _lane_vec(dn_a_log[l], 2 * DN_HEADS)
        dtb_vec = _lane_vec(dn_dt_bias[l], 2 * DN_HEADS)
        o_f, o_b = _deltanet(qkv, proj, alog_vec, dtb_vec, nb, n_ctx_chunks)
        x1 = _mixer_out(o_f, o_b, proj, hf, hb, dn_onorm_w[l][None], w_out_bf, l, h, mr,
                        ln_g[l, 0][None], ln_b[l, 0][None], nb, ctx_rows, row0)
        wr = jnp.zeros((d, LANES), F32).at[:, :N_GROUPS].set(router_wg[l]).at[
            :, N_GROUPS:N_GROUPS + N_EXPERTS].set(router_we[l])
        br = _lane_vec(jnp.concatenate([router_bg[l], router_be[l]]), 0)
        wr_hi = wr.astype(BF16)
        wr_hl = jnp.concatenate([wr_hi, (wr - wr_hi.astype(F32)).astype(BF16)], axis=1)
        hm, info, counts = _router(x1, mr, wr_hl, br, nb, moe_ctx_rows)
        dest, class_ends, class_cnt, tile_e_lo, tile_e_hi, n_active = _routing_tables(info, counts, n_tiles, MOE_TILE)
        xs = _dispatch(hm, dest, class_ends, class_cnt, n_tiles * MOE_TILE, MOE_TILE)
        ys = _class_ffn(xs, tile_e_lo, tile_e_hi, n_active, w_gate_up, w_down, l, MOE_TILE)
        h = _moe_out(ys, dest, info, x1, mr, ln_g[l, 1][None], ln_b[l, 1][None], nb, moe_ctx_rows,
                     batch_major=(l == DEPTH - 1))

    return h
```

```python
import functools
import math

import jax
import jax.numpy as jnp
import numpy as np
from jax import lax
from jax.experimental import pallas as pl
from jax.experimental.pallas import tpu as pltpu

F32 = jnp.float32
BF16 = jnp.bfloat16
HIGHEST = lax.Precision.HIGHEST

D_MODEL = 1024
DEPTH = 2
GRID_W = 64
DN_HEAD_DIM = 128
DN_WIDTH = 512
DN_HEADS = 4
RG_WIDTH = 512
RG_BLOCKS = 8
RG_BLOCK_DIM = 64
RG_C = 8.0
CONV_K = 4
N_GROUPS = 4
EXPERTS_PER_GROUP = 8
N_EXPERTS = 32
D_EXPERT = 256
DEEP_ALPHA = (2 * DEPTH) ** 0.25
LN_EPS = 1e-5
NORM_EPS = 1e-6

LANES = 128
CHUNK = 64
SOLVE_BLOCK = 16
DN_BATCHES = 4
D_PROJ = 3 * DN_WIDTH + DN_WIDTH + 2 * RG_WIDTH + LANES
COL_Z = 3
COL_RX = 4
COL_RY = 5
COL_GATE = (3 * DN_WIDTH + DN_WIDTH + 2 * RG_WIDTH) // LANES
ROW_TILE = 512
INPROJ_TILE = 1024
MOE_TILE = 256
VMEM_LIMIT = 56 * 1024 * 1024


def _cparams(sem):
    return pltpu.CompilerParams(dimension_semantics=sem, vmem_limit_bytes=VMEM_LIMIT)


def _layer_norm(x):
    mu = jnp.mean(x, axis=-1, keepdims=True)
    xc = x - mu
    var = jnp.mean(xc * xc, axis=-1, keepdims=True)
    return xc * lax.rsqrt(var + LN_EPS)


def _per_batch(x, v, nb):
    rows, c = x.shape
    return jnp.broadcast_to(v[None], (rows // nb, nb, c)).reshape(rows, c)


def _sigmoid(x):
    return 0.5 * jnp.tanh(0.5 * x) + 0.5


def _silu(x):
    h = 0.5 * x
    return h * jnp.tanh(h) + h


def _mod_kernel(cc_ref, w_ref, b_ref, o_ref):
    s = _silu(cc_ref[...])
    o_ref[...] = jnp.dot(s.astype(BF16), w_ref[...].astype(BF16), preferred_element_type=F32) + b_ref[...]


def _modulation(cc, w_ada, b_ada):
    depth, d, n6 = w_ada.shape
    rows = cc.shape[0]
    tn = 1536
    return pl.pallas_call(
        _mod_kernel,
        grid=(depth, n6 // tn),
        in_specs=[
            pl.BlockSpec((rows, d), lambda l, j: (0, 0)),
            pl.BlockSpec((None, d, tn), lambda l, j: (l, 0, j)),
            pl.BlockSpec((None, 1, tn), lambda l, j: (l, 0, j)),
        ],
        out_specs=pl.BlockSpec((None, rows, tn), lambda l, j: (l, 0, j)),
        out_shape=jax.ShapeDtypeStruct((depth, rows, n6), F32),
        compiler_params=_cparams(("parallel", "parallel")),
        name="adaln_mod",
    )(cc, w_ada, b_ada.reshape(depth, 1, n6))


def _inproj_kernel(nb, h_ref, sh_ref, sc_ref, w_ref, o_ref):
    xn = _layer_norm(h_ref[...])
    hm = xn * (1.0 + _per_batch(xn, sc_ref[...], nb)) + _per_batch(xn, sh_ref[...], nb)
    o_ref[...] = jnp.dot(hm.astype(BF16), w_ref[...], preferred_element_type=F32)


def _in_projection(h, mod_rows, w_in2, layer, nb, ctx_rows):
    rows, d = h.shape
    tm = INPROJ_TILE
    lat = lambda i: jnp.where(i * tm >= ctx_rows, 1, 0)
    return pl.pallas_call(
        functools.partial(_inproj_kernel, nb),
        grid=(rows // tm,),
        in_specs=[
            pl.BlockSpec((tm, d), lambda i: (i, 0)),
            pl.BlockSpec((None, nb, d), lambda i: (lat(i), 0, 0)),
            pl.BlockSpec((None, nb, d), lambda i: (lat(i), 0, 1)),
            pl.BlockSpec((None, d, D_PROJ), lambda i: (layer, 0, 0)),
        ],
        out_specs=pl.BlockSpec((tm, D_PROJ), lambda i: (i, 0)),
        out_shape=jax.ShapeDtypeStruct((rows, D_PROJ), F32),
        compiler_params=_cparams(("parallel",)),
        name="ln_mod_inproj",
    )(h, mod_rows, mod_rows, w_in2)


def _conv_tile(x, prev, nxt, w, nb, use_prev, use_next):
    prev = jnp.where(use_prev, prev, 0.0)
    nxt = jnp.where(use_next, nxt, 0.0)
    xm2 = jnp.concatenate([prev, x[: -2 * nb]], axis=0)
    xm1 = jnp.concatenate([prev[nb:], x[:-nb]], axis=0)
    xp1 = jnp.concatenate([x[nb:], nxt], axis=0)
    return xm2 * w[0:1] + xm1 * w[1:2] + x * w[2:3] + xp1 * w[3:4]


def _halo_flags(i, n_ctx_tiles):
    use_prev = jnp.logical_and(i > 0, i < n_ctx_tiles)
    use_next = i < n_ctx_tiles - 1
    return use_prev, use_next


def _qkv_prep_kernel(nb, n_ctx_tiles, x_ref, prev_ref, next_ref, w_ref, o_ref):
    i = pl.program_id(0)
    part = pl.program_id(1)
    use_prev, use_next = _halo_flags(i, n_ctx_tiles)
    y = _silu(_conv_tile(x_ref[...], prev_ref[...], next_ref[...], w_ref[...], nb, use_prev, use_next))
    qscale = jnp.where(part == 0, DN_HEAD_DIM ** -0.5, 1.0).astype(F32)
    for hh in range(DN_HEADS):
        yh = y[:, hh * DN_HEAD_DIM:(hh + 1) * DN_HEAD_DIM]
        inv = lax.rsqrt(jnp.sum(yh * yh, axis=-1, keepdims=True) + NORM_EPS) * qscale
        o_ref[hh] = yh * jnp.where(part < 2, inv, 1.0)


def _halo_specs(tile_rows, nb, cols, col_of, n_row_blocks16, tile_of=lambda i: i):
    prev_b = tile_rows // (2 * nb)
    next_b = tile_rows // nb
    return [
        pl.BlockSpec((tile_rows, cols), lambda i, *a: (tile_of(i), col_of(*a))),
        pl.BlockSpec((2 * nb, cols), lambda i, *a: (jnp.maximum(tile_of(i) * prev_b - 1, 0), col_of(*a))),
        pl.BlockSpec((nb, cols),
                     lambda i, *a: (jnp.minimum((tile_of(i) + 1) * next_b, n_row_blocks16 - 1), col_of(*a))),
    ]


def _qkv_prep(proj, conv_w, nb, n_ctx_tiles):
    rows = proj.shape[0]
    tr = CHUNK * nb
    return pl.pallas_call(
        functools.partial(_qkv_prep_kernel, nb, n_ctx_tiles),
        grid=(rows // tr, 3),
        in_specs=_halo_specs(tr, nb, DN_WIDTH, lambda p: p, rows // nb)
        + [pl.BlockSpec((CONV_K, DN_WIDTH), lambda i, p: (0, p))],
        out_specs=pl.BlockSpec((DN_HEADS, tr, DN_HEAD_DIM), lambda i, p: (p, i, 0)),
        out_shape=jax.ShapeDtypeStruct((3 * DN_HEADS, rows, DN_HEAD_DIM), F32),
        compiler_params=_cparams(("parallel", "parallel")),
        name="qkv_conv_norm",
    )(proj, proj, proj, conv_w)


def _bwd_chunk(i, n_ctx, n_tot):
    return jnp.where(i < n_ctx, n_ctx - 1 - i, n_tot + n_ctx - 1 - i)


def _rglru_kernel(nb, n_ctx_tiles, n_tot, xf_ref, pf_ref, nf_ref, xb_ref, pb_ref, nb_ref, cw_ref, cb_ref,
                  wg0_ref, wg1_ref, bg0_ref, bg1_ref, lam_ref, hf_ref, hb_ref, a_scr, b_scr, st_ref):
    i = pl.program_id(0)

    @pl.when(i == 0)
    def _():
        st_ref[...] = jnp.zeros_like(st_ref)

    w = RG_WIDTH
    sp = jax.nn.softplus(-lam_ref[...])
    tiles = (i, _bwd_chunk(i, n_ctx_tiles, n_tot))
    for d, (x_ref, p_ref, n_ref, wg_ref, bg_ref) in enumerate(
            ((xf_ref, pf_ref, nf_ref, wg0_ref, bg0_ref), (xb_ref, pb_ref, nb_ref, wg1_ref, bg1_ref))):
        use_prev, use_next = _halo_flags(tiles[d], n_ctx_tiles)
        xc = _conv_tile(x_ref[...], p_ref[...], n_ref[...], cw_ref[...], nb, use_prev, use_next) + cb_ref[...]
        gates = jnp.dot(xc.astype(BF16), wg_ref[...], preferred_element_type=F32) + bg_ref[...]
        ig = _sigmoid(gates[:, w:])
        c = (-0.5 * RG_C) * sp[d:d + 1]
        a = jnp.exp(c * jnp.tanh(0.5 * gates[:, :w]) + c)
        y = jnp.maximum(1.0 - a * a, 0.0)
        a_scr[d] = a
        b_scr[d] = jnp.where(y > 0.0, y * lax.rsqrt(y), 0.0) * (ig * xc)

    def body(t, carry):
        h0, h1 = carry
        r0 = pl.ds(pl.multiple_of(t * nb, nb), nb)
        r1 = pl.ds(pl.multiple_of((CHUNK - 1 - t) * nb, nb), nb)
        h0 = a_scr[0, r0, :] * h0 + b_scr[0, r0, :]
        h1 = a_scr[1, r1, :] * h1 + b_scr[1, r1, :]
        hf_ref[r0, :] = h0.astype(hf_ref.dtype)
        hb_ref[r1, :] = h1.astype(hb_ref.dtype)
        return h0, h1

    h0, h1 = lax.fori_loop(0, CHUNK, body, (st_ref[0], st_ref[1]), unroll=8)
    st_ref[0] = h0
    st_ref[1] = h1


def _rglru(proj, conv_w, conv_b, wg, bg, lam, nb, n_ctx_tiles):
    rows = proj.shape[0]
    tr = CHUNK * nb
    n_tot = rows // tr
    w = RG_WIDTH
    out = jax.ShapeDtypeStruct((rows, w), BF16)
    fwd = lambda i: i
    bwd = lambda i: _bwd_chunk(i, n_ctx_tiles, n_tot)
    full = lambda shape: pl.BlockSpec(shape, lambda i: (0,) * len(shape))
    return pl.pallas_call(
        functools.partial(_rglru_kernel, nb, n_ctx_tiles, n_tot),
        grid=(n_tot,),
        in_specs=_halo_specs(tr, nb, w, lambda: COL_RX, rows // nb, fwd)
        + _halo_specs(tr, nb, w, lambda: COL_RX, rows // nb, bwd)
        + [full((CONV_K, w)), full((1, w)),
           pl.BlockSpec((w, 2 * w), lambda i: (0, 0)), pl.BlockSpec((w, 2 * w), lambda i: (0, 1)),
           pl.BlockSpec((1, 2 * w), lambda i: (0, 0)), pl.BlockSpec((1, 2 * w), lambda i: (0, 1)),
           full((2, w))],
        out_specs=[pl.BlockSpec((tr, w), lambda i: (fwd(i), 0)), pl.BlockSpec((tr, w), lambda i: (bwd(i), 0))],
        out_shape=[out] * 2,
        scratch_shapes=[pltpu.VMEM((2, tr, w), F32), pltpu.VMEM((2, tr, w), F32), pltpu.VMEM((2, nb, w), F32)],
        compiler_params=_cparams(("arbitrary",)),
        name="rglru",
    )(proj, proj, proj, proj, proj, proj, conv_w, conv_b, wg, wg, bg, bg, lam)


def _bdot(a, b):
    return jnp.dot(a.astype(BF16), b.astype(BF16), preferred_element_type=F32)


def _deltanet_gates(d, raw, alog, dtb):
    c = CHUNK
    row = lax.broadcasted_iota(jnp.int32, (c, c), 0)
    col = lax.broadcasted_iota(jnp.int32, (c, c), 1)
    incl = (row >= col) if d == 0 else (row <= col)
    g_all = -jnp.exp(alog) * jax.nn.softplus(raw + dtb)
    gc_all = jnp.dot(incl.astype(F32), g_all, precision=HIGHEST, preferred_element_type=F32)
    g_tot = jnp.sum(g_all, axis=0, keepdims=True)
    return dict(incl=incl, strict=(row > col) if d == 0 else (row < col),
                diag_blk=(row // SOLVE_BLOCK) == (col // SOLVE_BLOCK),
                beta=_sigmoid(raw), gc=gc_all, gc_t=gc_all.T, egc=jnp.exp(gc_all),
                kdec=jnp.exp(g_tot - gc_all), glast=jnp.exp(g_tot))


def _deltanet_heads(gts, cbs, qs, ks, vs, ss):
    c = CHUNK
    dh = DN_HEAD_DIM
    n = len(cbs)
    rng = range(n)
    cgs = [2 * DN_HEADS + cb for cb in cbs]
    beta = [gts[i]["beta"][:, cbs[i]:cbs[i] + 1] for i in rng]
    egc = [gts[i]["egc"][:, cgs[i]:cgs[i] + 1] for i in rng]
    decay = [jnp.exp(jnp.where(gts[i]["incl"], gts[i]["gc"][:, cgs[i]:cgs[i] + 1] - gts[i]["gc_t"][cgs[i]:cgs[i] + 1, :],
                               -1e30)) for i in rng]
    kbeta = [ks[i] * beta[i] for i in rng]
    prod = [lax.dot_general(jnp.concatenate([kbeta[i], qs[i]], axis=0).astype(BF16), ks[i].astype(BF16),
                            (((1,), (1,)), ((), ())), preferred_element_type=F32) for i in rng]
    a_mat = [jnp.where(gts[i]["strict"], prod[i][:c] * decay[i], 0.0) for i in rng]
    intra = [prod[i][c:] * decay[i] for i in rng]
    ks_qs = [_bdot(jnp.concatenate([kbeta[i] * egc[i], qs[i] * egc[i]], axis=0), ss[i]) for i in rng]
    p = [jnp.where(gts[i]["diag_blk"], -a_mat[i], 0.0) for i in rng]
    y = [jnp.concatenate([vs[i] * beta[i] - ks_qs[i][:c], jnp.where(gts[i]["diag_blk"], 0.0, a_mat[i])], axis=1)
         for i in rng]
    for j in range(4):
        y = [y[i] + _bdot(p[i], y[i]) for i in rng]
        if j < 3:
            p = [_bdot(p[i], p[i]) for i in rng]
    e = [y[i][:, dh:] for i in rng]
    y = [y[i][:, :dh] for i in rng]
    e2 = [_bdot(e[i], e[i]) for i in rng]
    y = [y[i] + _bdot(e2[i], y[i]) for i in rng]
    v_new = [y[i] - _bdot(e[i], y[i]) for i in rng]
    o = [ks_qs[i][c:] + _bdot(intra[i], v_new[i]) for i in rng]
    s_new = [ss[i] * gts[i]["glast"][:, cgs[i]:cgs[i] + 1] + lax.dot_general(
        (ks[i] * gts[i]["kdec"][:, cgs[i]:cgs[i] + 1]).astype(BF16), v_new[i].astype(BF16),
        (((0,), (0,)), ((), ())), preferred_element_type=F32) for i in rng]
    return o, s_new


def _deltanet_kernel(nb, qf_ref, kf_ref, vf_ref, gf_ref, qb_ref, kb_ref, vb_ref, gb_ref, alog_ref, dtb_ref,
                     of_ref, ob_ref, s_ref):
    @pl.when(pl.program_id(0) == 0)
    def _():
        s_ref[...] = jnp.zeros_like(s_ref)

    dirs = ((qf_ref, kf_ref, vf_ref, gf_ref, of_ref), (qb_ref, kb_ref, vb_ref, gb_ref, ob_ref))

    def body(it, carry):
        idx = [(it * DN_BATCHES + j, d, hh) for j in range(DN_BATCHES) for d in range(2) for hh in range(DN_HEADS)]
        rows_of = {j: pl.ds(it * DN_BATCHES + j, CHUNK, stride=nb) for j in range(DN_BATCHES)}
        rows_b = [rows_of[i // (2 * DN_HEADS)] for i in range(len(idx))]
        cbs = [d * DN_HEADS + hh for _, d, hh in idx]
        gates = {(j, d): _deltanet_gates(d, dirs[d][3][rows_of[j], :], alog_ref[...], dtb_ref[...])
                 for j in range(DN_BATCHES) for d in range(2)}
        gts = [gates[(i // (2 * DN_HEADS), d)] for i, (_, d, _) in enumerate(idx)]
        qs = [dirs[d][0][hh, rows_b[i], :] for i, (_, d, hh) in enumerate(idx)]
        ks = [dirs[d][1][hh, rows_b[i], :] for i, (_, d, hh) in enumerate(idx)]
        vs = [dirs[d][2][hh, rows_b[i], :] for i, (_, d, hh) in enumerate(idx)]
        ss = [s_ref[b * (2 * DN_HEADS) + cbs[i]] for i, (b, _, _) in enumerate(idx)]
        o, s_new = _deltanet_heads(gts, cbs, qs, ks, vs, ss)
        for i, (b, d, hh) in enumerate(idx):
            dirs[d][4][hh, rows_b[i], :] = o[i]
            s_ref[b * (2 * DN_HEADS) + cbs[i]] = s_new[i]
        return carry

    lax.fori_loop(0, nb // DN_BATCHES, body, 0)


def _deltanet(qkv, proj, alog_vec, dtb_vec, nb, n_ctx_chunks):
    rows = qkv.shape[1]
    tr = CHUNK * nb
    n_tot = rows // tr
    fwd = lambda i: i
    bwd = lambda i: _bwd_chunk(i, n_ctx_chunks, n_tot)

    def specs(order):
        return [pl.BlockSpec((DN_HEADS, tr, DN_HEAD_DIM), lambda i, p=p: (p, order(i), 0)) for p in range(3)] + [
            pl.BlockSpec((tr, LANES), lambda i: (order(i), COL_GATE))]

    vec = pl.BlockSpec((1, LANES), lambda i: (0, 0))
    out = jax.ShapeDtypeStruct((DN_HEADS, rows, DN_HEAD_DIM), F32)
    return pl.pallas_call(
        functools.partial(_deltanet_kernel, nb),
        grid=(n_tot,),
        in_specs=specs(fwd) + specs(bwd) + [vec, vec],
        out_specs=[pl.BlockSpec((DN_HEADS, tr, DN_HEAD_DIM), lambda i: (0, fwd(i), 0)),
                   pl.BlockSpec((DN_HEADS, tr, DN_HEAD_DIM), lambda i: (0, bwd(i), 0))],
        out_shape=[out, out],
        scratch_shapes=[pltpu.VMEM((nb * 2 * DN_HEADS, DN_HEAD_DIM, DN_HEAD_DIM), F32)],
        compiler_params=_cparams(("arbitrary",)),
        name="deltanet",
    )(qkv, qkv, qkv, proj, qkv, qkv, qkv, proj, alog_vec, dtb_vec)


def _mixer_x1(nb, of_ref, ob_ref, z_ref, hf_ref, hb_ref, ry_ref, onw_ref, w_ref, h_ref, gt_ref, g_ref, b_ref):
    z = z_ref[...]
    parts = []
    for hh in range(DN_HEADS):
        sl = slice(hh * DN_HEAD_DIM, (hh + 1) * DN_HEAD_DIM)
        oh = of_ref[hh] + ob_ref[hh]
        inv = lax.rsqrt(jnp.mean(oh * oh, axis=-1, keepdims=True) + NORM_EPS)
        parts.append(oh * inv * onw_ref[...] * _silu(z[:, sl]))
    parts.append((hf_ref[...].astype(F32) + hb_ref[...].astype(F32)) * jax.nn.gelu(ry_ref[...]))
    y = jnp.concatenate(parts, axis=1).astype(BF16)
    u = jnp.dot(y, w_ref[...], preferred_element_type=F32)
    r = DEEP_ALPHA * h_ref[...] + _per_batch(u, gt_ref[...], nb) * u
    return _layer_norm(r) * g_ref[...] + b_ref[...]


PAIRS_PER_GROUP = EXPERTS_PER_GROUP * (EXPERTS_PER_GROUP - 1) // 2
N_CLASSES = N_GROUPS * PAIRS_PER_GROUP
assert N_CLASSES <= LANES
INFO_CLASS, INFO_RANK, INFO_W_LO, INFO_W_HI = range(4)
Y_SUB = 2 * (D_MODEL // LANES)


ROW_SUB = D_MODEL // LANES


def _store_row_tiles(ref, x):
    n = x.shape[0]
    for c in range(ROW_SUB):
        ref[pl.ds(c, n, stride=ROW_SUB), :] = x[:, c * LANES:(c + 1) * LANES]


def _load_row_tiles(ref, n):
    return jnp.concatenate([ref[pl.ds(c, n, stride=ROW_SUB), :] for c in range(ROW_SUB)], axis=1)


def _first_index(mask, lane):
    return jnp.min(jnp.where(mask, lane, LANES), axis=-1, keepdims=True)


def _route(nb, x, sh_ref, sc_ref, wr_ref, br_ref, hm_ref, info_ref, cnt_ref, run_ref):
    @pl.when(pl.program_id(0) == 0)
    def _():
        run_ref[...] = jnp.zeros_like(run_ref)

    xn = _layer_norm(x)
    hm = xn * (1.0 + _per_batch(xn, sc_ref[...], nb)) + _per_batch(xn, sh_ref[...], nb)
    _store_row_tiles(hm_ref, hm)
    tm = hm.shape[0]
    h_hi = hm.astype(BF16)
    h_lo = (hm - h_hi.astype(F32)).astype(BF16)
    hi_hl = jnp.dot(h_hi, wr_ref[...], preferred_element_type=F32)
    logits = (hi_hl[:, :LANES] + hi_hl[:, LANES:] + jnp.dot(h_lo, wr_ref[:, :LANES], preferred_element_type=F32)
              + br_ref[...])
    lane = lax.broadcasted_iota(jnp.int32, (tm, LANES), 1)
    neg = -jnp.inf
    is_g = lane < N_GROUPS
    gmax = jnp.max(jnp.where(is_g, logits, neg), axis=-1, keepdims=True)
    g_sel = _first_index(jnp.logical_and(is_g, logits == gmax), lane)
    p_group = 1.0 / jnp.sum(jnp.where(is_g, jnp.exp(logits - gmax), 0.0), axis=-1, keepdims=True)
    lo = N_GROUPS + EXPERTS_PER_GROUP * g_sel
    in_grp = jnp.logical_and(lane >= lo, lane < lo + EXPERTS_PER_GROUP)
    m1 = jnp.max(jnp.where(in_grp, logits, neg), axis=-1, keepdims=True)
    i1 = _first_index(jnp.logical_and(in_grp, logits == m1), lane)
    rest = jnp.logical_and(in_grp, lane != i1)
    m2 = jnp.max(jnp.where(rest, logits, neg), axis=-1, keepdims=True)
    i2 = _first_index(jnp.logical_and(rest, logits == m2), lane)
    e2 = jnp.exp(m2 - m1)
    w1 = p_group / (1.0 + e2)
    w2 = p_group * e2 / (1.0 + e2)
    l1 = i1 - lo
    l2 = i2 - lo
    e_lo = jnp.minimum(l1, l2)
    e_hi = jnp.maximum(l1, l2)
    pair = jnp.right_shift(e_lo * (2 * EXPERTS_PER_GROUP - 1 - e_lo), 1) + (e_hi - e_lo - 1)
    cls = g_sel * PAIRS_PER_GROUP + pair
    first_is_lo = l1 < l2
    w_lo = jnp.where(first_is_lo, w1, w2)
    w_hi = jnp.where(first_is_lo, w2, w1)
    oh = lane == cls
    r_i = lax.broadcasted_iota(jnp.int32, (tm, tm), 0)
    c_i = lax.broadcasted_iota(jnp.int32, (tm, tm), 1)
    before = jnp.dot((r_i > c_i).astype(BF16), oh.astype(BF16), preferred_element_type=F32) + run_ref[...]
    rank = jnp.sum(jnp.where(oh, before, 0.0), axis=-1, keepdims=True)
    run_ref[...] = run_ref[...] + jnp.sum(oh.astype(F32), axis=0, keepdims=True)
    cnt_ref[...] = run_ref[...]
    info = jnp.zeros((tm, LANES), F32)
    for idx, val in ((INFO_CLASS, cls.astype(F32)), (INFO_RANK, rank), (INFO_W_LO, w_lo), (INFO_W_HI, w_hi)):
        info = jnp.where(lane == idx, val, info)
    info_ref[...] = info


def _mixer_router_kernel(nb, of_ref, ob_ref, z_ref, hf_ref, hb_ref, ry_ref, onw_ref, w_ref, h_ref, gt_ref, g_ref, b_ref,
                         sh_ref, sc_ref, wr_ref, br_ref, x1_ref, hm_ref, info_ref, cnt_ref, run_ref):
    x1 = _mixer_x1(nb, of_ref, ob_ref, z_ref, hf_ref, hb_ref, ry_ref, onw_ref, w_ref, h_ref, gt_ref, g_ref, b_ref)
    x1_ref[...] = x1
    _route(nb, x1, sh_ref, sc_ref, wr_ref, br_ref, hm_ref, info_ref, cnt_ref, run_ref)


def _mixer_router(o_f, o_b, proj, hf, hb, onorm_w, w_out, layer, h, mod_rows, ln_g, ln_b, wr, br, nb, ctx_rows, row0):
    rows, d = h.shape
    tm = ROW_TILE
    t0 = row0 // tm
    n = rows - row0
    lat = lambda i: jnp.where((i + t0) * tm >= ctx_rows, 1, 0)
    half = pl.BlockSpec((tm, DN_WIDTH), lambda i: (i + t0, 0))
    planes = pl.BlockSpec((DN_HEADS, tm, DN_HEAD_DIM), lambda i: (0, i + t0, 0))
    vec = lambda m: pl.BlockSpec((1, m), lambda i: (0, 0))
    mod = lambda k: pl.BlockSpec((None, nb, d), lambda i: (lat(i), 0, k))
    return pl.pallas_call(
        functools.partial(_mixer_router_kernel, nb),
        grid=(n // tm,),
        in_specs=[planes, planes, pl.BlockSpec((tm, DN_WIDTH), lambda i: (i + t0, COL_Z)), half, half,
                  pl.BlockSpec((tm, RG_WIDTH), lambda i: (i + t0, COL_RY)),
                  vec(DN_HEAD_DIM), pl.BlockSpec((None, d, d), lambda i: (layer, 0, 0)),
                  pl.BlockSpec((tm, d), lambda i: (i + t0, 0)),
                  mod(2),
                  vec(d), vec(d),
                  mod(3), mod(4),
                  pl.BlockSpec((d, 2 * LANES), lambda i: (0, 0)), vec(LANES)],
        out_specs=[pl.BlockSpec((tm, d), lambda i: (i, 0)),
                   pl.BlockSpec((tm * ROW_SUB, LANES), lambda i: (i, 0)),
                   pl.BlockSpec((tm, LANES), lambda i: (i, 0)),
                   pl.BlockSpec((1, LANES), lambda i: (0, 0))],
        out_shape=[jax.ShapeDtypeStruct((n, d), F32),
                   jax.ShapeDtypeStruct((n * ROW_SUB, LANES), F32),
                   jax.ShapeDtypeStruct((n, LANES), F32),
                   jax.ShapeDtypeStruct((1, LANES), F32)],
        scratch_shapes=[pltpu.VMEM((1, LANES), F32)],
        compiler_params=_cparams(("arbitrary",)),
        name="mixer_router",
    )(o_f, o_b, proj, hf, hb, proj, onorm_w, w_out, h, mod_rows, ln_g, ln_b, mod_rows, mod_rows, wr, br)


def _dispatch_kernel(tmx, ends_ref, cnt_ref, dst_ref, hm_ref, xs_hbm, zero_scr, sem, zsem):
    rs = ROW_SUB
    tb = dst_ref.shape[1]

    @pl.when(pl.program_id(0) == 0)
    def _():
        zero_scr[...] = jnp.zeros_like(zero_scr)

        def tail_copy(c):
            start = pl.multiple_of((ends_ref[c] - tmx) * rs, tmx * rs)
            return pltpu.make_async_copy(zero_scr, xs_hbm.at[pl.ds(start, tmx * rs)], zsem)

        def start_one(c, carry):
            @pl.when(cnt_ref[c] > 0)
            def _():
                tail_copy(c).start()
            return carry

        def wait_one(c, carry):
            @pl.when(cnt_ref[c] > 0)
            def _():
                tail_copy(c).wait()
            return carry

        lax.fori_loop(0, N_CLASSES, start_one, 0)

        def spare_copy(j):
            return pltpu.make_async_copy(zero_scr, xs_hbm.at[pl.ds(pl.multiple_of(j * (tmx * rs), tmx * rs), tmx * rs)],
                                         zsem)

        n_used = ends_ref[N_CLASSES - 1] // tmx
        n_tiles = xs_hbm.shape[0] // (tmx * rs)
        lax.fori_loop(n_used, n_tiles, lambda j, carry: (spare_copy(j).start(), carry)[1], 0)
        lax.fori_loop(0, N_CLASSES, wait_one, 0)
        lax.fori_loop(n_used, n_tiles, lambda j, carry: (spare_copy(j).wait(), carry)[1], 0)

    for r in range(tb):
        dst = xs_hbm.at[pl.ds(pl.multiple_of(dst_ref[0, r], rs), rs)]
        pltpu.make_async_copy(hm_ref.at[pl.ds(r * rs, rs)], dst, sem).start(priority=r % 2)
    pltpu.make_async_copy(hm_ref, hm_ref, sem).wait()


def _dispatch(hm, dest, class_ends, class_cnt, sorted_rows, tmx):
    tb = ROW_TILE
    n_blocks = dest.shape[0] // tb
    grid_spec = pltpu.PrefetchScalarGridSpec(
        num_scalar_prefetch=2,
        grid=(n_blocks,),
        in_specs=[pl.BlockSpec((None, 1, tb), lambda i, e, c: (i, 0, 0), memory_space=pltpu.SMEM),
                  pl.BlockSpec((tb * ROW_SUB, LANES), lambda i, e, c: (i, 0))],
        out_specs=pl.BlockSpec(memory_space=pl.ANY),
        scratch_shapes=[pltpu.VMEM((tmx * ROW_SUB, LANES), F32), pltpu.SemaphoreType.DMA(()),
                        pltpu.SemaphoreType.DMA(())],
    )
    return pl.pallas_call(
        functools.partial(_dispatch_kernel, tmx),
        grid_spec=grid_spec,
        out_shape=jax.ShapeDtypeStruct((sorted_rows * ROW_SUB, LANES), F32),
        compiler_params=_cparams(("arbitrary",)),
        name="moe_dispatch",
    )(class_ends, class_cnt, (dest * ROW_SUB).reshape(n_blocks, 1, tb), hm)


def _class_ffn_kernel(telo_ref, tehi_ref, nact_ref, xs_ref, wgul_ref, wdl_ref, wguh_ref, wdh_ref, ys_ref, y_scr):
    j = pl.program_id(0)
    tmx = xs_ref.shape[0] // ROW_SUB
    de = wdl_ref.shape[0]

    @pl.when(j < nact_ref[0])
    def _():
        x = _load_row_tiles(xs_ref, tmx).astype(BF16)
        for half, (wgu_ref, wd_ref) in enumerate(((wgul_ref, wdl_ref), (wguh_ref, wdh_ref))):
            ab = jnp.dot(x, wgu_ref[...], preferred_element_type=F32)
            y = jnp.dot((_silu(ab[:, :de]) * ab[:, de:]).astype(BF16), wd_ref[...], preferred_element_type=F32)
            for c in range(ROW_SUB):
                y_scr[pl.ds(half * ROW_SUB + c, tmx, stride=Y_SUB), :] = y[:, c * LANES:(c + 1) * LANES]
        ys_ref[...] = y_scr[...].astype(BF16)

    @pl.when(j >= nact_ref[0])
    def _():
        ys_ref[...] = jnp.zeros_like(ys_ref)


def _class_ffn(xs, tile_e_lo, tile_e_hi, n_active, w_gate_up, w_down, layer, tmx):
    n_tiles = xs.shape[0] // (tmx * ROW_SUB)
    d, de2 = w_gate_up.shape[-2:]
    w_in = lambda te_idx: pl.BlockSpec((None, None, d, de2), lambda j, *te: (layer, te[te_idx][j], 0, 0))
    w_out = lambda te_idx: pl.BlockSpec((None, None, de2 // 2, d), lambda j, *te: (layer, te[te_idx][j], 0, 0))
    grid_spec = pltpu.PrefetchScalarGridSpec(
        num_scalar_prefetch=3,
        grid=(n_tiles,),
        in_specs=[pl.BlockSpec((tmx * ROW_SUB, LANES), lambda j, lo, hi, na: (jnp.minimum(j, na[0] - 1), 0)),
                  w_in(0), w_out(0), w_in(1), w_out(1)],
        out_specs=pl.BlockSpec((tmx * Y_SUB, LANES), lambda j, lo, hi, na: (j, 0)),
        scratch_shapes=[pltpu.VMEM((tmx * Y_SUB, LANES), F32)],
    )
    return pl.pallas_call(
        _class_ffn_kernel,
        grid_spec=grid_spec,
        out_shape=jax.ShapeDtypeStruct((n_tiles * tmx * Y_SUB, LANES), BF16),
        compiler_params=_cparams(("arbitrary",)),
        name="expert_ffn",
    )(tile_e_lo, tile_e_hi, n_active, xs, w_gate_up, w_down, w_gate_up, w_down)


def _moe_out_kernel(nb, batch_major, src0_ref, srcn_ref, ys_hbm, info_ref, x_ref, gt_ref, g_ref, b_ref, o_ref,
                    ybuf, y_scr, sem):
    i = pl.program_id(0)
    tm = info_ref.shape[0]
    slot = i % 2
    nslot = 1 - slot

    def gather_all(src_ref, s):
        for r in range(tm):
            pltpu.make_async_copy(ys_hbm.at[pl.ds(pl.multiple_of(src_ref[0, r], Y_SUB), Y_SUB)],
                                  ybuf.at[s, pl.ds(r * Y_SUB, Y_SUB)], sem.at[s]).start(priority=r % 2)

    def wait_rows(s):
        pltpu.make_async_copy(ybuf.at[s], ybuf.at[s], sem.at[s]).wait()

    @pl.when(i == 0)
    def _():
        gather_all(src0_ref, 0)

    wait_rows(slot)
    gather_all(srcn_ref, nslot)
    info = info_ref[...]
    y_scr[...] = ybuf[slot].astype(F32)
    y_lo = jnp.concatenate([y_scr[pl.ds(c, tm, stride=Y_SUB), :] for c in range(ROW_SUB)], axis=1)
    y_hi = jnp.concatenate([y_scr[pl.ds(ROW_SUB + c, tm, stride=Y_SUB), :] for c in range(ROW_SUB)], axis=1)
    f = y_lo * info[:, INFO_W_LO:INFO_W_LO + 1] + y_hi * info[:, INFO_W_HI:INFO_W_HI + 1]
    r = DEEP_ALPHA * x_ref[...] + _per_batch(f, gt_ref[...], nb) * f
    res = _layer_norm(r) * g_ref[...] + b_ref[...]
    if batch_major:
        for c in range(ROW_SUB):
            y_scr[pl.ds(c * tm, tm), :] = res[:, c * LANES:(c + 1) * LANES]
        for b in range(nb):
            for c in range(ROW_SUB):
                o_ref[b, :, c * LANES:(c + 1) * LANES] = y_scr[pl.ds(c * tm + b, tm // nb, stride=nb), :]
    else:
        o_ref[...] = res

    @pl.when(i == pl.num_programs(0) - 1)
    def _():
        wait_rows(nslot)


def _moe_out(ys, dest, info, x1, mod_rows, ln_g, ln_b, nb, ctx_rows, batch_major):
    rows, d = x1.shape
    tm = ROW_TILE
    n_blocks = rows // tm
    if batch_major:
        out_spec = pl.BlockSpec((nb, tm // nb, d), lambda i: (0, i, 0))
        out_shape = jax.ShapeDtypeStruct((nb, rows // nb, d), F32)
    else:
        out_spec = pl.BlockSpec((tm, d), lambda i: (i, 0))
        out_shape = jax.ShapeDtypeStruct((rows, d), F32)
    lat = lambda i: jnp.where(i * tm >= ctx_rows, 1, 0)
    vec = pl.BlockSpec((1, d), lambda i: (0, 0))
    src = (dest * Y_SUB).reshape(n_blocks, 1, tm)
    smem = lambda f: pl.BlockSpec((None, 1, tm), f, memory_space=pltpu.SMEM)
    return pl.pallas_call(
        functools.partial(_moe_out_kernel, nb, batch_major),
        grid=(n_blocks,),
        in_specs=[smem(lambda i: (0, 0, 0)),
                  smem(lambda i: (jnp.minimum(i + 1, n_blocks - 1), 0, 0)),
                  pl.BlockSpec(memory_space=pl.ANY),
                  pl.BlockSpec((tm, LANES), lambda i: (i, 0)),
                  pl.BlockSpec((tm, d), lambda i: (i, 0)),
                  pl.BlockSpec((None, nb, d), lambda i: (lat(i), 0, 5)),
                  vec, vec],
        out_specs=out_spec,
        out_shape=out_shape,
        scratch_shapes=[pltpu.VMEM((2, tm * Y_SUB, LANES), BF16), pltpu.VMEM((tm * Y_SUB, LANES), F32),
                        pltpu.SemaphoreType.DMA((2,))],
        compiler_params=_cparams(("arbitrary",)),
        name="moe_out",
    )(src, src, ys, info, x1, mod_rows, ln_g, ln_b)


def _class_experts():
    lo, hi = [], []
    for g in range(N_GROUPS):
        for a in range(EXPERTS_PER_GROUP):
            for b in range(a + 1, EXPERTS_PER_GROUP):
                lo.append(g * EXPERTS_PER_GROUP + a)
                hi.append(g * EXPERTS_PER_GROUP + b)
    return np.asarray(lo, np.int32), np.asarray(hi, np.int32)


def _routing_tables(info, counts, n_tiles, tmx):
    cnt = counts[0, :N_CLASSES].astype(jnp.int32)
    padded = ((cnt + tmx - 1) // tmx) * tmx
    ends = jnp.cumsum(padded)
    starts = ends - padded
    cls = info[:, INFO_CLASS].astype(jnp.int32)
    class_ids = jnp.arange(N_CLASSES, dtype=jnp.int32)
    start_of_tok = jnp.sum(jnp.where(cls[:, None] == class_ids[None, :], starts[None, :], 0), axis=1)
    dest = start_of_tok + info[:, INFO_RANK].astype(jnp.int32)
    tile_start = jnp.arange(n_tiles, dtype=jnp.int32) * tmx
    n_before = jnp.sum((ends[None, :] <= tile_start[:, None]).astype(jnp.int32), axis=1)
    tile_cls = jnp.minimum(n_before, N_CLASSES - 1)
    e_lo, e_hi = _class_experts()
    n_active = (ends[-1:] // tmx).astype(jnp.int32)
    return dest, ends, cnt, jnp.asarray(e_lo)[tile_cls], jnp.asarray(e_hi)[tile_cls], n_active


def _relayout_w_in(w_in):
    qkvz = w_in[..., :4 * DN_WIDTH]
    gates = w_in[..., 4 * DN_WIDTH:4 * DN_WIDTH + 4 * DN_HEADS]
    rxy = w_in[..., 4 * DN_WIDTH + 4 * DN_HEADS:]
    pad = jnp.zeros(w_in.shape[:-1] + (LANES - 4 * DN_HEADS,), w_in.dtype)
    return jnp.concatenate([qkvz, rxy, gates, pad], axis=-1).astype(BF16)


def _block_diag(w):
    n, c, _ = w.shape
    eye = jnp.eye(n, dtype=w.dtype)
    return (eye[:, None, :, None] * w[:, :, None, :]).reshape(n * c, n * c)


def _lane_vec(vals, offset):
    return jnp.zeros((1, LANES), F32).at[0, offset:offset + vals.size].set(vals.reshape(-1))


def kernel(x, c, ctx, c_ctx, w_ada, b_ada, w_in, conv_qkv_w, dn_a_log, dn_dt_bias, dn_onorm_w, rg_conv_w, rg_conv_b, rg_wa, rg_ba, rg_wi, rg_bi, rg_lambda, w_out, ln_g, ln_b, router_wg, router_bg, router_we, router_be, w_e_gate, w_e_up, w_e_down):
    nb, t_lat, d = x.shape
    t_ctx = ctx.shape[1]
    assert d == D_MODEL and nb % 8 == 0 and t_lat % GRID_W == 0 and t_ctx % CHUNK == 0
    assert (t_ctx * nb) % INPROJ_TILE == 0 and (t_lat * nb) % INPROJ_TILE == 0 and INPROJ_TILE % ROW_TILE == 0
    tt = t_ctx + t_lat
    rows = tt * nb
    ctx_rows = t_ctx * nb
    n_ctx_chunks = t_ctx // CHUNK

    h = jnp.concatenate([jnp.swapaxes(ctx, 0, 1), jnp.swapaxes(x, 0, 1)], axis=0).reshape(rows, d)

    n_cc = ((nb + 1 + 7) // 8) * 8
    cc = jnp.zeros((n_cc, d), F32).at[:nb].set(c).at[nb].set(c_ctx)
    mod = _modulation(cc, w_ada, b_ada)
    mod_ctx = jnp.broadcast_to(mod[:, nb:nb + 1], (DEPTH, nb, 6 * d))
    mod_rows = jnp.stack([mod_ctx, mod[:, :nb]], axis=1)

    w_in2 = _relayout_w_in(w_in)
    w_out_bf = w_out.astype(BF16)
    w_gate_up = jnp.concatenate([w_e_gate, w_e_up], axis=-1).astype(BF16)
    w_down = w_e_down.astype(BF16)
    for l in range(DEPTH):
        row0 = ctx_rows if l == DEPTH - 1 else 0
        moe_ctx_rows = ctx_rows - row0
        n_tiles = (rows - row0) // MOE_TILE + N_CLASSES
        mr = mod_rows[l]
        proj = _in_projection(h, mr, w_in2, l, nb, ctx_rows)
        qkv = _qkv_prep(proj, conv_qkv_w[l], nb, n_ctx_chunks)
        wg = jnp.concatenate([_block_diag(rg_wa[l, 0]), _block_diag(rg_wi[l, 0]),
                              _block_diag(rg_wa[l, 1]), _block_diag(rg_wi[l, 1])], axis=1).astype(BF16)
        bg = jnp.concatenate([rg_ba[l, 0], rg_bi[l, 0], rg_ba[l, 1], rg_bi[l, 1]])[None]
        hf, hb = _rglru(proj, rg_conv_w[l], rg_conv_b[l][None], wg, bg, rg_lambda[l], nb, n_ctx_chunks)
        alog_vec = _lane_vec(dn_a_log[l], 2 * DN_HEADS)
        dtb_vec = _lane_vec(dn_dt_bias[l], 2 * DN_HEADS)
        o_f, o_b = _deltanet(qkv, proj, alog_vec, dtb_vec, nb, n_ctx_chunks)
        wr = jnp.zeros((d, LANES), F32).at[:, :N_GROUPS].set(router_wg[l]).at[
            :, N_GROUPS:N_GROUPS + N_EXPERTS].set(router_we[l])
        br = _lane_vec(jnp.concatenate([router_bg[l], router_be[l]]), 0)
        wr_hi = wr.astype(BF16)
        wr_hl = jnp.concatenate([wr_hi, (wr - wr_hi.astype(F32)).astype(BF16)], axis=1)
        x1, hm, info, counts = _mixer_router(o_f, o_b, proj, hf, hb, dn_onorm_w[l][None], w_out_bf, l, h, mr,
                                             ln_g[l, 0][None], ln_b[l, 0][None], wr_hl, br, nb, ctx_rows, row0)
        dest, class_ends, class_cnt, tile_e_lo, tile_e_hi, n_active = _routing_tables(info, counts, n_tiles, MOE_TILE)
        xs = _dispatch(hm, dest, class_ends, class_cnt, n_tiles * MOE_TILE, MOE_TILE)
        ys = _class_ffn(xs, tile_e_lo, tile_e_hi, n_active, w_gate_up, w_down, l, MOE_TILE)
        h = _moe_out(ys, dest, info, x1, mr, ln_g[l, 1][None], ln_b[l, 1][None], nb, moe_ctx_rows,
                     batch_major=(l == DEPTH - 1))

    return h
```

```python
import functools
import math

import jax
import jax.numpy as jnp
import numpy as np
from jax import lax
from jax.experimental import pallas as pl
from jax.experimental.pallas import tpu as pltpu

F32 = jnp.float32
BF16 = jnp.bfloat16
HIGHEST = lax.Precision.HIGHEST

D_MODEL = 1024
DEPTH = 2
GRID_W = 64
DN_HEAD_DIM = 128
DN_WIDTH = 512
DN_HEADS = 4
RG_WIDTH = 512
RG_BLOCKS = 8
RG_BLOCK_DIM = 64
RG_C = 8.0
CONV_K = 4
N_GROUPS = 4
EXPERTS_PER_GROUP = 8
N_EXPERTS = 32
D_EXPERT = 256
DEEP_ALPHA = (2 * DEPTH) ** 0.25
LN_EPS = 1e-5
NORM_EPS = 1e-6

LANES = 128
CHUNK = 64
SOLVE_BLOCK = 16
DN_BATCHES = 4
D_PROJ = 3 * DN_WIDTH + DN_WIDTH + 2 * RG_WIDTH + LANES
COL_Z = 3
COL_RX = 4
COL_RY = 5
COL_GATE = (3 * DN_WIDTH + DN_WIDTH + 2 * RG_WIDTH) // LANES
ROW_TILE = 512
INPROJ_TILE = 1024
MOE_TILE = 256
VMEM_LIMIT = 56 * 1024 * 1024


def _cparams(sem):
    return pltpu.CompilerParams(dimension_semantics=sem, vmem_limit_bytes=VMEM_LIMIT)


def _layer_norm(x):
    mu = jnp.mean(x, axis=-1, keepdims=True)
    xc = x - mu
    var = jnp.mean(xc * xc, axis=-1, keepdims=True)
    return xc * lax.rsqrt(var + LN_EPS)


def _per_batch(x, v, nb):
    rows, c = x.shape
    return jnp.broadcast_to(v[None], (rows // nb, nb, c)).reshape(rows, c)


def _sigmoid(x):
    return 0.5 * jnp.tanh(0.5 * x) + 0.5


def _silu(x):
    h = 0.5 * x
    return h * jnp.tanh(h) + h


def _mod_kernel(cc_ref, w_ref, b_ref, o_ref):
    s = _silu(cc_ref[...])
    o_ref[...] = jnp.dot(s.astype(BF16), w_ref[...].astype(BF16), preferred_element_type=F32) + b_ref[...]


def _modulation(cc, w_ada, b_ada):
    depth, d, n6 = w_ada.shape
    rows = cc.shape[0]
    tn = 1536
    return pl.pallas_call(
        _mod_kernel,
        grid=(depth, n6 // tn),
        in_specs=[
            pl.BlockSpec((rows, d), lambda l, j: (0, 0)),
            pl.BlockSpec((None, d, tn), lambda l, j: (l, 0, j)),
            pl.BlockSpec((None, 1, tn), lambda l, j: (l, 0, j)),
        ],
        out_specs=pl.BlockSpec((None, rows, tn), lambda l, j: (l, 0, j)),
        out_shape=jax.ShapeDtypeStruct((depth, rows, n6), F32),
        compiler_params=_cparams(("parallel", "parallel")),
        name="adaln_mod",
    )(cc, w_ada, b_ada.reshape(depth, 1, n6))


def _inproj_kernel(nb, h_ref, sh_ref, sc_ref, w_ref, o_ref):
    xn = _layer_norm(h_ref[...])
    hm = xn * (1.0 + _per_batch(xn, sc_ref[...], nb)) + _per_batch(xn, sh_ref[...], nb)
    o_ref[...] = jnp.dot(hm.astype(BF16), w_ref[...], preferred_element_type=F32)


def _inproj_first_kernel(nb, n_ctx_tiles, ctx_ref, x_ref, sh_ref, sc_ref, w_ref, o_ref, h_ref, scr):
    tt = ctx_ref.shape[1]
    tm = nb * tt
    blk = jnp.where(pl.program_id(0) < n_ctx_tiles, ctx_ref[...], x_ref[...])
    for b in range(nb):
        for c in range(ROW_SUB):
            scr[pl.ds(c * tm + b, tt, stride=nb), :] = blk[b, :, c * LANES:(c + 1) * LANES]
    h = jnp.concatenate([scr[pl.ds(c * tm, tm), :] for c in range(ROW_SUB)], axis=1)
    h_ref[...] = h
    xn = _layer_norm(h)
    hm = xn * (1.0 + _per_batch(xn, sc_ref[...], nb)) + _per_batch(xn, sh_ref[...], nb)
    o_ref[...] = jnp.dot(hm.astype(BF16), w_ref[...], preferred_element_type=F32)


def _in_projection_first(ctx, x, mod_rows, w_in2, layer, nb):
    _, t_ctx, d = ctx.shape
    t_lat = x.shape[1]
    tm = ROW_TILE
    tt = tm // nb
    n_ctx_tiles = t_ctx // tt
    rows = (t_ctx + t_lat) * nb
    lat = lambda i: jnp.where(i >= n_ctx_tiles, 1, 0)
    return pl.pallas_call(
        functools.partial(_inproj_first_kernel, nb, n_ctx_tiles),
        grid=(rows // tm,),
        in_specs=[
            pl.BlockSpec((nb, tt, d), lambda i: (0, jnp.minimum(i, n_ctx_tiles - 1), 0)),
            pl.BlockSpec((nb, tt, d), lambda i: (0, jnp.maximum(i - n_ctx_tiles, 0), 0)),
            pl.BlockSpec((None, nb, d), lambda i: (lat(i), 0, 0)),
            pl.BlockSpec((None, nb, d), lambda i: (lat(i), 0, 1)),
            pl.BlockSpec((None, d, D_PROJ), lambda i: (layer, 0, 0)),
        ],
        out_specs=[pl.BlockSpec((tm, D_PROJ), lambda i: (i, 0)), pl.BlockSpec((tm, d), lambda i: (i, 0))],
        out_shape=[jax.ShapeDtypeStruct((rows, D_PROJ), F32), jax.ShapeDtypeStruct((rows, d), F32)],
        scratch_shapes=[pltpu.VMEM((ROW_SUB * tm, LANES), F32)],
        compiler_params=_cparams(("parallel",)),
        name="ln_mod_inproj_first",
    )(ctx, x, mod_rows, mod_rows, w_in2)


def _in_projection(h, mod_rows, w_in2, layer, nb, ctx_rows):
    rows, d = h.shape
    tm = INPROJ_TILE
    lat = lambda i: jnp.where(i * tm >= ctx_rows, 1, 0)
    return pl.pallas_call(
        functools.partial(_inproj_kernel, nb),
        grid=(rows // tm,),
        in_specs=[
            pl.BlockSpec((tm, d), lambda i: (i, 0)),
            pl.BlockSpec((None, nb, d), lambda i: (lat(i), 0, 0)),
            pl.BlockSpec((None, nb, d), lambda i: (lat(i), 0, 1)),
            pl.BlockSpec((None, d, D_PROJ), lambda i: (layer, 0, 0)),
        ],
        out_specs=pl.BlockSpec((tm, D_PROJ), lambda i: (i, 0)),
        out_shape=jax.ShapeDtypeStruct((rows, D_PROJ), F32),
        compiler_params=_cparams(("parallel",)),
        name="ln_mod_inproj",
    )(h, mod_rows, mod_rows, w_in2)


def _conv_tile(x, prev, nxt, w, nb, use_prev, use_next):
    prev = jnp.where(use_prev, prev, 0.0)
    nxt = jnp.where(use_next, nxt, 0.0)
    xm2 = jnp.concatenate([prev, x[: -2 * nb]], axis=0)
    xm1 = jnp.concatenate([prev[nb:], x[:-nb]], axis=0)
    xp1 = jnp.concatenate([x[nb:], nxt], axis=0)
    return xm2 * w[0:1] + xm1 * w[1:2] + x * w[2:3] + xp1 * w[3:4]


def _halo_flags(i, n_ctx_tiles):
    use_prev = jnp.logical_and(i > 0, i < n_ctx_tiles)
    use_next = i < n_ctx_tiles - 1
    return use_prev, use_next


def _qkv_prep_kernel(nb, n_ctx_tiles, x_ref, prev_ref, next_ref, w_ref, o_ref):
    i = pl.program_id(0)
    part = pl.program_id(1)
    use_prev, use_next = _halo_flags(i, n_ctx_tiles)
    y = _silu(_conv_tile(x_ref[...], prev_ref[...], next_ref[...], w_ref[...], nb, use_prev, use_next))
    qscale = jnp.where(part == 0, DN_HEAD_DIM ** -0.5, 1.0).astype(F32)
    for hh in range(DN_HEADS):
        yh = y[:, hh * DN_HEAD_DIM:(hh + 1) * DN_HEAD_DIM]
        inv = lax.rsqrt(jnp.sum(yh * yh, axis=-1, keepdims=True) + NORM_EPS) * qscale
        o_ref[hh] = yh * jnp.where(part < 2, inv, 1.0)


def _halo_specs(tile_rows, nb, cols, col_of, n_row_blocks16, tile_of=lambda i: i):
    prev_b = tile_rows // (2 * nb)
    next_b = tile_rows // nb
    return [
        pl.BlockSpec((tile_rows, cols), lambda i, *a: (tile_of(i), col_of(*a))),
        pl.BlockSpec((2 * nb, cols), lambda i, *a: (jnp.maximum(tile_of(i) * prev_b - 1, 0), col_of(*a))),
        pl.BlockSpec((nb, cols),
                     lambda i, *a: (jnp.minimum((tile_of(i) + 1) * next_b, n_row_blocks16 - 1), col_of(*a))),
    ]


def _qkv_prep(proj, conv_w, nb, n_ctx_tiles):
    rows = proj.shape[0]
    tr = CHUNK * nb
    return pl.pallas_call(
        functools.partial(_qkv_prep_kernel, nb, n_ctx_tiles),
        grid=(rows // tr, 3),
        in_specs=_halo_specs(tr, nb, DN_WIDTH, lambda p: p, rows // nb)
        + [pl.BlockSpec((CONV_K, DN_WIDTH), lambda i, p: (0, p))],
        out_specs=pl.BlockSpec((DN_HEADS, tr, DN_HEAD_DIM), lambda i, p: (p, i, 0)),
        out_shape=jax.ShapeDtypeStruct((3 * DN_HEADS, rows, DN_HEAD_DIM), F32),
        compiler_params=_cparams(("parallel", "parallel")),
        name="qkv_conv_norm",
    )(proj, proj, proj, conv_w)


def _bwd_chunk(i, n_ctx, n_tot):
    return jnp.where(i < n_ctx, n_ctx - 1 - i, n_tot + n_ctx - 1 - i)


def _rglru_kernel(nb, n_ctx_tiles, n_tot, xf_ref, pf_ref, nf_ref, xb_ref, pb_ref, nb_ref, cw_ref, cb_ref,
                  wg0_ref, wg1_ref, bg0_ref, bg1_ref, lam_ref, hf_ref, hb_ref, a_scr, b_scr, st_ref):
    i = pl.program_id(0)

    @pl.when(i == 0)
    def _():
        st_ref[...] = jnp.zeros_like(st_ref)

    w = RG_WIDTH
    sp = jax.nn.softplus(-lam_ref[...])
    tiles = (i, _bwd_chunk(i, n_ctx_tiles, n_tot))
    for d, (x_ref, p_ref, n_ref, wg_ref, bg_ref) in enumerate(
            ((xf_ref, pf_ref, nf_ref, wg0_ref, bg0_ref), (xb_ref, pb_ref, nb_ref, wg1_ref, bg1_ref))):
        use_prev, use_next = _halo_flags(tiles[d], n_ctx_tiles)
        xc = _conv_tile(x_ref[...], p_ref[...], n_ref[...], cw_ref[...], nb, use_prev, use_next) + cb_ref[...]
        gates = jnp.dot(xc.astype(BF16), wg_ref[...], preferred_element_type=F32) + bg_ref[...]
        ig = _sigmoid(gates[:, w:])
        c = (-0.5 * RG_C) * sp[d:d + 1]
        a = jnp.exp(c * jnp.tanh(0.5 * gates[:, :w]) + c)
        y = jnp.maximum(1.0 - a * a, 0.0)
        a_scr[d] = a
        b_scr[d] = jnp.where(y > 0.0, y * lax.rsqrt(y), 0.0) * (ig * xc)

    def body(t, carry):
        h0, h1 = carry
        r0 = pl.ds(pl.multiple_of(t * nb, nb), nb)
        r1 = pl.ds(pl.multiple_of((CHUNK - 1 - t) * nb, nb), nb)
        h0 = a_scr[0, r0, :] * h0 + b_scr[0, r0, :]
        h1 = a_scr[1, r1, :] * h1 + b_scr[1, r1, :]
        hf_ref[r0, :] = h0.astype(hf_ref.dtype)
        hb_ref[r1, :] = h1.astype(hb_ref.dtype)
        return h0, h1

    h0, h1 = lax.fori_loop(0, CHUNK, body, (st_ref[0], st_ref[1]), unroll=8)
    st_ref[0] = h0
    st_ref[1] = h1


def _rglru(proj, conv_w, conv_b, wg, bg, lam, nb, n_ctx_tiles):
    rows = proj.shape[0]
    tr = CHUNK * nb
    n_tot = rows // tr
    w = RG_WIDTH
    out = jax.ShapeDtypeStruct((rows, w), BF16)
    fwd = lambda i: i
    bwd = lambda i: _bwd_chunk(i, n_ctx_tiles, n_tot)
    full = lambda shape: pl.BlockSpec(shape, lambda i: (0,) * len(shape))
    return pl.pallas_call(
        functools.partial(_rglru_kernel, nb, n_ctx_tiles, n_tot),
        grid=(n_tot,),
        in_specs=_halo_specs(tr, nb, w, lambda: COL_RX, rows // nb, fwd)
        + _halo_specs(tr, nb, w, lambda: COL_RX, rows // nb, bwd)
        + [full((CONV_K, w)), full((1, w)),
           pl.BlockSpec((w, 2 * w), lambda i: (0, 0)), pl.BlockSpec((w, 2 * w), lambda i: (0, 1)),
           pl.BlockSpec((1, 2 * w), lambda i: (0, 0)), pl.BlockSpec((1, 2 * w), lambda i: (0, 1)),
           full((2, w))],
        out_specs=[pl.BlockSpec((tr, w), lambda i: (fwd(i), 0)), pl.BlockSpec((tr, w), lambda i: (bwd(i), 0))],
        out_shape=[out] * 2,
        scratch_shapes=[pltpu.VMEM((2, tr, w), F32), pltpu.VMEM((2, tr, w), F32), pltpu.VMEM((2, nb, w), F32)],
        compiler_params=_cparams(("arbitrary",)),
        name="rglru",
    )(proj, proj, proj, proj, proj, proj, conv_w, conv_b, wg, wg, bg, bg, lam)


def _bdot(a, b):
    return jnp.dot(a.astype(BF16), b.astype(BF16), preferred_element_type=F32)


def _deltanet_gates(d, raw, alog, dtb):
    c = CHUNK
    row = lax.broadcasted_iota(jnp.int32, (c, c), 0)
    col = lax.broadcasted_iota(jnp.int32, (c, c), 1)
    incl = (row >= col) if d == 0 else (row <= col)
    g_all = -jnp.exp(alog) * jax.nn.softplus(raw + dtb)
    gc_all = jnp.dot(incl.astype(F32), g_all, precision=HIGHEST, preferred_element_type=F32)
    g_tot = jnp.sum(g_all, axis=0, keepdims=True)
    return dict(incl=incl, strict=(row > col) if d == 0 else (row < col),
                diag_blk=(row // SOLVE_BLOCK) == (col // SOLVE_BLOCK),
                beta=_sigmoid(raw), gc=gc_all, gc_t=gc_all.T, egc=jnp.exp(gc_all),
                kdec=jnp.exp(g_tot - gc_all), glast=jnp.exp(g_tot))


def _deltanet_heads(gts, cbs, qs, ks, vs, ss):
    c = CHUNK
    dh = DN_HEAD_DIM
    n = len(cbs)
    rng = range(n)
    cgs = [2 * DN_HEADS + cb for cb in cbs]
    beta = [gts[i]["beta"][:, cbs[i]:cbs[i] + 1] for i in rng]
    egc = [gts[i]["egc"][:, cgs[i]:cgs[i] + 1] for i in rng]
    decay = [jnp.exp(jnp.where(gts[i]["incl"], gts[i]["gc"][:, cgs[i]:cgs[i] + 1] - gts[i]["gc_t"][cgs[i]:cgs[i] + 1, :],
                               -1e30)) for i in rng]
    kbeta = [ks[i] * beta[i] for i in rng]
    prod = [lax.dot_general(jnp.concatenate([kbeta[i], qs[i]], axis=0).astype(BF16), ks[i].astype(BF16),
                            (((1,), (1,)), ((), ())), preferred_element_type=F32) for i in rng]
    a_mat = [jnp.where(gts[i]["strict"], prod[i][:c] * decay[i], 0.0) for i in rng]
    intra = [prod[i][c:] * decay[i] for i in rng]
    ks_qs = [_bdot(jnp.concatenate([kbeta[i] * egc[i], qs[i] * egc[i]], axis=0), ss[i]) for i in rng]
    p = [jnp.where(gts[i]["diag_blk"], -a_mat[i], 0.0) for i in rng]
    y = [jnp.concatenate([vs[i] * beta[i] - ks_qs[i][:c], jnp.where(gts[i]["diag_blk"], 0.0, a_mat[i])], axis=1)
         for i in rng]
    for j in range(4):
        y = [y[i] + _bdot(p[i], y[i]) for i in rng]
        if j < 3:
            p = [_bdot(p[i], p[i]) for i in rng]
    e = [y[i][:, dh:] for i in rng]
    y = [y[i][:, :dh] for i in rng]
    e2 = [_bdot(e[i], e[i]) for i in rng]
    y = [y[i] + _bdot(e2[i], y[i]) for i in rng]
    v_new = [y[i] - _bdot(e[i], y[i]) for i in rng]
    o = [ks_qs[i][c:] + _bdot(intra[i], v_new[i]) for i in rng]
    s_new = [ss[i] * gts[i]["glast"][:, cgs[i]:cgs[i] + 1] + lax.dot_general(
        (ks[i] * gts[i]["kdec"][:, cgs[i]:cgs[i] + 1]).astype(BF16), v_new[i].astype(BF16),
        (((0,), (0,)), ((), ())), preferred_element_type=F32) for i in rng]
    return o, s_new


def _deltanet_kernel(nb, qf_ref, kf_ref, vf_ref, gf_ref, qb_ref, kb_ref, vb_ref, gb_ref, alog_ref, dtb_ref,
                     of_ref, ob_ref, s_ref):
    @pl.when(pl.program_id(0) == 0)
    def _():
        s_ref[...] = jnp.zeros_like(s_ref)

    dirs = ((qf_ref, kf_ref, vf_ref, gf_ref, of_ref), (qb_ref, kb_ref, vb_ref, gb_ref, ob_ref))

    def body(it, carry):
        idx = [(it * DN_BATCHES + j, d, hh) for j in range(DN_BATCHES) for d in range(2) for hh in range(DN_HEADS)]
        rows_of = {j: pl.ds(it * DN_BATCHES + j, CHUNK, stride=nb) for j in range(DN_BATCHES)}
        rows_b = [rows_of[i // (2 * DN_HEADS)] for i in range(len(idx))]
        cbs = [d * DN_HEADS + hh for _, d, hh in idx]
        gates = {(j, d): _deltanet_gates(d, dirs[d][3][rows_of[j], :], alog_ref[...], dtb_ref[...])
                 for j in range(DN_BATCHES) for d in range(2)}
        gts = [gates[(i // (2 * DN_HEADS), d)] for i, (_, d, _) in enumerate(idx)]
        qs = [dirs[d][0][hh, rows_b[i], :] for i, (_, d, hh) in enumerate(idx)]
        ks = [dirs[d][1][hh, rows_b[i], :] for i, (_, d, hh) in enumerate(idx)]
        vs = [dirs[d][2][hh, rows_b[i], :] for i, (_, d, hh) in enumerate(idx)]
        ss = [s_ref[b * (2 * DN_HEADS) + cbs[i]] for i, (b, _, _) in enumerate(idx)]
        o, s_new = _deltanet_heads(gts, cbs, qs, ks, vs, ss)
        for i, (b, d, hh) in enumerate(idx):
            dirs[d][4][hh, rows_b[i], :] = o[i]
            s_ref[b * (2 * DN_HEADS) + cbs[i]] = s_new[i]
        return carry

    lax.fori_loop(0, nb // DN_BATCHES, body, 0)


def _deltanet(qkv, proj, alog_vec, dtb_vec, nb, n_ctx_chunks):
    rows = qkv.shape[1]
    tr = CHUNK * nb
    n_tot = rows // tr
    fwd = lambda i: i
    bwd = lambda i: _bwd_chunk(i, n_ctx_chunks, n_tot)

    def specs(order):
        return [pl.BlockSpec((DN_HEADS, tr, DN_HEAD_DIM), lambda i, p=p: (p, order(i), 0)) for p in range(3)] + [
            pl.BlockSpec((tr, LANES), lambda i: (order(i), COL_GATE))]

    vec = pl.BlockSpec((1, LANES), lambda i: (0, 0))
    out = jax.ShapeDtypeStruct((DN_HEADS, rows, DN_HEAD_DIM), F32)
    return pl.pallas_call(
        functools.partial(_deltanet_kernel, nb),
        grid=(n_tot,),
        in_specs=specs(fwd) + specs(bwd) + [vec, vec],
        out_specs=[pl.BlockSpec((DN_HEADS, tr, DN_HEAD_DIM), lambda i: (0, fwd(i), 0)),
                   pl.BlockSpec((DN_HEADS, tr, DN_HEAD_DIM), lambda i: (0, bwd(i), 0))],
        out_shape=[out, out],
        scratch_shapes=[pltpu.VMEM((nb * 2 * DN_HEADS, DN_HEAD_DIM, DN_HEAD_DIM), F32)],
        compiler_params=_cparams(("arbitrary",)),
        name="deltanet",
    )(qkv, qkv, qkv, proj, qkv, qkv, qkv, proj, alog_vec, dtb_vec)


def _mixer_x1(nb, of_ref, ob_ref, z_ref, hf_ref, hb_ref, ry_ref, onw_ref, w_ref, h_ref, gt_ref, g_ref, b_ref):
    z = z_ref[...]
    parts = []
    for hh in range(DN_HEADS):
        sl = slice(hh * DN_HEAD_DIM, (hh + 1) * DN_HEAD_DIM)
        oh = of_ref[hh] + ob_ref[hh]
        inv = lax.rsqrt(jnp.mean(oh * oh, axis=-1, keepdims=True) + NORM_EPS)
        parts.append(oh * inv * onw_ref[...] * _silu(z[:, sl]))
    parts.append((hf_ref[...].astype(F32) + hb_ref[...].astype(F32)) * jax.nn.gelu(ry_ref[...]))
    y = jnp.concatenate(parts, axis=1).astype(BF16)
    u = jnp.dot(y, w_ref[...], preferred_element_type=F32)
    r = DEEP_ALPHA * h_ref[...] + _per_batch(u, gt_ref[...], nb) * u
    return _layer_norm(r) * g_ref[...] + b_ref[...]


PAIRS_PER_GROUP = EXPERTS_PER_GROUP * (EXPERTS_PER_GROUP - 1) // 2
N_CLASSES = N_GROUPS * PAIRS_PER_GROUP
assert N_CLASSES <= LANES
INFO_CLASS, INFO_RANK, INFO_W_LO, INFO_W_HI = range(4)
Y_SUB = 2 * (D_MODEL // LANES)


ROW_SUB = D_MODEL // LANES


def _store_row_tiles(ref, x):
    n = x.shape[0]
    for c in range(ROW_SUB):
        ref[pl.ds(c, n, stride=ROW_SUB), :] = x[:, c * LANES:(c + 1) * LANES]


def _load_row_tiles(ref, n):
    return jnp.concatenate([ref[pl.ds(c, n, stride=ROW_SUB), :] for c in range(ROW_SUB)], axis=1)


def _first_index(mask, lane):
    return jnp.min(jnp.where(mask, lane, LANES), axis=-1, keepdims=True)


def _route(nb, x, sh_ref, sc_ref, wr_ref, br_ref, hm_ref, info_ref, cnt_ref, run_ref):
    @pl.when(pl.program_id(0) == 0)
    def _():
        run_ref[...] = jnp.zeros_like(run_ref)

    xn = _layer_norm(x)
    hm = xn * (1.0 + _per_batch(xn, sc_ref[...], nb)) + _per_batch(xn, sh_ref[...], nb)
    _store_row_tiles(hm_ref, hm)
    tm = hm.shape[0]
    h_hi = hm.astype(BF16)
    h_lo = (hm - h_hi.astype(F32)).astype(BF16)
    hi_hl = jnp.dot(h_hi, wr_ref[...], preferred_element_type=F32)
    logits = (hi_hl[:, :LANES] + hi_hl[:, LANES:] + jnp.dot(h_lo, wr_ref[:, :LANES], preferred_element_type=F32)
              + br_ref[...])
    lane = lax.broadcasted_iota(jnp.int32, (tm, LANES), 1)
    neg = -jnp.inf
    is_g = lane < N_GROUPS
    gmax = jnp.max(jnp.where(is_g, logits, neg), axis=-1, keepdims=True)
    g_sel = _first_index(jnp.logical_and(is_g, logits == gmax), lane)
    p_group = 1.0 / jnp.sum(jnp.where(is_g, jnp.exp(logits - gmax), 0.0), axis=-1, keepdims=True)
    lo = N_GROUPS + EXPERTS_PER_GROUP * g_sel
    in_grp = jnp.logical_and(lane >= lo, lane < lo + EXPERTS_PER_GROUP)
    m1 = jnp.max(jnp.where(in_grp, logits, neg), axis=-1, keepdims=True)
    i1 = _first_index(jnp.logical_and(in_grp, logits == m1), lane)
    rest = jnp.logical_and(in_grp, lane != i1)
    m2 = jnp.max(jnp.where(rest, logits, neg), axis=-1, keepdims=True)
    i2 = _first_index(jnp.logical_and(rest, logits == m2), lane)
    e2 = jnp.exp(m2 - m1)
    w1 = p_group / (1.0 + e2)
    w2 = p_group * e2 / (1.0 + e2)
    l1 = i1 - lo
    l2 = i2 - lo
    e_lo = jnp.minimum(l1, l2)
    e_hi = jnp.maximum(l1, l2)
    pair = jnp.right_shift(e_lo * (2 * EXPERTS_PER_GROUP - 1 - e_lo), 1) + (e_hi - e_lo - 1)
    cls = g_sel * PAIRS_PER_GROUP + pair
    first_is_lo = l1 < l2
    w_lo = jnp.where(first_is_lo, w1, w2)
    w_hi = jnp.where(first_is_lo, w2, w1)
    oh = lane == cls
    r_i = lax.broadcasted_iota(jnp.int32, (tm, tm), 0)
    c_i = lax.broadcasted_iota(jnp.int32, (tm, tm), 1)
    before = jnp.dot((r_i > c_i).astype(BF16), oh.astype(BF16), preferred_element_type=F32) + run_ref[...]
    rank = jnp.sum(jnp.where(oh, before, 0.0), axis=-1, keepdims=True)
    run_ref[...] = run_ref[...] + jnp.sum(oh.astype(F32), axis=0, keepdims=True)
    cnt_ref[...] = run_ref[...]
    info = jnp.zeros((tm, LANES), F32)
    for idx, val in ((INFO_CLASS, cls.astype(F32)), (INFO_RANK, rank), (INFO_W_LO, w_lo), (INFO_W_HI, w_hi)):
        info = jnp.where(lane == idx, val, info)
    info_ref[...] = info


def _mixer_router_kernel(nb, of_ref, ob_ref, z_ref, hf_ref, hb_ref, ry_ref, onw_ref, w_ref, h_ref, gt_ref, g_ref, b_ref,
                         sh_ref, sc_ref, wr_ref, br_ref, x1_ref, hm_ref, info_ref, cnt_ref, run_ref):
    x1 = _mixer_x1(nb, of_ref, ob_ref, z_ref, hf_ref, hb_ref, ry_ref, onw_ref, w_ref, h_ref, gt_ref, g_ref, b_ref)
    x1_ref[...] = x1
    _route(nb, x1, sh_ref, sc_ref, wr_ref, br_ref, hm_ref, info_ref, cnt_ref, run_ref)


def _mixer_router(o_f, o_b, proj, hf, hb, onorm_w, w_out, layer, h, mod_rows, ln_g, ln_b, wr, br, nb, ctx_rows, row0):
    rows, d = h.shape
    tm = ROW_TILE
    t0 = row0 // tm
    n = rows - row0
    lat = lambda i: jnp.where((i + t0) * tm >= ctx_rows, 1, 0)
    half = pl.BlockSpec((tm, DN_WIDTH), lambda i: (i + t0, 0))
    planes = pl.BlockSpec((DN_HEADS, tm, DN_HEAD_DIM), lambda i: (0, i + t0, 0))
    vec = lambda m: pl.BlockSpec((1, m), lambda i: (0, 0))
    mod = lambda k: pl.BlockSpec((None, nb, d), lambda i: (lat(i), 0, k))
    return pl.pallas_call(
        functools.partial(_mixer_router_kernel, nb),
        grid=(n // tm,),
        in_specs=[planes, planes, pl.BlockSpec((tm, DN_WIDTH), lambda i: (i + t0, COL_Z)), half, half,
                  pl.BlockSpec((tm, RG_WIDTH), lambda i: (i + t0, COL_RY)),
                  vec(DN_HEAD_DIM), pl.BlockSpec((None, d, d), lambda i: (layer, 0, 0)),
                  pl.BlockSpec((tm, d), lambda i: (i + t0, 0)),
                  mod(2),
                  vec(d), vec(d),
                  mod(3), mod(4),
                  pl.BlockSpec((d, 2 * LANES), lambda i: (0, 0)), vec(LANES)],
        out_specs=[pl.BlockSpec((tm, d), lambda i: (i, 0)),
                   pl.BlockSpec((tm * ROW_SUB, LANES), lambda i: (i, 0)),
                   pl.BlockSpec((tm, LANES), lambda i: (i, 0)),
                   pl.BlockSpec((1, LANES), lambda i: (0, 0))],
        out_shape=[jax.ShapeDtypeStruct((n, d), F32),
                   jax.ShapeDtypeStruct((n * ROW_SUB, LANES), F32),
                   jax.ShapeDtypeStruct((n, LANES), F32),
                   jax.ShapeDtypeStruct((1, LANES), F32)],
        scratch_shapes=[pltpu.VMEM((1, LANES), F32)],
        compiler_params=_cparams(("arbitrary",)),
        name="mixer_router",
    )(o_f, o_b, proj, hf, hb, proj, onorm_w, w_out, h, mod_rows, ln_g, ln_b, mod_rows, mod_rows, wr, br)


def _dispatch_kernel(tmx, ends_ref, cnt_ref, dst_ref, hm_ref, xs_hbm, zero_scr, sem, zsem):
    rs = ROW_SUB
    tb = dst_ref.shape[1]

    @pl.when(pl.program_id(0) == 0)
    def _():
        zero_scr[...] = jnp.zeros_like(zero_scr)

        def tail_copy(c):
            start = pl.multiple_of((ends_ref[c] - tmx) * rs, tmx * rs)
            return pltpu.make_async_copy(zero_scr, xs_hbm.at[pl.ds(start, tmx * rs)], zsem)

        def start_one(c, carry):
            @pl.when(cnt_ref[c] > 0)
            def _():
                tail_copy(c).start()
            return carry

        def wait_one(c, carry):
            @pl.when(cnt_ref[c] > 0)
            def _():
                tail_copy(c).wait()
            return carry

        lax.fori_loop(0, N_CLASSES, start_one, 0)

        def spare_copy(j):
            return pltpu.make_async_copy(zero_scr, xs_hbm.at[pl.ds(pl.multiple_of(j * (tmx * rs), tmx * rs), tmx * rs)],
                                         zsem)

        n_used = ends_ref[N_CLASSES - 1] // tmx
        n_tiles = xs_hbm.shape[0] // (tmx * rs)
        lax.fori_loop(n_used, n_tiles, lambda j, carry: (spare_copy(j).start(), carry)[1], 0)
        lax.fori_loop(0, N_CLASSES, wait_one, 0)
        lax.fori_loop(n_used, n_tiles, lambda j, carry: (spare_copy(j).wait(), carry)[1], 0)

    for r in range(tb):
        dst = xs_hbm.at[pl.ds(pl.multiple_of(dst_ref[0, r], rs), rs)]
        pltpu.make_async_copy(hm_ref.at[pl.ds(r * rs, rs)], dst, sem).start(priority=r % 2)
    pltpu.make_async_copy(hm_ref, hm_ref, sem).wait()


def _dispatch(hm, dest, class_ends, class_cnt, sorted_rows, tmx):
    tb = ROW_TILE
    n_blocks = dest.shape[0] // tb
    grid_spec = pltpu.PrefetchScalarGridSpec(
        num_scalar_prefetch=2,
        grid=(n_blocks,),
        in_specs=[pl.BlockSpec((None, 1, tb), lambda i, e, c: (i, 0, 0), memory_space=pltpu.SMEM),
                  pl.BlockSpec((tb * ROW_SUB, LANES), lambda i, e, c: (i, 0))],
        out_specs=pl.BlockSpec(memory_space=pl.ANY),
        scratch_shapes=[pltpu.VMEM((tmx * ROW_SUB, LANES), F32), pltpu.SemaphoreType.DMA(()),
                        pltpu.SemaphoreType.DMA(())],
    )
    return pl.pallas_call(
        functools.partial(_dispatch_kernel, tmx),
        grid_spec=grid_spec,
        out_shape=jax.ShapeDtypeStruct((sorted_rows * ROW_SUB, LANES), F32),
        compiler_params=_cparams(("arbitrary",)),
        name="moe_dispatch",
    )(class_ends, class_cnt, (dest * ROW_SUB).reshape(n_blocks, 1, tb), hm)


def _class_ffn_kernel(telo_ref, tehi_ref, nact_ref, xs_ref, wgul_ref, wdl_ref, wguh_ref, wdh_ref, ys_ref, y_scr):
    j = pl.program_id(0)
    tmx = xs_ref.shape[0] // ROW_SUB
    de = wdl_ref.shape[0]

    @pl.when(j < nact_ref[0])
    def _():
        x = _load_row_tiles(xs_ref, tmx).astype(BF16)
        for half, (wgu_ref, wd_ref) in enumerate(((wgul_ref, wdl_ref), (wguh_ref, wdh_ref))):
            ab = jnp.dot(x, wgu_ref[...], preferred_element_type=F32)
            y = jnp.dot((_silu(ab[:, :de]) * ab[:, de:]).astype(BF16), wd_ref[...], preferred_element_type=F32)
            for c in range(ROW_SUB):
                y_scr[pl.ds(half * ROW_SUB + c, tmx, stride=Y_SUB), :] = y[:, c * LANES:(c + 1) * LANES]
        ys_ref[...] = y_scr[...].astype(BF16)

    @pl.when(j >= nact_ref[0])
    def _():
        ys_ref[...] = jnp.zeros_like(ys_ref)


def _class_ffn(xs, tile_e_lo, tile_e_hi, n_active, w_gate_up, w_down, layer, tmx):
    n_tiles = xs.shape[0] // (tmx * ROW_SUB)
    d, de2 = w_gate_up.shape[-2:]
    w_in = lambda te_idx: pl.BlockSpec((None, None, d, de2), lambda j, *te: (layer, te[te_idx][j], 0, 0))
    w_out = lambda te_idx: pl.BlockSpec((None, None, de2 // 2, d), lambda j, *te: (layer, te[te_idx][j], 0, 0))
    grid_spec = pltpu.PrefetchScalarGridSpec(
        num_scalar_prefetch=3,
        grid=(n_tiles,),
        in_specs=[pl.BlockSpec((tmx * ROW_SUB, LANES), lambda j, lo, hi, na: (jnp.minimum(j, na[0] - 1), 0)),
                  w_in(0), w_out(0), w_in(1), w_out(1)],
        out_specs=pl.BlockSpec((tmx * Y_SUB, LANES), lambda j, lo, hi, na: (j, 0)),
        scratch_shapes=[pltpu.VMEM((tmx * Y_SUB, LANES), F32)],
    )
    return pl.pallas_call(
        _class_ffn_kernel,
        grid_spec=grid_spec,
        out_shape=jax.ShapeDtypeStruct((n_tiles * tmx * Y_SUB, LANES), BF16),
        compiler_params=_cparams(("arbitrary",)),
        name="expert_ffn",
    )(tile_e_lo, tile_e_hi, n_active, xs, w_gate_up, w_down, w_gate_up, w_down)


def _moe_out_kernel(nb, batch_major, src0_ref, srcn_ref, ys_hbm, info_ref, x_ref, gt_ref, g_ref, b_ref, o_ref,
                    ybuf, y_scr, sem):
    i = pl.program_id(0)
    tm = info_ref.shape[0]
    slot = i % 2
    nslot = 1 - slot

    def gather_all(src_ref, s):
        for r in range(tm):
            pltpu.make_async_copy(ys_hbm.at[pl.ds(pl.multiple_of(src_ref[0, r], Y_SUB), Y_SUB)],
                                  ybuf.at[s, pl.ds(r * Y_SUB, Y_SUB)], sem.at[s]).start(priority=r % 2)

    def wait_rows(s):
        pltpu.make_async_copy(ybuf.at[s], ybuf.at[s], sem.at[s]).wait()

    @pl.when(i == 0)
    def _():
        gather_all(src0_ref, 0)

    wait_rows(slot)
    gather_all(srcn_ref, nslot)
    info = info_ref[...]
    y_scr[...] = ybuf[slot].astype(F32)
    y_lo = jnp.concatenate([y_scr[pl.ds(c, tm, stride=Y_SUB), :] for c in range(ROW_SUB)], axis=1)
    y_hi = jnp.concatenate([y_scr[pl.ds(ROW_SUB + c, tm, stride=Y_SUB), :] for c in range(ROW_SUB)], axis=1)
    f = y_lo * info[:, INFO_W_LO:INFO_W_LO + 1] + y_hi * info[:, INFO_W_HI:INFO_W_HI + 1]
    r = DEEP_ALPHA * x_ref[...] + _per_batch(f, gt_ref[...], nb) * f
    res = _layer_norm(r) * g_ref[...] + b_ref[...]
    if batch_major:
        for c in range(ROW_SUB):
            y_scr[pl.ds(c * tm, tm), :] = res[:, c * LANES:(c + 1) * LANES]
        for b in range(nb):
            for c in range(ROW_SUB):
                o_ref[b, :, c * LANES:(c + 1) * LANES] = y_scr[pl.ds(c * tm + b, tm // nb, stride=nb), :]
    else:
        o_ref[...] = res

    @pl.when(i == pl.num_programs(0) - 1)
    def _():
        wait_rows(nslot)


def _moe_out(ys, dest, info, x1, mod_rows, ln_g, ln_b, nb, ctx_rows, batch_major):
    rows, d = x1.shape
    tm = ROW_TILE
    n_blocks = rows // tm
    if batch_major:
        out_spec = pl.BlockSpec((nb, tm // nb, d), lambda i: (0, i, 0))
        out_shape = jax.ShapeDtypeStruct((nb, rows // nb, d), F32)
    else:
        out_spec = pl.BlockSpec((tm, d), lambda i: (i, 0))
        out_shape = jax.ShapeDtypeStruct((rows, d), F32)
    lat = lambda i: jnp.where(i * tm >= ctx_rows, 1, 0)
    vec = pl.BlockSpec((1, d), lambda i: (0, 0))
    src = (dest * Y_SUB).reshape(n_blocks, 1, tm)
    smem = lambda f: pl.BlockSpec((None, 1, tm), f, memory_space=pltpu.SMEM)
    return pl.pallas_call(
        functools.partial(_moe_out_kernel, nb, batch_major),
        grid=(n_blocks,),
        in_specs=[smem(lambda i: (0, 0, 0)),
                  smem(lambda i: (jnp.minimum(i + 1, n_blocks - 1), 0, 0)),
                  pl.BlockSpec(memory_space=pl.ANY),
                  pl.BlockSpec((tm, LANES), lambda i: (i, 0)),
                  pl.BlockSpec((tm, d), lambda i: (i, 0)),
                  pl.BlockSpec((None, nb, d), lambda i: (lat(i), 0, 5)),
                  vec, vec],
        out_specs=out_spec,
        out_shape=out_shape,
        scratch_shapes=[pltpu.VMEM((2, tm * Y_SUB, LANES), BF16), pltpu.VMEM((tm * Y_SUB, LANES), F32),
                        pltpu.SemaphoreType.DMA((2,))],
        compiler_params=_cparams(("arbitrary",)),
        name="moe_out",
    )(src, src, ys, info, x1, mod_rows, ln_g, ln_b)


def _class_experts():
    lo, hi = [], []
    for g in range(N_GROUPS):
        for a in range(EXPERTS_PER_GROUP):
            for b in range(a + 1, EXPERTS_PER_GROUP):
                lo.append(g * EXPERTS_PER_GROUP + a)
                hi.append(g * EXPERTS_PER_GROUP + b)
    return np.asarray(lo, np.int32), np.asarray(hi, np.int32)


def _routing_tables(info, counts, n_tiles, tmx):
    cnt = counts[0, :N_CLASSES].astype(jnp.int32)
    padded = ((cnt + tmx - 1) // tmx) * tmx
    ends = jnp.cumsum(padded)
    starts = ends - padded
    cls = info[:, INFO_CLASS].astype(jnp.int32)
    class_ids = jnp.arange(N_CLASSES, dtype=jnp.int32)
    start_of_tok = jnp.sum(jnp.where(cls[:, None] == class_ids[None, :], starts[None, :], 0), axis=1)
    dest = start_of_tok + info[:, INFO_RANK].astype(jnp.int32)
    tile_start = jnp.arange(n_tiles, dtype=jnp.int32) * tmx
    n_before = jnp.sum((ends[None, :] <= tile_start[:, None]).astype(jnp.int32), axis=1)
    tile_cls = jnp.minimum(n_before, N_CLASSES - 1)
    e_lo, e_hi = _class_experts()
    n_active = (ends[-1:] // tmx).astype(jnp.int32)
    return dest, ends, cnt, jnp.asarray(e_lo)[tile_cls], jnp.asarray(e_hi)[tile_cls], n_active


def _relayout_w_in(w_in):
    qkvz = w_in[..., :4 * DN_WIDTH]
    gates = w_in[..., 4 * DN_WIDTH:4 * DN_WIDTH + 4 * DN_HEADS]
    rxy = w_in[..., 4 * DN_WIDTH + 4 * DN_HEADS:]
    pad = jnp.zeros(w_in.shape[:-1] + (LANES - 4 * DN_HEADS,), w_in.dtype)
    return jnp.concatenate([qkvz, rxy, gates, pad], axis=-1).astype(BF16)


def _block_diag(w):
    n, c, _ = w.shape
    eye = jnp.eye(n, dtype=w.dtype)
    return (eye[:, None, :, None] * w[:, :, None, :]).reshape(n * c, n * c)


def _lane_vec(vals, offset):
    return jnp.zeros((1, LANES), F32).at[0, offset:offset + vals.size].set(vals.reshape(-1))


def kernel(x, c, ctx, c_ctx, w_ada, b_ada, w_in, conv_qkv_w, dn_a_log, dn_dt_bias, dn_onorm_w, rg_conv_w, rg_conv_b, rg_wa, rg_ba, rg_wi, rg_bi, rg_lambda, w_out, ln_g, ln_b, router_wg, router_bg, router_we, router_be, w_e_gate, w_e_up, w_e_down):
    nb, t_lat, d = x.shape
    t_ctx = ctx.shape[1]
    assert d == D_MODEL and nb % 8 == 0 and t_lat % GRID_W == 0 and t_ctx % CHUNK == 0
    assert (t_ctx * nb) % INPROJ_TILE == 0 and (t_lat * nb) % INPROJ_TILE == 0 and INPROJ_TILE % ROW_TILE == 0
    tt = t_ctx + t_lat
    rows = tt * nb
    ctx_rows = t_ctx * nb
    n_ctx_chunks = t_ctx // CHUNK

    n_cc = ((nb + 1 + 7) // 8) * 8
    cc = jnp.zeros((n_cc, d), F32).at[:nb].set(c).at[nb].set(c_ctx)
    mod = _modulation(cc, w_ada, b_ada)
    mod_ctx = jnp.broadcast_to(mod[:, nb:nb + 1], (DEPTH, nb, 6 * d))
    mod_rows = jnp.stack([mod_ctx, mod[:, :nb]], axis=1)

    w_in2 = _relayout_w_in(w_in)
    w_out_bf = w_out.astype(BF16)
    w_gate_up = jnp.concatenate([w_e_gate, w_e_up], axis=-1).astype(BF16)
    w_down = w_e_down.astype(BF16)
    for l in range(DEPTH):
        row0 = ctx_rows if l == DEPTH - 1 else 0
        moe_ctx_rows = ctx_rows - row0
        n_tiles = (rows - row0) // MOE_TILE + N_CLASSES
        mr = mod_rows[l]
        if l == 0:
            proj, h = _in_projection_first(ctx, x, mr, w_in2, l, nb)
        else:
            proj = _in_projection(h, mr, w_in2, l, nb, ctx_rows)
        qkv = _qkv_prep(proj, conv_qkv_w[l], nb, n_ctx_chunks)
        wg = jnp.concatenate([_block_diag(rg_wa[l, 0]), _block_diag(rg_wi[l, 0]),
                              _block_diag(rg_wa[l, 1]), _block_diag(rg_wi[l, 1])], axis=1).astype(BF16)
        bg = jnp.concatenate([rg_ba[l, 0], rg_bi[l, 0], rg_ba[l, 1], rg_bi[l, 1]])[None]
        hf, hb = _rglru(proj, rg_conv_w[l], rg_conv_b[l][None], wg, bg, rg_lambda[l], nb, n_ctx_chunks)
        alog_vec = _lane_vec(dn_a_log[l], 2 * DN_HEADS)
        dtb_vec = _lane_vec(dn_dt_bias[l], 2 * DN_HEADS)
        o_f, o_b = _deltanet(qkv, proj, alog_vec, dtb_vec, nb, n_ctx_chunks)
        wr = jnp.zeros((d, LANES), F32).at[:, :N_GROUPS].set(router_wg[l]).at[
            :, N_GROUPS:N_GROUPS + N_EXPERTS].set(router_we[l])
        br = _lane_vec(jnp.concatenate([router_bg[l], router_be[l]]), 0)
        wr_hi = wr.astype(BF16)
        wr_hl = jnp.concatenate([wr_hi, (wr - wr_hi.astype(F32)).astype(BF16)], axis=1)
        x1, hm, info, counts = _mixer_router(o_f, o_b, proj, hf, hb, dn_onorm_w[l][None], w_out_bf, l, h, mr,
                                             ln_g[l, 0][None], ln_b[l, 0][None], wr_hl, br, nb, ctx_rows, row0)
        dest, class_ends, class_cnt, tile_e_lo, tile_e_hi, n_active = _routing_tables(info, counts, n_tiles, MOE_TILE)
        xs = _dispatch(hm, dest, class_ends, class_cnt, n_tiles * MOE_TILE, MOE_TILE)
        ys = _class_ffn(xs, tile_e_lo, tile_e_hi, n_active, w_gate_up, w_down, l, MOE_TILE)
        h = _moe_out(ys, dest, info, x1, mr, ln_g[l, 1][None], ln_b[l, 1][None], nb, moe_ctx_rows,
                     batch_major=(l == DEPTH - 1))

    return h
```

```python
import functools

import jax
import jax.numpy as jnp
import numpy as np
from jax import lax
from jax.experimental import pallas as pl
from jax.experimental.pallas import tpu as pltpu

F32 = jnp.float32
BF16 = jnp.bfloat16

D_MODEL = 1024
DEPTH = 2
GRID_W = 64
DN_HEAD_DIM = 128
DN_WIDTH = 512
DN_HEADS = 4
RG_WIDTH = 512
RG_BLOCKS = 8
RG_BLOCK_DIM = 64
RG_C = 8.0
CONV_K = 4
N_GROUPS = 4
EXPERTS_PER_GROUP = 8
N_EXPERTS = 32
D_EXPERT = 256
DEEP_ALPHA = (2 * DEPTH) ** 0.25
LN_EPS = 1e-5
NORM_EPS = 1e-6

LANES = 128
CHUNK = 64
SOLVE_BLOCK = 16
DN_BATCHES = 4
D_PROJ = 3 * DN_WIDTH + DN_WIDTH + 2 * RG_WIDTH + LANES
COL_Z = 3
COL_RX = 4
COL_RY = 5
COL_GATE = (3 * DN_WIDTH + DN_WIDTH + 2 * RG_WIDTH) // LANES
ROW_TILE = 512
INPROJ_TILE = 1024
MOE_TILE = 256
VMEM_LIMIT = 56 * 1024 * 1024


def _cparams(sem):
    return pltpu.CompilerParams(dimension_semantics=sem, vmem_limit_bytes=VMEM_LIMIT)


def _layer_norm(x):
    mu = jnp.mean(x, axis=-1, keepdims=True)
    xc = x - mu
    var = jnp.mean(xc * xc, axis=-1, keepdims=True)
    return xc * lax.rsqrt(var + LN_EPS)


def _per_batch(x, v, nb):
    rows, c = x.shape
    return jnp.broadcast_to(v[None], (rows // nb, nb, c)).reshape(rows, c)


def _sigmoid(x):
    return 0.5 * jnp.tanh(0.5 * x) + 0.5


def _silu(x):
    h = 0.5 * x
    return h * jnp.tanh(h) + h


def _mod_kernel(cc_ref, w_ref, b_ref, o_ref):
    s = _silu(cc_ref[...])
    o_ref[...] = jnp.dot(s.astype(BF16), w_ref[...].astype(BF16), preferred_element_type=F32) + b_ref[...]


def _modulation(cc, w_ada, b_ada):
    depth, d, n6 = w_ada.shape
    rows = cc.shape[0]
    tn = 1536
    return pl.pallas_call(
        _mod_kernel,
        grid=(depth, n6 // tn),
        in_specs=[
            pl.BlockSpec((rows, d), lambda l, j: (0, 0)),
            pl.BlockSpec((None, d, tn), lambda l, j: (l, 0, j)),
            pl.BlockSpec((None, 1, tn), lambda l, j: (l, 0, j)),
        ],
        out_specs=pl.BlockSpec((None, rows, tn), lambda l, j: (l, 0, j)),
        out_shape=jax.ShapeDtypeStruct((depth, rows, n6), F32),
        compiler_params=_cparams(("parallel", "parallel")),
        name="adaln_mod",
    )(cc, w_ada, b_ada.reshape(depth, 1, n6))


def _inproj_kernel(nb, h_ref, sh_ref, sc_ref, w_ref, o_ref):
    xn = _layer_norm(h_ref[...])
    hm = xn * (1.0 + _per_batch(xn, sc_ref[...], nb)) + _per_batch(xn, sh_ref[...], nb)
    o_ref[...] = jnp.dot(hm.astype(BF16), w_ref[...], preferred_element_type=F32)


def _inproj_first_kernel(nb, n_ctx_tiles, ctx_ref, x_ref, sh_ref, sc_ref, w_ref, o_ref, h_ref, scr):
    tt = ctx_ref.shape[1]
    tm = nb * tt
    blk = jnp.where(pl.program_id(0) < n_ctx_tiles, ctx_ref[...], x_ref[...])
    for b in range(nb):
        for c in range(ROW_SUB):
            scr[pl.ds(c * tm + b, tt, stride=nb), :] = blk[b, :, c * LANES:(c + 1) * LANES]
    h = jnp.concatenate([scr[pl.ds(c * tm, tm), :] for c in range(ROW_SUB)], axis=1)
    h_ref[...] = h
    xn = _layer_norm(h)
    hm = xn * (1.0 + _per_batch(xn, sc_ref[...], nb)) + _per_batch(xn, sh_ref[...], nb)
    o_ref[...] = jnp.dot(hm.astype(BF16), w_ref[...], preferred_element_type=F32)


def _in_projection_first(ctx, x, mod_rows, w_in2, layer, nb):
    _, t_ctx, d = ctx.shape
    t_lat = x.shape[1]
    tm = ROW_TILE
    tt = tm // nb
    n_ctx_tiles = t_ctx // tt
    rows = (t_ctx + t_lat) * nb
    lat = lambda i: jnp.where(i >= n_ctx_tiles, 1, 0)
    return pl.pallas_call(
        functools.partial(_inproj_first_kernel, nb, n_ctx_tiles),
        grid=(rows // tm,),
        in_specs=[
            pl.BlockSpec((nb, tt, d), lambda i: (0, jnp.minimum(i, n_ctx_tiles - 1), 0)),
            pl.BlockSpec((nb, tt, d), lambda i: (0, jnp.maximum(i - n_ctx_tiles, 0), 0)),
            pl.BlockSpec((None, nb, d), lambda i: (lat(i), 0, 0)),
            pl.BlockSpec((None, nb, d), lambda i: (lat(i), 0, 1)),
            pl.BlockSpec((None, d, D_PROJ), lambda i: (layer, 0, 0)),
        ],
        out_specs=[pl.BlockSpec((tm, D_PROJ), lambda i: (i, 0)), pl.BlockSpec((tm, d), lambda i: (i, 0))],
        out_shape=[jax.ShapeDtypeStruct((rows, D_PROJ), F32), jax.ShapeDtypeStruct((rows, d), F32)],
        scratch_shapes=[pltpu.VMEM((ROW_SUB * tm, LANES), F32)],
        compiler_params=_cparams(("parallel",)),
        name="ln_mod_inproj_first",
    )(ctx, x, mod_rows, mod_rows, w_in2)


def _in_projection(h, mod_rows, w_in2, layer, nb, ctx_rows):
    rows, d = h.shape
    tm = INPROJ_TILE
    lat = lambda i: jnp.where(i * tm >= ctx_rows, 1, 0)
    return pl.pallas_call(
        functools.partial(_inproj_kernel, nb),
        grid=(rows // tm,),
        in_specs=[
            pl.BlockSpec((tm, d), lambda i: (i, 0)),
            pl.BlockSpec((None, nb, d), lambda i: (lat(i), 0, 0)),
            pl.BlockSpec((None, nb, d), lambda i: (lat(i), 0, 1)),
            pl.BlockSpec((None, d, D_PROJ), lambda i: (layer, 0, 0)),
        ],
        out_specs=pl.BlockSpec((tm, D_PROJ), lambda i: (i, 0)),
        out_shape=jax.ShapeDtypeStruct((rows, D_PROJ), F32),
        compiler_params=_cparams(("parallel",)),
        name="ln_mod_inproj",
    )(h, mod_rows, mod_rows, w_in2)


def _conv_tile(x, prev, nxt, w, nb, use_prev, use_next):
    prev = jnp.where(use_prev, prev, 0.0)
    nxt = jnp.where(use_next, nxt, 0.0)
    xm2 = jnp.concatenate([prev, x[: -2 * nb]], axis=0)
    xm1 = jnp.concatenate([prev[nb:], x[:-nb]], axis=0)
    xp1 = jnp.concatenate([x[nb:], nxt], axis=0)
    return xm2 * w[0:1] + xm1 * w[1:2] + x * w[2:3] + xp1 * w[3:4]


def _halo_flags(i, n_ctx_tiles):
    use_prev = jnp.logical_and(i > 0, i < n_ctx_tiles)
    use_next = i < n_ctx_tiles - 1
    return use_prev, use_next


def _qkv_prep_kernel(nb, n_ctx_tiles, x_ref, prev_ref, next_ref, w_ref, o_ref):
    i = pl.program_id(0)
    part = pl.program_id(1)
    use_prev, use_next = _halo_flags(i, n_ctx_tiles)
    y = _silu(_conv_tile(x_ref[...], prev_ref[...], next_ref[...], w_ref[...], nb, use_prev, use_next))
    qscale = jnp.where(part == 0, DN_HEAD_DIM ** -0.5, 1.0).astype(F32)
    for hh in range(DN_HEADS):
        yh = y[:, hh * DN_HEAD_DIM:(hh + 1) * DN_HEAD_DIM]
        inv = lax.rsqrt(jnp.sum(yh * yh, axis=-1, keepdims=True) + NORM_EPS) * qscale
        o_ref[hh] = yh * jnp.where(part < 2, inv, 1.0)


def _halo_specs(tile_rows, nb, cols, col_of, n_row_blocks16, tile_of=lambda i: i):
    prev_b = tile_rows // (2 * nb)
    next_b = tile_rows // nb
    return [
        pl.BlockSpec((tile_rows, cols), lambda i, *a: (tile_of(i), col_of(*a))),
        pl.BlockSpec((2 * nb, cols), lambda i, *a: (jnp.maximum(tile_of(i) * prev_b - 1, 0), col_of(*a))),
        pl.BlockSpec((nb, cols),
                     lambda i, *a: (jnp.minimum((tile_of(i) + 1) * next_b, n_row_blocks16 - 1), col_of(*a))),
    ]


def _qkv_prep(proj, conv_w, nb, n_ctx_tiles):
    rows = proj.shape[0]
    tr = CHUNK * nb
    return pl.pallas_call(
        functools.partial(_qkv_prep_kernel, nb, n_ctx_tiles),
        grid=(rows // tr, 3),
        in_specs=_halo_specs(tr, nb, DN_WIDTH, lambda p: p, rows // nb)
        + [pl.BlockSpec((CONV_K, DN_WIDTH), lambda i, p: (0, p))],
        out_specs=pl.BlockSpec((DN_HEADS, tr, DN_HEAD_DIM), lambda i, p: (p, i, 0)),
        out_shape=jax.ShapeDtypeStruct((3 * DN_HEADS, rows, DN_HEAD_DIM), F32),
        compiler_params=_cparams(("parallel", "parallel")),
        name="qkv_conv_norm",
    )(proj, proj, proj, conv_w)


def _bwd_chunk(i, n_ctx, n_tot):
    return jnp.where(i < n_ctx, n_ctx - 1 - i, n_tot + n_ctx - 1 - i)


def _rglru_kernel(nb, n_ctx_tiles, n_tot, xf_ref, pf_ref, nf_ref, xb_ref, pb_ref, nb_ref, cw_ref, cb_ref,
                  wg0_ref, wg1_ref, bg0_ref, bg1_ref, lam_ref, hf_ref, hb_ref, a_scr, b_scr, st_ref):
    i = pl.program_id(0)

    @pl.when(i == 0)
    def _():
        st_ref[...] = jnp.zeros_like(st_ref)

    w = RG_WIDTH
    sp = jax.nn.softplus(-lam_ref[...])
    tiles = (i, _bwd_chunk(i, n_ctx_tiles, n_tot))
    for d, (x_ref, p_ref, n_ref, wg_ref, bg_ref) in enumerate(
            ((xf_ref, pf_ref, nf_ref, wg0_ref, bg0_ref), (xb_ref, pb_ref, nb_ref, wg1_ref, bg1_ref))):
        use_prev, use_next = _halo_flags(tiles[d], n_ctx_tiles)
        xc = _conv_tile(x_ref[...], p_ref[...], n_ref[...], cw_ref[...], nb, use_prev, use_next) + cb_ref[...]
        gates = jnp.dot(xc.astype(BF16), wg_ref[...], preferred_element_type=F32) + bg_ref[...]
        ig = _sigmoid(gates[:, w:])
        c = (-0.5 * RG_C) * sp[d:d + 1]
        a = jnp.exp(c * jnp.tanh(0.5 * gates[:, :w]) + c)
        y = jnp.maximum(1.0 - a * a, 0.0)
        a_scr[d] = a
        b_scr[d] = jnp.where(y > 0.0, y * lax.rsqrt(y), 0.0) * (ig * xc)

    def body(t, carry):
        h0, h1 = carry
        r0 = pl.ds(pl.multiple_of(t * nb, nb), nb)
        r1 = pl.ds(pl.multiple_of((CHUNK - 1 - t) * nb, nb), nb)
        h0 = a_scr[0, r0, :] * h0 + b_scr[0, r0, :]
        h1 = a_scr[1, r1, :] * h1 + b_scr[1, r1, :]
        hf_ref[r0, :] = h0.astype(hf_ref.dtype)
        hb_ref[r1, :] = h1.astype(hb_ref.dtype)
        return h0, h1

    h0, h1 = lax.fori_loop(0, CHUNK, body, (st_ref[0], st_ref[1]), unroll=8)
    st_ref[0] = h0
    st_ref[1] = h1


def _rglru(proj, conv_w, conv_b, wg, bg, lam, nb, n_ctx_tiles):
    rows = proj.shape[0]
    tr = CHUNK * nb
    n_tot = rows // tr
    w = RG_WIDTH
    out = jax.ShapeDtypeStruct((rows, w), BF16)
    fwd = lambda i: i
    bwd = lambda i: _bwd_chunk(i, n_ctx_tiles, n_tot)
    full = lambda shape: pl.BlockSpec(shape, lambda i: (0,) * len(shape))
    return pl.pallas_call(
        functools.partial(_rglru_kernel, nb, n_ctx_tiles, n_tot),
        grid=(n_tot,),
        in_specs=_halo_specs(tr, nb, w, lambda: COL_RX, rows // nb, fwd)
        + _halo_specs(tr, nb, w, lambda: COL_RX, rows // nb, bwd)
        + [full((CONV_K, w)), full((1, w)),
           pl.BlockSpec((w, 2 * w), lambda i: (0, 0)), pl.BlockSpec((w, 2 * w), lambda i: (0, 1)),
           pl.BlockSpec((1, 2 * w), lambda i: (0, 0)), pl.BlockSpec((1, 2 * w), lambda i: (0, 1)),
           full((2, w))],
        out_specs=[pl.BlockSpec((tr, w), lambda i: (fwd(i), 0)), pl.BlockSpec((tr, w), lambda i: (bwd(i), 0))],
        out_shape=[out] * 2,
        scratch_shapes=[pltpu.VMEM((2, tr, w), F32), pltpu.VMEM((2, tr, w), F32), pltpu.VMEM((2, nb, w), F32)],
        compiler_params=_cparams(("arbitrary",)),
        name="rglru",
    )(proj, proj, proj, proj, proj, proj, conv_w, conv_b, wg, wg, bg, bg, lam)


def _bdot(a, b):
    return jnp.dot(a.astype(BF16), b.astype(BF16), preferred_element_type=F32)


def _deltanet_gates(d, raw, alog, dtb):
    c = CHUNK
    row = lax.broadcasted_iota(jnp.int32, (c, c), 0)
    col = lax.broadcasted_iota(jnp.int32, (c, c), 1)
    incl = (row >= col) if d == 0 else (row <= col)
    g_all = -jnp.exp(alog) * jax.nn.softplus(raw + dtb)
    g_hi = g_all.astype(BF16)
    g_lo = (g_all - g_hi.astype(F32)).astype(BF16)
    tri = incl.astype(BF16)
    gc_all = (jnp.dot(tri, g_hi, preferred_element_type=F32) + jnp.dot(tri, g_lo, preferred_element_type=F32))
    g_tot = jnp.sum(g_all, axis=0, keepdims=True)
    return dict(incl=incl, strict=(row > col) if d == 0 else (row < col),
                diag_blk=(row // SOLVE_BLOCK) == (col // SOLVE_BLOCK),
                beta=_sigmoid(raw), gc=gc_all, gc_t=gc_all.T, egc=jnp.exp(gc_all),
                kdec=jnp.exp(g_tot - gc_all), glast=jnp.exp(g_tot))


def _deltanet_heads(gts, cbs, qs, ks, vs, ss):
    c = CHUNK
    dh = DN_HEAD_DIM
    n = len(cbs)
    rng = range(n)
    cgs = [2 * DN_HEADS + cb for cb in cbs]
    beta = [gts[i]["beta"][:, cbs[i]:cbs[i] + 1] for i in rng]
    egc = [gts[i]["egc"][:, cgs[i]:cgs[i] + 1] for i in rng]
    decay = [jnp.exp(jnp.where(gts[i]["incl"], gts[i]["gc"][:, cgs[i]:cgs[i] + 1] - gts[i]["gc_t"][cgs[i]:cgs[i] + 1, :],
                               -1e30)) for i in rng]
    kbeta = [ks[i] * beta[i] for i in rng]
    prod = [lax.dot_general(jnp.concatenate([kbeta[i], qs[i]], axis=0).astype(BF16), ks[i].astype(BF16),
                            (((1,), (1,)), ((), ())), preferred_element_type=F32) for i in rng]
    a_mat = [jnp.where(gts[i]["strict"], prod[i][:c] * decay[i], 0.0) for i in rng]
    intra = [prod[i][c:] * decay[i] for i in rng]
    ks_qs = [_bdot(jnp.concatenate([kbeta[i] * egc[i], qs[i] * egc[i]], axis=0), ss[i]) for i in rng]
    p = [jnp.where(gts[i]["diag_blk"], -a_mat[i], 0.0) for i in rng]
    y = [jnp.concatenate([vs[i] * beta[i] - ks_qs[i][:c], jnp.where(gts[i]["diag_blk"], 0.0, a_mat[i])], axis=1)
         for i in rng]
    for j in range(4):
        y = [y[i] + _bdot(p[i], y[i]) for i in rng]
        if j < 3:
            p = [_bdot(p[i], p[i]) for i in rng]
    e = [y[i][:, dh:] for i in rng]
    y = [y[i][:, :dh] for i in rng]
    e2 = [_bdot(e[i], e[i]) for i in rng]
    y = [y[i] + _bdot(e2[i], y[i]) for i in rng]
    v_new = [y[i] - _bdot(e[i], y[i]) for i in rng]
    o = [ks_qs[i][c:] + _bdot(intra[i], v_new[i]) for i in rng]
    s_new = [ss[i] * gts[i]["glast"][:, cgs[i]:cgs[i] + 1] + lax.dot_general(
        (ks[i] * gts[i]["kdec"][:, cgs[i]:cgs[i] + 1]).astype(BF16), v_new[i].astype(BF16),
        (((0,), (0,)), ((), ())), preferred_element_type=F32) for i in rng]
    return o, s_new


def _deltanet_kernel(nb, qf_ref, kf_ref, vf_ref, gf_ref, qb_ref, kb_ref, vb_ref, gb_ref, alog_ref, dtb_ref,
                     of_ref, ob_ref, s_ref):
    @pl.when(pl.program_id(0) == 0)
    def _():
        s_ref[...] = jnp.zeros_like(s_ref)

    dirs = ((qf_ref, kf_ref, vf_ref, gf_ref, of_ref), (qb_ref, kb_ref, vb_ref, gb_ref, ob_ref))

    def body(it, carry):
        idx = [(it * DN_BATCHES + j, d, hh) for j in range(DN_BATCHES) for d in range(2) for hh in range(DN_HEADS)]
        rows_of = {j: pl.ds(it * DN_BATCHES + j, CHUNK, stride=nb) for j in range(DN_BATCHES)}
        rows_b = [rows_of[i // (2 * DN_HEADS)] for i in range(len(idx))]
        cbs = [d * DN_HEADS + hh for _, d, hh in idx]
        gates = {(j, d): _deltanet_gates(d, dirs[d][3][rows_of[j], :], alog_ref[...], dtb_ref[...])
                 for j in range(DN_BATCHES) for d in range(2)}
        gts = [gates[(i // (2 * DN_HEADS), d)] for i, (_, d, _) in enumerate(idx)]
        qs = [dirs[d][0][hh, rows_b[i], :] for i, (_, d, hh) in enumerate(idx)]
        ks = [dirs[d][1][hh, rows_b[i], :] for i, (_, d, hh) in enumerate(idx)]
        vs = [dirs[d][2][hh, rows_b[i], :] for i, (_, d, hh) in enumerate(idx)]
        ss = [s_ref[b * (2 * DN_HEADS) + cbs[i]] for i, (b, _, _) in enumerate(idx)]
        o, s_new = _deltanet_heads(gts, cbs, qs, ks, vs, ss)
        for i, (b, d, hh) in enumerate(idx):
            dirs[d][4][hh, rows_b[i], :] = o[i]
            s_ref[b * (2 * DN_HEADS) + cbs[i]] = s_new[i]
        return carry

    lax.fori_loop(0, nb // DN_BATCHES, body, 0)


def _deltanet(qkv, proj, alog_vec, dtb_vec, nb, n_ctx_chunks):
    rows = qkv.shape[1]
    tr = CHUNK * nb
    n_tot = rows // tr
    fwd = lambda i: i
    bwd = lambda i: _bwd_chunk(i, n_ctx_chunks, n_tot)

    def specs(order):
        return [pl.BlockSpec((DN_HEADS, tr, DN_HEAD_DIM), lambda i, p=p: (p, order(i), 0)) for p in range(3)] + [
            pl.BlockSpec((tr, LANES), lambda i: (order(i), COL_GATE))]

    vec = pl.BlockSpec((1, LANES), lambda i: (0, 0))
    out = jax.ShapeDtypeStruct((DN_HEADS, rows, DN_HEAD_DIM), F32)
    return pl.pallas_call(
        functools.partial(_deltanet_kernel, nb),
        grid=(n_tot,),
        in_specs=specs(fwd) + specs(bwd) + [vec, vec],
        out_specs=[pl.BlockSpec((DN_HEADS, tr, DN_HEAD_DIM), lambda i: (0, fwd(i), 0)),
                   pl.BlockSpec((DN_HEADS, tr, DN_HEAD_DIM), lambda i: (0, bwd(i), 0))],
        out_shape=[out, out],
        scratch_shapes=[pltpu.VMEM((nb * 2 * DN_HEADS, DN_HEAD_DIM, DN_HEAD_DIM), F32)],
        compiler_params=_cparams(("arbitrary",)),
        name="deltanet",
    )(qkv, qkv, qkv, proj, qkv, qkv, qkv, proj, alog_vec, dtb_vec)


def _mixer_x1(nb, of_ref, ob_ref, z_ref, hf_ref, hb_ref, ry_ref, onw_ref, w_ref, h_ref, gt_ref, g_ref, b_ref):
    z = z_ref[...]
    parts = []
    for hh in range(DN_HEADS):
        sl = slice(hh * DN_HEAD_DIM, (hh + 1) * DN_HEAD_DIM)
        oh = of_ref[hh] + ob_ref[hh]
        inv = lax.rsqrt(jnp.mean(oh * oh, axis=-1, keepdims=True) + NORM_EPS)
        parts.append(oh * inv * onw_ref[...] * _silu(z[:, sl]))
    parts.append((hf_ref[...].astype(F32) + hb_ref[...].astype(F32)) * jax.nn.gelu(ry_ref[...]))
    y = jnp.concatenate(parts, axis=1).astype(BF16)
    u = jnp.dot(y, w_ref[...], preferred_element_type=F32)
    r = DEEP_ALPHA * h_ref[...] + _per_batch(u, gt_ref[...], nb) * u
    return _layer_norm(r) * g_ref[...] + b_ref[...]


PAIRS_PER_GROUP = EXPERTS_PER_GROUP * (EXPERTS_PER_GROUP - 1) // 2
N_CLASSES = N_GROUPS * PAIRS_PER_GROUP
assert N_CLASSES <= LANES
INFO_CLASS, INFO_RANK, INFO_W_LO, INFO_W_HI = range(4)
Y_SUB = 2 * (D_MODEL // LANES)


ROW_SUB = D_MODEL // LANES


def _store_row_tiles(ref, x):
    n = x.shape[0]
    for c in range(ROW_SUB):
        ref[pl.ds(c, n, stride=ROW_SUB), :] = x[:, c * LANES:(c + 1) * LANES]


def _load_row_tiles(ref, n):
    return jnp.concatenate([ref[pl.ds(c, n, stride=ROW_SUB), :] for c in range(ROW_SUB)], axis=1)


def _first_index(mask, lane):
    return jnp.min(jnp.where(mask, lane, LANES), axis=-1, keepdims=True)


def _route(nb, x, sh_ref, sc_ref, wr_ref, br_ref, hm_ref, info_ref, cnt_ref, run_ref):
    @pl.when(pl.program_id(0) == 0)
    def _():
        run_ref[...] = jnp.zeros_like(run_ref)

    xn = _layer_norm(x)
    hm = xn * (1.0 + _per_batch(xn, sc_ref[...], nb)) + _per_batch(xn, sh_ref[...], nb)
    _store_row_tiles(hm_ref, hm)
    tm = hm.shape[0]
    h_hi = hm.astype(BF16)
    h_lo = (hm - h_hi.astype(F32)).astype(BF16)
    hi_hl = jnp.dot(h_hi, wr_ref[...], preferred_element_type=F32)
    logits = (hi_hl[:, :LANES] + hi_hl[:, LANES:] + jnp.dot(h_lo, wr_ref[:, :LANES], preferred_element_type=F32)
              + br_ref[...])
    lane = lax.broadcasted_iota(jnp.int32, (tm, LANES), 1)
    neg = -jnp.inf
    is_g = lane < N_GROUPS
    gmax = jnp.max(jnp.where(is_g, logits, neg), axis=-1, keepdims=True)
    g_sel = _first_index(jnp.logical_and(is_g, logits == gmax), lane)
    p_group = 1.0 / jnp.sum(jnp.where(is_g, jnp.exp(logits - gmax), 0.0), axis=-1, keepdims=True)
    lo = N_GROUPS + EXPERTS_PER_GROUP * g_sel
    in_grp = jnp.logical_and(lane >= lo, lane < lo + EXPERTS_PER_GROUP)
    m1 = jnp.max(jnp.where(in_grp, logits, neg), axis=-1, keepdims=True)
    i1 = _first_index(jnp.logical_and(in_grp, logits == m1), lane)
    rest = jnp.logical_and(in_grp, lane != i1)
    m2 = jnp.max(jnp.where(rest, logits, neg), axis=-1, keepdims=True)
    i2 = _first_index(jnp.logical_and(rest, logits == m2), lane)
    e2 = jnp.exp(m2 - m1)
    w1 = p_group / (1.0 + e2)
    w2 = p_group * e2 / (1.0 + e2)
    l1 = i1 - lo
    l2 = i2 - lo
    e_lo = jnp.minimum(l1, l2)
    e_hi = jnp.maximum(l1, l2)
    pair = jnp.right_shift(e_lo * (2 * EXPERTS_PER_GROUP - 1 - e_lo), 1) + (e_hi - e_lo - 1)
    cls = g_sel * PAIRS_PER_GROUP + pair
    first_is_lo = l1 < l2
    w_lo = jnp.where(first_is_lo, w1, w2)
    w_hi = jnp.where(first_is_lo, w2, w1)
    oh = lane == cls
    r_i = lax.broadcasted_iota(jnp.int32, (tm, tm), 0)
    c_i = lax.broadcasted_iota(jnp.int32, (tm, tm), 1)
    before = jnp.dot((r_i > c_i).astype(BF16), oh.astype(BF16), preferred_element_type=F32) + run_ref[...]
    rank = jnp.sum(jnp.where(oh, before, 0.0), axis=-1, keepdims=True)
    run_ref[...] = run_ref[...] + jnp.sum(oh.astype(F32), axis=0, keepdims=True)
    cnt_ref[...] = run_ref[...]
    info = jnp.zeros((tm, LANES), F32)
    for idx, val in ((INFO_CLASS, cls.astype(F32)), (INFO_RANK, rank), (INFO_W_LO, w_lo), (INFO_W_HI, w_hi)):
        info = jnp.where(lane == idx, val, info)
    info_ref[...] = info


def _mixer_router_kernel(nb, of_ref, ob_ref, z_ref, hf_ref, hb_ref, ry_ref, onw_ref, w_ref, h_ref, gt_ref, g_ref, b_ref,
                         sh_ref, sc_ref, wr_ref, br_ref, x1_ref, hm_ref, info_ref, cnt_ref, run_ref):
    x1 = _mixer_x1(nb, of_ref, ob_ref, z_ref, hf_ref, hb_ref, ry_ref, onw_ref, w_ref, h_ref, gt_ref, g_ref, b_ref)
    x1_ref[...] = x1
    _route(nb, x1, sh_ref, sc_ref, wr_ref, br_ref, hm_ref, info_ref, cnt_ref, run_ref)


def _mixer_router(o_f, o_b, proj, hf, hb, onorm_w, w_out, layer, h, mod_rows, ln_g, ln_b, wr, br, nb, ctx_rows, row0):
    rows, d = h.shape
    tm = ROW_TILE
    t0 = row0 // tm
    n = rows - row0
    lat = lambda i: jnp.where((i + t0) * tm >= ctx_rows, 1, 0)
    half = pl.BlockSpec((tm, DN_WIDTH), lambda i: (i + t0, 0))
    planes = pl.BlockSpec((DN_HEADS, tm, DN_HEAD_DIM), lambda i: (0, i + t0, 0))
    vec = lambda m: pl.BlockSpec((1, m), lambda i: (0, 0))
    mod = lambda k: pl.BlockSpec((None, nb, d), lambda i: (lat(i), 0, k))
    return pl.pallas_call(
        functools.partial(_mixer_router_kernel, nb),
        grid=(n // tm,),
        in_specs=[planes, planes, pl.BlockSpec((tm, DN_WIDTH), lambda i: (i + t0, COL_Z)), half, half,
                  pl.BlockSpec((tm, RG_WIDTH), lambda i: (i + t0, COL_RY)),
                  vec(DN_HEAD_DIM), pl.BlockSpec((None, d, d), lambda i: (layer, 0, 0)),
                  pl.BlockSpec((tm, d), lambda i: (i + t0, 0)),
                  mod(2),
                  vec(d), vec(d),
                  mod(3), mod(4),
                  pl.BlockSpec((d, 2 * LANES), lambda i: (0, 0)), vec(LANES)],
        out_specs=[pl.BlockSpec((tm, d), lambda i: (i, 0)),
                   pl.BlockSpec((tm * ROW_SUB, LANES), lambda i: (i, 0)),
                   pl.BlockSpec((tm, LANES), lambda i: (i, 0)),
                   pl.BlockSpec((1, LANES), lambda i: (0, 0))],
        out_shape=[jax.ShapeDtypeStruct((n, d), F32),
                   jax.ShapeDtypeStruct((n * ROW_SUB, LANES), F32),
                   jax.ShapeDtypeStruct((n, LANES), F32),
                   jax.ShapeDtypeStruct((1, LANES), F32)],
        scratch_shapes=[pltpu.VMEM((1, LANES), F32)],
        compiler_params=_cparams(("arbitrary",)),
        name="mixer_router",
    )(o_f, o_b, proj, hf, hb, proj, onorm_w, w_out, h, mod_rows, ln_g, ln_b, mod_rows, mod_rows, wr, br)


def _dispatch_kernel(tmx, ends_ref, cnt_ref, dst_ref, hm_ref, xs_hbm, zero_scr, sem, zsem):
    rs = ROW_SUB
    tb = dst_ref.shape[1]

    @pl.when(pl.program_id(0) == 0)
    def _():
        zero_scr[...] = jnp.zeros_like(zero_scr)

        def tail_copy(c):
            start = pl.multiple_of((ends_ref[c] - tmx) * rs, tmx * rs)
            return pltpu.make_async_copy(zero_scr, xs_hbm.at[pl.ds(start, tmx * rs)], zsem)

        def start_one(c, carry):
            @pl.when(cnt_ref[c] > 0)
            def _():
                tail_copy(c).start()
            return carry

        def wait_one(c, carry):
            @pl.when(cnt_ref[c] > 0)
            def _():
                tail_copy(c).wait()
            return carry

        lax.fori_loop(0, N_CLASSES, start_one, 0)

        def spare_copy(j):
            return pltpu.make_async_copy(zero_scr, xs_hbm.at[pl.ds(pl.multiple_of(j * (tmx * rs), tmx * rs), tmx * rs)],
                                         zsem)

        n_used = ends_ref[N_CLASSES - 1] // tmx
        n_tiles = xs_hbm.shape[0] // (tmx * rs)
        lax.fori_loop(n_used, n_tiles, lambda j, carry: (spare_copy(j).start(), carry)[1], 0)
        lax.fori_loop(0, N_CLASSES, wait_one, 0)
        lax.fori_loop(n_used, n_tiles, lambda j, carry: (spare_copy(j).wait(), carry)[1], 0)

    for r in range(tb):
        dst = xs_hbm.at[pl.ds(pl.multiple_of(dst_ref[0, r], rs), rs)]
        pltpu.make_async_copy(hm_ref.at[pl.ds(r * rs, rs)], dst, sem).start(priority=r % 2)
    pltpu.make_async_copy(hm_ref, hm_ref, sem).wait()


def _dispatch(hm, dest, class_ends, class_cnt, sorted_rows, tmx):
    tb = ROW_TILE
    n_blocks = dest.shape[0] // tb
    grid_spec = pltpu.PrefetchScalarGridSpec(
        num_scalar_prefetch=2,
        grid=(n_blocks,),
        in_specs=[pl.BlockSpec((None, 1, tb), lambda i, e, c: (i, 0, 0), memory_space=pltpu.SMEM),
                  pl.BlockSpec((tb * ROW_SUB, LANES), lambda i, e, c: (i, 0))],
        out_specs=pl.BlockSpec(memory_space=pl.ANY),
        scratch_shapes=[pltpu.VMEM((tmx * ROW_SUB, LANES), F32), pltpu.SemaphoreType.DMA(()),
                        pltpu.SemaphoreType.DMA(())],
    )
    return pl.pallas_call(
        functools.partial(_dispatch_kernel, tmx),
        grid_spec=grid_spec,
        out_shape=jax.ShapeDtypeStruct((sorted_rows * ROW_SUB, LANES), F32),
        compiler_params=_cparams(("arbitrary",)),
        name="moe_dispatch",
    )(class_ends, class_cnt, (dest * ROW_SUB).reshape(n_blocks, 1, tb), hm)


def _class_ffn_kernel(telo_ref, tehi_ref, nact_ref, xs_ref, wgul_ref, wdl_ref, wguh_ref, wdh_ref, ys_ref, y_scr):
    j = pl.program_id(0)
    tmx = xs_ref.shape[0] // ROW_SUB
    de = wdl_ref.shape[0]

    @pl.when(j < nact_ref[0])
    def _():
        x = _load_row_tiles(xs_ref, tmx).astype(BF16)
        for half, (wgu_ref, wd_ref) in enumerate(((wgul_ref, wdl_ref), (wguh_ref, wdh_ref))):
            ab = jnp.dot(x, wgu_ref[...], preferred_element_type=F32)
            y = jnp.dot((_silu(ab[:, :de]) * ab[:, de:]).astype(BF16), wd_ref[...], preferred_element_type=F32)
            for c in range(ROW_SUB):
                y_scr[pl.ds(half * ROW_SUB + c, tmx, stride=Y_SUB), :] = y[:, c * LANES:(c + 1) * LANES]
        ys_ref[...] = y_scr[...].astype(BF16)

    @pl.when(j >= nact_ref[0])
    def _():
        ys_ref[...] = jnp.zeros_like(ys_ref)


def _class_ffn(xs, tile_e_lo, tile_e_hi, n_active, w_gate_up, w_down, layer, tmx):
    n_tiles = xs.shape[0] // (tmx * ROW_SUB)
    d, de2 = w_gate_up.shape[-2:]
    w_in = lambda te_idx: pl.BlockSpec((None, None, d, de2), lambda j, *te: (layer, te[te_idx][j], 0, 0))
    w_out = lambda te_idx: pl.BlockSpec((None, None, de2 // 2, d), lambda j, *te: (layer, te[te_idx][j], 0, 0))
    grid_spec = pltpu.PrefetchScalarGridSpec(
        num_scalar_prefetch=3,
        grid=(n_tiles,),
        in_specs=[pl.BlockSpec((tmx * ROW_SUB, LANES), lambda j, lo, hi, na: (jnp.minimum(j, na[0] - 1), 0)),
                  w_in(0), w_out(0), w_in(1), w_out(1)],
        out_specs=pl.BlockSpec((tmx * Y_SUB, LANES), lambda j, lo, hi, na: (j, 0)),
        scratch_shapes=[pltpu.VMEM((tmx * Y_SUB, LANES), F32)],
    )
    return pl.pallas_call(
        _class_ffn_kernel,
        grid_spec=grid_spec,
        out_shape=jax.ShapeDtypeStruct((n_tiles * tmx * Y_SUB, LANES), BF16),
        compiler_params=_cparams(("arbitrary",)),
        name="expert_ffn",
    )(tile_e_lo, tile_e_hi, n_active, xs, w_gate_up, w_down, w_gate_up, w_down)


def _moe_out_kernel(nb, batch_major, src0_ref, srcn_ref, ys_hbm, info_ref, x_ref, gt_ref, g_ref, b_ref, o_ref,
                    ybuf, y_scr, sem):
    i = pl.program_id(0)
    tm = info_ref.shape[0]
    slot = i % 2
    nslot = 1 - slot

    def gather_all(src_ref, s):
        for r in range(tm):
            pltpu.make_async_copy(ys_hbm.at[pl.ds(pl.multiple_of(src_ref[0, r], Y_SUB), Y_SUB)],
                                  ybuf.at[s, pl.ds(r * Y_SUB, Y_SUB)], sem.at[s]).start(priority=r % 2)

    def wait_rows(s):
        pltpu.make_async_copy(ybuf.at[s], ybuf.at[s], sem.at[s]).wait()

    @pl.when(i == 0)
    def _():
        gather_all(src0_ref, 0)

    wait_rows(slot)
    gather_all(srcn_ref, nslot)
    info = info_ref[...]
    y_scr[...] = ybuf[slot].astype(F32)
    y_lo = jnp.concatenate([y_scr[pl.ds(c, tm, stride=Y_SUB), :] for c in range(ROW_SUB)], axis=1)
    y_hi = jnp.concatenate([y_scr[pl.ds(ROW_SUB + c, tm, stride=Y_SUB), :] for c in range(ROW_SUB)], axis=1)
    f = y_lo * info[:, INFO_W_LO:INFO_W_LO + 1] + y_hi * info[:, INFO_W_HI:INFO_W_HI + 1]
    r = DEEP_ALPHA * x_ref[...] + _per_batch(f, gt_ref[...], nb) * f
    res = _layer_norm(r) * g_ref[...] + b_ref[...]
    if batch_major:
        for c in range(ROW_SUB):
            y_scr[pl.ds(c * tm, tm), :] = res[:, c * LANES:(c + 1) * LANES]
        for b in range(nb):
            for c in range(ROW_SUB):
                o_ref[b, :, c * LANES:(c + 1) * LANES] = y_scr[pl.ds(c * tm + b, tm // nb, stride=nb), :]
    else:
        o_ref[...] = res

    @pl.when(i == pl.num_programs(0) - 1)
    def _():
        wait_rows(nslot)


def _moe_out(ys, dest, info, x1, mod_rows, ln_g, ln_b, nb, ctx_rows, batch_major):
    rows, d = x1.shape
    tm = ROW_TILE
    n_blocks = rows // tm
    if batch_major:
        out_spec = pl.BlockSpec((nb, tm // nb, d), lambda i: (0, i, 0))
        out_shape = jax.ShapeDtypeStruct((nb, rows // nb, d), F32)
    else:
        out_spec = pl.BlockSpec((tm, d), lambda i: (i, 0))
        out_shape = jax.ShapeDtypeStruct((rows, d), F32)
    lat = lambda i: jnp.where(i * tm >= ctx_rows, 1, 0)
    vec = pl.BlockSpec((1, d), lambda i: (0, 0))
    src = (dest * Y_SUB).reshape(n_blocks, 1, tm)
    smem = lambda f: pl.BlockSpec((None, 1, tm), f, memory_space=pltpu.SMEM)
    return pl.pallas_call(
        functools.partial(_moe_out_kernel, nb, batch_major),
        grid=(n_blocks,),
        in_specs=[smem(lambda i: (0, 0, 0)),
                  smem(lambda i: (jnp.minimum(i + 1, n_blocks - 1), 0, 0)),
                  pl.BlockSpec(memory_space=pl.ANY),
                  pl.BlockSpec((tm, LANES), lambda i: (i, 0)),
                  pl.BlockSpec((tm, d), lambda i: (i, 0)),
                  pl.BlockSpec((None, nb, d), lambda i: (lat(i), 0, 5)),
                  vec, vec],
        out_specs=out_spec,
        out_shape=out_shape,
        scratch_shapes=[pltpu.VMEM((2, tm * Y_SUB, LANES), BF16), pltpu.VMEM((tm * Y_SUB, LANES), F32),
                        pltpu.SemaphoreType.DMA((2,))],
        compiler_params=_cparams(("arbitrary",)),
        name="moe_out",
    )(src, src, ys, info, x1, mod_rows, ln_g, ln_b)


def _class_experts():
    lo, hi = [], []
    for g in range(N_GROUPS):
        for a in range(EXPERTS_PER_GROUP):
            for b in range(a + 1, EXPERTS_PER_GROUP):
                lo.append(g * EXPERTS_PER_GROUP + a)
                hi.append(g * EXPERTS_PER_GROUP + b)
    return np.asarray(lo, np.int32), np.asarray(hi, np.int32)


def _routing_tables(info, counts, n_tiles, tmx):
    cnt = counts[0, :N_CLASSES].astype(jnp.int32)
    padded = ((cnt + tmx - 1) // tmx) * tmx
    ends = jnp.cumsum(padded)
    starts = ends - padded
    cls = info[:, INFO_CLASS].astype(jnp.int32)
    class_ids = jnp.arange(N_CLASSES, dtype=jnp.int32)
    start_of_tok = jnp.sum(jnp.where(cls[:, None] == class_ids[None, :], starts[None, :], 0), axis=1)
    dest = start_of_tok + info[:, INFO_RANK].astype(jnp.int32)
    tile_start = jnp.arange(n_tiles, dtype=jnp.int32) * tmx
    n_before = jnp.sum((ends[None, :] <= tile_start[:, None]).astype(jnp.int32), axis=1)
    tile_cls = jnp.minimum(n_before, N_CLASSES - 1)
    e_lo, e_hi = _class_experts()
    n_active = (ends[-1:] // tmx).astype(jnp.int32)
    return dest, ends, cnt, jnp.asarray(e_lo)[tile_cls], jnp.asarray(e_hi)[tile_cls], n_active


def _relayout_w_in(w_in):
    qkvz = w_in[..., :4 * DN_WIDTH]
    gates = w_in[..., 4 * DN_WIDTH:4 * DN_WIDTH + 4 * DN_HEADS]
    rxy = w_in[..., 4 * DN_WIDTH + 4 * DN_HEADS:]
    pad = jnp.zeros(w_in.shape[:-1] + (LANES - 4 * DN_HEADS,), w_in.dtype)
    return jnp.concatenate([qkvz, rxy, gates, pad], axis=-1).astype(BF16)


def _block_diag(w):
    n, c, _ = w.shape
    eye = jnp.eye(n, dtype=w.dtype)
    return (eye[:, None, :, None] * w[:, :, None, :]).reshape(n * c, n * c)


def _lane_vec(vals, offset):
    return jnp.zeros((1, LANES), F32).at[0, offset:offset + vals.size].set(vals.reshape(-1))


def kernel(x, c, ctx, c_ctx, w_ada, b_ada, w_in, conv_qkv_w, dn_a_log, dn_dt_bias, dn_onorm_w, rg_conv_w, rg_conv_b, rg_wa, rg_ba, rg_wi, rg_bi, rg_lambda, w_out, ln_g, ln_b, router_wg, router_bg, router_we, router_be, w_e_gate, w_e_up, w_e_down):
    nb, t_lat, d = x.shape
    t_ctx = ctx.shape[1]
    assert d == D_MODEL and nb % 8 == 0 and t_lat % GRID_W == 0 and t_ctx % CHUNK == 0
    assert (t_ctx * nb) % INPROJ_TILE == 0 and (t_lat * nb) % INPROJ_TILE == 0 and INPROJ_TILE % ROW_TILE == 0
    tt = t_ctx + t_lat
    rows = tt * nb
    ctx_rows = t_ctx * nb
    n_ctx_chunks = t_ctx // CHUNK

    n_cc = ((nb + 1 + 7) // 8) * 8
    cc = jnp.zeros((n_cc, d), F32).at[:nb].set(c).at[nb].set(c_ctx)
    mod = _modulation(cc, w_ada, b_ada)
    mod_ctx = jnp.broadcast_to(mod[:, nb:nb + 1], (DEPTH, nb, 6 * d))
    mod_rows = jnp.stack([mod_ctx, mod[:, :nb]], axis=1)

    w_in2 = _relayout_w_in(w_in)
    w_out_bf = w_out.astype(BF16)
    w_gate_up = jnp.concatenate([w_e_gate, w_e_up], axis=-1).astype(BF16)
    w_down = w_e_down.astype(BF16)
    for l in range(DEPTH):
        row0 = ctx_rows if l == DEPTH - 1 else 0
        moe_ctx_rows = ctx_rows - row0
        n_tiles = (rows - row0) // MOE_TILE + N_CLASSES
        mr = mod_rows[l]
        if l == 0:
            proj, h = _in_projection_first(ctx, x, mr, w_in2, l, nb)
        else:
            proj = _in_projection(h, mr, w_in2, l, nb, ctx_rows)
        qkv = _qkv_prep(proj, conv_qkv_w[l], nb, n_ctx_chunks)
        wg = jnp.concatenate([_block_diag(rg_wa[l, 0]), _block_diag(rg_wi[l, 0]),
                              _block_diag(rg_wa[l, 1]), _block_diag(rg_wi[l, 1])], axis=1).astype(BF16)
        bg = jnp.concatenate([rg_ba[l, 0], rg_bi[l, 0], rg_ba[l, 1], rg_bi[l, 1]])[None]
        hf, hb = _rglru(proj, rg_conv_w[l], rg_conv_b[l][None], wg, bg, rg_lambda[l], nb, n_ctx_chunks)
        alog_vec = _lane_vec(dn_a_log[l], 2 * DN_HEADS)
        dtb_vec = _lane_vec(dn_dt_bias[l], 2 * DN_HEADS)
        o_f, o_b = _deltanet(qkv, proj, alog_vec, dtb_vec, nb, n_ctx_chunks)
        wr = jnp.zeros((d, LANES), F32).at[:, :N_GROUPS].set(router_wg[l]).at[
            :, N_GROUPS:N_GROUPS + N_EXPERTS].set(router_we[l])
        br = _lane_vec(jnp.concatenate([router_bg[l], router_be[l]]), 0)
        wr_hi = wr.astype(BF16)
        wr_hl = jnp.concatenate([wr_hi, (wr - wr_hi.astype(F32)).astype(BF16)], axis=1)
        x1, hm, info, counts = _mixer_router(o_f, o_b, proj, hf, hb, dn_onorm_w[l][None], w_out_bf, l, h, mr,
                                             ln_g[l, 0][None], ln_b[l, 0][None], wr_hl, br, nb, ctx_rows, row0)
        dest, class_ends, class_cnt, tile_e_lo, tile_e_hi, n_active = _routing_tables(info, counts, n_tiles, MOE_TILE)
        xs = _dispatch(hm, dest, class_ends, class_cnt, n_tiles * MOE_TILE, MOE_TILE)
        ys = _class_ffn(xs, tile_e_lo, tile_e_hi, n_active, w_gate_up, w_down, l, MOE_TILE)
        h = _moe_out(ys, dest, info, x1, mr, ln_g[l, 1][None], ln_b[l, 1][None], nb, moe_ctx_rows,
                     batch_major=(l == DEPTH - 1))

    return h
```

```python
import functools

import jax
import jax.numpy as jnp
import numpy as np
from jax import lax
from jax.experimental import pallas as pl
from jax.experimental.pallas import tpu as pltpu

F32 = jnp.float32
BF16 = jnp.bfloat16

D_MODEL = 1024
DEPTH = 2
GRID_W = 64
DN_HEAD_DIM = 128
DN_WIDTH = 512
DN_HEADS = 4
RG_WIDTH = 512
RG_BLOCKS = 8
RG_BLOCK_DIM = 64
RG_C = 8.0
CONV_K = 4
N_GROUPS = 4
EXPERTS_PER_GROUP = 8
N_EXPERTS = 32
D_EXPERT = 256
DEEP_ALPHA = (2 * DEPTH) ** 0.25
LN_EPS = 1e-5
NORM_EPS = 1e-6

LANES = 128
CHUNK = 64
SOLVE_BLOCK = 16
DN_BATCHES = 4
D_PROJ = 3 * DN_WIDTH + DN_WIDTH + 2 * RG_WIDTH + LANES
COL_Z = 3
COL_RX = 4
COL_RY = 5
COL_GATE = (3 * DN_WIDTH + DN_WIDTH + 2 * RG_WIDTH) // LANES
ROW_TILE = 512
INPROJ_TILE = 1024
MOE_TILE = 384
VMEM_LIMIT = 56 * 1024 * 1024


def _cparams(sem):
    return pltpu.CompilerParams(dimension_semantics=sem, vmem_limit_bytes=VMEM_LIMIT)


def _layer_norm(x):
    mu = jnp.mean(x, axis=-1, keepdims=True)
    xc = x - mu
    var = jnp.mean(xc * xc, axis=-1, keepdims=True)
    return xc * lax.rsqrt(var + LN_EPS)


def _per_batch(x, v, nb):
    rows, c = x.shape
    return jnp.broadcast_to(v[None], (rows // nb, nb, c)).reshape(rows, c)


def _sigmoid(x):
    return 0.5 * jnp.tanh(0.5 * x) + 0.5


def _silu(x):
    h = 0.5 * x
    return h * jnp.tanh(h) + h


def _mod_kernel(cc_ref, w_ref, b_ref, o_ref):
    s = _silu(cc_ref[...])
    o_ref[...] = jnp.dot(s.astype(BF16), w_ref[...].astype(BF16), preferred_element_type=F32) + b_ref[...]


def _modulation(cc, w_ada, b_ada):
    depth, d, n6 = w_ada.shape
    rows = cc.shape[0]
    tn = 1536
    return pl.pallas_call(
        _mod_kernel,
        grid=(depth, n6 // tn),
        in_specs=[
            pl.BlockSpec((rows, d), lambda l, j: (0, 0)),
            pl.BlockSpec((None, d, tn), lambda l, j: (l, 0, j)),
            pl.BlockSpec((None, 1, tn), lambda l, j: (l, 0, j)),
        ],
        out_specs=pl.BlockSpec((None, rows, tn), lambda l, j: (l, 0, j)),
        out_shape=jax.ShapeDtypeStruct((depth, rows, n6), F32),
        compiler_params=_cparams(("parallel", "parallel")),
        name="adaln_mod",
    )(cc, w_ada, b_ada.reshape(depth, 1, n6))


def _inproj_kernel(nb, h_ref, sh_ref, sc_ref, w_ref, o_ref):
    xn = _layer_norm(h_ref[...])
    hm = xn * (1.0 + _per_batch(xn, sc_ref[...], nb)) + _per_batch(xn, sh_ref[...], nb)
    o_ref[...] = jnp.dot(hm.astype(BF16), w_ref[...], preferred_element_type=F32)


def _inproj_first_kernel(nb, n_ctx_tiles, ctx_ref, x_ref, sh_ref, sc_ref, w_ref, o_ref, h_ref, scr):
    tt = ctx_ref.shape[1]
    tm = nb * tt
    blk = jnp.where(pl.program_id(0) < n_ctx_tiles, ctx_ref[...], x_ref[...])
    for b in range(nb):
        for c in range(ROW_SUB):
            scr[pl.ds(c * tm + b, tt, stride=nb), :] = blk[b, :, c * LANES:(c + 1) * LANES]
    h = jnp.concatenate([scr[pl.ds(c * tm, tm), :] for c in range(ROW_SUB)], axis=1)
    h_ref[...] = h
    xn = _layer_norm(h)
    hm = xn * (1.0 + _per_batch(xn, sc_ref[...], nb)) + _per_batch(xn, sh_ref[...], nb)
    o_ref[...] = jnp.dot(hm.astype(BF16), w_ref[...], preferred_element_type=F32)


def _in_projection_first(ctx, x, mod_rows, w_in2, layer, nb):
    _, t_ctx, d = ctx.shape
    t_lat = x.shape[1]
    tm = ROW_TILE
    tt = tm // nb
    n_ctx_tiles = t_ctx // tt
    rows = (t_ctx + t_lat) * nb
    lat = lambda i: jnp.where(i >= n_ctx_tiles, 1, 0)
    return pl.pallas_call(
        functools.partial(_inproj_first_kernel, nb, n_ctx_tiles),
        grid=(rows // tm,),
        in_specs=[
            pl.BlockSpec((nb, tt, d), lambda i: (0, jnp.minimum(i, n_ctx_tiles - 1), 0)),
            pl.BlockSpec((nb, tt, d), lambda i: (0, jnp.maximum(i - n_ctx_tiles, 0), 0)),
            pl.BlockSpec((None, nb, d), lambda i: (lat(i), 0, 0)),
            pl.BlockSpec((None, nb, d), lambda i: (lat(i), 0, 1)),
            pl.BlockSpec((None, d, D_PROJ), lambda i: (layer, 0, 0)),
        ],
        out_specs=[pl.BlockSpec((tm, D_PROJ), lambda i: (i, 0)), pl.BlockSpec((tm, d), lambda i: (i, 0))],
        out_shape=[jax.ShapeDtypeStruct((rows, D_PROJ), F32), jax.ShapeDtypeStruct((rows, d), F32)],
        scratch_shapes=[pltpu.VMEM((ROW_SUB * tm, LANES), F32)],
        compiler_params=_cparams(("parallel",)),
        name="ln_mod_inproj_first",
    )(ctx, x, mod_rows, mod_rows, w_in2)


def _in_projection(h, mod_rows, w_in2, layer, nb, ctx_rows):
    rows, d = h.shape
    tm = INPROJ_TILE
    lat = lambda i: jnp.where(i * tm >= ctx_rows, 1, 0)
    return pl.pallas_call(
        functools.partial(_inproj_kernel, nb),
        grid=(rows // tm,),
        in_specs=[
            pl.BlockSpec((tm, d), lambda i: (i, 0)),
            pl.BlockSpec((None, nb, d), lambda i: (lat(i), 0, 0)),
            pl.BlockSpec((None, nb, d), lambda i: (lat(i), 0, 1)),
            pl.BlockSpec((None, d, D_PROJ), lambda i: (layer, 0, 0)),
        ],
        out_specs=pl.BlockSpec((tm, D_PROJ), lambda i: (i, 0)),
        out_shape=jax.ShapeDtypeStruct((rows, D_PROJ), F32),
        compiler_params=_cparams(("parallel",)),
        name="ln_mod_inproj",
    )(h, mod_rows, mod_rows, w_in2)


def _conv_tile(x, prev, nxt, w, nb, use_prev, use_next):
    prev = jnp.where(use_prev, prev, 0.0)
    nxt = jnp.where(use_next, nxt, 0.0)
    xm2 = jnp.concatenate([prev, x[: -2 * nb]], axis=0)
    xm1 = jnp.concatenate([prev[nb:], x[:-nb]], axis=0)
    xp1 = jnp.concatenate([x[nb:], nxt], axis=0)
    return xm2 * w[0:1] + xm1 * w[1:2] + x * w[2:3] + xp1 * w[3:4]


def _halo_flags(i, n_ctx_tiles):
    use_prev = jnp.logical_and(i > 0, i < n_ctx_tiles)
    use_next = i < n_ctx_tiles - 1
    return use_prev, use_next


def _qkv_prep_kernel(nb, n_ctx_tiles, x_ref, prev_ref, next_ref, w_ref, o_ref):
    i = pl.program_id(0)
    part = pl.program_id(1)
    use_prev, use_next = _halo_flags(i, n_ctx_tiles)
    y = _silu(_conv_tile(x_ref[...], prev_ref[...], next_ref[...], w_ref[...], nb, use_prev, use_next))
    qscale = jnp.where(part == 0, DN_HEAD_DIM ** -0.5, 1.0).astype(F32)
    for hh in range(DN_HEADS):
        yh = y[:, hh * DN_HEAD_DIM:(hh + 1) * DN_HEAD_DIM]
        inv = lax.rsqrt(jnp.sum(yh * yh, axis=-1, keepdims=True) + NORM_EPS) * qscale
        o_ref[hh] = yh * jnp.where(part < 2, inv, 1.0)


def _halo_specs(tile_rows, nb, cols, col_of, n_row_blocks16, tile_of=lambda i: i):
    prev_b = tile_rows // (2 * nb)
    next_b = tile_rows // nb
    return [
        pl.BlockSpec((tile_rows, cols), lambda i, *a: (tile_of(i), col_of(*a))),
        pl.BlockSpec((2 * nb, cols), lambda i, *a: (jnp.maximum(tile_of(i) * prev_b - 1, 0), col_of(*a))),
        pl.BlockSpec((nb, cols),
                     lambda i, *a: (jnp.minimum((tile_of(i) + 1) * next_b, n_row_blocks16 - 1), col_of(*a))),
    ]


def _qkv_prep(proj, conv_w, nb, n_ctx_tiles):
    rows = proj.shape[0]
    tr = CHUNK * nb
    return pl.pallas_call(
        functools.partial(_qkv_prep_kernel, nb, n_ctx_tiles),
        grid=(rows // tr, 3),
        in_specs=_halo_specs(tr, nb, DN_WIDTH, lambda p: p, rows // nb)
        + [pl.BlockSpec((CONV_K, DN_WIDTH), lambda i, p: (0, p))],
        out_specs=pl.BlockSpec((DN_HEADS, tr, DN_HEAD_DIM), lambda i, p: (p, i, 0)),
        out_shape=jax.ShapeDtypeStruct((3 * DN_HEADS, rows, DN_HEAD_DIM), F32),
        compiler_params=_cparams(("parallel", "parallel")),
        name="qkv_conv_norm",
    )(proj, proj, proj, conv_w)


def _bwd_chunk(i, n_ctx, n_tot):
    return jnp.where(i < n_ctx, n_ctx - 1 - i, n_tot + n_ctx - 1 - i)


def _rglru_kernel(nb, n_ctx_tiles, n_tot, xf_ref, pf_ref, nf_ref, xb_ref, pb_ref, nb_ref, cw_ref, cb_ref,
                  wg0_ref, wg1_ref, bg0_ref, bg1_ref, lam_ref, hf_ref, hb_ref, a_scr, b_scr, st_ref):
    i = pl.program_id(0)

    @pl.when(i == 0)
    def _():
        st_ref[...] = jnp.zeros_like(st_ref)

    w = RG_WIDTH
    sp = jax.nn.softplus(-lam_ref[...])
    tiles = (i, _bwd_chunk(i, n_ctx_tiles, n_tot))
    for d, (x_ref, p_ref, n_ref, wg_ref, bg_ref) in enumerate(
            ((xf_ref, pf_ref, nf_ref, wg0_ref, bg0_ref), (xb_ref, pb_ref, nb_ref, wg1_ref, bg1_ref))):
        use_prev, use_next = _halo_flags(tiles[d], n_ctx_tiles)
        xc = _conv_tile(x_ref[...], p_ref[...], n_ref[...], cw_ref[...], nb, use_prev, use_next) + cb_ref[...]
        gates = jnp.dot(xc.astype(BF16), wg_ref[...], preferred_element_type=F32) + bg_ref[...]
        ig = _sigmoid(gates[:, w:])
        c = (-0.5 * RG_C) * sp[d:d + 1]
        a = jnp.exp(c * jnp.tanh(0.5 * gates[:, :w]) + c)
        y = jnp.maximum(1.0 - a * a, 0.0)
        a_scr[d] = a
        b_scr[d] = jnp.where(y > 0.0, y * lax.rsqrt(y), 0.0) * (ig * xc)

    def body(t, carry):
        h0, h1 = carry
        r0 = pl.ds(pl.multiple_of(t * nb, nb), nb)
        r1 = pl.ds(pl.multiple_of((CHUNK - 1 - t) * nb, nb), nb)
        h0 = a_scr[0, r0, :] * h0 + b_scr[0, r0, :]
        h1 = a_scr[1, r1, :] * h1 + b_scr[1, r1, :]
        hf_ref[r0, :] = h0.astype(hf_ref.dtype)
        hb_ref[r1, :] = h1.astype(hb_ref.dtype)
        return h0, h1

    h0, h1 = lax.fori_loop(0, CHUNK, body, (st_ref[0], st_ref[1]), unroll=8)
    st_ref[0] = h0
    st_ref[1] = h1


def _rglru(proj, conv_w, conv_b, wg, bg, lam, nb, n_ctx_tiles):
    rows = proj.shape[0]
    tr = CHUNK * nb
    n_tot = rows // tr
    w = RG_WIDTH
    out = jax.ShapeDtypeStruct((rows, w), BF16)
    fwd = lambda i: i
    bwd = lambda i: _bwd_chunk(i, n_ctx_tiles, n_tot)
    full = lambda shape: pl.BlockSpec(shape, lambda i: (0,) * len(shape))
    return pl.pallas_call(
        functools.partial(_rglru_kernel, nb, n_ctx_tiles, n_tot),
        grid=(n_tot,),
        in_specs=_halo_specs(tr, nb, w, lambda: COL_RX, rows // nb, fwd)
        + _halo_specs(tr, nb, w, lambda: COL_RX, rows // nb, bwd)
        + [full((CONV_K, w)), full((1, w)),
           pl.BlockSpec((w, 2 * w), lambda i: (0, 0)), pl.BlockSpec((w, 2 * w), lambda i: (0, 1)),
           pl.BlockSpec((1, 2 * w), lambda i: (0, 0)), pl.BlockSpec((1, 2 * w), lambda i: (0, 1)),
           full((2, w))],
        out_specs=[pl.BlockSpec((tr, w), lambda i: (fwd(i), 0)), pl.BlockSpec((tr, w), lambda i: (bwd(i), 0))],
        out_shape=[out] * 2,
        scratch_shapes=[pltpu.VMEM((2, tr, w), F32), pltpu.VMEM((2, tr, w), F32), pltpu.VMEM((2, nb, w), F32)],
        compiler_params=_cparams(("arbitrary",)),
        name="rglru",
    )(proj, proj, proj, proj, proj, proj, conv_w, conv_b, wg, wg, bg, bg, lam)


def _bdot(a, b):
    return jnp.dot(a.astype(BF16), b.astype(BF16), preferred_element_type=F32)


def _deltanet_gates(d, raw, alog, dtb):
    c = CHUNK
    row = lax.broadcasted_iota(jnp.int32, (c, c), 0)
    col = lax.broadcasted_iota(jnp.int32, (c, c), 1)
    incl = (row >= col) if d == 0 else (row <= col)
    g_all = -jnp.exp(alog) * jax.nn.softplus(raw + dtb)
    g_hi = g_all.astype(BF16)
    g_lo = (g_all - g_hi.astype(F32)).astype(BF16)
    tri = incl.astype(BF16)
    gc_all = (jnp.dot(tri, g_hi, preferred_element_type=F32) + jnp.dot(tri, g_lo, preferred_element_type=F32))
    g_tot = jnp.sum(g_all, axis=0, keepdims=True)
    return dict(incl=incl, strict=(row > col) if d == 0 else (row < col),
                diag_blk=(row // SOLVE_BLOCK) == (col // SOLVE_BLOCK),
                beta=_sigmoid(raw), gc=gc_all, gc_t=gc_all.T, egc=jnp.exp(gc_all),
                kdec=jnp.exp(g_tot - gc_all), glast=jnp.exp(g_tot))


def _deltanet_heads(gts, cbs, qs, ks, vs, ss):
    c = CHUNK
    dh = DN_HEAD_DIM
    n = len(cbs)
    rng = range(n)
    cgs = [2 * DN_HEADS + cb for cb in cbs]
    beta = [gts[i]["beta"][:, cbs[i]:cbs[i] + 1] for i in rng]
    egc = [gts[i]["egc"][:, cgs[i]:cgs[i] + 1] for i in rng]
    decay = [jnp.exp(jnp.where(gts[i]["incl"], gts[i]["gc"][:, cgs[i]:cgs[i] + 1] - gts[i]["gc_t"][cgs[i]:cgs[i] + 1, :],
                               -1e30)) for i in rng]
    kbeta = [ks[i] * beta[i] for i in rng]
    prod = [lax.dot_general(jnp.concatenate([kbeta[i], qs[i]], axis=0).astype(BF16), ks[i].astype(BF16),
                            (((1,), (1,)), ((), ())), preferred_element_type=F32) for i in rng]
    a_mat = [jnp.where(gts[i]["strict"], prod[i][:c] * decay[i], 0.0) for i in rng]
    intra = [prod[i][c:] * decay[i] for i in rng]
    ks_qs = [_bdot(jnp.concatenate([kbeta[i] * egc[i], qs[i] * egc[i]], axis=0), ss[i]) for i in rng]
    p = [jnp.where(gts[i]["diag_blk"], -a_mat[i], 0.0) for i in rng]
    y = [jnp.concatenate([vs[i] * beta[i] - ks_qs[i][:c], jnp.where(gts[i]["diag_blk"], 0.0, a_mat[i])], axis=1)
         for i in rng]
    for j in range(4):
        y = [y[i] + _bdot(p[i], y[i]) for i in rng]
        if j < 3:
            p = [_bdot(p[i], p[i]) for i in rng]
    e = [y[i][:, dh:] for i in rng]
    y = [y[i][:, :dh] for i in rng]
    e2 = [_bdot(e[i], e[i]) for i in rng]
    y = [y[i] + _bdot(e2[i], y[i]) for i in rng]
    v_new = [y[i] - _bdot(e[i], y[i]) for i in rng]
    o = [ks_qs[i][c:] + _bdot(intra[i], v_new[i]) for i in rng]
    s_new = [ss[i] * gts[i]["glast"][:, cgs[i]:cgs[i] + 1] + lax.dot_general(
        (ks[i] * gts[i]["kdec"][:, cgs[i]:cgs[i] + 1]).astype(BF16), v_new[i].astype(BF16),
        (((0,), (0,)), ((), ())), preferred_element_type=F32) for i in rng]
    return o, s_new


def _deltanet_kernel(nb, qf_ref, kf_ref, vf_ref, gf_ref, qb_ref, kb_ref, vb_ref, gb_ref, alog_ref, dtb_ref,
                     of_ref, ob_ref, s_ref):
    @pl.when(pl.program_id(0) == 0)
    def _():
        s_ref[...] = jnp.zeros_like(s_ref)

    dirs = ((qf_ref, kf_ref, vf_ref, gf_ref, of_ref), (qb_ref, kb_ref, vb_ref, gb_ref, ob_ref))

    def body(it, carry):
        idx = [(it * DN_BATCHES + j, d, hh) for j in range(DN_BATCHES) for d in range(2) for hh in range(DN_HEADS)]
        rows_of = {j: pl.ds(it * DN_BATCHES + j, CHUNK, stride=nb) for j in range(DN_BATCHES)}
        rows_b = [rows_of[i // (2 * DN_HEADS)] for i in range(len(idx))]
        cbs = [d * DN_HEADS + hh for _, d, hh in idx]
        gates = {(j, d): _deltanet_gates(d, dirs[d][3][rows_of[j], :], alog_ref[...], dtb_ref[...])
                 for j in range(DN_BATCHES) for d in range(2)}
        gts = [gates[(i // (2 * DN_HEADS), d)] for i, (_, d, _) in enumerate(idx)]
        qs = [dirs[d][0][hh, rows_b[i], :] for i, (_, d, hh) in enumerate(idx)]
        ks = [dirs[d][1][hh, rows_b[i], :] for i, (_, d, hh) in enumerate(idx)]
        vs = [dirs[d][2][hh, rows_b[i], :] for i, (_, d, hh) in enumerate(idx)]
        ss = [s_ref[b * (2 * DN_HEADS) + cbs[i]] for i, (b, _, _) in enumerate(idx)]
        o, s_new = _deltanet_heads(gts, cbs, qs, ks, vs, ss)
        for i, (b, d, hh) in enumerate(idx):
            dirs[d][4][hh, rows_b[i], :] = o[i]
            s_ref[b * (2 * DN_HEADS) + cbs[i]] = s_new[i]
        return carry

    lax.fori_loop(0, nb // DN_BATCHES, body, 0)


def _deltanet(qkv, proj, alog_vec, dtb_vec, nb, n_ctx_chunks):
    rows = qkv.shape[1]
    tr = CHUNK * nb
    n_tot = rows // tr
    fwd = lambda i: i
    bwd = lambda i: _bwd_chunk(i, n_ctx_chunks, n_tot)

    def specs(order):
        return [pl.BlockSpec((DN_HEADS, tr, DN_HEAD_DIM), lambda i, p=p: (p, order(i), 0)) for p in range(3)] + [
            pl.BlockSpec((tr, LANES), lambda i: (order(i), COL_GATE))]

    vec = pl.BlockSpec((1, LANES), lambda i: (0, 0))
    out = jax.ShapeDtypeStruct((DN_HEADS, rows, DN_HEAD_DIM), F32)
    return pl.pallas_call(
        functools.partial(_deltanet_kernel, nb),
        grid=(n_tot,),
        in_specs=specs(fwd) + specs(bwd) + [vec, vec],
        out_specs=[pl.BlockSpec((DN_HEADS, tr, DN_HEAD_DIM), lambda i: (0, fwd(i), 0)),
                   pl.BlockSpec((DN_HEADS, tr, DN_HEAD_DIM), lambda i: (0, bwd(i), 0))],
        out_shape=[out, out],
        scratch_shapes=[pltpu.VMEM((nb * 2 * DN_HEADS, DN_HEAD_DIM, DN_HEAD_DIM), F32)],
        compiler_params=_cparams(("arbitrary",)),
        name="deltanet",
    )(qkv, qkv, qkv, proj, qkv, qkv, qkv, proj, alog_vec, dtb_vec)


def _mixer_x1(nb, of_ref, ob_ref, z_ref, hf_ref, hb_ref, ry_ref, onw_ref, w_ref, h_ref, gt_ref, g_ref, b_ref):
    z = z_ref[...]
    parts = []
    for hh in range(DN_HEADS):
        sl = slice(hh * DN_HEAD_DIM, (hh + 1) * DN_HEAD_DIM)
        oh = of_ref[hh] + ob_ref[hh]
        inv = lax.rsqrt(jnp.mean(oh * oh, axis=-1, keepdims=True) + NORM_EPS)
        parts.append(oh * inv * onw_ref[...] * _silu(z[:, sl]))
    parts.append((hf_ref[...].astype(F32) + hb_ref[...].astype(F32)) * jax.nn.gelu(ry_ref[...]))
    y = jnp.concatenate(parts, axis=1).astype(BF16)
    u = jnp.dot(y, w_ref[...], preferred_element_type=F32)
    r = DEEP_ALPHA * h_ref[...] + _per_batch(u, gt_ref[...], nb) * u
    return _layer_norm(r) * g_ref[...] + b_ref[...]


PAIRS_PER_GROUP = EXPERTS_PER_GROUP * (EXPERTS_PER_GROUP - 1) // 2
N_CLASSES = N_GROUPS * PAIRS_PER_GROUP
assert N_CLASSES <= LANES
INFO_CLASS, INFO_RANK, INFO_W_LO, INFO_W_HI = range(4)
Y_SUB = 2 * (D_MODEL // LANES)


ROW_SUB = D_MODEL // LANES


def _store_row_tiles(ref, x):
    n = x.shape[0]
    for c in range(ROW_SUB):
        ref[pl.ds(c, n, stride=ROW_SUB), :] = x[:, c * LANES:(c + 1) * LANES]


def _load_row_tiles(ref, n):
    return jnp.concatenate([ref[pl.ds(c, n, stride=ROW_SUB), :] for c in range(ROW_SUB)], axis=1)


def _first_index(mask, lane):
    return jnp.min(jnp.where(mask, lane, LANES), axis=-1, keepdims=True)


def _route(nb, x, sh_ref, sc_ref, wr_ref, br_ref, hm_ref, info_ref, cnt_ref, run_ref):
    @pl.when(pl.program_id(0) == 0)
    def _():
        run_ref[...] = jnp.zeros_like(run_ref)

    xn = _layer_norm(x)
    hm = xn * (1.0 + _per_batch(xn, sc_ref[...], nb)) + _per_batch(xn, sh_ref[...], nb)
    _store_row_tiles(hm_ref, hm)
    tm = hm.shape[0]
    h_hi = hm.astype(BF16)
    h_lo = (hm - h_hi.astype(F32)).astype(BF16)
    hi_hl = jnp.dot(h_hi, wr_ref[...], preferred_element_type=F32)
    logits = (hi_hl[:, :LANES] + hi_hl[:, LANES:] + jnp.dot(h_lo, wr_ref[:, :LANES], preferred_element_type=F32)
              + br_ref[...])
    lane = lax.broadcasted_iota(jnp.int32, (tm, LANES), 1)
    neg = -jnp.inf
    is_g = lane < N_GROUPS
    gmax = jnp.max(jnp.where(is_g, logits, neg), axis=-1, keepdims=True)
    g_sel = _first_index(jnp.logical_and(is_g, logits == gmax), lane)
    p_group = 1.0 / jnp.sum(jnp.where(is_g, jnp.exp(logits - gmax), 0.0), axis=-1, keepdims=True)
    lo = N_GROUPS + EXPERTS_PER_GROUP * g_sel
    in_grp = jnp.logical_and(lane >= lo, lane < lo + EXPERTS_PER_GROUP)
    m1 = jnp.max(jnp.where(in_grp, logits, neg), axis=-1, keepdims=True)
    i1 = _first_index(jnp.logical_and(in_grp, logits == m1), lane)
    rest = jnp.logical_and(in_grp, lane != i1)
    m2 = jnp.max(jnp.where(rest, logits, neg), axis=-1, keepdims=True)
    i2 = _first_index(jnp.logical_and(rest, logits == m2), lane)
    e2 = jnp.exp(m2 - m1)
    w1 = p_group / (1.0 + e2)
    w2 = p_group * e2 / (1.0 + e2)
    l1 = i1 - lo
    l2 = i2 - lo
    e_lo = jnp.minimum(l1, l2)
    e_hi = jnp.maximum(l1, l2)
    pair = jnp.right_shift(e_lo * (2 * EXPERTS_PER_GROUP - 1 - e_lo), 1) + (e_hi - e_lo - 1)
    cls = g_sel * PAIRS_PER_GROUP + pair
    first_is_lo = l1 < l2
    w_lo = jnp.where(first_is_lo, w1, w2)
    w_hi = jnp.where(first_is_lo, w2, w1)
    oh = lane == cls
    r_i = lax.broadcasted_iota(jnp.int32, (tm, tm), 0)
    c_i = lax.broadcasted_iota(jnp.int32, (tm, tm), 1)
    before = jnp.dot((r_i > c_i).astype(BF16), oh.astype(BF16), preferred_element_type=F32) + run_ref[...]
    rank = jnp.sum(jnp.where(oh, before, 0.0), axis=-1, keepdims=True)
    run_ref[...] = run_ref[...] + jnp.sum(oh.astype(F32), axis=0, keepdims=True)
    cnt_ref[...] = run_ref[...]
    info = jnp.zeros((tm, LANES), F32)
    for idx, val in ((INFO_CLASS, cls.astype(F32)), (INFO_RANK, rank), (INFO_W_LO, w_lo), (INFO_W_HI, w_hi)):
        info = jnp.where(lane == idx, val, info)
    info_ref[...] = info


def _mixer_router_kernel(nb, of_ref, ob_ref, z_ref, hf_ref, hb_ref, ry_ref, onw_ref, w_ref, h_ref, gt_ref, g_ref, b_ref,
                         sh_ref, sc_ref, wr_ref, br_ref, x1_ref, hm_ref, info_ref, cnt_ref, run_ref):
    x1 = _mixer_x1(nb, of_ref, ob_ref, z_ref, hf_ref, hb_ref, ry_ref, onw_ref, w_ref, h_ref, gt_ref, g_ref, b_ref)
    x1_ref[...] = x1
    _route(nb, x1, sh_ref, sc_ref, wr_ref, br_ref, hm_ref, info_ref, cnt_ref, run_ref)


def _mixer_router(o_f, o_b, proj, hf, hb, onorm_w, w_out, layer, h, mod_rows, ln_g, ln_b, wr, br, nb, ctx_rows, row0):
    rows, d = h.shape
    tm = ROW_TILE
    t0 = row0 // tm
    n = rows - row0
    lat = lambda i: jnp.where((i + t0) * tm >= ctx_rows, 1, 0)
    half = pl.BlockSpec((tm, DN_WIDTH), lambda i: (i + t0, 0))
    planes = pl.BlockSpec((DN_HEADS, tm, DN_HEAD_DIM), lambda i: (0, i + t0, 0))
    vec = lambda m: pl.BlockSpec((1, m), lambda i: (0, 0))
    mod = lambda k: pl.BlockSpec((None, nb, d), lambda i: (lat(i), 0, k))
    return pl.pallas_call(
        functools.partial(_mixer_router_kernel, nb),
        grid=(n // tm,),
        in_specs=[planes, planes, pl.BlockSpec((tm, DN_WIDTH), lambda i: (i + t0, COL_Z)), half, half,
                  pl.BlockSpec((tm, RG_WIDTH), lambda i: (i + t0, COL_RY)),
                  vec(DN_HEAD_DIM), pl.BlockSpec((None, d, d), lambda i: (layer, 0, 0)),
                  pl.BlockSpec((tm, d), lambda i: (i + t0, 0)),
                  mod(2),
                  vec(d), vec(d),
                  mod(3), mod(4),
                  pl.BlockSpec((d, 2 * LANES), lambda i: (0, 0)), vec(LANES)],
        out_specs=[pl.BlockSpec((tm, d), lambda i: (i, 0)),
                   pl.BlockSpec((tm * ROW_SUB, LANES), lambda i: (i, 0)),
                   pl.BlockSpec((tm, LANES), lambda i: (i, 0)),
                   pl.BlockSpec((1, LANES), lambda i: (0, 0))],
        out_shape=[jax.ShapeDtypeStruct((n, d), F32),
                   jax.ShapeDtypeStruct((n * ROW_SUB, LANES), F32),
                   jax.ShapeDtypeStruct((n, LANES), F32),
                   jax.ShapeDtypeStruct((1, LANES), F32)],
        scratch_shapes=[pltpu.VMEM((1, LANES), F32)],
        compiler_params=_cparams(("arbitrary",)),
        name="mixer_router",
    )(o_f, o_b, proj, hf, hb, proj, onorm_w, w_out, h, mod_rows, ln_g, ln_b, mod_rows, mod_rows, wr, br)


def _dispatch_kernel(tmx, ends_ref, cnt_ref, dst_ref, hm_ref, xs_hbm, zero_scr, sem, zsem):
    rs = ROW_SUB
    tb = dst_ref.shape[1]

    @pl.when(pl.program_id(0) == 0)
    def _():
        zero_scr[...] = jnp.zeros_like(zero_scr)

        def tail_copy(c):
            start = pl.multiple_of((ends_ref[c] - tmx) * rs, tmx * rs)
            return pltpu.make_async_copy(zero_scr, xs_hbm.at[pl.ds(start, tmx * rs)], zsem)

        def start_one(c, carry):
            @pl.when(cnt_ref[c] > 0)
            def _():
                tail_copy(c).start()
            return carry

        def wait_one(c, carry):
            @pl.when(cnt_ref[c] > 0)
            def _():
                tail_copy(c).wait()
            return carry

        lax.fori_loop(0, N_CLASSES, start_one, 0)

        def spare_copy(j):
            return pltpu.make_async_copy(zero_scr, xs_hbm.at[pl.ds(pl.multiple_of(j * (tmx * rs), tmx * rs), tmx * rs)],
                                         zsem)

        n_used = ends_ref[N_CLASSES - 1] // tmx
        n_tiles = xs_hbm.shape[0] // (tmx * rs)
        lax.fori_loop(n_used, n_tiles, lambda j, carry: (spare_copy(j).start(), carry)[1], 0)
        lax.fori_loop(0, N_CLASSES, wait_one, 0)
        lax.fori_loop(n_used, n_tiles, lambda j, carry: (spare_copy(j).wait(), carry)[1], 0)

    for r in range(tb):
        dst = xs_hbm.at[pl.ds(pl.multiple_of(dst_ref[0, r], rs), rs)]
        pltpu.make_async_copy(hm_ref.at[pl.ds(r * rs, rs)], dst, sem).start(priority=r % 2)
    pltpu.make_async_copy(hm_ref, hm_ref, sem).wait()


def _dispatch(hm, dest, class_ends, class_cnt, sorted_rows, tmx):
    tb = ROW_TILE
    n_blocks = dest.shape[0] // tb
    grid_spec = pltpu.PrefetchScalarGridSpec(
        num_scalar_prefetch=2,
        grid=(n_blocks,),
        in_specs=[pl.BlockSpec((None, 1, tb), lambda i, e, c: (i, 0, 0), memory_space=pltpu.SMEM),
                  pl.BlockSpec((tb * ROW_SUB, LANES), lambda i, e, c: (i, 0))],
        out_specs=pl.BlockSpec(memory_space=pl.ANY),
        scratch_shapes=[pltpu.VMEM((tmx * ROW_SUB, LANES), F32), pltpu.SemaphoreType.DMA(()),
                        pltpu.SemaphoreType.DMA(())],
    )
    return pl.pallas_call(
        functools.partial(_dispatch_kernel, tmx),
        grid_spec=grid_spec,
        out_shape=jax.ShapeDtypeStruct((sorted_rows * ROW_SUB, LANES), F32),
        compiler_params=_cparams(("arbitrary",)),
        name="moe_dispatch",
    )(class_ends, class_cnt, (dest * ROW_SUB).reshape(n_blocks, 1, tb), hm)


def _class_ffn_kernel(telo_ref, tehi_ref, nact_ref, xs_ref, wgul_ref, wdl_ref, wguh_ref, wdh_ref, ys_ref, y_scr):
    j = pl.program_id(0)
    tmx = xs_ref.shape[0] // ROW_SUB
    de = wdl_ref.shape[0]

    @pl.when(j < nact_ref[0])
    def _():
        x = _load_row_tiles(xs_ref, tmx).astype(BF16)
        for half, (wgu_ref, wd_ref) in enumerate(((wgul_ref, wdl_ref), (wguh_ref, wdh_ref))):
            ab = jnp.dot(x, wgu_ref[...], preferred_element_type=F32)
            y = jnp.dot((_silu(ab[:, :de]) * ab[:, de:]).astype(BF16), wd_ref[...], preferred_element_type=F32)
            for c in range(ROW_SUB):
                y_scr[pl.ds(half * ROW_SUB + c, tmx, stride=Y_SUB), :] = y[:, c * LANES:(c + 1) * LANES]
        ys_ref[...] = y_scr[...].astype(BF16)

    @pl.when(j >= nact_ref[0])
    def _():
        ys_ref[...] = jnp.zeros_like(ys_ref)


def _class_ffn(xs, tile_e_lo, tile_e_hi, n_active, w_gate_up, w_down, layer, tmx):
    n_tiles = xs.shape[0] // (tmx * ROW_SUB)
    d, de2 = w_gate_up.shape[-2:]
    w_in = lambda te_idx: pl.BlockSpec((None, None, d, de2), lambda j, *te: (layer, te[te_idx][j], 0, 0))
    w_out = lambda te_idx: pl.BlockSpec((None, None, de2 // 2, d), lambda j, *te: (layer, te[te_idx][j], 0, 0))
    grid_spec = pltpu.PrefetchScalarGridSpec(
        num_scalar_prefetch=3,
        grid=(n_tiles,),
        in_specs=[pl.BlockSpec((tmx * ROW_SUB, LANES), lambda j, lo, hi, na: (jnp.minimum(j, na[0] - 1), 0)),
                  w_in(0), w_out(0), w_in(1), w_out(1)],
        out_specs=pl.BlockSpec((tmx * Y_SUB, LANES), lambda j, lo, hi, na: (j, 0)),
        scratch_shapes=[pltpu.VMEM((tmx * Y_SUB, LANES), F32)],
    )
    return pl.pallas_call(
        _class_ffn_kernel,
        grid_spec=grid_spec,
        out_shape=jax.ShapeDtypeStruct((n_tiles * tmx * Y_SUB, LANES), BF16),
        compiler_params=_cparams(("arbitrary",)),
        name="expert_ffn",
    )(tile_e_lo, tile_e_hi, n_active, xs, w_gate_up, w_down, w_gate_up, w_down)


def _moe_out_kernel(nb, batch_major, src0_ref, srcn_ref, ys_hbm, info_ref, x_ref, gt_ref, g_ref, b_ref, o_ref,
                    ybuf, y_scr, sem):
    i = pl.program_id(0)
    tm = info_ref.shape[0]
    slot = i % 2
    nslot = 1 - slot

    def gather_all(src_ref, s):
        for r in range(tm):
            pltpu.make_async_copy(ys_hbm.at[pl.ds(pl.multiple_of(src_ref[0, r], Y_SUB), Y_SUB)],
                                  ybuf.at[s, pl.ds(r * Y_SUB, Y_SUB)], sem.at[s]).start(priority=r % 2)

    def wait_rows(s):
        pltpu.make_async_copy(ybuf.at[s], ybuf.at[s], sem.at[s]).wait()

    @pl.when(i == 0)
    def _():
        gather_all(src0_ref, 0)

    wait_rows(slot)
    gather_all(srcn_ref, nslot)
    info = info_ref[...]
    y_scr[...] = ybuf[slot].astype(F32)
    y_lo = jnp.concatenate([y_scr[pl.ds(c, tm, stride=Y_SUB), :] for c in range(ROW_SUB)], axis=1)
    y_hi = jnp.concatenate([y_scr[pl.ds(ROW_SUB + c, tm, stride=Y_SUB), :] for c in range(ROW_SUB)], axis=1)
    f = y_lo * info[:, INFO_W_LO:INFO_W_LO + 1] + y_hi * info[:, INFO_W_HI:INFO_W_HI + 1]
    r = DEEP_ALPHA * x_ref[...] + _per_batch(f, gt_ref[...], nb) * f
    res = _layer_norm(r) * g_ref[...] + b_ref[...]
    if batch_major:
        for c in range(ROW_SUB):
            y_scr[pl.ds(c * tm, tm), :] = res[:, c * LANES:(c + 1) * LANES]
        for b in range(nb):
            for c in range(ROW_SUB):
                o_ref[b, :, c * LANES:(c + 1) * LANES] = y_scr[pl.ds(c * tm + b, tm // nb, stride=nb), :]
    else:
        o_ref[...] = res

    @pl.when(i == pl.num_programs(0) - 1)
    def _():
        wait_rows(nslot)


def _moe_out(ys, dest, info, x1, mod_rows, ln_g, ln_b, nb, ctx_rows, batch_major):
    rows, d = x1.shape
    tm = ROW_TILE
    n_blocks = rows // tm
    if batch_major:
        out_spec = pl.BlockSpec((nb, tm // nb, d), lambda i: (0, i, 0))
        out_shape = jax.ShapeDtypeStruct((nb, rows // nb, d), F32)
    else:
        out_spec = pl.BlockSpec((tm, d), lambda i: (i, 0))
        out_shape = jax.ShapeDtypeStruct((rows, d), F32)
    lat = lambda i: jnp.where(i * tm >= ctx_rows, 1, 0)
    vec = pl.BlockSpec((1, d), lambda i: (0, 0))
    src = (dest * Y_SUB).reshape(n_blocks, 1, tm)
    smem = lambda f: pl.BlockSpec((None, 1, tm), f, memory_space=pltpu.SMEM)
    return pl.pallas_call(
        functools.partial(_moe_out_kernel, nb, batch_major),
        grid=(n_blocks,),
        in_specs=[smem(lambda i: (0, 0, 0)),
                  smem(lambda i: (jnp.minimum(i + 1, n_blocks - 1), 0, 0)),
                  pl.BlockSpec(memory_space=pl.ANY),
                  pl.BlockSpec((tm, LANES), lambda i: (i, 0)),
                  pl.BlockSpec((tm, d), lambda i: (i, 0)),
                  pl.BlockSpec((None, nb, d), lambda i: (lat(i), 0, 5)),
                  vec, vec],
        out_specs=out_spec,
        out_shape=out_shape,
        scratch_shapes=[pltpu.VMEM((2, tm * Y_SUB, LANES), BF16), pltpu.VMEM((tm * Y_SUB, LANES), F32),
                        pltpu.SemaphoreType.DMA((2,))],
        compiler_params=_cparams(("arbitrary",)),
        name="moe_out",
    )(src, src, ys, info, x1, mod_rows, ln_g, ln_b)


def _class_experts():
    lo, hi = [], []
    for g in range(N_GROUPS):
        for a in range(EXPERTS_PER_GROUP):
            for b in range(a + 1, EXPERTS_PER_GROUP):
                lo.append(g * EXPERTS_PER_GROUP + a)
                hi.append(g * EXPERTS_PER_GROUP + b)
    return np.asarray(lo, np.int32), np.asarray(hi, np.int32)


def _routing_tables(info, counts, n_tiles, tmx):
    cnt = counts[0, :N_CLASSES].astype(jnp.int32)
    padded = ((cnt + tmx - 1) // tmx) * tmx
    ends = jnp.cumsum(padded)
    starts = ends - padded
    cls = info[:, INFO_CLASS].astype(jnp.int32)
    class_ids = jnp.arange(N_CLASSES, dtype=jnp.int32)
    start_of_tok = jnp.sum(jnp.where(cls[:, None] == class_ids[None, :], starts[None, :], 0), axis=1)
    dest = start_of_tok + info[:, INFO_RANK].astype(jnp.int32)
    tile_start = jnp.arange(n_tiles, dtype=jnp.int32) * tmx
    n_before = jnp.sum((ends[None, :] <= tile_start[:, None]).astype(jnp.int32), axis=1)
    tile_cls = jnp.minimum(n_before, N_CLASSES - 1)
    e_lo, e_hi = _class_experts()
    n_active = (ends[-1:] // tmx).astype(jnp.int32)
    return dest, ends, cnt, jnp.asarray(e_lo)[tile_cls], jnp.asarray(e_hi)[tile_cls], n_active


def _relayout_w_in(w_in):
    qkvz = w_in[..., :4 * DN_WIDTH]
    gates = w_in[..., 4 * DN_WIDTH:4 * DN_WIDTH + 4 * DN_HEADS]
    rxy = w_in[..., 4 * DN_WIDTH + 4 * DN_HEADS:]
    pad = jnp.zeros(w_in.shape[:-1] + (LANES - 4 * DN_HEADS,), w_in.dtype)
    return jnp.concatenate([qkvz, rxy, gates, pad], axis=-1).astype(BF16)


def _block_diag(w):
    n, c, _ = w.shape
    eye = jnp.eye(n, dtype=w.dtype)
    return (eye[:, None, :, None] * w[:, :, None, :]).reshape(n * c, n * c)


def _lane_vec(vals, offset):
    return jnp.zeros((1, LANES), F32).at[0, offset:offset + vals.size].set(vals.reshape(-1))


def kernel(x, c, ctx, c_ctx, w_ada, b_ada, w_in, conv_qkv_w, dn_a_log, dn_dt_bias, dn_onorm_w, rg_conv_w, rg_conv_b, rg_wa, rg_ba, rg_wi, rg_bi, rg_lambda, w_out, ln_g, ln_b, router_wg, router_bg, router_we, router_be, w_e_gate, w_e_up, w_e_down):
    nb, t_lat, d = x.shape
    t_ctx = ctx.shape[1]
    assert d == D_MODEL and nb % 8 == 0 and t_lat % GRID_W == 0 and t_ctx % CHUNK == 0
    assert (t_ctx * nb) % INPROJ_TILE == 0 and (t_lat * nb) % INPROJ_TILE == 0 and INPROJ_TILE % ROW_TILE == 0
    tt = t_ctx + t_lat
    rows = tt * nb
    ctx_rows = t_ctx * nb
    n_ctx_chunks = t_ctx // CHUNK

    n_cc = ((nb + 1 + 7) // 8) * 8
    cc = jnp.zeros((n_cc, d), F32).at[:nb].set(c).at[nb].set(c_ctx)
    mod = _modulation(cc, w_ada, b_ada)
    mod_ctx = jnp.broadcast_to(mod[:, nb:nb + 1], (DEPTH, nb, 6 * d))
    mod_rows = jnp.stack([mod_ctx, mod[:, :nb]], axis=1)

    w_in2 = _relayout_w_in(w_in)
    w_out_bf = w_out.astype(BF16)
    w_gate_up = jnp.concatenate([w_e_gate, w_e_up], axis=-1).astype(BF16)
    w_down = w_e_down.astype(BF16)
    for l in range(DEPTH):
        row0 = ctx_rows if l == DEPTH - 1 else 0
        moe_ctx_rows = ctx_rows - row0
        n_tiles = -(-(rows - row0) // MOE_TILE) + N_CLASSES
        mr = mod_rows[l]
        if l == 0:
            proj, h = _in_projection_first(ctx, x, mr, w_in2, l, nb)
        else:
            proj = _in_projection(h, mr, w_in2, l, nb, ctx_rows)
        qkv = _qkv_prep(proj, conv_qkv_w[l], nb, n_ctx_chunks)
        wg = jnp.concatenate([_block_diag(rg_wa[l, 0]), _block_diag(rg_wi[l, 0]),
                              _block_diag(rg_wa[l, 1]), _block_diag(rg_wi[l, 1])], axis=1).astype(BF16)
        bg = jnp.concatenate([rg_ba[l, 0], rg_bi[l, 0], rg_ba[l, 1], rg_bi[l, 1]])[None]
        hf, hb = _rglru(proj, rg_conv_w[l], rg_conv_b[l][None], wg, bg, rg_lambda[l], nb, n_ctx_chunks)
        alog_vec = _lane_vec(dn_a_log[l], 2 * DN_HEADS)
        dtb_vec = _lane_vec(dn_dt_bias[l], 2 * DN_HEADS)
        o_f, o_b = _deltanet(qkv, proj, alog_vec, dtb_vec, nb, n_ctx_chunks)
        wr = jnp.zeros((d, LANES), F32).at[:, :N_GROUPS].set(router_wg[l]).at[
            :, N_GROUPS:N_GROUPS + N_EXPERTS].set(router_we[l])
        br = _lane_vec(jnp.concatenate([router_bg[l], router_be[l]]), 0)
        wr_hi = wr.astype(BF16)
        wr_hl = jnp.concatenate([wr_hi, (wr - wr_hi.astype(F32)).astype(BF16)], axis=1)
        x1, hm, info, counts = _mixer_router(o_f, o_b, proj, hf, hb, dn_onorm_w[l][None], w_out_bf, l, h, mr,
                                             ln_g[l, 0][None], ln_b[l, 0][None], wr_hl, br, nb, ctx_rows, row0)
        dest, class_ends, class_cnt, tile_e_lo, tile_e_hi, n_active = _routing_tables(info, counts, n_tiles, MOE_TILE)
        xs = _dispatch(hm, dest, class_ends, class_cnt, n_tiles * MOE_TILE, MOE_TILE)
        ys = _class_ffn(xs, tile_e_lo, tile_e_hi, n_active, w_gate_up, w_down, l, MOE_TILE)
        h = _moe_out(ys, dest, info, x1, mr, ln_g[l, 1][None], ln_b[l, 1][None], nb, moe_ctx_rows,
                     batch_major=(l == DEPTH - 1))

    return h
```

```python
import functools

import jax
import jax.numpy as jnp
import numpy as np
from jax import lax
from jax.experimental import pallas as pl
from jax.experimental.pallas import tpu as pltpu

F32 = jnp.float32
BF16 = jnp.bfloat16

D_MODEL = 1024
DEPTH = 2
GRID_W = 64
DN_HEAD_DIM = 128
DN_WIDTH = 512
DN_HEADS = 4
RG_WIDTH = 512
RG_BLOCKS = 8
RG_BLOCK_DIM = 64
RG_C = 8.0
CONV_K = 4
N_GROUPS = 4
EXPERTS_PER_GROUP = 8
N_EXPERTS = 32
D_EXPERT = 256
DEEP_ALPHA = (2 * DEPTH) ** 0.25
LN_EPS = 1e-5
NORM_EPS = 1e-6

LANES = 128
CHUNK = 64
SOLVE_BLOCK = 16
DN_BATCHES = 4
D_PROJ = 3 * DN_WIDTH + DN_WIDTH + 2 * RG_WIDTH + LANES
COL_Z = 3
COL_RX = 4
COL_RY = 5
COL_GATE = (3 * DN_WIDTH + DN_WIDTH + 2 * RG_WIDTH) // LANES
ROW_TILE = 512
INPROJ_TILE = 1024
DISPATCH_TILE = 2048
MOE_TILE = 256
VMEM_LIMIT = 56 * 1024 * 1024


def _cparams(sem):
    return pltpu.CompilerParams(dimension_semantics=sem, vmem_limit_bytes=VMEM_LIMIT)


def _layer_norm(x):
    mu = jnp.mean(x, axis=-1, keepdims=True)
    xc = x - mu
    var = jnp.mean(xc * xc, axis=-1, keepdims=True)
    return xc * lax.rsqrt(var + LN_EPS)


def _per_batch(x, v, nb):
    rows, c = x.shape
    return jnp.broadcast_to(v[None], (rows // nb, nb, c)).reshape(rows, c)


def _sigmoid(x):
    return 0.5 * jnp.tanh(0.5 * x) + 0.5


def _silu(x):
    h = 0.5 * x
    return h * jnp.tanh(h) + h


def _mod_kernel(cc_ref, w_ref, b_ref, o_ref):
    s = _silu(cc_ref[...])
    o_ref[...] = jnp.dot(s.astype(BF16), w_ref[...].astype(BF16), preferred_element_type=F32) + b_ref[...]


def _modulation(cc, w_ada, b_ada):
    depth, d, n6 = w_ada.shape
    rows = cc.shape[0]
    tn = 1536
    return pl.pallas_call(
        _mod_kernel,
        grid=(depth, n6 // tn),
        in_specs=[
            pl.BlockSpec((rows, d), lambda l, j: (0, 0)),
            pl.BlockSpec((None, d, tn), lambda l, j: (l, 0, j)),
            pl.BlockSpec((None, 1, tn), lambda l, j: (l, 0, j)),
        ],
        out_specs=pl.BlockSpec((None, rows, tn), lambda l, j: (l, 0, j)),
        out_shape=jax.ShapeDtypeStruct((depth, rows, n6), F32),
        compiler_params=_cparams(("parallel", "parallel")),
        name="adaln_mod",
    )(cc, w_ada, b_ada.reshape(depth, 1, n6))


def _inproj_kernel(nb, h_ref, sh_ref, sc_ref, w_ref, o_ref):
    xn = _layer_norm(h_ref[...])
    hm = xn * (1.0 + _per_batch(xn, sc_ref[...], nb)) + _per_batch(xn, sh_ref[...], nb)
    o_ref[...] = jnp.dot(hm.astype(BF16), w_ref[...], preferred_element_type=F32)


def _inproj_first_kernel(nb, n_ctx_tiles, ctx_ref, x_ref, sh_ref, sc_ref, w_ref, o_ref, h_ref, scr):
    tt = ctx_ref.shape[1]
    tm = nb * tt
    blk = jnp.where(pl.program_id(0) < n_ctx_tiles, ctx_ref[...], x_ref[...])
    for b in range(nb):
        for c in range(ROW_SUB):
            scr[pl.ds(c * tm + b, tt, stride=nb), :] = blk[b, :, c * LANES:(c + 1) * LANES]
    h = jnp.concatenate([scr[pl.ds(c * tm, tm), :] for c in range(ROW_SUB)], axis=1)
    h_ref[...] = h
    xn = _layer_norm(h)
    hm = xn * (1.0 + _per_batch(xn, sc_ref[...], nb)) + _per_batch(xn, sh_ref[...], nb)
    o_ref[...] = jnp.dot(hm.astype(BF16), w_ref[...], preferred_element_type=F32)


def _in_projection_first(ctx, x, mod_rows, w_in2, layer, nb):
    _, t_ctx, d = ctx.shape
    t_lat = x.shape[1]
    tm = ROW_TILE
    tt = tm // nb
    n_ctx_tiles = t_ctx // tt
    rows = (t_ctx + t_lat) * nb
    lat = lambda i: jnp.where(i >= n_ctx_tiles, 1, 0)
    return pl.pallas_call(
        functools.partial(_inproj_first_kernel, nb, n_ctx_tiles),
        grid=(rows // tm,),
        in_specs=[
            pl.BlockSpec((nb, tt, d), lambda i: (0, jnp.minimum(i, n_ctx_tiles - 1), 0)),
            pl.BlockSpec((nb, tt, d), lambda i: (0, jnp.maximum(i - n_ctx_tiles, 0), 0)),
            pl.BlockSpec((None, nb, d), lambda i: (lat(i), 0, 0)),
            pl.BlockSpec((None, nb, d), lambda i: (lat(i), 0, 1)),
            pl.BlockSpec((None, d, D_PROJ), lambda i: (layer, 0, 0)),
        ],
        out_specs=[pl.BlockSpec((tm, D_PROJ), lambda i: (i, 0)), pl.BlockSpec((tm, d), lambda i: (i, 0))],
        out_shape=[jax.ShapeDtypeStruct((rows, D_PROJ), F32), jax.ShapeDtypeStruct((rows, d), F32)],
        scratch_shapes=[pltpu.VMEM((ROW_SUB * tm, LANES), F32)],
        compiler_params=_cparams(("parallel",)),
        name="ln_mod_inproj_first",
    )(ctx, x, mod_rows, mod_rows, w_in2)


def _in_projection(h, mod_rows, w_in2, layer, nb, ctx_rows):
    rows, d = h.shape
    tm = INPROJ_TILE
    lat = lambda i: jnp.where(i * tm >= ctx_rows, 1, 0)
    return pl.pallas_call(
        functools.partial(_inproj_kernel, nb),
        grid=(rows // tm,),
        in_specs=[
            pl.BlockSpec((tm, d), lambda i: (i, 0)),
            pl.BlockSpec((None, nb, d), lambda i: (lat(i), 0, 0)),
            pl.BlockSpec((None, nb, d), lambda i: (lat(i), 0, 1)),
            pl.BlockSpec((None, d, D_PROJ), lambda i: (layer, 0, 0)),
        ],
        out_specs=pl.BlockSpec((tm, D_PROJ), lambda i: (i, 0)),
        out_shape=jax.ShapeDtypeStruct((rows, D_PROJ), F32),
        compiler_params=_cparams(("parallel",)),
        name="ln_mod_inproj",
    )(h, mod_rows, mod_rows, w_in2)


def _conv_tile(x, prev, nxt, w, nb, use_prev, use_next):
    prev = jnp.where(use_prev, prev, 0.0)
    nxt = jnp.where(use_next, nxt, 0.0)
    xm2 = jnp.concatenate([prev, x[: -2 * nb]], axis=0)
    xm1 = jnp.concatenate([prev[nb:], x[:-nb]], axis=0)
    xp1 = jnp.concatenate([x[nb:], nxt], axis=0)
    return xm2 * w[0:1] + xm1 * w[1:2] + x * w[2:3] + xp1 * w[3:4]


def _halo_flags(i, n_ctx_tiles):
    use_prev = jnp.logical_and(i > 0, i < n_ctx_tiles)
    use_next = i < n_ctx_tiles - 1
    return use_prev, use_next


def _qkv_prep_kernel(nb, n_ctx_tiles, x_ref, prev_ref, next_ref, w_ref, o_ref):
    i = pl.program_id(0)
    part = pl.program_id(1)
    use_prev, use_next = _halo_flags(i, n_ctx_tiles)
    y = _silu(_conv_tile(x_ref[...], prev_ref[...], next_ref[...], w_ref[...], nb, use_prev, use_next))
    qscale = jnp.where(part == 0, DN_HEAD_DIM ** -0.5, 1.0).astype(F32)
    for hh in range(DN_HEADS):
        yh = y[:, hh * DN_HEAD_DIM:(hh + 1) * DN_HEAD_DIM]
        inv = lax.rsqrt(jnp.sum(yh * yh, axis=-1, keepdims=True) + NORM_EPS) * qscale
        o_ref[hh] = yh * jnp.where(part < 2, inv, 1.0)


def _halo_specs(tile_rows, nb, cols, col_of, n_row_blocks16, tile_of=lambda i: i):
    prev_b = tile_rows // (2 * nb)
    next_b = tile_rows // nb
    return [
        pl.BlockSpec((tile_rows, cols), lambda i, *a: (tile_of(i), col_of(*a))),
        pl.BlockSpec((2 * nb, cols), lambda i, *a: (jnp.maximum(tile_of(i) * prev_b - 1, 0), col_of(*a))),
        pl.BlockSpec((nb, cols),
                     lambda i, *a: (jnp.minimum((tile_of(i) + 1) * next_b, n_row_blocks16 - 1), col_of(*a))),
    ]


def _qkv_prep(proj, conv_w, nb, n_ctx_tiles):
    rows = proj.shape[0]
    tr = CHUNK * nb
    return pl.pallas_call(
        functools.partial(_qkv_prep_kernel, nb, n_ctx_tiles),
        grid=(rows // tr, 3),
        in_specs=_halo_specs(tr, nb, DN_WIDTH, lambda p: p, rows // nb)
        + [pl.BlockSpec((CONV_K, DN_WIDTH), lambda i, p: (0, p))],
        out_specs=pl.BlockSpec((DN_HEADS, tr, DN_HEAD_DIM), lambda i, p: (p, i, 0)),
        out_shape=jax.ShapeDtypeStruct((3 * DN_HEADS, rows, DN_HEAD_DIM), F32),
        compiler_params=_cparams(("parallel", "parallel")),
        name="qkv_conv_norm",
    )(proj, proj, proj, conv_w)


def _bwd_chunk(i, n_ctx, n_tot):
    return jnp.where(i < n_ctx, n_ctx - 1 - i, n_tot + n_ctx - 1 - i)


def _rglru_kernel(nb, n_ctx_tiles, n_tot, xf_ref, pf_ref, nf_ref, xb_ref, pb_ref, nb_ref, cw_ref, cb_ref,
                  wg0_ref, wg1_ref, bg0_ref, bg1_ref, lam_ref, hf_ref, hb_ref, a_scr, b_scr, st_ref):
    i = pl.program_id(0)

    @pl.when(i == 0)
    def _():
        st_ref[...] = jnp.zeros_like(st_ref)

    w = RG_WIDTH
    sp = jax.nn.softplus(-lam_ref[...])
    tiles = (i, _bwd_chunk(i, n_ctx_tiles, n_tot))
    for d, (x_ref, p_ref, n_ref, wg_ref, bg_ref) in enumerate(
            ((xf_ref, pf_ref, nf_ref, wg0_ref, bg0_ref), (xb_ref, pb_ref, nb_ref, wg1_ref, bg1_ref))):
        use_prev, use_next = _halo_flags(tiles[d], n_ctx_tiles)
        xc = _conv_tile(x_ref[...], p_ref[...], n_ref[...], cw_ref[...], nb, use_prev, use_next) + cb_ref[...]
        gates = jnp.dot(xc.astype(BF16), wg_ref[...], preferred_element_type=F32) + bg_ref[...]
        ig = _sigmoid(gates[:, w:])
        c = (-0.5 * RG_C) * sp[d:d + 1]
        a = jnp.exp(c * jnp.tanh(0.5 * gates[:, :w]) + c)
        y = jnp.maximum(1.0 - a * a, 0.0)
        a_scr[d] = a
        b_scr[d] = jnp.where(y > 0.0, y * lax.rsqrt(y), 0.0) * (ig * xc)

    def body(t, carry):
        h0, h1 = carry
        r0 = pl.ds(pl.multiple_of(t * nb, nb), nb)
        r1 = pl.ds(pl.multiple_of((CHUNK - 1 - t) * nb, nb), nb)
        h0 = a_scr[0, r0, :] * h0 + b_scr[0, r0, :]
        h1 = a_scr[1, r1, :] * h1 + b_scr[1, r1, :]
        hf_ref[r0, :] = h0.astype(hf_ref.dtype)
        hb_ref[r1, :] = h1.astype(hb_ref.dtype)
        return h0, h1

    h0, h1 = lax.fori_loop(0, CHUNK, body, (st_ref[0], st_ref[1]), unroll=8)
    st_ref[0] = h0
    st_ref[1] = h1


def _rglru(proj, conv_w, conv_b, wg, bg, lam, nb, n_ctx_tiles):
    rows = proj.shape[0]
    tr = CHUNK * nb
    n_tot = rows // tr
    w = RG_WIDTH
    out = jax.ShapeDtypeStruct((rows, w), BF16)
    fwd = lambda i: i
    bwd = lambda i: _bwd_chunk(i, n_ctx_tiles, n_tot)
    full = lambda shape: pl.BlockSpec(shape, lambda i: (0,) * len(shape))
    return pl.pallas_call(
        functools.partial(_rglru_kernel, nb, n_ctx_tiles, n_tot),
        grid=(n_tot,),
        in_specs=_halo_specs(tr, nb, w, lambda: COL_RX, rows // nb, fwd)
        + _halo_specs(tr, nb, w, lambda: COL_RX, rows // nb, bwd)
        + [full((CONV_K, w)), full((1, w)),
           pl.BlockSpec((w, 2 * w), lambda i: (0, 0)), pl.BlockSpec((w, 2 * w), lambda i: (0, 1)),
           pl.BlockSpec((1, 2 * w), lambda i: (0, 0)), pl.BlockSpec((1, 2 * w), lambda i: (0, 1)),
           full((2, w))],
        out_specs=[pl.BlockSpec((tr, w), lambda i: (fwd(i), 0)), pl.BlockSpec((tr, w), lambda i: (bwd(i), 0))],
        out_shape=[out] * 2,
        scratch_shapes=[pltpu.VMEM((2, tr, w), F32), pltpu.VMEM((2, tr, w), F32), pltpu.VMEM((2, nb, w), F32)],
        compiler_params=_cparams(("arbitrary",)),
        name="rglru",
    )(proj, proj, proj, proj, proj, proj, conv_w, conv_b, wg, wg, bg, bg, lam)


def _bdot(a, b):
    return jnp.dot(a.astype(BF16), b.astype(BF16), preferred_element_type=F32)


def _deltanet_gates(d, raw, alog, dtb):
    c = CHUNK
    row = lax.broadcasted_iota(jnp.int32, (c, c), 0)
    col = lax.broadcasted_iota(jnp.int32, (c, c), 1)
    incl = (row >= col) if d == 0 else (row <= col)
    g_all = -jnp.exp(alog) * jax.nn.softplus(raw + dtb)
    g_hi = g_all.astype(BF16)
    g_lo = (g_all - g_hi.astype(F32)).astype(BF16)
    tri = incl.astype(BF16)
    gc_all = (jnp.dot(tri, g_hi, preferred_element_type=F32) + jnp.dot(tri, g_lo, preferred_element_type=F32))
    g_tot = jnp.sum(g_all, axis=0, keepdims=True)
    return dict(incl=incl, strict=(row > col) if d == 0 else (row < col),
                diag_blk=(row // SOLVE_BLOCK) == (col // SOLVE_BLOCK),
                beta=_sigmoid(raw), gc=gc_all, gc_t=gc_all.T, egc=jnp.exp(gc_all),
                kdec=jnp.exp(g_tot - gc_all), glast=jnp.exp(g_tot))


def _deltanet_heads(gts, cbs, qs, ks, vs, ss):
    c = CHUNK
    dh = DN_HEAD_DIM
    n = len(cbs)
    rng = range(n)
    cgs = [2 * DN_HEADS + cb for cb in cbs]
    beta = [gts[i]["beta"][:, cbs[i]:cbs[i] + 1] for i in rng]
    egc = [gts[i]["egc"][:, cgs[i]:cgs[i] + 1] for i in rng]
    decay = [jnp.exp(jnp.where(gts[i]["incl"], gts[i]["gc"][:, cgs[i]:cgs[i] + 1] - gts[i]["gc_t"][cgs[i]:cgs[i] + 1, :],
                               -1e30)) for i in rng]
    kbeta = [ks[i] * beta[i] for i in rng]
    prod = [lax.dot_general(jnp.concatenate([kbeta[i], qs[i]], axis=0).astype(BF16), ks[i].astype(BF16),
                            (((1,), (1,)), ((), ())), preferred_element_type=F32) for i in rng]
    a_mat = [jnp.where(gts[i]["strict"], prod[i][:c] * decay[i], 0.0) for i in rng]
    intra = [prod[i][c:] * decay[i] for i in rng]
    ks_qs = [_bdot(jnp.concatenate([kbeta[i] * egc[i], qs[i] * egc[i]], axis=0), ss[i]) for i in rng]
    p = [jnp.where(gts[i]["diag_blk"], -a_mat[i], 0.0) for i in rng]
    y = [jnp.concatenate([vs[i] * beta[i] - ks_qs[i][:c], jnp.where(gts[i]["diag_blk"], 0.0, a_mat[i])], axis=1)
         for i in rng]
    for j in range(4):
        y = [y[i] + _bdot(p[i], y[i]) for i in rng]
        if j < 3:
            p = [_bdot(p[i], p[i]) for i in rng]
    e = [y[i][:, dh:] for i in rng]
    y = [y[i][:, :dh] for i in rng]
    e2 = [_bdot(e[i], e[i]) for i in rng]
    y = [y[i] + _bdot(e2[i], y[i]) for i in rng]
    v_new = [y[i] - _bdot(e[i], y[i]) for i in rng]
    o = [ks_qs[i][c:] + _bdot(intra[i], v_new[i]) for i in rng]
    s_new = [ss[i] * gts[i]["glast"][:, cgs[i]:cgs[i] + 1] + lax.dot_general(
        (ks[i] * gts[i]["kdec"][:, cgs[i]:cgs[i] + 1]).astype(BF16), v_new[i].astype(BF16),
        (((0,), (0,)), ((), ())), preferred_element_type=F32) for i in rng]
    return o, s_new


def _deltanet_kernel(nb, qf_ref, kf_ref, vf_ref, gf_ref, qb_ref, kb_ref, vb_ref, gb_ref, alog_ref, dtb_ref,
                     of_ref, ob_ref, s_ref):
    @pl.when(pl.program_id(0) == 0)
    def _():
        s_ref[...] = jnp.zeros_like(s_ref)

    dirs = ((qf_ref, kf_ref, vf_ref, gf_ref, of_ref), (qb_ref, kb_ref, vb_ref, gb_ref, ob_ref))

    def body(it, carry):
        idx = [(it * DN_BATCHES + j, d, hh) for j in range(DN_BATCHES) for d in range(2) for hh in range(DN_HEADS)]
        rows_of = {j: pl.ds(it * DN_BATCHES + j, CHUNK, stride=nb) for j in range(DN_BATCHES)}
        rows_b = [rows_of[i // (2 * DN_HEADS)] for i in range(len(idx))]
        cbs = [d * DN_HEADS + hh for _, d, hh in idx]
        gates = {(j, d): _deltanet_gates(d, dirs[d][3][rows_of[j], :], alog_ref[...], dtb_ref[...])
                 for j in range(DN_BATCHES) for d in range(2)}
        gts = [gates[(i // (2 * DN_HEADS), d)] for i, (_, d, _) in enumerate(idx)]
        qs = [dirs[d][0][hh, rows_b[i], :] for i, (_, d, hh) in enumerate(idx)]
        ks = [dirs[d][1][hh, rows_b[i], :] for i, (_, d, hh) in enumerate(idx)]
        vs = [dirs[d][2][hh, rows_b[i], :] for i, (_, d, hh) in enumerate(idx)]
        ss = [s_ref[b * (2 * DN_HEADS) + cbs[i]] for i, (b, _, _) in enumerate(idx)]
        o, s_new = _deltanet_heads(gts, cbs, qs, ks, vs, ss)
        for i, (b, d, hh) in enumerate(idx):
            dirs[d][4][hh, rows_b[i], :] = o[i]
            s_ref[b * (2 * DN_HEADS) + cbs[i]] = s_new[i]
        return carry

    lax.fori_loop(0, nb // DN_BATCHES, body, 0)


def _deltanet(qkv, proj, alog_vec, dtb_vec, nb, n_ctx_chunks):
    rows = qkv.shape[1]
    tr = CHUNK * nb
    n_tot = rows // tr
    fwd = lambda i: i
    bwd = lambda i: _bwd_chunk(i, n_ctx_chunks, n_tot)

    def specs(order):
        return [pl.BlockSpec((DN_HEADS, tr, DN_HEAD_DIM), lambda i, p=p: (p, order(i), 0)) for p in range(3)] + [
            pl.BlockSpec((tr, LANES), lambda i: (order(i), COL_GATE))]

    vec = pl.BlockSpec((1, LANES), lambda i: (0, 0))
    out = jax.ShapeDtypeStruct((DN_HEADS, rows, DN_HEAD_DIM), F32)
    return pl.pallas_call(
        functools.partial(_deltanet_kernel, nb),
        grid=(n_tot,),
        in_specs=specs(fwd) + specs(bwd) + [vec, vec],
        out_specs=[pl.BlockSpec((DN_HEADS, tr, DN_HEAD_DIM), lambda i: (0, fwd(i), 0)),
                   pl.BlockSpec((DN_HEADS, tr, DN_HEAD_DIM), lambda i: (0, bwd(i), 0))],
        out_shape=[out, out],
        scratch_shapes=[pltpu.VMEM((nb * 2 * DN_HEADS, DN_HEAD_DIM, DN_HEAD_DIM), F32)],
        compiler_params=_cparams(("arbitrary",)),
        name="deltanet",
    )(qkv, qkv, qkv, proj, qkv, qkv, qkv, proj, alog_vec, dtb_vec)


def _mixer_x1(nb, of_ref, ob_ref, z_ref, hf_ref, hb_ref, ry_ref, onw_ref, w_ref, h_ref, gt_ref, g_ref, b_ref):
    z = z_ref[...]
    parts = []
    for hh in range(DN_HEADS):
        sl = slice(hh * DN_HEAD_DIM, (hh + 1) * DN_HEAD_DIM)
        oh = of_ref[hh] + ob_ref[hh]
        inv = lax.rsqrt(jnp.mean(oh * oh, axis=-1, keepdims=True) + NORM_EPS)
        parts.append(oh * inv * onw_ref[...] * _silu(z[:, sl]))
    parts.append((hf_ref[...].astype(F32) + hb_ref[...].astype(F32)) * jax.nn.gelu(ry_ref[...]))
    y = jnp.concatenate(parts, axis=1).astype(BF16)
    u = jnp.dot(y, w_ref[...], preferred_element_type=F32)
    r = DEEP_ALPHA * h_ref[...] + _per_batch(u, gt_ref[...], nb) * u
    return _layer_norm(r) * g_ref[...] + b_ref[...]


PAIRS_PER_GROUP = EXPERTS_PER_GROUP * (EXPERTS_PER_GROUP - 1) // 2
N_CLASSES = N_GROUPS * PAIRS_PER_GROUP
assert N_CLASSES <= LANES
INFO_CLASS, INFO_RANK, INFO_W_LO, INFO_W_HI = range(4)
Y_SUB = 2 * (D_MODEL // LANES)


ROW_SUB = D_MODEL // LANES


def _store_row_tiles(ref, x):
    n = x.shape[0]
    for c in range(ROW_SUB):
        ref[pl.ds(c, n, stride=ROW_SUB), :] = x[:, c * LANES:(c + 1) * LANES]


def _load_row_tiles(ref, n):
    return jnp.concatenate([ref[pl.ds(c, n, stride=ROW_SUB), :] for c in range(ROW_SUB)], axis=1)


def _first_index(mask, lane):
    return jnp.min(jnp.where(mask, lane, LANES), axis=-1, keepdims=True)


def _route(nb, x, sh_ref, sc_ref, wr_ref, br_ref, hm_ref, info_ref, cnt_ref, run_ref):
    @pl.when(pl.program_id(0) == 0)
    def _():
        run_ref[...] = jnp.zeros_like(run_ref)

    xn = _layer_norm(x)
    hm = xn * (1.0 + _per_batch(xn, sc_ref[...], nb)) + _per_batch(xn, sh_ref[...], nb)
    _store_row_tiles(hm_ref, hm)
    tm = hm.shape[0]
    h_hi = hm.astype(BF16)
    h_lo = (hm - h_hi.astype(F32)).astype(BF16)
    hi_hl = jnp.dot(h_hi, wr_ref[...], preferred_element_type=F32)
    logits = (hi_hl[:, :LANES] + hi_hl[:, LANES:] + jnp.dot(h_lo, wr_ref[:, :LANES], preferred_element_type=F32)
              + br_ref[...])
    lane = lax.broadcasted_iota(jnp.int32, (tm, LANES), 1)
    neg = -jnp.inf
    is_g = lane < N_GROUPS
    gmax = jnp.max(jnp.where(is_g, logits, neg), axis=-1, keepdims=True)
    g_sel = _first_index(jnp.logical_and(is_g, logits == gmax), lane)
    p_group = 1.0 / jnp.sum(jnp.where(is_g, jnp.exp(logits - gmax), 0.0), axis=-1, keepdims=True)
    lo = N_GROUPS + EXPERTS_PER_GROUP * g_sel
    in_grp = jnp.logical_and(lane >= lo, lane < lo + EXPERTS_PER_GROUP)
    m1 = jnp.max(jnp.where(in_grp, logits, neg), axis=-1, keepdims=True)
    i1 = _first_index(jnp.logical_and(in_grp, logits == m1), lane)
    rest = jnp.logical_and(in_grp, lane != i1)
    m2 = jnp.max(jnp.where(rest, logits, neg), axis=-1, keepdims=True)
    i2 = _first_index(jnp.logical_and(rest, logits == m2), lane)
    e2 = jnp.exp(m2 - m1)
    w1 = p_group / (1.0 + e2)
    w2 = p_group * e2 / (1.0 + e2)
    l1 = i1 - lo
    l2 = i2 - lo
    e_lo = jnp.minimum(l1, l2)
    e_hi = jnp.maximum(l1, l2)
    pair = jnp.right_shift(e_lo * (2 * EXPERTS_PER_GROUP - 1 - e_lo), 1) + (e_hi - e_lo - 1)
    cls = g_sel * PAIRS_PER_GROUP + pair
    first_is_lo = l1 < l2
    w_lo = jnp.where(first_is_lo, w1, w2)
    w_hi = jnp.where(first_is_lo, w2, w1)
    oh = lane == cls
    r_i = lax.broadcasted_iota(jnp.int32, (tm, tm), 0)
    c_i = lax.broadcasted_iota(jnp.int32, (tm, tm), 1)
    before = jnp.dot((r_i > c_i).astype(BF16), oh.astype(BF16), preferred_element_type=F32) + run_ref[...]
    rank = jnp.sum(jnp.where(oh, before, 0.0), axis=-1, keepdims=True)
    run_ref[...] = run_ref[...] + jnp.sum(oh.astype(F32), axis=0, keepdims=True)
    cnt_ref[...] = run_ref[...]
    info = jnp.zeros((tm, LANES), F32)
    for idx, val in ((INFO_CLASS, cls.astype(F32)), (INFO_RANK, rank), (INFO_W_LO, w_lo), (INFO_W_HI, w_hi)):
        info = jnp.where(lane == idx, val, info)
    info_ref[...] = info


def _mixer_router_kernel(nb, of_ref, ob_ref, z_ref, hf_ref, hb_ref, ry_ref, onw_ref, w_ref, h_ref, gt_ref, g_ref, b_ref,
                         sh_ref, sc_ref, wr_ref, br_ref, x1_ref, hm_ref, info_ref, cnt_ref, run_ref):
    x1 = _mixer_x1(nb, of_ref, ob_ref, z_ref, hf_ref, hb_ref, ry_ref, onw_ref, w_ref, h_ref, gt_ref, g_ref, b_ref)
    x1_ref[...] = x1
    _route(nb, x1, sh_ref, sc_ref, wr_ref, br_ref, hm_ref, info_ref, cnt_ref, run_ref)


def _mixer_router(o_f, o_b, proj, hf, hb, onorm_w, w_out, layer, h, mod_rows, ln_g, ln_b, wr, br, nb, ctx_rows, row0):
    rows, d = h.shape
    tm = ROW_TILE
    t0 = row0 // tm
    n = rows - row0
    lat = lambda i: jnp.where((i + t0) * tm >= ctx_rows, 1, 0)
    half = pl.BlockSpec((tm, DN_WIDTH), lambda i: (i + t0, 0))
    planes = pl.BlockSpec((DN_HEADS, tm, DN_HEAD_DIM), lambda i: (0, i + t0, 0))
    vec = lambda m: pl.BlockSpec((1, m), lambda i: (0, 0))
    mod = lambda k: pl.BlockSpec((None, nb, d), lambda i: (lat(i), 0, k))
    return pl.pallas_call(
        functools.partial(_mixer_router_kernel, nb),
        grid=(n // tm,),
        in_specs=[planes, planes, pl.BlockSpec((tm, DN_WIDTH), lambda i: (i + t0, COL_Z)), half, half,
                  pl.BlockSpec((tm, RG_WIDTH), lambda i: (i + t0, COL_RY)),
                  vec(DN_HEAD_DIM), pl.BlockSpec((None, d, d), lambda i: (layer, 0, 0)),
                  pl.BlockSpec((tm, d), lambda i: (i + t0, 0)),
                  mod(2),
                  vec(d), vec(d),
                  mod(3), mod(4),
                  pl.BlockSpec((d, 2 * LANES), lambda i: (0, 0)), vec(LANES)],
        out_specs=[pl.BlockSpec((tm, d), lambda i: (i, 0)),
                   pl.BlockSpec((tm * ROW_SUB, LANES), lambda i: (i, 0)),
                   pl.BlockSpec((tm, LANES), lambda i: (i, 0)),
                   pl.BlockSpec((1, LANES), lambda i: (0, 0))],
        out_shape=[jax.ShapeDtypeStruct((n, d), F32),
                   jax.ShapeDtypeStruct((n * ROW_SUB, LANES), F32),
                   jax.ShapeDtypeStruct((n, LANES), F32),
                   jax.ShapeDtypeStruct((1, LANES), F32)],
        scratch_shapes=[pltpu.VMEM((1, LANES), F32)],
        compiler_params=_cparams(("arbitrary",)),
        name="mixer_router",
    )(o_f, o_b, proj, hf, hb, proj, onorm_w, w_out, h, mod_rows, ln_g, ln_b, mod_rows, mod_rows, wr, br)


def _dispatch_kernel(tmx, ends_ref, cnt_ref, dst_ref, hm_ref, xs_hbm, zero_scr, sem, zsem):
    rs = ROW_SUB
    tb = dst_ref.shape[1]

    @pl.when(pl.program_id(0) == 0)
    def _():
        zero_scr[...] = jnp.zeros_like(zero_scr)

        def tail_copy(c):
            start = pl.multiple_of((ends_ref[c] - tmx) * rs, tmx * rs)
            return pltpu.make_async_copy(zero_scr, xs_hbm.at[pl.ds(start, tmx * rs)], zsem)

        def start_one(c, carry):
            @pl.when(cnt_ref[c] > 0)
            def _():
                tail_copy(c).start()
            return carry

        def wait_one(c, carry):
            @pl.when(cnt_ref[c] > 0)
            def _():
                tail_copy(c).wait()
            return carry

        lax.fori_loop(0, N_CLASSES, start_one, 0)

        def spare_copy(j):
            return pltpu.make_async_copy(zero_scr, xs_hbm.at[pl.ds(pl.multiple_of(j * (tmx * rs), tmx * rs), tmx * rs)],
                                         zsem)

        n_used = ends_ref[N_CLASSES - 1] // tmx
        n_tiles = xs_hbm.shape[0] // (tmx * rs)
        lax.fori_loop(n_used, n_tiles, lambda j, carry: (spare_copy(j).start(), carry)[1], 0)
        lax.fori_loop(0, N_CLASSES, wait_one, 0)
        lax.fori_loop(n_used, n_tiles, lambda j, carry: (spare_copy(j).wait(), carry)[1], 0)

    for r in range(tb):
        dst = xs_hbm.at[pl.ds(pl.multiple_of(dst_ref[0, r], rs), rs)]
        pltpu.make_async_copy(hm_ref.at[pl.ds(r * rs, rs)], dst, sem).start(priority=r % 2)
    pltpu.make_async_copy(hm_ref, hm_ref, sem).wait()


def _dispatch(hm, dest, class_ends, class_cnt, sorted_rows, tmx):
    tb = DISPATCH_TILE
    n_blocks = dest.shape[0] // tb
    grid_spec = pltpu.PrefetchScalarGridSpec(
        num_scalar_prefetch=2,
        grid=(n_blocks,),
        in_specs=[pl.BlockSpec((None, 1, tb), lambda i, e, c: (i, 0, 0), memory_space=pltpu.SMEM),
                  pl.BlockSpec((tb * ROW_SUB, LANES), lambda i, e, c: (i, 0))],
        out_specs=pl.BlockSpec(memory_space=pl.ANY),
        scratch_shapes=[pltpu.VMEM((tmx * ROW_SUB, LANES), F32), pltpu.SemaphoreType.DMA(()),
                        pltpu.SemaphoreType.DMA(())],
    )
    return pl.pallas_call(
        functools.partial(_dispatch_kernel, tmx),
        grid_spec=grid_spec,
        out_shape=jax.ShapeDtypeStruct((sorted_rows * ROW_SUB, LANES), F32),
        compiler_params=_cparams(("arbitrary",)),
        name="moe_dispatch",
    )(class_ends, class_cnt, (dest * ROW_SUB).reshape(n_blocks, 1, tb), hm)


def _class_ffn_kernel(telo_ref, tehi_ref, nact_ref, xs_ref, wgul_ref, wdl_ref, wguh_ref, wdh_ref, ys_ref, y_scr):
    j = pl.program_id(0)
    tmx = xs_ref.shape[0] // ROW_SUB
    de = wdl_ref.shape[0]

    @pl.when(j < nact_ref[0])
    def _():
        x = _load_row_tiles(xs_ref, tmx).astype(BF16)
        for half, (wgu_ref, wd_ref) in enumerate(((wgul_ref, wdl_ref), (wguh_ref, wdh_ref))):
            ab = jnp.dot(x, wgu_ref[...], preferred_element_type=F32)
            y = jnp.dot((_silu(ab[:, :de]) * ab[:, de:]).astype(BF16), wd_ref[...], preferred_element_type=F32)
            for c in range(ROW_SUB):
                y_scr[pl.ds(half * ROW_SUB + c, tmx, stride=Y_SUB), :] = y[:, c * LANES:(c + 1) * LANES]
        ys_ref[...] = y_scr[...].astype(BF16)

    @pl.when(j >= nact_ref[0])
    def _():
        ys_ref[...] = jnp.zeros_like(ys_ref)


def _class_ffn(xs, tile_e_lo, tile_e_hi, n_active, w_gate_up, w_down, layer, tmx):
    n_tiles = xs.shape[0] // (tmx * ROW_SUB)
    d, de2 = w_gate_up.shape[-2:]
    w_in = lambda te_idx: pl.BlockSpec((None, None, d, de2), lambda j, *te: (layer, te[te_idx][j], 0, 0))
    w_out = lambda te_idx: pl.BlockSpec((None, None, de2 // 2, d), lambda j, *te: (layer, te[te_idx][j], 0, 0))
    grid_spec = pltpu.PrefetchScalarGridSpec(
        num_scalar_prefetch=3,
        grid=(n_tiles,),
        in_specs=[pl.BlockSpec((tmx * ROW_SUB, LANES), lambda j, lo, hi, na: (jnp.minimum(j, na[0] - 1), 0)),
                  w_in(0), w_out(0), w_in(1), w_out(1)],
        out_specs=pl.BlockSpec((tmx * Y_SUB, LANES), lambda j, lo, hi, na: (j, 0)),
        scratch_shapes=[pltpu.VMEM((tmx * Y_SUB, LANES), F32)],
    )
    return pl.pallas_call(
        _class_ffn_kernel,
        grid_spec=grid_spec,
        out_shape=jax.ShapeDtypeStruct((n_tiles * tmx * Y_SUB, LANES), BF16),
        compiler_params=_cparams(("arbitrary",)),
        name="expert_ffn",
    )(tile_e_lo, tile_e_hi, n_active, xs, w_gate_up, w_down, w_gate_up, w_down)


def _moe_out_kernel(nb, batch_major, src0_ref, srcn_ref, ys_hbm, info_ref, x_ref, gt_ref, g_ref, b_ref, o_ref,
                    ybuf, y_scr, sem):
    i = pl.program_id(0)
    tm = info_ref.shape[0]
    slot = i % 2
    nslot = 1 - slot

    def gather_all(src_ref, s):
        for r in range(tm):
            pltpu.make_async_copy(ys_hbm.at[pl.ds(pl.multiple_of(src_ref[0, r], Y_SUB), Y_SUB)],
                                  ybuf.at[s, pl.ds(r * Y_SUB, Y_SUB)], sem.at[s]).start(priority=r % 2)

    def wait_rows(s):
        pltpu.make_async_copy(ybuf.at[s], ybuf.at[s], sem.at[s]).wait()

    @pl.when(i == 0)
    def _():
        gather_all(src0_ref, 0)

    wait_rows(slot)
    gather_all(srcn_ref, nslot)
    info = info_ref[...]
    y_scr[...] = ybuf[slot].astype(F32)
    y_lo = jnp.concatenate([y_scr[pl.ds(c, tm, stride=Y_SUB), :] for c in range(ROW_SUB)], axis=1)
    y_hi = jnp.concatenate([y_scr[pl.ds(ROW_SUB + c, tm, stride=Y_SUB), :] for c in range(ROW_SUB)], axis=1)
    f = y_lo * info[:, INFO_W_LO:INFO_W_LO + 1] + y_hi * info[:, INFO_W_HI:INFO_W_HI + 1]
    r = DEEP_ALPHA * x_ref[...] + _per_batch(f, gt_ref[...], nb) * f
    res = _layer_norm(r) * g_ref[...] + b_ref[...]
    if batch_major:
        for c in range(ROW_SUB):
            y_scr[pl.ds(c * tm, tm), :] = res[:, c * LANES:(c + 1) * LANES]
        for b in range(nb):
            for c in range(ROW_SUB):
                o_ref[b, :, c * LANES:(c + 1) * LANES] = y_scr[pl.ds(c * tm + b, tm // nb, stride=nb), :]
    else:
        o_ref[...] = res

    @pl.when(i == pl.num_programs(0) - 1)
    def _():
        wait_rows(nslot)


def _moe_out(ys, dest, info, x1, mod_rows, ln_g, ln_b, nb, ctx_rows, batch_major):
    rows, d = x1.shape
    tm = ROW_TILE
    n_blocks = rows // tm
    if batch_major:
        out_spec = pl.BlockSpec((nb, tm // nb, d), lambda i: (0, i, 0))
        out_shape = jax.ShapeDtypeStruct((nb, rows // nb, d), F32)
    else:
        out_spec = pl.BlockSpec((tm, d), lambda i: (i, 0))
        out_shape = jax.ShapeDtypeStruct((rows, d), F32)
    lat = lambda i: jnp.where(i * tm >= ctx_rows, 1, 0)
    vec = pl.BlockSpec((1, d), lambda i: (0, 0))
    src = (dest * Y_SUB).reshape(n_blocks, 1, tm)
    smem = lambda f: pl.BlockSpec((None, 1, tm), f, memory_space=pltpu.SMEM)
    return pl.pallas_call(
        functools.partial(_moe_out_kernel, nb, batch_major),
        grid=(n_blocks,),
        in_specs=[smem(lambda i: (0, 0, 0)),
                  smem(lambda i: (jnp.minimum(i + 1, n_blocks - 1), 0, 0)),
                  pl.BlockSpec(memory_space=pl.ANY),
                  pl.BlockSpec((tm, LANES), lambda i: (i, 0)),
                  pl.BlockSpec((tm, d), lambda i: (i, 0)),
                  pl.BlockSpec((None, nb, d), lambda i: (lat(i), 0, 5)),
                  vec, vec],
        out_specs=out_spec,
        out_shape=out_shape,
        scratch_shapes=[pltpu.VMEM((2, tm * Y_SUB, LANES), BF16), pltpu.VMEM((tm * Y_SUB, LANES), F32),
                        pltpu.SemaphoreType.DMA((2,))],
        compiler_params=_cparams(("arbitrary",)),
        name="moe_out",
    )(src, src, ys, info, x1, mod_rows, ln_g, ln_b)


def _class_experts():
    lo, hi = [], []
    for g in range(N_GROUPS):
        for a in range(EXPERTS_PER_GROUP):
            for b in range(a + 1, EXPERTS_PER_GROUP):
                lo.append(g * EXPERTS_PER_GROUP + a)
                hi.append(g * EXPERTS_PER_GROUP + b)
    return np.asarray(lo, np.int32), np.asarray(hi, np.int32)


def _routing_tables(info, counts, n_tiles, tmx):
    cnt = counts[0, :N_CLASSES].astype(jnp.int32)
    padded = ((cnt + tmx - 1) // tmx) * tmx
    ends = jnp.cumsum(padded)
    starts = ends - padded
    cls = info[:, INFO_CLASS].astype(jnp.int32)
    class_ids = jnp.arange(N_CLASSES, dtype=jnp.int32)
    start_of_tok = jnp.sum(jnp.where(cls[:, None] == class_ids[None, :], starts[None, :], 0), axis=1)
    dest = start_of_tok + info[:, INFO_RANK].astype(jnp.int32)
    tile_start = jnp.arange(n_tiles, dtype=jnp.int32) * tmx
    n_before = jnp.sum((ends[None, :] <= tile_start[:, None]).astype(jnp.int32), axis=1)
    tile_cls = jnp.minimum(n_before, N_CLASSES - 1)
    e_lo, e_hi = _class_experts()
    n_active = (ends[-1:] // tmx).astype(jnp.int32)
    return dest, ends, cnt, jnp.asarray(e_lo)[tile_cls], jnp.asarray(e_hi)[tile_cls], n_active


def _relayout_w_in(w_in):
    qkvz = w_in[..., :4 * DN_WIDTH]
    gates = w_in[..., 4 * DN_WIDTH:4 * DN_WIDTH + 4 * DN_HEADS]
    rxy = w_in[..., 4 * DN_WIDTH + 4 * DN_HEADS:]
    pad = jnp.zeros(w_in.shape[:-1] + (LANES - 4 * DN_HEADS,), w_in.dtype)
    return jnp.concatenate([qkvz, rxy, gates, pad], axis=-1).astype(BF16)


def _block_diag(w):
    n, c, _ = w.shape
    eye = jnp.eye(n, dtype=w.dtype)
    return (eye[:, None, :, None] * w[:, :, None, :]).reshape(n * c, n * c)


def _lane_vec(vals, offset):
    return jnp.zeros((1, LANES), F32).at[0, offset:offset + vals.size].set(vals.reshape(-1))


def kernel(x, c, ctx, c_ctx, w_ada, b_ada, w_in, conv_qkv_w, dn_a_log, dn_dt_bias, dn_onorm_w, rg_conv_w, rg_conv_b, rg_wa, rg_ba, rg_wi, rg_bi, rg_lambda, w_out, ln_g, ln_b, router_wg, router_bg, router_we, router_be, w_e_gate, w_e_up, w_e_down):
    nb, t_lat, d = x.shape
    t_ctx = ctx.shape[1]
    assert d == D_MODEL and nb % 8 == 0 and t_lat % GRID_W == 0 and t_ctx % CHUNK == 0
    assert (t_ctx * nb) % INPROJ_TILE == 0 and (t_lat * nb) % INPROJ_TILE == 0 and INPROJ_TILE % ROW_TILE == 0
    assert (t_ctx * nb) % DISPATCH_TILE == 0 and (t_lat * nb) % DISPATCH_TILE == 0
    tt = t_ctx + t_lat
    rows = tt * nb
    ctx_rows = t_ctx * nb
    n_ctx_chunks = t_ctx // CHUNK

    n_cc = ((nb + 1 + 7) // 8) * 8
    cc = jnp.zeros((n_cc, d), F32).at[:nb].set(c).at[nb].set(c_ctx)
    mod = _modulation(cc, w_ada, b_ada)
    mod_ctx = jnp.broadcast_to(mod[:, nb:nb + 1], (DEPTH, nb, 6 * d))
    mod_rows = jnp.stack([mod_ctx, mod[:, :nb]], axis=1)

    w_in2 = _relayout_w_in(w_in)
    w_out_bf = w_out.astype(BF16)
    w_gate_up = jnp.concatenate([w_e_gate, w_e_up], axis=-1).astype(BF16)
    w_down = w_e_down.astype(BF16)
    for l in range(DEPTH):
        row0 = ctx_rows if l == DEPTH - 1 else 0
        moe_ctx_rows = ctx_rows - row0
        n_tiles = (rows - row0) // MOE_TILE + N_CLASSES
        mr = mod_rows[l]
        if l == 0:
            proj, h = _in_projection_first(ctx, x, mr, w_in2, l, nb)
        else:
            proj = _in_projection(h, mr, w_in2, l, nb, ctx_rows)
        qkv = _qkv_prep(proj, conv_qkv_w[l], nb, n_ctx_chunks)
        wg = jnp.concatenate([_block_diag(rg_wa[l, 0]), _block_diag(rg_wi[l, 0]),
                              _block_diag(rg_wa[l, 1]), _block_diag(rg_wi[l, 1])], axis=1).astype(BF16)
        bg = jnp.concatenate([rg_ba[l, 0], rg_bi[l, 0], rg_ba[l, 1], rg_bi[l, 1]])[None]
        hf, hb = _rglru(proj, rg_conv_w[l], rg_conv_b[l][None], wg, bg, rg_lambda[l], nb, n_ctx_chunks)
        alog_vec = _lane_vec(dn_a_log[l], 2 * DN_HEADS)
        dtb_vec = _lane_vec(dn_dt_bias[l], 2 * DN_HEADS)
        o_f, o_b = _deltanet(qkv, proj, alog_vec, dtb_vec, nb, n_ctx_chunks)
        wr = jnp.zeros((d, LANES), F32).at[:, :N_GROUPS].set(router_wg[l]).at[
            :, N_GROUPS:N_GROUPS + N_EXPERTS].set(router_we[l])
        br = _lane_vec(jnp.concatenate([router_bg[l], router_be[l]]), 0)
        wr_hi = wr.astype(BF16)
        wr_hl = jnp.concatenate([wr_hi, (wr - wr_hi.astype(F32)).astype(BF16)], axis=1)
        x1, hm, info, counts = _mixer_router(o_f, o_b, proj, hf, hb, dn_onorm_w[l][None], w_out_bf, l, h, mr,
                                             ln_g[l, 0][None], ln_b[l, 0][None], wr_hl, br, nb, ctx_rows, row0)
        dest, class_ends, class_cnt, tile_e_lo, tile_e_hi, n_active = _routing_tables(info, counts, n_tiles, MOE_TILE)
        xs = _dispatch(hm, dest, class_ends, class_cnt, n_tiles * MOE_TILE, MOE_TILE)
        ys = _class_ffn(xs, tile_e_lo, tile_e_hi, n_active, w_gate_up, w_down, l, MOE_TILE)
        h = _moe_out(ys, dest, info, x1, mr, ln_g[l, 1][None], ln_b[l, 1][None], nb, moe_ctx_rows,
                     batch_major=(l == DEPTH - 1))

    return h
```
